```python
import math
import jax, jax.numpy as jnp
from jax import lax
import numpy as np

D_MODEL = 1024
BATCH = 8
SEQ = 4096
DEPTH = 2

GRID_W = 64
CTX_LEN = 256
RET_HEADS = 4
RET_QK_DIM = 64
RET_V_DIM = 128
RET_CHUNK = 128
ATT_HEADS = 4
ATT_KV_HEADS = 2
ATT_HEAD_DIM = 128
ATT_BLOCK = 128
ROPE_THETA = 10000.0
POOL_WINDOWS = (2, 4, 8, 16)
POOL_GROUP_DIM = 128
N_BRANCH = 3
BRANCH_WIDTH = 512
FFN_DIM = 3584
N_EXPERTS = 8
TOP_K = 2
NORM_EPS = 1e-6
GN_EPS = 1e-5
RET_QK_WIDTH = RET_HEADS * RET_QK_DIM
RET_WIDTH = RET_HEADS * RET_V_DIM
ATT_Q_WIDTH = ATT_HEADS * ATT_HEAD_DIM
ATT_KV_WIDTH = ATT_KV_HEADS * ATT_HEAD_DIM
POOL_WIDTH = len(POOL_WINDOWS) * POOL_GROUP_DIM
IN_SPLITS = (RET_QK_WIDTH, RET_QK_WIDTH, RET_WIDTH, RET_WIDTH, ATT_Q_WIDTH, ATT_KV_WIDTH, ATT_KV_WIDTH, POOL_WIDTH, N_BRANCH * D_MODEL)
IN_WIDTH = sum(IN_SPLITS)

kernel_name = 'hybrid_retention_gqa_pool_moe_dit'


def rms_norm(x, gain):
    xf = x.astype(jnp.float32)
    y = xf * lax.rsqrt(jnp.mean(xf * xf, axis=-1, keepdims=True) + NORM_EPS)
    return (y * gain.astype(jnp.float32)).astype(x.dtype)


def modulate(x, shift, scale):
    return x * (1 + scale) + shift


def flip(a):
    return a[:, ::-1]


def rope_angles(pos, dim):
    freqs = 1.0 / (ROPE_THETA ** (jnp.arange(0, dim, 2, dtype=jnp.float32) / dim))
    return pos[:, None] * freqs[None, :]


def apply_rope(x, ang):
    xf = x.astype(jnp.float32).reshape(x.shape[:-1] + (x.shape[-1] // 2, 2))
    cos = jnp.cos(ang)[None, :, None, :]
    sin = jnp.sin(ang)[None, :, None, :]
    x0, x1 = xf[..., 0], xf[..., 1]
    out = jnp.stack([x0 * cos - x1 * sin, x0 * sin + x1 * cos], axis=-1)
    return out.reshape(x.shape).astype(x.dtype)


def retention_scan(q, k, v, log_g, state0):
    b, n, h, _ = q.shape
    dv = v.shape[-1]
    nc = n // RET_CHUNK

    def chunks(a):
        return a.reshape(b, nc, RET_CHUNK, h, a.shape[-1]).transpose(1, 0, 3, 2, 4)

    pos = jnp.arange(RET_CHUNK, dtype=jnp.float32)
    diff = pos[:, None] - pos[None, :]
    decay_in = jnp.where(diff >= 0, jnp.exp(jnp.maximum(diff, 0.0) * log_g[:, None, None]), 0.0)
    xi = jnp.exp((pos + 1.0) * log_g[:, None])[None, :, :, None]
    zeta = jnp.exp((RET_CHUNK - 1.0 - pos) * log_g[:, None])[None, :, :, None]
    g_chunk = jnp.exp(RET_CHUNK * log_g)[None, :, None, None]

    def step(state, inp):
        qc, kc, vc = inp
        scores = jnp.einsum('bhld,bhmd->bhlm', qc, kc) * decay_in
        out = jnp.einsum('bhlm,bhme->bhle', scores, vc) + jnp.einsum('bhld,bhde->bhle', qc, state) * xi
        state = g_chunk * state + jnp.einsum('bhld,bhle->bhde', kc * zeta, vc)
        return state, out

    state, out = lax.scan(step, state0, (chunks(q), chunks(k), chunks(v)))
    out = out.transpose(1, 0, 3, 2, 4).reshape(b, n, h, dv)
    return out, state


def retention_branch(q, k, v, g, qc, kc, vc, gc, ret_decay, ret_gn, ang_ret, need_ctx):
    f32 = jnp.float32
    b = q.shape[0]

    def heads(a, d):
        return a.astype(f32).reshape(a.shape[0], a.shape[1], RET_HEADS, d)

    scale = RET_QK_DIM ** -0.5
    q_l = apply_rope(heads(q, RET_QK_DIM), ang_ret)
    k_l = apply_rope(heads(k, RET_QK_DIM), ang_ret) * scale
    v_l = heads(v, RET_V_DIM)
    q_c = heads(qc, RET_QK_DIM)
    k_c = heads(kc, RET_QK_DIM) * scale
    v_c = heads(vc, RET_V_DIM)
    log_g = -jnp.exp(ret_decay.astype(f32))
    zeros = jnp.zeros((b, RET_HEADS, RET_QK_DIM, RET_V_DIM), f32)
    oc_f, st_f = retention_scan(q_c, k_c, v_c, log_g[0], zeros)
    oc_b, st_b = retention_scan(flip(q_c), flip(k_c), flip(v_c), log_g[1], zeros)
    o_f, _ = retention_scan(q_l, k_l, v_l, log_g[0], st_f)
    o_b, _ = retention_scan(flip(q_l), flip(k_l), flip(v_l), log_g[1], st_b)
    gn_w = ret_gn.astype(f32).reshape(RET_HEADS, RET_V_DIM)

    def finish(o, gate):
        mu = jnp.mean(o, axis=-1, keepdims=True)
        var = jnp.mean(jnp.square(o - mu), axis=-1, keepdims=True)
        o = (o - mu) * lax.rsqrt(var + GN_EPS) * gn_w
        o = o.reshape(o.shape[0], o.shape[1], RET_WIDTH)
        return (jax.nn.silu(gate.astype(f32)) * o).astype(gate.dtype)

    y = finish(o_f + flip(o_b), g)
    yc = finish(oc_f + flip(oc_b), gc) if need_ctx else None
    return y, yc


def attention_branch(q, k, v, qc, kc, vc, attn_qn, attn_kn, ang_att, need_ctx):
    b, s, _ = q.shape
    grp = ATT_HEADS // ATT_KV_HEADS

    def prep_q(a):
        return rms_norm(a.reshape(a.shape[0], a.shape[1], ATT_HEADS, ATT_HEAD_DIM), attn_qn)

    def prep_k(a):
        return rms_norm(a.reshape(a.shape[0], a.shape[1], ATT_KV_HEADS, ATT_HEAD_DIM), attn_kn)

    q_l = apply_rope(prep_q(q), ang_att)
    k_l = apply_rope(prep_k(k), ang_att)
    v_l = v.reshape(b, s, ATT_KV_HEADS, ATT_HEAD_DIM)
    k_c = prep_k(kc)
    v_c = vc.reshape(b, vc.shape[1], ATT_KV_HEADS, ATT_HEAD_DIM)
    keys = jnp.concatenate([k_l, k_c], axis=1)
    vals = jnp.concatenate([v_l, v_c], axis=1)
    scale = ATT_HEAD_DIM ** -0.5

    def attend(qb, kk, vv):
        sc = jnp.einsum('bqkgd,bskd->bkgqs', qb, kk).astype(jnp.float32) * scale
        p = jax.nn.softmax(sc, axis=-1).astype(vv.dtype)
        return jnp.einsum('bkgqs,bskd->bqkgd', p, vv)

    nb = s // ATT_BLOCK
    qblocks = q_l.reshape(b, nb, ATT_BLOCK, ATT_KV_HEADS, grp, ATT_HEAD_DIM).transpose(1, 0, 2, 3, 4, 5)
    o = lax.map(lambda qb: attend(qb, keys, vals), qblocks)
    y = o.transpose(1, 0, 2, 3, 4, 5).reshape(b, s, ATT_Q_WIDTH)
    yc = None
    if need_ctx:
        q_c = prep_q(qc).reshape(b, qc.shape[1], ATT_KV_HEADS, grp, ATT_HEAD_DIM)
        yc = attend(q_c, k_c, v_c).reshape(b, qc.shape[1], ATT_Q_WIDTH)
    return y, yc


def pool_mixer(u, pool_w, pool_scale):
    b, n, _ = u.shape
    uf = u.astype(jnp.float32)
    cs = jnp.concatenate([jnp.zeros((b, 1, POOL_WIDTH), jnp.float32), jnp.cumsum(uf, axis=1)], axis=1)
    t = jnp.arange(n)
    diffs = []
    for gi, w in enumerate(POOL_WINDOWS):
        lo = jnp.clip(t - w // 2, 0, n)
        hi = jnp.clip(t + w // 2, 0, n)
        sl = slice(gi * POOL_GROUP_DIM, (gi + 1) * POOL_GROUP_DIM)
        csg = cs[..., sl]
        mean = (jnp.take(csg, hi, axis=1) - jnp.take(csg, lo, axis=1)) / (hi - lo).astype(jnp.float32)[None, :, None]
        diffs.append(mean - uf[..., sl])
    d = jnp.stack(diffs, axis=2).astype(u.dtype)
    y = jnp.einsum('bngc,gcd->bngd', d, pool_w)
    return y.reshape(b, n, POOL_WIDTH) * pool_scale


def token_mixer(h, hc, w_in, ret_decay, ret_gn, attn_qn, attn_kn, pool_w, pool_scale, w_branch, w_out, ang_att, ang_ret, need_ctx):
    offs = np.cumsum(IN_SPLITS)[:-1].tolist()
    p = jnp.split(h @ w_in, offs, axis=-1)
    pc = jnp.split(hc @ w_in, offs, axis=-1)
    y_ret, yc_ret = retention_branch(p[0], p[1], p[2], p[3], pc[0], pc[1], pc[2], pc[3], ret_decay, ret_gn, ang_ret, need_ctx)
    y_att, yc_att = attention_branch(p[4], p[5], p[6], pc[4], pc[5], pc[6], attn_qn, attn_kn, ang_att, need_ctx)

    def merge(y_r, y_a, y_p, gate_logits):
        gates = jax.nn.sigmoid(gate_logits.astype(jnp.float32)).astype(y_r.dtype)
        gates = gates.reshape(gates.shape[:-1] + (N_BRANCH, D_MODEL))
        mixed = (gates[..., 0, :] * (y_r @ w_branch[0])
                 + gates[..., 1, :] * (y_a @ w_branch[1])
                 + gates[..., 2, :] * (y_p @ w_branch[2]))
        return mixed @ w_out

    y = merge(y_ret, y_att, pool_mixer(p[7], pool_w, pool_scale), p[8])
    yc = merge(yc_ret, yc_att, pool_mixer(pc[7], pool_w, pool_scale), pc[8]) if need_ctx else None
    return y, yc


def swiglu(h, w1, w3, w2):
    return (jax.nn.silu(h @ w1) * (h @ w3)) @ w2


def moe_swiglu(h, router_w, router_b, w1, w3, w2):
    shape = h.shape
    t = h.reshape(-1, shape[-1])
    logits = (t @ router_w + router_b).astype(jnp.float32)
    top_logit, top_idx = lax.top_k(logits, TOP_K)
    weights = jax.nn.softmax(top_logit, axis=-1)
    flat_e = top_idx.reshape(-1)
    order = jnp.argsort(flat_e)
    tok = order // TOP_K
    sizes = jnp.bincount(flat_e, length=N_EXPERTS).astype(jnp.int32)
    xs = t[tok]
    a = lax.ragged_dot(xs, w1, sizes)
    bb = lax.ragged_dot(xs, w3, sizes)
    ys = lax.ragged_dot(jax.nn.silu(a) * bb, w2, sizes)
    ys = ys * weights.reshape(-1)[order][:, None].astype(ys.dtype)
    out = jnp.zeros_like(t).at[tok].add(ys)
    return out.reshape(shape)


def channel_mixer(a, layer, ffn_w1, ffn_w3, ffn_w2, moe_router, moe_router_b, moe_w1, moe_w3, moe_w2):
    j = layer // 2
    if layer % 2 == 0:
        return swiglu(a, ffn_w1[j], ffn_w3[j], ffn_w2[j])
    return moe_swiglu(a, moe_router[j], moe_router_b[j], moe_w1[j], moe_w3[j], moe_w2[j])


def setup_inputs(seed: int = 0) -> dict:
    key = jax.random.key(seed)
    ks = jax.random.split(key, 26)
    f32 = jnp.float32
    n_dense = (DEPTH + 1) // 2
    n_moe = DEPTH // 2

    def normal(k, shape, scale):
        return jax.random.normal(k, shape, f32) * scale

    def gain(k, shape):
        return 1.0 + 0.02 * jax.random.normal(k, shape, f32)

    heads = jnp.arange(RET_HEADS, dtype=f32)
    base_decay = jnp.log(-jnp.log(1.0 - 2.0 ** (-5.0 - heads)))
    return {
        'x': normal(ks[0], (BATCH, SEQ, D_MODEL), 1.0),
        'c': normal(ks[1], (BATCH, D_MODEL), 1.0),
        'ctx': normal(ks[2], (BATCH, CTX_LEN, D_MODEL), 1.0),
        'c_ctx': normal(ks[3], (D_MODEL,), 1.0),
        'w_ada': normal(ks[4], (DEPTH, D_MODEL, 6 * D_MODEL), 0.5 * D_MODEL ** -0.5),
        'b_ada': normal(ks[5], (DEPTH, 6 * D_MODEL), 0.02),
        'norm_mix': gain(ks[6], (DEPTH, D_MODEL)),
        'norm_ffn': gain(ks[7], (DEPTH, D_MODEL)),
        'w_in': normal(ks[8], (DEPTH, D_MODEL, IN_WIDTH), D_MODEL ** -0.5),
        'ret_decay': base_decay + normal(ks[9], (DEPTH, 2, RET_HEADS), 0.1),
        'ret_gn': gain(ks[10], (DEPTH, RET_WIDTH)),
        'attn_qn': gain(ks[11], (DEPTH, ATT_HEAD_DIM)),
        'attn_kn': gain(ks[12], (DEPTH, ATT_HEAD_DIM)),
        'pool_w': normal(ks[13], (DEPTH, len(POOL_WINDOWS), POOL_GROUP_DIM, POOL_GROUP_DIM), POOL_GROUP_DIM ** -0.5),
        'pool_scale': gain(ks[14], (DEPTH, POOL_WIDTH)),
        'w_branch': normal(ks[15], (DEPTH, N_BRANCH, BRANCH_WIDTH, D_MODEL), BRANCH_WIDTH ** -0.5),
        'w_out': normal(ks[16], (DEPTH, D_MODEL, D_MODEL), D_MODEL ** -0.5),
        'ffn_w1': normal(ks[17], (n_dense, D_MODEL, FFN_DIM), D_MODEL ** -0.5),
        'ffn_w3': normal(ks[18], (n_dense, D_MODEL, FFN_DIM), D_MODEL ** -0.5),
        'ffn_w2': normal(ks[19], (n_dense, FFN_DIM, D_MODEL), FFN_DIM ** -0.5),
        'moe_router': normal(ks[20], (n_moe, D_MODEL, N_EXPERTS), D_MODEL ** -0.5),
        'moe_router_b': normal(ks[21], (n_moe, N_EXPERTS), 0.01),
        'moe_w1': normal(ks[22], (n_moe, N_EXPERTS, D_MODEL, FFN_DIM), D_MODEL ** -0.5),
        'moe_w3': normal(ks[23], (n_moe, N_EXPERTS, D_MODEL, FFN_DIM), D_MODEL ** -0.5),
        'moe_w2': normal(ks[24], (n_moe, N_EXPERTS, FFN_DIM, D_MODEL), FFN_DIM ** -0.5),
        'final_norm': gain(ks[25], (D_MODEL,)),
    }


def reference(x, c, ctx, c_ctx, w_ada, b_ada, norm_mix, norm_ffn, w_in, ret_decay, ret_gn, attn_qn, attn_kn, pool_w, pool_scale, w_branch, w_out, ffn_w1, ffn_w3, ffn_w2, moe_router, moe_router_b, moe_w1, moe_w3, moe_w2, final_norm):
    s = x.shape[1]
    rows = s // GRID_W
    row = jnp.repeat(jnp.arange(rows, dtype=jnp.float32), GRID_W)
    col = jnp.tile(jnp.arange(GRID_W, dtype=jnp.float32), rows)
    half = ATT_HEAD_DIM // 2
    ang_att = jnp.concatenate([rope_angles(row, half), rope_angles(col, half)], axis=-1)
    ang_ret = rope_angles(jnp.arange(s, dtype=jnp.float32), RET_QK_DIM)
    s_c = jax.nn.silu(c)
    s_cc = jax.nn.silu(c_ctx)
    xc = ctx
    for i in range(DEPTH):
        need_ctx = i < DEPTH - 1
        sh1, sc1, g1, sh2, sc2, g2 = jnp.split((s_c @ w_ada[i] + b_ada[i])[:, None, :], 6, axis=-1)
        csh1, csc1, cg1, csh2, csc2, cg2 = jnp.split(s_cc @ w_ada[i] + b_ada[i], 6, axis=-1)
        h = modulate(rms_norm(x, norm_mix[i]), sh1, sc1)
        hc = modulate(rms_norm(xc, norm_mix[i]), csh1, csc1)
        y, yc = token_mixer(h, hc, w_in[i], ret_decay[i], ret_gn[i], attn_qn[i], attn_kn[i], pool_w[i], pool_scale[i], w_branch[i], w_out[i], ang_att, ang_ret, need_ctx)
        x = x + g1 * y
        x = x + g2 * channel_mixer(modulate(rms_norm(x, norm_ffn[i]), sh2, sc2), i, ffn_w1, ffn_w3, ffn_w2, moe_router, moe_router_b, moe_w1, moe_w3, moe_w2)
        if need_ctx:
            xc = xc + cg1 * yc
            xc = xc + cg2 * channel_mixer(modulate(rms_norm(xc, norm_ffn[i]), csh2, csc2), i, ffn_w1, ffn_w3, ffn_w2, moe_router, moe_router_b, moe_w1, moe_w3, moe_w2)
    return rms_norm(x, final_norm)
```

```python
import functools
import math

import numpy as np
import jax
import jax.numpy as jnp
from jax import lax
from jax.experimental import pallas as pl
from jax.experimental.pallas import tpu as pltpu

F32 = jnp.float32
BF16 = jnp.bfloat16

GRID_W = 64
RET_HEADS = 4
RET_QK_DIM = 64
RET_V_DIM = 128
ATT_HEADS = 4
ATT_KV_HEADS = 2
ATT_HEAD_DIM = 128
ROPE_THETA = 10000.0
POOL_WINDOWS = (2, 4, 8, 16)
POOL_GROUP_DIM = 128
N_EXPERTS = 8
NORM_EPS = 1e-6
GN_EPS = 1e-5

LANES = 128
SUBLANES = 8
VMEM_LIMIT_BYTES = 56 * 1024 * 1024

TM_PROJ = 1024
TN_PROJ = 2048
TM_PREP = 1024
TM_POOL = 1024
TM_MERGE = 512
TM_FFN = 1024
TF_FFN = 512
TM_MOE = 2048
RB_MOE = 256
RET_L = 256
TQ_ATT = 256
TK_ATT = 512
POOL_HALO = 8


def _cparams(sem):
    return pltpu.CompilerParams(dimension_semantics=sem, vmem_limit_bytes=VMEM_LIMIT_BYTES)


def _dot(a, b):
    return jnp.dot(a.astype(BF16), b.astype(BF16), preferred_element_type=F32)


def _dot_nt(a, b):
    return lax.dot_general(a.astype(BF16), b.astype(BF16), (((1,), (1,)), ((), ())),
                           preferred_element_type=F32)


def _dot_tn(a, b):
    return lax.dot_general(a.astype(BF16), b.astype(BF16), (((0,), (0,)), ((), ())),
                           preferred_element_type=F32)


def _split_bf16(a):
    hi = a.astype(BF16)
    lo = (a - hi.astype(F32)).astype(BF16)
    return hi, lo


def _dot3(a, b):
    ah, al = _split_bf16(a)
    bh, bl = _split_bf16(b)
    d = functools.partial(jnp.dot, preferred_element_type=F32)
    return d(ah, bh) + (d(al, bh) + d(ah, bl))


def _norm_mod(x, gain, shift, scale):
    y = x * lax.rsqrt(jnp.mean(x * x, axis=-1, keepdims=True) + NORM_EPS)
    return (y * gain) * (1.0 + scale) + shift


def _sigmoid(x):
    return 1.0 / (1.0 + jnp.exp(-x))


def _silu(x):
    return x * _sigmoid(x)


class Dims:
    def __init__(self, b, s, ctx, d):
        self.b, self.s, self.ctx, self.d = b, s, ctx, d
        self.n_lat = b * s
        self.n_all = b * s + b * ctx

    def tile(self, preferred):
        t = preferred
        while self.s % t or (self.b * self.ctx) % t:
            t //= 2
        return t

    def mod_row(self, tm):
        lat_tiles, per_batch, b = self.n_lat // tm, self.s // tm, self.b
        return lambda i: jnp.where(i < lat_tiles, i // per_batch, b)


def _ada_kernel(c_ref, w_ref, b_ref, o_ref):
    o_ref[0] = _dot3(_silu(c_ref[...]), w_ref[0]) + b_ref[0]


def ada_table(cc, w_ada, b_ada):
    depth, d, n6 = w_ada.shape
    tn = n6 // 4
    rows = cc.shape[0]
    return pl.pallas_call(
        _ada_kernel,
        grid=(depth, n6 // tn),
        in_specs=[pl.BlockSpec((rows, d), lambda l, j: (0, 0)),
                  pl.BlockSpec((1, d, tn), lambda l, j: (l, 0, j)),
                  pl.BlockSpec((1, 1, tn), lambda l, j: (l, 0, j))],
        out_specs=pl.BlockSpec((1, rows, tn), lambda l, j: (l, 0, j)),
        out_shape=jax.ShapeDtypeStruct((depth, rows, n6), F32),
        compiler_params=_cparams(("arbitrary", "arbitrary")),
        name="ada_table",
    )(cc, w_ada, b_ada.reshape(depth, 1, n6))


def _proj_in_kernel(x_ref, mod_ref, gain_ref, w_ref, o_ref, h_ref):
    @pl.when(pl.program_id(1) == 0)
    def _():
        m = mod_ref[0]
        h_ref[...] = _norm_mod(x_ref[...], gain_ref[...], m[0:1], m[1:2]).astype(BF16)

    o_ref[...] = jnp.dot(h_ref[...], w_ref[...], preferred_element_type=F32)


def proj_in(dm, x, mod, gain, w):
    n, d = x.shape
    nout = w.shape[1]
    tm, tn = dm.tile(TM_PROJ), TN_PROJ
    row = dm.mod_row(tm)
    return pl.pallas_call(
        _proj_in_kernel,
        grid=(n // tm, nout // tn),
        in_specs=[pl.BlockSpec((tm, d), lambda i, j: (i, 0)),
                  pl.BlockSpec((1, 6, d), lambda i, j: (row(i), 0, 0)),
                  pl.BlockSpec((1, d), lambda i, j: (0, 0)),
                  pl.BlockSpec((d, tn), lambda i, j: (0, j))],
        out_specs=pl.BlockSpec((tm, tn), lambda i, j: (i, j)),
        out_shape=jax.ShapeDtypeStruct((n, nout), F32),
        scratch_shapes=[pltpu.VMEM((tm, d), BF16)],
        compiler_params=_cparams(("parallel", "arbitrary")),
        name="proj_in",
    )(x, mod, gain.reshape(1, d), w)


def _swap_halves_64(x):
    lane = lax.broadcasted_iota(jnp.int32, x.shape, 1)
    first = (lane % RET_QK_DIM) < (RET_QK_DIM // 2)
    return jnp.where(first, pltpu.roll(x, LANES - RET_QK_DIM // 2, 1), pltpu.roll(x, RET_QK_DIM // 2, 1))


def _ret_rope(x, cos, sin):
    halves = []
    for j in range(x.shape[1] // LANES):
        xh = x[:, j * LANES:(j + 1) * LANES]
        halves.append(xh * cos + _swap_halves_64(xh) * sin)
    return jnp.concatenate(halves, axis=1)


def _ret_kernel(dec_ref, q_ref, k_ref, v_ref, g_ref, cos_ref, sin_ref, gn_ref, o_ref,
                sf_ref, sb_ref, sball_ref, *, ncc, nlc):
    phase = pl.program_id(1)
    s = pl.program_id(2)
    L = RET_L
    pos = lax.broadcasted_iota(jnp.int32, (L, 1), 0).astype(F32)
    cos = cos_ref[...]
    sin = sin_ref[...]
    k = _ret_rope(k_ref[...], cos, sin) * (RET_QK_DIM ** -0.5)
    v = v_ref[...]

    def log_g(direction, h):
        return -jnp.exp(jnp.full((1, 1), dec_ref[direction, h], F32))

    @pl.when(phase == 0)
    def _():
        @pl.when(s == 0)
        def _():
            sb_ref[...] = jnp.zeros_like(sb_ref)

        for h in range(RET_HEADS):
            lgb = log_g(1, h)
            kh = k[:, h * RET_QK_DIM:(h + 1) * RET_QK_DIM]
            vh = v[:, h * RET_V_DIM:(h + 1) * RET_V_DIM]
            st = sb_ref[h]
            sball_ref[s, h] = st
            sb_ref[h] = jnp.exp(L * lgb) * st + _dot_tn(kh * jnp.exp(pos * lgb), vh)

    @pl.when(phase == 1)
    def _():
        @pl.when(s == 0)
        def _():
            sf_ref[...] = jnp.zeros_like(sf_ref)

        slot = jnp.where(s < ncc, ncc - 1 - s, ncc + nlc - 1 - (s - ncc))
        q = _ret_rope(q_ref[...], cos, sin)
        g = g_ref[...]
        li = lax.broadcasted_iota(jnp.int32, (L, L), 0)
        mi = lax.broadcasted_iota(jnp.int32, (L, L), 1)
        diff = (li - mi).astype(F32)
        for h in range(RET_HEADS):
            lgf = log_g(0, h)
            lgb = log_g(1, h)
            decay = (jnp.where(diff >= 0, jnp.exp(jnp.maximum(diff, 0.0) * lgf), 0.0)
                     + jnp.where(diff <= 0, jnp.exp(jnp.maximum(-diff, 0.0) * lgb), 0.0))
            qh = q[:, h * RET_QK_DIM:(h + 1) * RET_QK_DIM]
            kh = k[:, h * RET_QK_DIM:(h + 1) * RET_QK_DIM]
            vh = v[:, h * RET_V_DIM:(h + 1) * RET_V_DIM]
            st_f = sf_ref[h]
            st_b = sball_ref[slot, h]
            o = _dot(_dot_nt(qh, kh) * decay, vh)
            o = o + _dot(qh, st_f) * jnp.exp((pos + 1.0) * lgf)
            o = o + _dot(qh, st_b) * jnp.exp((L - pos) * lgb)
            sf_ref[h] = jnp.exp(L * lgf) * st_f + _dot_tn(kh * jnp.exp((L - 1.0 - pos) * lgf), vh)
            mu = jnp.mean(o, axis=-1, keepdims=True)
            var = jnp.mean(jnp.square(o - mu), axis=-1, keepdims=True)
            o = (o - mu) * lax.rsqrt(var + GN_EPS) * gn_ref[:, h * RET_V_DIM:(h + 1) * RET_V_DIM]
            gh = g[:, h * RET_V_DIM:(h + 1) * RET_V_DIM]
            o_ref[:, h * RET_V_DIM:(h + 1) * RET_V_DIM] = (_silu(gh) * o).astype(o_ref.dtype)


def retention(dm, p, dec, cos_t, sin_t, gn):
    L = RET_L
    ncc, nlc = dm.ctx // L, dm.s // L
    nc = ncc + nlc
    lat_blocks = dm.n_lat // L
    qk_w = RET_HEADS * RET_QK_DIM
    v_w = RET_HEADS * RET_V_DIM

    def order(ph, s):
        is_ctx = s < ncc
        c_ctx = jnp.where(ph == 0, ncc - 1 - s, s)
        c_lat = jnp.where(ph == 0, nlc - 1 - (s - ncc), s - ncc)
        return is_ctx, c_ctx, c_lat

    def rows(b, ph, s):
        is_ctx, c_ctx, c_lat = order(ph, s)
        return jnp.where(is_ctx, lat_blocks + b * ncc + c_ctx, b * nlc + c_lat)

    def table_rows(b, ph, s):
        is_ctx, c_ctx, c_lat = order(ph, s)
        return jnp.where(is_ctx, nlc + c_ctx, c_lat)

    def out_rows(b, ph, s):
        return jnp.where(ph == 0, rows(b, 1, 0), rows(b, 1, s))

    grid_spec = pltpu.PrefetchScalarGridSpec(
        num_scalar_prefetch=1,
        grid=(dm.b, 2, nc),
        in_specs=[pl.BlockSpec((L, qk_w), lambda b, ph, s, d: (rows(b, ph, s), 0)),
                  pl.BlockSpec((L, qk_w), lambda b, ph, s, d: (rows(b, ph, s), 1)),
                  pl.BlockSpec((L, v_w), lambda b, ph, s, d: (rows(b, ph, s), 1)),
                  pl.BlockSpec((L, v_w), lambda b, ph, s, d: (rows(b, ph, s), 2)),
                  pl.BlockSpec((L, LANES), lambda b, ph, s, d: (table_rows(b, ph, s), 0)),
                  pl.BlockSpec((L, LANES), lambda b, ph, s, d: (table_rows(b, ph, s), 0)),
                  pl.BlockSpec((1, v_w), lambda b, ph, s, d: (0, 0))],
        out_specs=pl.BlockSpec((L, v_w), lambda b, ph, s, d: (out_rows(b, ph, s), 0)),
        scratch_shapes=[pltpu.VMEM((RET_HEADS, RET_QK_DIM, RET_V_DIM), F32),
                        pltpu.VMEM((RET_HEADS, RET_QK_DIM, RET_V_DIM), F32),
                        pltpu.VMEM((nc, RET_HEADS, RET_QK_DIM, RET_V_DIM), F32)],
    )
    return pl.pallas_call(
        functools.partial(_ret_kernel, ncc=ncc, nlc=nlc),
        grid_spec=grid_spec,
        out_shape=jax.ShapeDtypeStruct((dm.n_all, v_w), BF16),
        compiler_params=_cparams(("parallel", "arbitrary", "arbitrary")),
        name="retention",
    )(dec, p, p, p, p, cos_t, sin_t, gn.reshape(1, v_w))


def _att_prep_kernel(q_ref, k_ref, v_ref, cos_ref, sin_ref, gq_ref, gk_ref, qo_ref, ko_ref, vo_ref,
                     *, lat_tiles):
    is_lat = pl.program_id(0) < lat_tiles
    cos = jnp.where(is_lat, cos_ref[...], 1.0)
    sin = jnp.where(is_lat, sin_ref[...], 0.0)

    def prep(x_ref, gain_ref, out_ref):
        for h in range(x_ref.shape[1] // ATT_HEAD_DIM):
            xh = x_ref[:, h * ATT_HEAD_DIM:(h + 1) * ATT_HEAD_DIM]
            xn = xh * lax.rsqrt(jnp.mean(xh * xh, axis=-1, keepdims=True) + NORM_EPS) * gain_ref[...]
            y = xn * cos + pltpu.roll(xn, ATT_HEAD_DIM // 2, 1) * sin
            out_ref[:, h * ATT_HEAD_DIM:(h + 1) * ATT_HEAD_DIM] = y.astype(out_ref.dtype)

    prep(q_ref, gq_ref, qo_ref)
    prep(k_ref, gk_ref, ko_ref)
    vo_ref[...] = v_ref[...].astype(vo_ref.dtype)


def att_prep(dm, p, cos_t, sin_t, gq, gk):
    tm = dm.tile(TM_PREP)
    n = dm.n_all
    lat_tiles, per_batch = dm.n_lat // tm, dm.s // tm
    qw = ATT_HEADS * ATT_HEAD_DIM
    kw = ATT_KV_HEADS * ATT_HEAD_DIM
    q_blk, k_blk, v_blk = 1536 // qw, 2048 // kw, 2304 // kw
    tab = lambda i: (jnp.where(i < lat_tiles, i % per_batch, 0), 0)
    return pl.pallas_call(
        functools.partial(_att_prep_kernel, lat_tiles=lat_tiles),
        grid=(n // tm,),
        in_specs=[pl.BlockSpec((tm, qw), lambda i: (i, q_blk)),
                  pl.BlockSpec((tm, kw), lambda i: (i, k_blk)),
                  pl.BlockSpec((tm, kw), lambda i: (i, v_blk)),
                  pl.BlockSpec((tm, ATT_HEAD_DIM), tab),
                  pl.BlockSpec((tm, ATT_HEAD_DIM), tab),
                  pl.BlockSpec((1, ATT_HEAD_DIM), lambda i: (0, 0)),
                  pl.BlockSpec((1, ATT_HEAD_DIM), lambda i: (0, 0))],
        out_specs=[pl.BlockSpec((tm, qw), lambda i: (i, 0)),
                   pl.BlockSpec((tm, kw), lambda i: (i, 0)),
                   pl.BlockSpec((tm, kw), lambda i: (i, 0))],
        out_shape=[jax.ShapeDtypeStruct((n, qw), BF16),
                   jax.ShapeDtypeStruct((n, kw), BF16),
                   jax.ShapeDtypeStruct((n, kw), BF16)],
        compiler_params=_cparams(("parallel",)),
        name="att_prep",
    )(p, p, p, cos_t, sin_t, gq.reshape(1, -1), gk.reshape(1, -1))


def _attn_kernel(*refs, with_latent_keys):
    if with_latent_keys:
        q_ref, kl_ref, vl_ref, kc_ref, vc_ref, o_ref = refs
    else:
        q_ref, kc_ref, vc_ref, o_ref = refs
    hd = ATT_HEAD_DIM
    grp = ATT_HEADS // ATT_KV_HEADS
    tq = q_ref.shape[0]
    q2 = jnp.concatenate([q_ref[:, j * hd:(j + 1) * hd] for j in range(grp)], axis=0)
    c = (hd ** -0.5) * math.log2(math.e)

    def step(kk, vv, carry):
        m, l, acc = carry
        sc = _dot_nt(q2, kk)
        m_new = jnp.maximum(m, jnp.max(sc, axis=-1, keepdims=True))
        p = jnp.exp2((sc - m_new) * c)
        alpha = jnp.exp2((m - m_new) * c)
        l = l * alpha + jnp.sum(p, axis=-1, keepdims=True)
        acc = acc * alpha + jnp.dot(p.astype(BF16), vv, preferred_element_type=F32)
        return m_new, l, acc

    carry = (jnp.full((grp * tq, 1), -1e30, F32), jnp.zeros((grp * tq, 1), F32),
             jnp.zeros((grp * tq, hd), F32))
    if with_latent_keys:
        tk = min(TK_ATT, kl_ref.shape[0])

        def body(j, carry):
            off = pl.multiple_of(j * tk, tk)
            return step(kl_ref[pl.ds(off, tk), :], vl_ref[pl.ds(off, tk), :], carry)

        carry = lax.fori_loop(0, kl_ref.shape[0] // tk, body, carry)
    m, l, acc = step(kc_ref[...], vc_ref[...], carry)
    out = acc / l
    for j in range(grp):
        o_ref[:, j * hd:(j + 1) * hd] = out[j * tq:(j + 1) * tq].astype(o_ref.dtype)


def attention(dm, qn, kn, vb, latent):
    hd = ATT_HEAD_DIM
    grp = ATT_HEADS // ATT_KV_HEADS
    ctx_blk0 = dm.n_lat // dm.ctx
    kc_spec = pl.BlockSpec((dm.ctx, hd), lambda b, h, i: (ctx_blk0 + b, h))
    if latent:
        tq = TQ_ATT
        nq = dm.s // tq
        q_map = lambda b, h, i: (b * nq + i, h)
        kl_spec = pl.BlockSpec((dm.s, hd), lambda b, h, i: (b, h))
        in_specs = [pl.BlockSpec((tq, grp * hd), q_map), kl_spec, kl_spec, kc_spec, kc_spec]
        args = (qn, kn, vb, kn, vb)
        rows = dm.n_lat
    else:
        tq, nq = dm.ctx, 1
        q_map = lambda b, h, i: (ctx_blk0 + b, h)
        in_specs = [pl.BlockSpec((tq, grp * hd), q_map), kc_spec, kc_spec]
        args = (qn, kn, vb)
        rows = dm.b * dm.ctx
    out_map = (lambda b, h, i: (b * nq + i, h))
    return pl.pallas_call(
        functools.partial(_attn_kernel, with_latent_keys=latent),
        grid=(dm.b, ATT_KV_HEADS, nq),
        in_specs=in_specs,
        out_specs=pl.BlockSpec((tq, grp * hd), out_map),
        out_shape=jax.ShapeDtypeStruct((rows, ATT_HEADS * hd), BF16),
        compiler_params=_cparams(("parallel", "parallel", "arbitrary")),
        name="attention_latent" if latent else "attention_ctx",
    )(*args)


def _pool_kernel(prev_ref, u_ref, next_ref, w_ref, sc_ref, o_ref, a_ref, b_ref, *, dm):
    tm = u_ref.shape[0]
    H = POOL_HALO
    i = pl.program_id(0)
    lat_tiles = dm.n_lat // tm
    is_lat = i < lat_tiles
    n_seg = jnp.where(is_lat, dm.s, dm.ctx)
    base = jnp.where(is_lat, (i * tm) % dm.s, 0)
    r = lax.broadcasted_iota(jnp.int32, (tm + 2 * H, 1), 0) - H
    pos = jnp.where(is_lat, base + r, r % dm.ctx)
    win = pl.ds(H, tm + 2 * H)

    def shifted(ref, d):
        return ref[pl.ds(H + d, tm + 2 * H), :]

    zeros = jnp.zeros((H, POOL_GROUP_DIM), F32)
    for ref in (a_ref, b_ref):
        ref[pl.ds(0, H), :] = zeros
        ref[pl.ds(tm + 3 * H, H), :] = zeros

    for gi, w in enumerate(POOL_WINDOWS):
        cols = slice(gi * POOL_GROUP_DIM, (gi + 1) * POOL_GROUP_DIM)
        half = w // 2
        a_ref[pl.ds(H, H), :] = prev_ref[:, cols]
        a_ref[pl.ds(2 * H, tm), :] = u_ref[:, cols]
        a_ref[pl.ds(2 * H + tm, H), :] = next_ref[:, cols]
        b_ref[win, :] = jnp.where(pos >= 1, shifted(a_ref, -1), 0.0)
        step = 1
        while step < half:
            b_ref[win, :] = b_ref[win, :] + jnp.where(pos >= step, shifted(b_ref, -step), 0.0)
            step *= 2
        step = 1
        while step < half:
            a_ref[win, :] = a_ref[win, :] + jnp.where(pos + step < n_seg, shifted(a_ref, step), 0.0)
            step *= 2
        tile = pl.ds(2 * H, tm)
        pt = pos[H:H + tm]
        cnt = (jnp.minimum(pt + half, n_seg) - jnp.maximum(pt - half, 0)).astype(F32)
        u = u_ref[:, cols]
        diff = (a_ref[tile, :] + b_ref[tile, :]) / cnt - u
        o_ref[:, cols] = (_dot(diff, w_ref[gi]) * sc_ref[:, cols]).astype(o_ref.dtype)


def pool_mixer(dm, p, pool_w, pool_scale):
    tm = dm.tile(TM_POOL)
    n = dm.n_all
    H = POOL_HALO
    width = len(POOL_WINDOWS) * POOL_GROUP_DIM
    col = 2560 // width
    per = tm // H
    last = n // H - 1
    return pl.pallas_call(
        functools.partial(_pool_kernel, dm=dm),
        grid=(n // tm,),
        in_specs=[pl.BlockSpec((H, width), lambda i: (jnp.maximum(i * per - 1, 0), col)),
                  pl.BlockSpec((tm, width), lambda i: (i, col)),
                  pl.BlockSpec((H, width), lambda i: (jnp.minimum((i + 1) * per, last), col)),
                  pl.BlockSpec(pool_w.shape, lambda i: (0, 0, 0)),
                  pl.BlockSpec((1, width), lambda i: (0, 0))],
        out_specs=pl.BlockSpec((tm, width), lambda i: (i, 0)),
        out_shape=jax.ShapeDtypeStruct((n, width), BF16),
        scratch_shapes=[pltpu.VMEM((tm + 4 * H, POOL_GROUP_DIM), F32),
                        pltpu.VMEM((tm + 4 * H, POOL_GROUP_DIM), F32)],
        compiler_params=_cparams(("parallel",)),
        name="pool_mixer",
    )(p, p, p, pool_w, pool_scale.reshape(1, width))


def _merge_kernel(yr_ref, ya_ref, yp_ref, gl_ref, x_ref, mod_ref, wb_ref, wo_ref, o_ref):
    d = x_ref.shape[1]
    mixed = None
    for j, y_ref in enumerate((yr_ref, ya_ref, yp_ref)):
        gate = _sigmoid(gl_ref[:, j * d:(j + 1) * d])
        term = gate * jnp.dot(y_ref[...], wb_ref[j], preferred_element_type=F32)
        mixed = term if mixed is None else mixed + term
    y = _dot(mixed, wo_ref[...])
    o_ref[...] = x_ref[...] + mod_ref[0][2:3] * y


def merge(dm, n_rows, yr, ya, yp, p, x, mod, wb, wo):
    tm = dm.tile(TM_MERGE)
    d = dm.d
    bw = yr.shape[1]
    row = dm.mod_row(tm)
    return pl.pallas_call(
        _merge_kernel,
        grid=(n_rows // tm,),
        in_specs=[pl.BlockSpec((tm, bw), lambda i: (i, 0)),
                  pl.BlockSpec((tm, bw), lambda i: (i, 0)),
                  pl.BlockSpec((tm, bw), lambda i: (i, 0)),
                  pl.BlockSpec((tm, 3 * d), lambda i: (i, 1)),
                  pl.BlockSpec((tm, d), lambda i: (i, 0)),
                  pl.BlockSpec((1, 6, d), lambda i: (row(i), 0, 0)),
                  pl.BlockSpec(wb.shape, lambda i: (0, 0, 0)),
                  pl.BlockSpec(wo.shape, lambda i: (0, 0))],
        out_specs=pl.BlockSpec((tm, d), lambda i: (i, 0)),
        out_shape=jax.ShapeDtypeStruct((n_rows, d), F32),
        compiler_params=_cparams(("parallel",)),
        name="merge",
    )(yr, ya, yp, p, x, mod, wb, wo)


def _ffn_kernel(x_ref, mod_ref, gain_ref, w1_ref, w3_ref, w2_ref, o_ref, h_ref, acc_ref):
    f = pl.program_id(1)
    m = mod_ref[0]

    @pl.when(f == 0)
    def _():
        h_ref[...] = _norm_mod(x_ref[...], gain_ref[...], m[3:4], m[4:5]).astype(BF16)
        acc_ref[...] = jnp.zeros_like(acc_ref)

    h = h_ref[...]
    a = jnp.dot(h, w1_ref[...], preferred_element_type=F32)
    b = jnp.dot(h, w3_ref[...], preferred_element_type=F32)
    acc_ref[...] += _dot(_silu(a) * b, w2_ref[...])

    @pl.when(f == pl.num_programs(1) - 1)
    def _():
        o_ref[...] = x_ref[...] + m[5:6] * acc_ref[...]


def ffn_dense(dm, x, mod, gain, w1, w3, w2):
    n, d = x.shape
    fdim = w1.shape[1]
    tm, tf = dm.tile(TM_FFN), TF_FFN
    row = dm.mod_row(tm)
    return pl.pallas_call(
        _ffn_kernel,
        grid=(n // tm, fdim // tf),
        in_specs=[pl.BlockSpec((tm, d), lambda i, f: (i, 0)),
                  pl.BlockSpec((1, 6, d), lambda i, f: (row(i), 0, 0)),
                  pl.BlockSpec((1, d), lambda i, f: (0, 0)),
                  pl.BlockSpec((d, tf), lambda i, f: (0, f)),
                  pl.BlockSpec((d, tf), lambda i, f: (0, f)),
                  pl.BlockSpec((tf, d), lambda i, f: (f, 0))],
        out_specs=pl.BlockSpec((tm, d), lambda i, f: (i, 0)),
        out_shape=jax.ShapeDtypeStruct((n, d), F32),
        scratch_shapes=[pltpu.VMEM((tm, d), BF16), pltpu.VMEM((tm, d), F32)],
        compiler_params=_cparams(("parallel", "arbitrary")),
        name="ffn_dense",
    )(x, mod, gain.reshape(1, d), w1, w3, w2)


def _router_kernel(x_ref, mod_ref, gain_ref, wr_ref, br_ref, h_ref, slot_ref, slot_t_ref, wt_ref, cnt_ref):
    tm = x_ref.shape[0]
    m = mod_ref[0]
    h = _norm_mod(x_ref[...], gain_ref[...], m[3:4], m[4:5])
    h_ref[...] = h.astype(BF16)
    logits = _dot3(h, wr_ref[...]) + br_ref[...]
    lane = lax.broadcasted_iota(jnp.int32, logits.shape, 1)
    m1 = jnp.max(logits, axis=-1, keepdims=True)
    i1 = jnp.min(jnp.where(logits == m1, lane, LANES), axis=-1, keepdims=True)
    rest = jnp.where(lane == i1, -jnp.inf, logits)
    m2 = jnp.max(rest, axis=-1, keepdims=True)
    i2 = jnp.min(jnp.where(rest == m2, lane, LANES), axis=-1, keepdims=True)
    e = jnp.exp(m2 - m1)
    w1 = 1.0 / (1.0 + e)
    w2 = e / (1.0 + e)
    wt_ref[...] = jnp.where(lane == i1, w1, jnp.where(lane == i2, w2, 0.0))
    routed = jnp.where((lane == i1) | (lane == i2), 1.0, 0.0)
    rb = RB_MOE
    lower = (lax.broadcasted_iota(jnp.int32, (rb, rb), 0) > lax.broadcasted_iota(jnp.int32, (rb, rb), 1))
    lower = lower.astype(BF16)
    offset = jnp.zeros((1, LANES), F32)
    for blk in range(tm // rb):
        rows = slice(blk * rb, (blk + 1) * rb)
        mask = routed[rows]
        rank = jnp.dot(lower, mask.astype(BF16), preferred_element_type=F32) + offset
        slot_ref[rows, :] = jnp.where(mask > 0.0, rank, -1.0)
        offset = offset + jnp.sum(mask, axis=0, keepdims=True)
    cnt_ref[0] = jnp.broadcast_to(offset, (SUBLANES, LANES)).astype(jnp.int32)
    slot_t_ref[...] = slot_ref[...].T[:SUBLANES, :]


def moe_route(dm, n_rows, x, mod, gain, wr_pad, br_pad):
    tm = dm.tile(TM_MOE)
    d = dm.d
    row = dm.mod_row(tm)
    nt = n_rows // tm
    return pl.pallas_call(
        _router_kernel,
        grid=(nt,),
        in_specs=[pl.BlockSpec((tm, d), lambda i: (i, 0)),
                  pl.BlockSpec((1, 6, d), lambda i: (row(i), 0, 0)),
                  pl.BlockSpec((1, d), lambda i: (0, 0)),
                  pl.BlockSpec((d, LANES), lambda i: (0, 0)),
                  pl.BlockSpec((1, LANES), lambda i: (0, 0))],
        out_specs=[pl.BlockSpec((tm, d), lambda i: (i, 0)),
                   pl.BlockSpec((tm, LANES), lambda i: (i, 0)),
                   pl.BlockSpec((SUBLANES, tm), lambda i: (0, i)),
                   pl.BlockSpec((tm, LANES), lambda i: (i, 0)),
                   pl.BlockSpec((1, SUBLANES, LANES), lambda i: (i, 0, 0))],
        out_shape=[jax.ShapeDtypeStruct((n_rows, d), BF16),
                   jax.ShapeDtypeStruct((n_rows, LANES), F32),
                   jax.ShapeDtypeStruct((SUBLANES, n_rows), F32),
                   jax.ShapeDtypeStruct((n_rows, LANES), F32),
                   jax.ShapeDtypeStruct((nt, SUBLANES, LANES), jnp.int32)],
        compiler_params=_cparams(("parallel",)),
        name="moe_router",
    )(x, mod, gain.reshape(1, d), wr_pad, br_pad)


def _moe_kernel(cnt_ref, h_ref, slot_ref, slot_t_ref, wt_ref, w1_ref, w3_ref, w2_ref, o_ref,
                xs_ref, acc_ref):
    i, e, f = pl.program_id(0), pl.program_id(1), pl.program_id(2)
    tm = h_ref.shape[0]
    rb = RB_MOE
    nblk = (cnt_ref[i, e] + rb - 1) // rb

    @pl.when((e == 0) & (f == 0))
    def _():
        o_ref[...] = jnp.zeros_like(o_ref)

    @pl.when(f == 0)
    def _():
        slot_row = slot_t_ref[pl.ds(e, 1), :]

        def gather(r, carry):
            want = lax.broadcasted_iota(jnp.int32, (rb, tm), 0).astype(F32) + (r * rb).astype(F32)
            onehot = (slot_row == want).astype(BF16)
            xs_ref[r] = jnp.dot(onehot, h_ref[...], preferred_element_type=F32).astype(BF16)
            acc_ref[r] = jnp.zeros((rb, acc_ref.shape[2]), F32)
            return carry

        lax.fori_loop(0, nblk, gather, 0)

    def expert(r, carry):
        xs = xs_ref[r]
        a = jnp.dot(xs, w1_ref[0], preferred_element_type=F32)
        b = jnp.dot(xs, w3_ref[0], preferred_element_type=F32)
        acc_ref[r] += _dot(_silu(a) * b, w2_ref[0])
        return carry

    lax.fori_loop(0, nblk, expert, 0)

    @pl.when(f == pl.num_programs(2) - 1)
    def _():
        lane = lax.broadcasted_iota(jnp.int32, (tm, LANES), 1)
        pick = lane == e
        slot_col = jnp.sum(jnp.where(pick, slot_ref[...], 0.0), axis=-1, keepdims=True)
        wt_col = jnp.sum(jnp.where(pick, wt_ref[...], 0.0), axis=-1, keepdims=True)

        def scatter(r, carry):
            want = lax.broadcasted_iota(jnp.int32, (tm, rb), 1).astype(F32) + (r * rb).astype(F32)
            onehot = (slot_col == want).astype(BF16)
            o_ref[...] += wt_col * _dot(onehot, acc_ref[r])
            return carry

        lax.fori_loop(0, nblk, scatter, 0)


def moe_experts(dm, n_rows, h, slot, slot_t, wt, counts, w1, w3, w2):
    tm, tf = dm.tile(TM_MOE), TF_FFN
    d = dm.d
    ne, _, fdim = w1.shape
    grid_spec = pltpu.PrefetchScalarGridSpec(
        num_scalar_prefetch=1,
        grid=(n_rows // tm, ne, fdim // tf),
        in_specs=[pl.BlockSpec((tm, d), lambda i, e, f, c: (i, 0)),
                  pl.BlockSpec((tm, LANES), lambda i, e, f, c: (i, 0)),
                  pl.BlockSpec((SUBLANES, tm), lambda i, e, f, c: (0, i)),
                  pl.BlockSpec((tm, LANES), lambda i, e, f, c: (i, 0)),
                  pl.BlockSpec((1, d, tf), lambda i, e, f, c: (e, 0, f)),
                  pl.BlockSpec((1, d, tf), lambda i, e, f, c: (e, 0, f)),
                  pl.BlockSpec((1, tf, d), lambda i, e, f, c: (e, f, 0))],
        out_specs=pl.BlockSpec((tm, d), lambda i, e, f, c: (i, 0)),
        scratch_shapes=[pltpu.VMEM((tm // RB_MOE, RB_MOE, d), BF16),
                        pltpu.VMEM((tm // RB_MOE, RB_MOE, d), F32)],
    )
    return pl.pallas_call(
        _moe_kernel,
        grid_spec=grid_spec,
        out_shape=jax.ShapeDtypeStruct((n_rows, d), F32),
        compiler_params=_cparams(("parallel", "arbitrary", "arbitrary")),
        name="moe_experts",
    )(counts, h, slot, slot_t, wt, w1, w3, w2)


def _residual_kernel(x_ref, y_ref, mod_ref, o_ref):
    o_ref[...] = x_ref[...] + mod_ref[0][5:6] * y_ref[...]


def _final_kernel(x_ref, y_ref, mod_ref, gain_ref, o_ref):
    x = x_ref[...] + mod_ref[0][5:6] * y_ref[...]
    o_ref[...] = x * lax.rsqrt(jnp.mean(x * x, axis=-1, keepdims=True) + NORM_EPS) * gain_ref[...]


def residual(dm, n_rows, x, y, mod, final_gain=None):
    tm = dm.tile(TM_FFN)
    d = dm.d
    row = dm.mod_row(tm)
    in_specs = [pl.BlockSpec((tm, d), lambda i: (i, 0)),
                pl.BlockSpec((tm, d), lambda i: (i, 0)),
                pl.BlockSpec((1, 6, d), lambda i: (row(i), 0, 0))]
    args = [x, y, mod]
    if final_gain is not None:
        in_specs.append(pl.BlockSpec((1, d), lambda i: (0, 0)))
        args.append(final_gain.reshape(1, d))
    return pl.pallas_call(
        _residual_kernel if final_gain is None else _final_kernel,
        grid=(n_rows // tm,),
        in_specs=in_specs,
        out_specs=pl.BlockSpec((tm, d), lambda i: (i, 0)),
        out_shape=jax.ShapeDtypeStruct((n_rows, d), F32),
        compiler_params=_cparams(("parallel",)),
        name="residual" if final_gain is None else "residual_final_norm",
    )(*args)


def _final_norm_kernel(x_ref, gain_ref, o_ref):
    x = x_ref[...]
    o_ref[...] = x * lax.rsqrt(jnp.mean(x * x, axis=-1, keepdims=True) + NORM_EPS) * gain_ref[...]


def final_rmsnorm(dm, n_rows, x, gain):
    tm = dm.tile(TM_FFN)
    d = dm.d
    return pl.pallas_call(
        _final_norm_kernel,
        grid=(n_rows // tm,),
        in_specs=[pl.BlockSpec((tm, d), lambda i: (i, 0)), pl.BlockSpec((1, d), lambda i: (0, 0))],
        out_specs=pl.BlockSpec((tm, d), lambda i: (i, 0)),
        out_shape=jax.ShapeDtypeStruct((n_rows, d), F32),
        compiler_params=_cparams(("parallel",)),
        name="final_norm",
    )(x, gain.reshape(1, d))


def _deinterleave(width, head_dim):
    idx = np.arange(width).reshape(width // head_dim, head_dim // 2, 2)
    return np.concatenate([idx[..., 0], idx[..., 1]], axis=-1).reshape(-1)


def _w_in_perm(in_width):
    perm = np.arange(in_width)
    qk = RET_HEADS * RET_QK_DIM
    perm[0:qk] = _deinterleave(qk, RET_QK_DIM)
    perm[qk:2 * qk] = qk + _deinterleave(qk, RET_QK_DIM)
    q0 = 2 * qk + 2 * RET_HEADS * RET_V_DIM
    qw = ATT_HEADS * ATT_HEAD_DIM
    kw = ATT_KV_HEADS * ATT_HEAD_DIM
    perm[q0:q0 + qw] = q0 + _deinterleave(qw, ATT_HEAD_DIM)
    perm[q0 + qw:q0 + qw + kw] = q0 + qw + _deinterleave(kw, ATT_HEAD_DIM)
    return perm


def _rope_tables(s, ctx):
    def angles(pos, dim):
        freqs = 1.0 / (ROPE_THETA ** (jnp.arange(0, dim, 2, dtype=F32) / dim))
        return pos[:, None] * freqs[None, :]

    rows = s // GRID_W
    row = jnp.repeat(jnp.arange(rows, dtype=F32), GRID_W)
    col = jnp.tile(jnp.arange(GRID_W, dtype=F32), rows)
    half = ATT_HEAD_DIM // 2
    ang_att = jnp.concatenate([angles(row, half), angles(col, half)], axis=-1)
    att_cos = jnp.concatenate([jnp.cos(ang_att)] * 2, axis=-1)
    att_sin = jnp.concatenate([-jnp.sin(ang_att), jnp.sin(ang_att)], axis=-1)
    ang_ret = angles(jnp.arange(s, dtype=F32), RET_QK_DIM)
    ang_ret = jnp.concatenate([ang_ret, jnp.zeros((ctx, RET_QK_DIM // 2), F32)], axis=0)
    reps = LANES // RET_QK_DIM
    ret_cos = jnp.concatenate([jnp.cos(ang_ret)] * (2 * reps), axis=-1)
    ret_sin = jnp.concatenate([-jnp.sin(ang_ret), jnp.sin(ang_ret)] * reps, axis=-1)
    return att_cos, att_sin, ret_cos, ret_sin


def kernel(x, c, ctx, c_ctx, w_ada, b_ada, norm_mix, norm_ffn, w_in, ret_decay, ret_gn, attn_qn, attn_kn, pool_w, pool_scale, w_branch, w_out, ffn_w1, ffn_w3, ffn_w2, moe_router, moe_router_b, moe_w1, moe_w3, moe_w2, final_norm):
    b, s, d = x.shape
    n_ctx = ctx.shape[1]
    depth = w_in.shape[0]
    dm = Dims(b, s, n_ctx, d)

    xa = jnp.concatenate([x.reshape(b * s, d), ctx.reshape(b * n_ctx, d)], axis=0)
    mod_rows = 2 * SUBLANES
    cc = jnp.concatenate([c, c_ctx[None, :], jnp.zeros((mod_rows - b - 1, d), F32)], axis=0)
    mods = ada_table(cc, w_ada, b_ada).reshape(depth, mod_rows, 6, d)

    att_cos, att_sin, ret_cos, ret_sin = _rope_tables(s, n_ctx)
    perm = _w_in_perm(w_in.shape[2])
    qn_perm = _deinterleave(ATT_HEAD_DIM, ATT_HEAD_DIM)

    for i in range(depth):
        need_ctx = i < depth - 1
        n_rows = dm.n_all if need_ctx else dm.n_lat
        mod = mods[i]
        p = proj_in(dm, xa, mod, norm_mix[i], w_in[i][:, perm].astype(BF16))
        y_ret = retention(dm, p, ret_decay[i], ret_cos, ret_sin, ret_gn[i])
        y_pool = pool_mixer(dm, p, pool_w[i].astype(BF16), pool_scale[i])
        qn, kn, vb = att_prep(dm, p, att_cos, att_sin, attn_qn[i][qn_perm], attn_kn[i][qn_perm])
        y_att = attention(dm, qn, kn, vb, latent=True)
        if need_ctx:
            y_att = jnp.concatenate([y_att, attention(dm, qn, kn, vb, latent=False)], axis=0)
        xa = merge(dm, n_rows, y_ret, y_att, y_pool, p, xa, mod,
                   w_branch[i].astype(BF16), w_out[i].astype(BF16))
        j = i // 2
        last = i == depth - 1
        if i % 2 == 0:
            xa = ffn_dense(dm, xa, mod, norm_ffn[i], ffn_w1[j].astype(BF16), ffn_w3[j].astype(BF16),
                           ffn_w2[j].astype(BF16))
            if last:
                xa = final_rmsnorm(dm, n_rows, xa, final_norm)
        else:
            wr = jnp.zeros((d, LANES), F32).at[:, :N_EXPERTS].set(moe_router[j])
            br = jnp.full((1, LANES), -1e30, F32).at[0, :N_EXPERTS].set(moe_router_b[j])
            h, slot, slot_t, wt, cnt = moe_route(dm, n_rows, xa, mod, norm_ffn[i], wr, br)
            y = moe_experts(dm, n_rows, h, slot, slot_t, wt, cnt[:, 0, :N_EXPERTS],
                            moe_w1[j].astype(BF16), moe_w3[j].astype(BF16), moe_w2[j].astype(BF16))
            xa = residual(dm, n_rows, xa, y, mod, final_norm if last else None)
    return xa[:dm.n_lat].reshape(b, s, d)
```

```python
import functools
import math

import numpy as np
import jax
import jax.numpy as jnp
from jax import lax
from jax.experimental import pallas as pl
from jax.experimental.pallas import tpu as pltpu

F32 = jnp.float32
BF16 = jnp.bfloat16

GRID_W = 64
RET_HEADS = 4
RET_QK_DIM = 64
RET_V_DIM = 128
ATT_HEADS = 4
ATT_KV_HEADS = 2
ATT_HEAD_DIM = 128
ROPE_THETA = 10000.0
POOL_WINDOWS = (2, 4, 8, 16)
POOL_GROUP_DIM = 128
N_EXPERTS = 8
NORM_EPS = 1e-6
GN_EPS = 1e-5

LANES = 128
SUBLANES = 8
VMEM_LIMIT_BYTES = 56 * 1024 * 1024

TM_PROJ = 1024
TN_PROJ = 2048
TM_PREP = 1024
TM_POOL = 1024
TM_MERGE = 512
TM_FFN = 1024
TF_FFN = 512
TM_MOE = 2048
RB_MOE = 256
RET_L = 256
TQ_ATT = 256
ATT_SCORE_SCALE = (ATT_HEAD_DIM ** -0.5) * math.log2(math.e)
ATT_V_ROWS = ATT_HEAD_DIM + 16
ATT_STAGES = 3
ATT_UNROLL = 2
ATT_REDUCE_GROUPS = 8
POOL_HALO = 8


def _cparams(sem):
    return pltpu.CompilerParams(dimension_semantics=sem, vmem_limit_bytes=VMEM_LIMIT_BYTES)


def _dot(a, b):
    return jnp.dot(a.astype(BF16), b.astype(BF16), preferred_element_type=F32)


def _dot_nt(a, b):
    return lax.dot_general(a.astype(BF16), b.astype(BF16), (((1,), (1,)), ((), ())),
                           preferred_element_type=F32)


def _dot_tn(a, b):
    return lax.dot_general(a.astype(BF16), b.astype(BF16), (((0,), (0,)), ((), ())),
                           preferred_element_type=F32)


def _split_bf16(a):
    hi = a.astype(BF16)
    lo = (a - hi.astype(F32)).astype(BF16)
    return hi, lo


def _dot3(a, b):
    ah, al = _split_bf16(a)
    bh, bl = _split_bf16(b)
    d = functools.partial(jnp.dot, preferred_element_type=F32)
    return d(ah, bh) + (d(al, bh) + d(ah, bl))


def _norm_mod(x, gain, shift, scale):
    y = x * lax.rsqrt(jnp.mean(x * x, axis=-1, keepdims=True) + NORM_EPS)
    return (y * gain) * (1.0 + scale) + shift


def _sigmoid(x):
    return 1.0 / (1.0 + jnp.exp(-x))


def _silu(x):
    return x * _sigmoid(x)


class Dims:
    def __init__(self, b, s, ctx, d):
        self.b, self.s, self.ctx, self.d = b, s, ctx, d
        self.n_lat = b * s
        self.n_all = b * s + b * ctx

    def tile(self, preferred):
        t = preferred
        while self.s % t or (self.b * self.ctx) % t:
            t //= 2
        return t

    def mod_row(self, tm):
        lat_tiles, per_batch, b = self.n_lat // tm, self.s // tm, self.b
        return lambda i: jnp.where(i < lat_tiles, i // per_batch, b)


def _ada_kernel(c_ref, w_ref, b_ref, o_ref):
    o_ref[0] = _dot3(_silu(c_ref[...]), w_ref[0]) + b_ref[0]


def ada_table(cc, w_ada, b_ada):
    depth, d, n6 = w_ada.shape
    tn = n6 // 4
    rows = cc.shape[0]
    return pl.pallas_call(
        _ada_kernel,
        grid=(depth, n6 // tn),
        in_specs=[pl.BlockSpec((rows, d), lambda l, j: (0, 0)),
                  pl.BlockSpec((1, d, tn), lambda l, j: (l, 0, j)),
                  pl.BlockSpec((1, 1, tn), lambda l, j: (l, 0, j))],
        out_specs=pl.BlockSpec((1, rows, tn), lambda l, j: (l, 0, j)),
        out_shape=jax.ShapeDtypeStruct((depth, rows, n6), F32),
        compiler_params=_cparams(("arbitrary", "arbitrary")),
        name="ada_table",
    )(cc, w_ada, b_ada.reshape(depth, 1, n6))


def _proj_in_kernel(x_ref, mod_ref, gain_ref, w_ref, o_ref, h_ref):
    @pl.when(pl.program_id(1) == 0)
    def _():
        m = mod_ref[0]
        h_ref[...] = _norm_mod(x_ref[...], gain_ref[...], m[0:1], m[1:2]).astype(BF16)

    o_ref[...] = jnp.dot(h_ref[...], w_ref[...], preferred_element_type=F32)


def proj_in(dm, x, mod, gain, w):
    n, d = x.shape
    nout = w.shape[1]
    tm, tn = dm.tile(TM_PROJ), TN_PROJ
    row = dm.mod_row(tm)
    return pl.pallas_call(
        _proj_in_kernel,
        grid=(n // tm, nout // tn),
        in_specs=[pl.BlockSpec((tm, d), lambda i, j: (i, 0)),
                  pl.BlockSpec((1, 6, d), lambda i, j: (row(i), 0, 0)),
                  pl.BlockSpec((1, d), lambda i, j: (0, 0)),
                  pl.BlockSpec((d, tn), lambda i, j: (0, j))],
        out_specs=pl.BlockSpec((tm, tn), lambda i, j: (i, j)),
        out_shape=jax.ShapeDtypeStruct((n, nout), F32),
        scratch_shapes=[pltpu.VMEM((tm, d), BF16)],
        compiler_params=_cparams(("parallel", "arbitrary")),
        name="proj_in",
    )(x, mod, gain.reshape(1, d), w)


def _swap_halves_64(x):
    lane = lax.broadcasted_iota(jnp.int32, x.shape, 1)
    first = (lane % RET_QK_DIM) < (RET_QK_DIM // 2)
    return jnp.where(first, pltpu.roll(x, LANES - RET_QK_DIM // 2, 1), pltpu.roll(x, RET_QK_DIM // 2, 1))


def _ret_rope(x, cos, sin):
    halves = []
    for j in range(x.shape[1] // LANES):
        xh = x[:, j * LANES:(j + 1) * LANES]
        halves.append(xh * cos + _swap_halves_64(xh) * sin)
    return jnp.concatenate(halves, axis=1)


def _ret_kernel(dec_ref, q_ref, k_ref, v_ref, g_ref, cos_ref, sin_ref, gn_ref, o_ref,
                sf_ref, sb_ref, sball_ref, *, ncc, nlc):
    phase = pl.program_id(1)
    s = pl.program_id(2)
    L = RET_L
    pos = lax.broadcasted_iota(jnp.int32, (L, 1), 0).astype(F32)
    cos = cos_ref[...]
    sin = sin_ref[...]
    k = _ret_rope(k_ref[...], cos, sin) * (RET_QK_DIM ** -0.5)
    v = v_ref[...]

    def log_g(direction, h):
        return -jnp.exp(jnp.full((1, 1), dec_ref[direction, h], F32))

    @pl.when(phase == 0)
    def _():
        @pl.when(s == 0)
        def _():
            sb_ref[...] = jnp.zeros_like(sb_ref)

        for h in range(RET_HEADS):
            lgb = log_g(1, h)
            kh = k[:, h * RET_QK_DIM:(h + 1) * RET_QK_DIM]
            vh = v[:, h * RET_V_DIM:(h + 1) * RET_V_DIM]
            st = sb_ref[h]
            sball_ref[s, h] = st
            sb_ref[h] = jnp.exp(L * lgb) * st + _dot_tn(kh * jnp.exp(pos * lgb), vh)

    @pl.when(phase == 1)
    def _():
        @pl.when(s == 0)
        def _():
            sf_ref[...] = jnp.zeros_like(sf_ref)

        slot = jnp.where(s < ncc, ncc - 1 - s, ncc + nlc - 1 - (s - ncc))
        q = _ret_rope(q_ref[...], cos, sin)
        g = g_ref[...]
        li = lax.broadcasted_iota(jnp.int32, (L, L), 0)
        mi = lax.broadcasted_iota(jnp.int32, (L, L), 1)
        diff = (li - mi).astype(F32)
        for h in range(RET_HEADS):
            lgf = log_g(0, h)
            lgb = log_g(1, h)
            decay = (jnp.where(diff >= 0, jnp.exp(jnp.maximum(diff, 0.0) * lgf), 0.0)
                     + jnp.where(diff <= 0, jnp.exp(jnp.maximum(-diff, 0.0) * lgb), 0.0))
            qh = q[:, h * RET_QK_DIM:(h + 1) * RET_QK_DIM]
            kh = k[:, h * RET_QK_DIM:(h + 1) * RET_QK_DIM]
            vh = v[:, h * RET_V_DIM:(h + 1) * RET_V_DIM]
            st_f = sf_ref[h]
            st_b = sball_ref[slot, h]
            o = _dot(_dot_nt(qh, kh) * decay, vh)
            o = o + _dot(qh, st_f) * jnp.exp((pos + 1.0) * lgf)
            o = o + _dot(qh, st_b) * jnp.exp((L - pos) * lgb)
            sf_ref[h] = jnp.exp(L * lgf) * st_f + _dot_tn(kh * jnp.exp((L - 1.0 - pos) * lgf), vh)
            mu = jnp.mean(o, axis=-1, keepdims=True)
            var = jnp.mean(jnp.square(o - mu), axis=-1, keepdims=True)
            o = (o - mu) * lax.rsqrt(var + GN_EPS) * gn_ref[:, h * RET_V_DIM:(h + 1) * RET_V_DIM]
            gh = g[:, h * RET_V_DIM:(h + 1) * RET_V_DIM]
            o_ref[:, h * RET_V_DIM:(h + 1) * RET_V_DIM] = (_silu(gh) * o).astype(o_ref.dtype)


def retention(dm, p, dec, cos_t, sin_t, gn):
    L = RET_L
    ncc, nlc = dm.ctx // L, dm.s // L
    nc = ncc + nlc
    lat_blocks = dm.n_lat // L
    qk_w = RET_HEADS * RET_QK_DIM
    v_w = RET_HEADS * RET_V_DIM

    def order(ph, s):
        is_ctx = s < ncc
        c_ctx = jnp.where(ph == 0, ncc - 1 - s, s)
        c_lat = jnp.where(ph == 0, nlc - 1 - (s - ncc), s - ncc)
        return is_ctx, c_ctx, c_lat

    def rows(b, ph, s):
        is_ctx, c_ctx, c_lat = order(ph, s)
        return jnp.where(is_ctx, lat_blocks + b * ncc + c_ctx, b * nlc + c_lat)

    def table_rows(b, ph, s):
        is_ctx, c_ctx, c_lat = order(ph, s)
        return jnp.where(is_ctx, nlc + c_ctx, c_lat)

    def out_rows(b, ph, s):
        return jnp.where(ph == 0, rows(b, 1, 0), rows(b, 1, s))

    grid_spec = pltpu.PrefetchScalarGridSpec(
        num_scalar_prefetch=1,
        grid=(dm.b, 2, nc),
        in_specs=[pl.BlockSpec((L, qk_w), lambda b, ph, s, d: (rows(b, ph, s), 0)),
                  pl.BlockSpec((L, qk_w), lambda b, ph, s, d: (rows(b, ph, s), 1)),
                  pl.BlockSpec((L, v_w), lambda b, ph, s, d: (rows(b, ph, s), 1)),
                  pl.BlockSpec((L, v_w), lambda b, ph, s, d: (rows(b, ph, s), 2)),
                  pl.BlockSpec((L, LANES), lambda b, ph, s, d: (table_rows(b, ph, s), 0)),
                  pl.BlockSpec((L, LANES), lambda b, ph, s, d: (table_rows(b, ph, s), 0)),
                  pl.BlockSpec((1, v_w), lambda b, ph, s, d: (0, 0))],
        out_specs=pl.BlockSpec((L, v_w), lambda b, ph, s, d: (out_rows(b, ph, s), 0)),
        scratch_shapes=[pltpu.VMEM((RET_HEADS, RET_QK_DIM, RET_V_DIM), F32),
                        pltpu.VMEM((RET_HEADS, RET_QK_DIM, RET_V_DIM), F32),
                        pltpu.VMEM((nc, RET_HEADS, RET_QK_DIM, RET_V_DIM), F32)],
    )
    return pl.pallas_call(
        functools.partial(_ret_kernel, ncc=ncc, nlc=nlc),
        grid_spec=grid_spec,
        out_shape=jax.ShapeDtypeStruct((dm.n_all, v_w), BF16),
        compiler_params=_cparams(("parallel", "arbitrary", "arbitrary")),
        name="retention",
    )(dec, p, p, p, p, cos_t, sin_t, gn.reshape(1, v_w))


def _att_prep_kernel(q_ref, k_ref, v_ref, cos_ref, sin_ref, gq_ref, gk_ref, qo_ref, ko_ref, vo_ref,
                     *, lat_tiles):
    is_lat = pl.program_id(0) < lat_tiles
    cos = jnp.where(is_lat, cos_ref[...], 1.0)
    sin = jnp.where(is_lat, sin_ref[...], 0.0)

    hd = ATT_HEAD_DIM

    def prep(x_ref, gain_ref, h):
        xh = x_ref[:, h * hd:(h + 1) * hd]
        xn = xh * lax.rsqrt(jnp.mean(xh * xh, axis=-1, keepdims=True) + NORM_EPS) * gain_ref[...]
        return xn * cos + pltpu.roll(xn, hd // 2, 1) * sin

    for h in range(ATT_HEADS):
        qo_ref[h * hd:(h + 1) * hd, :] = (prep(q_ref, gq_ref, h) * ATT_SCORE_SCALE).T.astype(qo_ref.dtype)
    ones = jnp.ones((ATT_V_ROWS - hd, vo_ref.shape[1]), vo_ref.dtype)
    for h in range(ATT_KV_HEADS):
        ko_ref[:, h * hd:(h + 1) * hd] = prep(k_ref, gk_ref, h).astype(ko_ref.dtype)
        vo_ref[h * ATT_V_ROWS:h * ATT_V_ROWS + hd, :] = v_ref[:, h * hd:(h + 1) * hd].T.astype(vo_ref.dtype)
        vo_ref[h * ATT_V_ROWS + hd:(h + 1) * ATT_V_ROWS, :] = ones


def att_prep(dm, p, cos_t, sin_t, gq, gk):
    tm = dm.tile(TM_PREP)
    n = dm.n_all
    lat_tiles, per_batch = dm.n_lat // tm, dm.s // tm
    qw = ATT_HEADS * ATT_HEAD_DIM
    kw = ATT_KV_HEADS * ATT_HEAD_DIM
    q_blk, k_blk, v_blk = 1536 // qw, 2048 // kw, 2304 // kw
    tab = lambda i: (jnp.where(i < lat_tiles, i % per_batch, 0), 0)
    return pl.pallas_call(
        functools.partial(_att_prep_kernel, lat_tiles=lat_tiles),
        grid=(n // tm,),
        in_specs=[pl.BlockSpec((tm, qw), lambda i: (i, q_blk)),
                  pl.BlockSpec((tm, kw), lambda i: (i, k_blk)),
                  pl.BlockSpec((tm, kw), lambda i: (i, v_blk)),
                  pl.BlockSpec((tm, ATT_HEAD_DIM), tab),
                  pl.BlockSpec((tm, ATT_HEAD_DIM), tab),
                  pl.BlockSpec((1, ATT_HEAD_DIM), lambda i: (0, 0)),
                  pl.BlockSpec((1, ATT_HEAD_DIM), lambda i: (0, 0))],
        out_specs=[pl.BlockSpec((qw, tm), lambda i: (0, i)),
                   pl.BlockSpec((tm, kw), lambda i: (i, 0)),
                   pl.BlockSpec((ATT_KV_HEADS * ATT_V_ROWS, tm), lambda i: (0, i))],
        out_shape=[jax.ShapeDtypeStruct((qw, n), BF16),
                   jax.ShapeDtypeStruct((n, kw), BF16),
                   jax.ShapeDtypeStruct((ATT_KV_HEADS * ATT_V_ROWS, n), BF16)],
        compiler_params=_cparams(("parallel",)),
        name="att_prep",
    )(p, p, p, cos_t, sin_t, gq.reshape(1, -1), gk.reshape(1, -1))


def _attn_kernel(*refs, with_latent_keys):
    ns = ATT_STAGES
    if with_latent_keys:
        qt_ref, kl_ref, vtl_ref, kc_ref, vtc_ref, o_ref, acc_ref = refs[:7]
    else:
        qt_ref, kc_ref, vtc_ref, o_ref, acc_ref = refs[:5]
    s_refs, p_refs = refs[-2 * ns:-ns], refs[-ns:]
    hd = ATT_HEAD_DIM
    grp = ATT_HEADS // ATT_KV_HEADS
    tq = qt_ref.shape[1]
    nq = grp * tq
    tk = kc_ref.shape[0]
    n = 1 + (kl_ref.shape[0] // tk if with_latent_keys else 0)
    qt = jnp.concatenate([qt_ref[j * hd:(j + 1) * hd, :] for j in range(grp)], axis=1)

    def key_chunk(j):
        if isinstance(j, int) and j == 0:
            return kc_ref[...]
        return kl_ref[pl.ds(pl.multiple_of((j - 1) * tk, tk), tk), :]

    def value_chunk(j):
        if isinstance(j, int) and j == 0:
            return vtc_ref[...]
        return vtl_ref[:, pl.ds(pl.multiple_of((j - 1) * tk, tk), tk)]

    def max_keys(x):
        groups = math.gcd(tk // SUBLANES, ATT_REDUCE_GROUPS)
        part = jnp.max(x.reshape(groups, tk // groups, nq), axis=0)
        return jnp.max(part, axis=0, keepdims=True)

    def time_step(t, slot, carry, do_scores=True, do_softmax=True, do_values=True):
        m, alpha = carry
        if do_scores:
            s_refs[slot][...] = jnp.dot(key_chunk(t), qt, preferred_element_type=F32)
        if do_values:
            pv = jnp.dot(value_chunk(t - 2), p_refs[(slot - 2) % ns][...], preferred_element_type=F32)
            acc_ref[...] = acc_ref[...] * alpha + pv
        if do_softmax:
            st = s_refs[(slot - 1) % ns][...]
            m_new = jnp.maximum(m, max_keys(st))
            p_refs[(slot - 1) % ns][...] = jnp.exp2(st - m_new).astype(BF16)
            alpha = jnp.exp2(m - m_new)
            m = m_new
        return m, alpha

    acc_ref[...] = jnp.zeros_like(acc_ref)
    carry = (jnp.full((1, nq), -1e30, F32), jnp.zeros((1, nq), F32))
    fill = min(3, n + 2)
    for t in range(fill):
        carry = time_step(t, t % ns, carry, t < n, 0 <= t - 1 < n, 0 <= t - 2 < n)
    steady = max(n - fill, 0)
    body_steps = ns * ATT_UNROLL

    def unrolled(i, cr):
        for k in range(body_steps):
            cr = time_step(fill + body_steps * i + k, (fill + k) % ns, cr)
        return cr

    if steady // body_steps:
        carry = lax.fori_loop(0, steady // body_steps, unrolled, carry)
    for t in range(fill + body_steps * (steady // body_steps), max(n, fill)):
        carry = time_step(t, t % ns, carry)
    for t in range(max(n, fill), n + 2):
        carry = time_step(t, t % ns, carry, False, t - 1 < n, True)
    out = (acc_ref[0:hd, :] / acc_ref[hd:hd + 1, :]).T
    for j in range(grp):
        o_ref[:, j * hd:(j + 1) * hd] = out[j * tq:(j + 1) * tq].astype(o_ref.dtype)


def attention(dm, qt, kn, vt, latent):
    hd = ATT_HEAD_DIM
    grp = ATT_HEADS // ATT_KV_HEADS
    tk = dm.ctx
    assert dm.s % tk == 0
    ctx_blk0 = dm.n_lat // dm.ctx
    kc_spec = pl.BlockSpec((tk, hd), lambda b, h, i: (ctx_blk0 + b, h))
    vtc_spec = pl.BlockSpec((ATT_V_ROWS, tk), lambda b, h, i: (h, ctx_blk0 + b))
    if latent:
        tq = TQ_ATT
        nq = dm.s // tq
        q_spec = pl.BlockSpec((grp * hd, tq), lambda b, h, i: (h, b * nq + i))
        kl_spec = pl.BlockSpec((dm.s, hd), lambda b, h, i: (b, h))
        vtl_spec = pl.BlockSpec((ATT_V_ROWS, dm.s), lambda b, h, i: (h, b))
        in_specs = [q_spec, kl_spec, vtl_spec, kc_spec, vtc_spec]
        args = (qt, kn, vt, kn, vt)
        rows = dm.n_lat
    else:
        tq, nq = dm.ctx, 1
        q_spec = pl.BlockSpec((grp * hd, tq), lambda b, h, i: (h, ctx_blk0 + b))
        in_specs = [q_spec, kc_spec, vtc_spec]
        args = (qt, kn, vt)
        rows = dm.b * dm.ctx
    return pl.pallas_call(
        functools.partial(_attn_kernel, with_latent_keys=latent),
        grid=(dm.b, ATT_KV_HEADS, nq),
        in_specs=in_specs,
        out_specs=pl.BlockSpec((tq, grp * hd), lambda b, h, i: (b * nq + i, h)),
        out_shape=jax.ShapeDtypeStruct((rows, ATT_HEADS * hd), BF16),
        scratch_shapes=([pltpu.VMEM((ATT_V_ROWS, grp * tq), F32)]
                        + [pltpu.VMEM((tk, grp * tq), F32)] * ATT_STAGES
                        + [pltpu.VMEM((tk, grp * tq), BF16)] * ATT_STAGES),
        compiler_params=_cparams(("parallel", "parallel", "arbitrary")),
        name="attention_latent" if latent else "attention_ctx",
    )(*args)


def _pool_kernel(prev_ref, u_ref, next_ref, w_ref, sc_ref, o_ref, a_ref, b_ref, *, dm):
    tm = u_ref.shape[0]
    H = POOL_HALO
    i = pl.program_id(0)
    lat_tiles = dm.n_lat // tm
    is_lat = i < lat_tiles
    n_seg = jnp.where(is_lat, dm.s, dm.ctx)
    base = jnp.where(is_lat, (i * tm) % dm.s, 0)
    r = lax.broadcasted_iota(jnp.int32, (tm + 2 * H, 1), 0) - H
    pos = jnp.where(is_lat, base + r, r % dm.ctx)
    win = pl.ds(H, tm + 2 * H)

    def shifted(ref, d):
        return ref[pl.ds(H + d, tm + 2 * H), :]

    zeros = jnp.zeros((H, POOL_GROUP_DIM), F32)
    for ref in (a_ref, b_ref):
        ref[pl.ds(0, H), :] = zeros
        ref[pl.ds(tm + 3 * H, H), :] = zeros

    for gi, w in enumerate(POOL_WINDOWS):
        cols = slice(gi * POOL_GROUP_DIM, (gi + 1) * POOL_GROUP_DIM)
        half = w // 2
        a_ref[pl.ds(H, H), :] = prev_ref[:, cols]
        a_ref[pl.ds(2 * H, tm), :] = u_ref[:, cols]
        a_ref[pl.ds(2 * H + tm, H), :] = next_ref[:, cols]
        b_ref[win, :] = jnp.where(pos >= 1, shifted(a_ref, -1), 0.0)
        step = 1
        while step < half:
            b_ref[win, :] = b_ref[win, :] + jnp.where(pos >= step, shifted(b_ref, -step), 0.0)
            step *= 2
        step = 1
        while step < half:
            a_ref[win, :] = a_ref[win, :] + jnp.where(pos + step < n_seg, shifted(a_ref, step), 0.0)
            step *= 2
        tile = pl.ds(2 * H, tm)
        pt = pos[H:H + tm]
        cnt = (jnp.minimum(pt + half, n_seg) - jnp.maximum(pt - half, 0)).astype(F32)
        u = u_ref[:, cols]
        diff = (a_ref[tile, :] + b_ref[tile, :]) / cnt - u
        o_ref[:, cols] = (_dot(diff, w_ref[gi]) * sc_ref[:, cols]).astype(o_ref.dtype)


def pool_mixer(dm, p, pool_w, pool_scale):
    tm = dm.tile(TM_POOL)
    n = dm.n_all
    H = POOL_HALO
    width = len(POOL_WINDOWS) * POOL_GROUP_DIM
    col = 2560 // width
    per = tm // H
    last = n // H - 1
    return pl.pallas_call(
        functools.partial(_pool_kernel, dm=dm),
        grid=(n // tm,),
        in_specs=[pl.BlockSpec((H, width), lambda i: (jnp.maximum(i * per - 1, 0), col)),
                  pl.BlockSpec((tm, width), lambda i: (i, col)),
                  pl.BlockSpec((H, width), lambda i: (jnp.minimum((i + 1) * per, last), col)),
                  pl.BlockSpec(pool_w.shape, lambda i: (0, 0, 0)),
                  pl.BlockSpec((1, width), lambda i: (0, 0))],
        out_specs=pl.BlockSpec((tm, width), lambda i: (i, 0)),
        out_shape=jax.ShapeDtypeStruct((n, width), BF16),
        scratch_shapes=[pltpu.VMEM((tm + 4 * H, POOL_GROUP_DIM), F32),
                        pltpu.VMEM((tm + 4 * H, POOL_GROUP_DIM), F32)],
        compiler_params=_cparams(("parallel",)),
        name="pool_mixer",
    )(p, p, p, pool_w, pool_scale.reshape(1, width))


def _merge_kernel(yr_ref, ya_ref, yp_ref, gl_ref, x_ref, mod_ref, wb_ref, wo_ref, o_ref):
    d = x_ref.shape[1]
    mixed = None
    for j, y_ref in enumerate((yr_ref, ya_ref, yp_ref)):
        gate = _sigmoid(gl_ref[:, j * d:(j + 1) * d])
        term = gate * jnp.dot(y_ref[...], wb_ref[j], preferred_element_type=F32)
        mixed = term if mixed is None else mixed + term
    y = _dot(mixed, wo_ref[...])
    o_ref[...] = x_ref[...] + mod_ref[0][2:3] * y


def merge(dm, n_rows, yr, ya, yp, p, x, mod, wb, wo):
    tm = dm.tile(TM_MERGE)
    d = dm.d
    bw = yr.shape[1]
    row = dm.mod_row(tm)
    return pl.pallas_call(
        _merge_kernel,
        grid=(n_rows // tm,),
        in_specs=[pl.BlockSpec((tm, bw), lambda i: (i, 0)),
                  pl.BlockSpec((tm, bw), lambda i: (i, 0)),
                  pl.BlockSpec((tm, bw), lambda i: (i, 0)),
                  pl.BlockSpec((tm, 3 * d), lambda i: (i, 1)),
                  pl.BlockSpec((tm, d), lambda i: (i, 0)),
                  pl.BlockSpec((1, 6, d), lambda i: (row(i), 0, 0)),
                  pl.BlockSpec(wb.shape, lambda i: (0, 0, 0)),
                  pl.BlockSpec(wo.shape, lambda i: (0, 0))],
        out_specs=pl.BlockSpec((tm, d), lambda i: (i, 0)),
        out_shape=jax.ShapeDtypeStruct((n_rows, d), F32),
        compiler_params=_cparams(("parallel",)),
        name="merge",
    )(yr, ya, yp, p, x, mod, wb, wo)


def _ffn_kernel(x_ref, mod_ref, gain_ref, w1_ref, w3_ref, w2_ref, o_ref, h_ref, acc_ref):
    f = pl.program_id(1)
    m = mod_ref[0]

    @pl.when(f == 0)
    def _():
        h_ref[...] = _norm_mod(x_ref[...], gain_ref[...], m[3:4], m[4:5]).astype(BF16)
        acc_ref[...] = jnp.zeros_like(acc_ref)

    h = h_ref[...]
    a = jnp.dot(h, w1_ref[...], preferred_element_type=F32)
    b = jnp.dot(h, w3_ref[...], preferred_element_type=F32)
    acc_ref[...] += _dot(_silu(a) * b, w2_ref[...])

    @pl.when(f == pl.num_programs(1) - 1)
    def _():
        o_ref[...] = x_ref[...] + m[5:6] * acc_ref[...]


def ffn_dense(dm, x, mod, gain, w1, w3, w2):
    n, d = x.shape
    fdim = w1.shape[1]
    tm, tf = dm.tile(TM_FFN), TF_FFN
    row = dm.mod_row(tm)
    return pl.pallas_call(
        _ffn_kernel,
        grid=(n // tm, fdim // tf),
        in_specs=[pl.BlockSpec((tm, d), lambda i, f: (i, 0)),
                  pl.BlockSpec((1, 6, d), lambda i, f: (row(i), 0, 0)),
                  pl.BlockSpec((1, d), lambda i, f: (0, 0)),
                  pl.BlockSpec((d, tf), lambda i, f: (0, f)),
                  pl.BlockSpec((d, tf), lambda i, f: (0, f)),
                  pl.BlockSpec((tf, d), lambda i, f: (f, 0))],
        out_specs=pl.BlockSpec((tm, d), lambda i, f: (i, 0)),
        out_shape=jax.ShapeDtypeStruct((n, d), F32),
        scratch_shapes=[pltpu.VMEM((tm, d), BF16), pltpu.VMEM((tm, d), F32)],
        compiler_params=_cparams(("parallel", "arbitrary")),
        name="ffn_dense",
    )(x, mod, gain.reshape(1, d), w1, w3, w2)


def _router_kernel(x_ref, mod_ref, gain_ref, wr_ref, br_ref, h_ref, slot_ref, slot_t_ref, wt_ref, cnt_ref):
    tm = x_ref.shape[0]
    m = mod_ref[0]
    h = _norm_mod(x_ref[...], gain_ref[...], m[3:4], m[4:5])
    h_ref[...] = h.astype(BF16)
    logits = _dot3(h, wr_ref[...]) + br_ref[...]
    lane = lax.broadcasted_iota(jnp.int32, logits.shape, 1)
    m1 = jnp.max(logits, axis=-1, keepdims=True)
    i1 = jnp.min(jnp.where(logits == m1, lane, LANES), axis=-1, keepdims=True)
    rest = jnp.where(lane == i1, -jnp.inf, logits)
    m2 = jnp.max(rest, axis=-1, keepdims=True)
    i2 = jnp.min(jnp.where(rest == m2, lane, LANES), axis=-1, keepdims=True)
    e = jnp.exp(m2 - m1)
    w1 = 1.0 / (1.0 + e)
    w2 = e / (1.0 + e)
    wt_ref[...] = jnp.where(lane == i1, w1, jnp.where(lane == i2, w2, 0.0))
    routed = jnp.where((lane == i1) | (lane == i2), 1.0, 0.0)
    rb = RB_MOE
    lower = (lax.broadcasted_iota(jnp.int32, (rb, rb), 0) > lax.broadcasted_iota(jnp.int32, (rb, rb), 1))
    lower = lower.astype(BF16)
    offset = jnp.zeros((1, LANES), F32)
    for blk in range(tm // rb):
        rows = slice(blk * rb, (blk + 1) * rb)
        mask = routed[rows]
        rank = jnp.dot(lower, mask.astype(BF16), preferred_element_type=F32) + offset
        slot_ref[rows, :] = jnp.where(mask > 0.0, rank, -1.0)
        offset = offset + jnp.sum(mask, axis=0, keepdims=True)
    cnt_ref[0] = jnp.broadcast_to(offset, (SUBLANES, LANES)).astype(jnp.int32)
    slot_t_ref[...] = slot_ref[...].T[:SUBLANES, :]


def moe_route(dm, n_rows, x, mod, gain, wr_pad, br_pad):
    tm = dm.tile(TM_MOE)
    d = dm.d
    row = dm.mod_row(tm)
    nt = n_rows // tm
    return pl.pallas_call(
        _router_kernel,
        grid=(nt,),
        in_specs=[pl.BlockSpec((tm, d), lambda i: (i, 0)),
                  pl.BlockSpec((1, 6, d), lambda i: (row(i), 0, 0)),
                  pl.BlockSpec((1, d), lambda i: (0, 0)),
                  pl.BlockSpec((d, LANES), lambda i: (0, 0)),
                  pl.BlockSpec((1, LANES), lambda i: (0, 0))],
        out_specs=[pl.BlockSpec((tm, d), lambda i: (i, 0)),
                   pl.BlockSpec((tm, LANES), lambda i: (i, 0)),
                   pl.BlockSpec((SUBLANES, tm), lambda i: (0, i)),
                   pl.BlockSpec((tm, LANES), lambda i: (i, 0)),
                   pl.BlockSpec((1, SUBLANES, LANES), lambda i: (i, 0, 0))],
        out_shape=[jax.ShapeDtypeStruct((n_rows, d), BF16),
                   jax.ShapeDtypeStruct((n_rows, LANES), F32),
                   jax.ShapeDtypeStruct((SUBLANES, n_rows), F32),
                   jax.ShapeDtypeStruct((n_rows, LANES), F32),
                   jax.ShapeDtypeStruct((nt, SUBLANES, LANES), jnp.int32)],
        compiler_params=_cparams(("parallel",)),
        name="moe_router",
    )(x, mod, gain.reshape(1, d), wr_pad, br_pad)


def _moe_kernel(cnt_ref, h_ref, slot_ref, slot_t_ref, wt_ref, w1_ref, w3_ref, w2_ref, o_ref,
                xs_ref, acc_ref):
    i, e, f = pl.program_id(0), pl.program_id(1), pl.program_id(2)
    tm = h_ref.shape[0]
    rb = RB_MOE
    nblk = (cnt_ref[i, e] + rb - 1) // rb

    @pl.when((e == 0) & (f == 0))
    def _():
        o_ref[...] = jnp.zeros_like(o_ref)

    @pl.when(f == 0)
    def _():
        slot_row = slot_t_ref[pl.ds(e, 1), :]

        def gather(r, carry):
            want = lax.broadcasted_iota(jnp.int32, (rb, tm), 0).astype(F32) + (r * rb).astype(F32)
            onehot = (slot_row == want).astype(BF16)
            xs_ref[r] = jnp.dot(onehot, h_ref[...], preferred_element_type=F32).astype(BF16)
            acc_ref[r] = jnp.zeros((rb, acc_ref.shape[2]), F32)
            return carry

        lax.fori_loop(0, nblk, gather, 0)

    def expert(r, carry):
        xs = xs_ref[r]
        a = jnp.dot(xs, w1_ref[0], preferred_element_type=F32)
        b = jnp.dot(xs, w3_ref[0], preferred_element_type=F32)
        acc_ref[r] += _dot(_silu(a) * b, w2_ref[0])
        return carry

    lax.fori_loop(0, nblk, expert, 0)

    @pl.when(f == pl.num_programs(2) - 1)
    def _():
        lane = lax.broadcasted_iota(jnp.int32, (tm, LANES), 1)
        pick = lane == e
        slot_col = jnp.sum(jnp.where(pick, slot_ref[...], 0.0), axis=-1, keepdims=True)
        wt_col = jnp.sum(jnp.where(pick, wt_ref[...], 0.0), axis=-1, keepdims=True)

        def scatter(r, carry):
            want = lax.broadcasted_iota(jnp.int32, (tm, rb), 1).astype(F32) + (r * rb).astype(F32)
            onehot = (slot_col == want).astype(BF16)
            o_ref[...] += wt_col * _dot(onehot, acc_ref[r])
            return carry

        lax.fori_loop(0, nblk, scatter, 0)


def moe_experts(dm, n_rows, h, slot, slot_t, wt, counts, w1, w3, w2):
    tm, tf = dm.tile(TM_MOE), TF_FFN
    d = dm.d
    ne, _, fdim = w1.shape
    grid_spec = pltpu.PrefetchScalarGridSpec(
        num_scalar_prefetch=1,
        grid=(n_rows // tm, ne, fdim // tf),
        in_specs=[pl.BlockSpec((tm, d), lambda i, e, f, c: (i, 0)),
                  pl.BlockSpec((tm, LANES), lambda i, e, f, c: (i, 0)),
                  pl.BlockSpec((SUBLANES, tm), lambda i, e, f, c: (0, i)),
                  pl.BlockSpec((tm, LANES), lambda i, e, f, c: (i, 0)),
                  pl.BlockSpec((1, d, tf), lambda i, e, f, c: (e, 0, f)),
                  pl.BlockSpec((1, d, tf), lambda i, e, f, c: (e, 0, f)),
                  pl.BlockSpec((1, tf, d), lambda i, e, f, c: (e, f, 0))],
        out_specs=pl.BlockSpec((tm, d), lambda i, e, f, c: (i, 0)),
        scratch_shapes=[pltpu.VMEM((tm // RB_MOE, RB_MOE, d), BF16),
                        pltpu.VMEM((tm // RB_MOE, RB_MOE, d), F32)],
    )
    return pl.pallas_call(
        _moe_kernel,
        grid_spec=grid_spec,
        out_shape=jax.ShapeDtypeStruct((n_rows, d), F32),
        compiler_params=_cparams(("parallel", "arbitrary", "arbitrary")),
        name="moe_experts",
    )(counts, h, slot, slot_t, wt, w1, w3, w2)


def _residual_kernel(x_ref, y_ref, mod_ref, o_ref):
    o_ref[...] = x_ref[...] + mod_ref[0][5:6] * y_ref[...]


def _final_kernel(x_ref, y_ref, mod_ref, gain_ref, o_ref):
    x = x_ref[...] + mod_ref[0][5:6] * y_ref[...]
    o_ref[...] = x * lax.rsqrt(jnp.mean(x * x, axis=-1, keepdims=True) + NORM_EPS) * gain_ref[...]


def residual(dm, n_rows, x, y, mod, final_gain=None):
    tm = dm.tile(TM_FFN)
    d = dm.d
    row = dm.mod_row(tm)
    in_specs = [pl.BlockSpec((tm, d), lambda i: (i, 0)),
                pl.BlockSpec((tm, d), lambda i: (i, 0)),
                pl.BlockSpec((1, 6, d), lambda i: (row(i), 0, 0))]
    args = [x, y, mod]
    if final_gain is not None:
        in_specs.append(pl.BlockSpec((1, d), lambda i: (0, 0)))
        args.append(final_gain.reshape(1, d))
    return pl.pallas_call(
        _residual_kernel if final_gain is None else _final_kernel,
        grid=(n_rows // tm,),
        in_specs=in_specs,
        out_specs=pl.BlockSpec((tm, d), lambda i: (i, 0)),
        out_shape=jax.ShapeDtypeStruct((n_rows, d), F32),
        compiler_params=_cparams(("parallel",)),
        name="residual" if final_gain is None else "residual_final_norm",
    )(*args)


def _final_norm_kernel(x_ref, gain_ref, o_ref):
    x = x_ref[...]
    o_ref[...] = x * lax.rsqrt(jnp.mean(x * x, axis=-1, keepdims=True) + NORM_EPS) * gain_ref[...]


def final_rmsnorm(dm, n_rows, x, gain):
    tm = dm.tile(TM_FFN)
    d = dm.d
    return pl.pallas_call(
        _final_norm_kernel,
        grid=(n_rows // tm,),
        in_specs=[pl.BlockSpec((tm, d), lambda i: (i, 0)), pl.BlockSpec((1, d), lambda i: (0, 0))],
        out_specs=pl.BlockSpec((tm, d), lambda i: (i, 0)),
        out_shape=jax.ShapeDtypeStruct((n_rows, d), F32),
        compiler_params=_cparams(("parallel",)),
        name="final_norm",
    )(x, gain.reshape(1, d))


def _deinterleave(width, head_dim):
    idx = np.arange(width).reshape(width // head_dim, head_dim // 2, 2)
    return np.concatenate([idx[..., 0], idx[..., 1]], axis=-1).reshape(-1)


def _w_in_perm(in_width):
    perm = np.arange(in_width)
    qk = RET_HEADS * RET_QK_DIM
    perm[0:qk] = _deinterleave(qk, RET_QK_DIM)
    perm[qk:2 * qk] = qk + _deinterleave(qk, RET_QK_DIM)
    q0 = 2 * qk + 2 * RET_HEADS * RET_V_DIM
    qw = ATT_HEADS * ATT_HEAD_DIM
    kw = ATT_KV_HEADS * ATT_HEAD_DIM
    perm[q0:q0 + qw] = q0 + _deinterleave(qw, ATT_HEAD_DIM)
    perm[q0 + qw:q0 + qw + kw] = q0 + qw + _deinterleave(kw, ATT_HEAD_DIM)
    return perm


def _rope_tables(s, ctx):
    def angles(pos, dim):
        freqs = 1.0 / (ROPE_THETA ** (jnp.arange(0, dim, 2, dtype=F32) / dim))
        return pos[:, None] * freqs[None, :]

    rows = s // GRID_W
    row = jnp.repeat(jnp.arange(rows, dtype=F32), GRID_W)
    col = jnp.tile(jnp.arange(GRID_W, dtype=F32), rows)
    half = ATT_HEAD_DIM // 2
    ang_att = jnp.concatenate([angles(row, half), angles(col, half)], axis=-1)
    att_cos = jnp.concatenate([jnp.cos(ang_att)] * 2, axis=-1)
    att_sin = jnp.concatenate([-jnp.sin(ang_att), jnp.sin(ang_att)], axis=-1)
    ang_ret = angles(jnp.arange(s, dtype=F32), RET_QK_DIM)
    ang_ret = jnp.concatenate([ang_ret, jnp.zeros((ctx, RET_QK_DIM // 2), F32)], axis=0)
    reps = LANES // RET_QK_DIM
    ret_cos = jnp.concatenate([jnp.cos(ang_ret)] * (2 * reps), axis=-1)
    ret_sin = jnp.concatenate([-jnp.sin(ang_ret), jnp.sin(ang_ret)] * reps, axis=-1)
    return att_cos, att_sin, ret_cos, ret_sin


def kernel(x, c, ctx, c_ctx, w_ada, b_ada, norm_mix, norm_ffn, w_in, ret_decay, ret_gn, attn_qn, attn_kn, pool_w, pool_scale, w_branch, w_out, ffn_w1, ffn_w3, ffn_w2, moe_router, moe_router_b, moe_w1, moe_w3, moe_w2, final_norm):
    b, s, d = x.shape
    n_ctx = ctx.shape[1]
    depth = w_in.shape[0]
    dm = Dims(b, s, n_ctx, d)

    xa = jnp.concatenate([x.reshape(b * s, d), ctx.reshape(b * n_ctx, d)], axis=0)
    mod_rows = 2 * SUBLANES
    cc = jnp.concatenate([c, c_ctx[None, :], jnp.zeros((mod_rows - b - 1, d), F32)], axis=0)
    mods = ada_table(cc, w_ada, b_ada).reshape(depth, mod_rows, 6, d)

    att_cos, att_sin, ret_cos, ret_sin = _rope_tables(s, n_ctx)
    perm = _w_in_perm(w_in.shape[2])
    qn_perm = _deinterleave(ATT_HEAD_DIM, ATT_HEAD_DIM)

    for i in range(depth):
        need_ctx = i < depth - 1
        n_rows = dm.n_all if need_ctx else dm.n_lat
        mod = mods[i]
        p = proj_in(dm, xa, mod, norm_mix[i], w_in[i][:, perm].astype(BF16))
        y_ret = retention(dm, p, ret_decay[i], ret_cos, ret_sin, ret_gn[i])
        y_pool = pool_mixer(dm, p, pool_w[i].astype(BF16), pool_scale[i])
        qn, kn, vb = att_prep(dm, p, att_cos, att_sin, attn_qn[i][qn_perm], attn_kn[i][qn_perm])
        y_att = attention(dm, qn, kn, vb, latent=True)
        if need_ctx:
            y_att = jnp.concatenate([y_att, attention(dm, qn, kn, vb, latent=False)], axis=0)
        xa = merge(dm, n_rows, y_ret, y_att, y_pool, p, xa, mod,
                   w_branch[i].astype(BF16), w_out[i].astype(BF16))
        j = i // 2
        last = i == depth - 1
        if i % 2 == 0:
            xa = ffn_dense(dm, xa, mod, norm_ffn[i], ffn_w1[j].astype(BF16), ffn_w3[j].astype(BF16),
                           ffn_w2[j].astype(BF16))
            if last:
                xa = final_rmsnorm(dm, n_rows, xa, final_norm)
        else:
            wr = jnp.zeros((d, LANES), F32).at[:, :N_EXPERTS].set(moe_router[j])
            br = jnp.full((1, LANES), -1e30, F32).at[0, :N_EXPERTS].set(moe_router_b[j])
            h, slot, slot_t, wt, cnt = moe_route(dm, n_rows, xa, mod, norm_ffn[i], wr, br)
            y = moe_experts(dm, n_rows, h, slot, slot_t, wt, cnt[:, 0, :N_EXPERTS],
                            moe_w1[j].astype(BF16), moe_w3[j].astype(BF16), moe_w2[j].astype(BF16))
            xa = residual(dm, n_rows, xa, y, mod, final_norm if last else None)
    return xa[:dm.n_lat].reshape(b, s, d)
```

```python
import functools
import math

import numpy as np
import jax
import jax.numpy as jnp
from jax import lax
from jax.experimental import pallas as pl
from jax.experimental.pallas import tpu as pltpu

F32 = jnp.float32
BF16 = jnp.bfloat16

GRID_W = 64
RET_HEADS = 4
RET_QK_DIM = 64
RET_V_DIM = 128
ATT_HEADS = 4
ATT_KV_HEADS = 2
ATT_HEAD_DIM = 128
ROPE_THETA = 10000.0
POOL_WINDOWS = (2, 4, 8, 16)
POOL_GROUP_DIM = 128
N_EXPERTS = 8
NORM_EPS = 1e-6
GN_EPS = 1e-5

LANES = 128
SUBLANES = 8
VMEM_LIMIT_BYTES = 56 * 1024 * 1024

TM_PROJ = 1024
TN_PROJ = 2048
TM_PREP = 1024
TM_POOL = 1024
TM_MERGE = 512
TM_FFN = 1024
TF_FFN = 512
TM_MOE = 2048
TF_MOE = 896
RB_MOE = 288
WIN_MOE = 256
SUB_MOE = 512
SLOT_ALIGN = 16
assert RB_MOE % 32 == 0
RET_L = 256
TQ_ATT = 256
ATT_SCORE_SCALE = (ATT_HEAD_DIM ** -0.5) * math.log2(math.e)
ATT_V_ROWS = ATT_HEAD_DIM + 16
ATT_STAGES = 3
ATT_UNROLL = 2
ATT_REDUCE_GROUPS = 8
POOL_HALO = 8


def _cparams(sem):
    return pltpu.CompilerParams(dimension_semantics=sem, vmem_limit_bytes=VMEM_LIMIT_BYTES)


def _dot(a, b):
    return jnp.dot(a.astype(BF16), b.astype(BF16), preferred_element_type=F32)


def _dot_nt(a, b):
    return lax.dot_general(a.astype(BF16), b.astype(BF16), (((1,), (1,)), ((), ())),
                           preferred_element_type=F32)


def _dot_tn(a, b):
    return lax.dot_general(a.astype(BF16), b.astype(BF16), (((0,), (0,)), ((), ())),
                           preferred_element_type=F32)


def _split_bf16(a):
    hi = a.astype(BF16)
    lo = (a - hi.astype(F32)).astype(BF16)
    return hi, lo


def _dot3(a, b):
    ah, al = _split_bf16(a)
    bh, bl = _split_bf16(b)
    d = functools.partial(jnp.dot, preferred_element_type=F32)
    return d(ah, bh) + (d(al, bh) + d(ah, bl))


def _norm_mod(x, gain, shift, scale):
    y = x * lax.rsqrt(jnp.mean(x * x, axis=-1, keepdims=True) + NORM_EPS)
    return (y * gain) * (1.0 + scale) + shift


def _sigmoid(x):
    return 1.0 / (1.0 + jnp.exp(-x))


def _silu(x):
    return x * _sigmoid(x)


class Dims:
    def __init__(self, b, s, ctx, d):
        self.b, self.s, self.ctx, self.d = b, s, ctx, d
        self.n_lat = b * s
        self.n_all = b * s + b * ctx

    def tile(self, preferred):
        t = preferred
        while self.s % t or (self.b * self.ctx) % t:
            t //= 2
        return t

    def mod_row(self, tm):
        lat_tiles, per_batch, b = self.n_lat // tm, self.s // tm, self.b
        return lambda i: jnp.where(i < lat_tiles, i // per_batch, b)


def _ada_kernel(c_ref, w_ref, b_ref, o_ref):
    o_ref[0] = _dot3(_silu(c_ref[...]), w_ref[0]) + b_ref[0]


def ada_table(cc, w_ada, b_ada):
    depth, d, n6 = w_ada.shape
    tn = n6 // 4
    rows = cc.shape[0]
    return pl.pallas_call(
        _ada_kernel,
        grid=(depth, n6 // tn),
        in_specs=[pl.BlockSpec((rows, d), lambda l, j: (0, 0)),
                  pl.BlockSpec((1, d, tn), lambda l, j: (l, 0, j)),
                  pl.BlockSpec((1, 1, tn), lambda l, j: (l, 0, j))],
        out_specs=pl.BlockSpec((1, rows, tn), lambda l, j: (l, 0, j)),
        out_shape=jax.ShapeDtypeStruct((depth, rows, n6), F32),
        compiler_params=_cparams(("arbitrary", "arbitrary")),
        name="ada_table",
    )(cc, w_ada, b_ada.reshape(depth, 1, n6))


def _proj_in_kernel(x_ref, mod_ref, gain_ref, w_ref, o_ref, h_ref):
    @pl.when(pl.program_id(1) == 0)
    def _():
        m = mod_ref[0]
        h_ref[...] = _norm_mod(x_ref[...], gain_ref[...], m[0:1], m[1:2]).astype(BF16)

    o_ref[...] = jnp.dot(h_ref[...], w_ref[...], preferred_element_type=F32)


def proj_in(dm, x, mod, gain, w):
    n, d = x.shape
    nout = w.shape[1]
    tm, tn = dm.tile(TM_PROJ), TN_PROJ
    row = dm.mod_row(tm)
    return pl.pallas_call(
        _proj_in_kernel,
        grid=(n // tm, nout // tn),
        in_specs=[pl.BlockSpec((tm, d), lambda i, j: (i, 0)),
                  pl.BlockSpec((1, 6, d), lambda i, j: (row(i), 0, 0)),
                  pl.BlockSpec((1, d), lambda i, j: (0, 0)),
                  pl.BlockSpec((d, tn), lambda i, j: (0, j))],
        out_specs=pl.BlockSpec((tm, tn), lambda i, j: (i, j)),
        out_shape=jax.ShapeDtypeStruct((n, nout), F32),
        scratch_shapes=[pltpu.VMEM((tm, d), BF16)],
        compiler_params=_cparams(("parallel", "arbitrary")),
        name="proj_in",
    )(x, mod, gain.reshape(1, d), w)


def _rope_pairs(x, cos, sin):
    lane = lax.broadcasted_iota(jnp.int32, x.shape, 1)
    partner = jnp.where(lane % 2 == 0, pltpu.roll(x, LANES - 1, 1), pltpu.roll(x, 1, 1))
    return x * cos + partner * sin


def _ret_rope(x, cos, sin):
    return jnp.concatenate([_rope_pairs(x[:, j * LANES:(j + 1) * LANES], cos, sin)
                            for j in range(x.shape[1] // LANES)], axis=1)


def _ret_kernel(dec_ref, q_ref, k_ref, v_ref, g_ref, cos_ref, sin_ref, gn_ref, o_ref,
                sf_ref, sb_ref, sball_ref, decay_ref, *, ncc, nlc):
    phase = pl.program_id(1)
    s = pl.program_id(2)
    L = RET_L
    pos = lax.broadcasted_iota(jnp.int32, (L, 1), 0).astype(F32)
    cos = cos_ref[...]
    sin = sin_ref[...]
    k = _ret_rope(k_ref[...], cos, sin) * (RET_QK_DIM ** -0.5)
    v = v_ref[...]

    def log_g(direction, h):
        return -jnp.exp(jnp.full((1, 1), dec_ref[direction, h], F32))

    @pl.when(phase == 0)
    def _():
        @pl.when(s == 0)
        def _():
            sb_ref[...] = jnp.zeros_like(sb_ref)
            diff = (lax.broadcasted_iota(jnp.int32, (L, L), 0)
                    - lax.broadcasted_iota(jnp.int32, (L, L), 1)).astype(F32)
            for h in range(RET_HEADS):
                decay_ref[h] = (jnp.where(diff >= 0, jnp.exp(jnp.maximum(diff, 0.0) * log_g(0, h)), 0.0)
                                + jnp.where(diff <= 0, jnp.exp(jnp.maximum(-diff, 0.0) * log_g(1, h)), 0.0))

        for h in range(RET_HEADS):
            lgb = log_g(1, h)
            kh = k[:, h * RET_QK_DIM:(h + 1) * RET_QK_DIM]
            vh = v[:, h * RET_V_DIM:(h + 1) * RET_V_DIM]
            st = sb_ref[h]
            sball_ref[s, h] = st
            sb_ref[h] = jnp.exp(L * lgb) * st + _dot_tn(kh * jnp.exp(pos * lgb), vh)

    @pl.when(phase == 1)
    def _():
        @pl.when(s == 0)
        def _():
            sf_ref[...] = jnp.zeros_like(sf_ref)

        slot = jnp.where(s < ncc, ncc - 1 - s, ncc + nlc - 1 - (s - ncc))
        q = _ret_rope(q_ref[...], cos, sin)
        g = g_ref[...]
        for h in range(RET_HEADS):
            lgf = log_g(0, h)
            lgb = log_g(1, h)
            qh = q[:, h * RET_QK_DIM:(h + 1) * RET_QK_DIM]
            kh = k[:, h * RET_QK_DIM:(h + 1) * RET_QK_DIM]
            vh = v[:, h * RET_V_DIM:(h + 1) * RET_V_DIM]
            st_f = sf_ref[h]
            st_b = sball_ref[slot, h]
            o = _dot(_dot_nt(qh, kh) * decay_ref[h], vh)
            o = o + _dot(qh, st_f) * jnp.exp((pos + 1.0) * lgf)
            o = o + _dot(qh, st_b) * jnp.exp((L - pos) * lgb)
            sf_ref[h] = jnp.exp(L * lgf) * st_f + _dot_tn(kh * jnp.exp((L - 1.0 - pos) * lgf), vh)
            mu = jnp.mean(o, axis=-1, keepdims=True)
            var = jnp.mean(jnp.square(o - mu), axis=-1, keepdims=True)
            o = (o - mu) * lax.rsqrt(var + GN_EPS) * gn_ref[:, h * RET_V_DIM:(h + 1) * RET_V_DIM]
            gh = g[:, h * RET_V_DIM:(h + 1) * RET_V_DIM]
            o_ref[:, h * RET_V_DIM:(h + 1) * RET_V_DIM] = (_silu(gh) * o).astype(o_ref.dtype)


def retention(dm, p, dec, cos_t, sin_t, gn):
    L = RET_L
    ncc, nlc = dm.ctx // L, dm.s // L
    nc = ncc + nlc
    lat_blocks = dm.n_lat // L
    qk_w = RET_HEADS * RET_QK_DIM
    v_w = RET_HEADS * RET_V_DIM

    def order(ph, s):
        is_ctx = s < ncc
        c_ctx = jnp.where(ph == 0, ncc - 1 - s, s)
        c_lat = jnp.where(ph == 0, nlc - 1 - (s - ncc), s - ncc)
        return is_ctx, c_ctx, c_lat

    def rows(b, ph, s):
        is_ctx, c_ctx, c_lat = order(ph, s)
        return jnp.where(is_ctx, lat_blocks + b * ncc + c_ctx, b * nlc + c_lat)

    def table_rows(b, ph, s):
        is_ctx, c_ctx, c_lat = order(ph, s)
        return jnp.where(is_ctx, nlc + c_ctx, c_lat)

    def out_rows(b, ph, s):
        return jnp.where(ph == 0, rows(b, 1, 0), rows(b, 1, s))

    grid_spec = pltpu.PrefetchScalarGridSpec(
        num_scalar_prefetch=1,
        grid=(dm.b, 2, nc),
        in_specs=[pl.BlockSpec((L, qk_w), lambda b, ph, s, d: (rows(b, ph, s), 0)),
                  pl.BlockSpec((L, qk_w), lambda b, ph, s, d: (rows(b, ph, s), 1)),
                  pl.BlockSpec((L, v_w), lambda b, ph, s, d: (rows(b, ph, s), 1)),
                  pl.BlockSpec((L, v_w), lambda b, ph, s, d: (rows(b, ph, s), 2)),
                  pl.BlockSpec((L, LANES), lambda b, ph, s, d: (table_rows(b, ph, s), 0)),
                  pl.BlockSpec((L, LANES), lambda b, ph, s, d: (table_rows(b, ph, s), 0)),
                  pl.BlockSpec((1, v_w), lambda b, ph, s, d: (0, 0))],
        out_specs=pl.BlockSpec((L, v_w), lambda b, ph, s, d: (out_rows(b, ph, s), 0)),
        scratch_shapes=[pltpu.VMEM((RET_HEADS, RET_QK_DIM, RET_V_DIM), F32),
                        pltpu.VMEM((RET_HEADS, RET_QK_DIM, RET_V_DIM), F32),
                        pltpu.VMEM((nc, RET_HEADS, RET_QK_DIM, RET_V_DIM), F32),
                        pltpu.VMEM((RET_HEADS, L, L), F32)],
    )
    return pl.pallas_call(
        functools.partial(_ret_kernel, ncc=ncc, nlc=nlc),
        grid_spec=grid_spec,
        out_shape=jax.ShapeDtypeStruct((dm.n_all, v_w), BF16),
        compiler_params=_cparams(("parallel", "arbitrary", "arbitrary")),
        name="retention",
    )(dec, p, p, p, p, cos_t, sin_t, gn.reshape(1, v_w))


def _att_prep_kernel(q_ref, k_ref, v_ref, cos_ref, sin_ref, gq_ref, gk_ref, qo_ref, ko_ref, vo_ref,
                     *, lat_tiles):
    is_lat = pl.program_id(0) < lat_tiles
    cos = jnp.where(is_lat, cos_ref[...], 1.0)
    sin = jnp.where(is_lat, sin_ref[...], 0.0)

    hd = ATT_HEAD_DIM

    def prep(x_ref, gain_ref, h):
        xh = x_ref[:, h * hd:(h + 1) * hd]
        xn = xh * lax.rsqrt(jnp.mean(xh * xh, axis=-1, keepdims=True) + NORM_EPS) * gain_ref[...]
        return _rope_pairs(xn, cos, sin)

    for h in range(ATT_HEADS):
        qo_ref[h * hd:(h + 1) * hd, :] = (prep(q_ref, gq_ref, h) * ATT_SCORE_SCALE).T.astype(qo_ref.dtype)
    ones = jnp.ones((ATT_V_ROWS - hd, vo_ref.shape[1]), vo_ref.dtype)
    for h in range(ATT_KV_HEADS):
        ko_ref[:, h * hd:(h + 1) * hd] = prep(k_ref, gk_ref, h).astype(ko_ref.dtype)
        vo_ref[h * ATT_V_ROWS:h * ATT_V_ROWS + hd, :] = v_ref[:, h * hd:(h + 1) * hd].T.astype(vo_ref.dtype)
        vo_ref[h * ATT_V_ROWS + hd:(h + 1) * ATT_V_ROWS, :] = ones


def att_prep(dm, p, cos_t, sin_t, gq, gk):
    tm = dm.tile(TM_PREP)
    n = dm.n_all
    lat_tiles, per_batch = dm.n_lat // tm, dm.s // tm
    qw = ATT_HEADS * ATT_HEAD_DIM
    kw = ATT_KV_HEADS * ATT_HEAD_DIM
    q_blk, k_blk, v_blk = 1536 // qw, 2048 // kw, 2304 // kw
    tab = lambda i: (jnp.where(i < lat_tiles, i % per_batch, 0), 0)
    return pl.pallas_call(
        functools.partial(_att_prep_kernel, lat_tiles=lat_tiles),
        grid=(n // tm,),
        in_specs=[pl.BlockSpec((tm, qw), lambda i: (i, q_blk)),
                  pl.BlockSpec((tm, kw), lambda i: (i, k_blk)),
                  pl.BlockSpec((tm, kw), lambda i: (i, v_blk)),
                  pl.BlockSpec((tm, ATT_HEAD_DIM), tab),
                  pl.BlockSpec((tm, ATT_HEAD_DIM), tab),
                  pl.BlockSpec((1, ATT_HEAD_DIM), lambda i: (0, 0)),
                  pl.BlockSpec((1, ATT_HEAD_DIM), lambda i: (0, 0))],
        out_specs=[pl.BlockSpec((qw, tm), lambda i: (0, i)),
                   pl.BlockSpec((tm, kw), lambda i: (i, 0)),
                   pl.BlockSpec((ATT_KV_HEADS * ATT_V_ROWS, tm), lambda i: (0, i))],
        out_shape=[jax.ShapeDtypeStruct((qw, n), BF16),
                   jax.ShapeDtypeStruct((n, kw), BF16),
                   jax.ShapeDtypeStruct((ATT_KV_HEADS * ATT_V_ROWS, n), BF16)],
        compiler_params=_cparams(("parallel",)),
        name="att_prep",
    )(p, p, p, cos_t, sin_t, gq.reshape(1, -1), gk.reshape(1, -1))


def _attn_kernel(*refs, with_latent_keys):
    ns = ATT_STAGES
    if with_latent_keys:
        qt_ref, kl_ref, vtl_ref, kc_ref, vtc_ref, o_ref, acc_ref = refs[:7]
    else:
        qt_ref, kc_ref, vtc_ref, o_ref, acc_ref = refs[:5]
    s_refs, p_refs = refs[-2 * ns:-ns], refs[-ns:]
    hd = ATT_HEAD_DIM
    grp = ATT_HEADS // ATT_KV_HEADS
    tq = qt_ref.shape[1]
    nq = grp * tq
    tk = kc_ref.shape[0]
    n = 1 + (kl_ref.shape[0] // tk if with_latent_keys else 0)
    qt = jnp.concatenate([qt_ref[j * hd:(j + 1) * hd, :] for j in range(grp)], axis=1)

    def key_chunk(j):
        if isinstance(j, int) and j == 0:
            return kc_ref[...]
        return kl_ref[pl.ds(pl.multiple_of((j - 1) * tk, tk), tk), :]

    def value_chunk(j):
        if isinstance(j, int) and j == 0:
            return vtc_ref[...]
        return vtl_ref[:, pl.ds(pl.multiple_of((j - 1) * tk, tk), tk)]

    def max_keys(x):
        groups = math.gcd(tk // SUBLANES, ATT_REDUCE_GROUPS)
        part = jnp.max(x.reshape(groups, tk // groups, nq), axis=0)
        return jnp.max(part, axis=0, keepdims=True)

    def time_step(t, slot, carry, do_scores=True, do_softmax=True, do_values=True):
        m, alpha = carry
        if do_scores:
            s_refs[slot][...] = jnp.dot(key_chunk(t), qt, preferred_element_type=F32)
        if do_values:
            pv = jnp.dot(value_chunk(t - 2), p_refs[(slot - 2) % ns][...], preferred_element_type=F32)
            acc_ref[...] = acc_ref[...] * alpha + pv
        if do_softmax:
            st = s_refs[(slot - 1) % ns][...]
            m_new = jnp.maximum(m, max_keys(st))
            p_refs[(slot - 1) % ns][...] = jnp.exp2(st - m_new).astype(BF16)
            alpha = jnp.exp2(m - m_new)
            m = m_new
        return m, alpha

    acc_ref[...] = jnp.zeros_like(acc_ref)
    carry = (jnp.full((1, nq), -1e30, F32), jnp.zeros((1, nq), F32))
    fill = min(3, n + 2)
    for t in range(fill):
        carry = time_step(t, t % ns, carry, t < n, 0 <= t - 1 < n, 0 <= t - 2 < n)
    steady = max(n - fill, 0)
    body_steps = ns * ATT_UNROLL

    def unrolled(i, cr):
        for k in range(body_steps):
            cr = time_step(fill + body_steps * i + k, (fill + k) % ns, cr)
        return cr

    if steady // body_steps:
        carry = lax.fori_loop(0, steady // body_steps, unrolled, carry)
    for t in range(fill + body_steps * (steady // body_steps), max(n, fill)):
        carry = time_step(t, t % ns, carry)
    for t in range(max(n, fill), n + 2):
        carry = time_step(t, t % ns, carry, False, t - 1 < n, True)
    out = (acc_ref[0:hd, :] / acc_ref[hd:hd + 1, :]).T
    for j in range(grp):
        o_ref[:, j * hd:(j + 1) * hd] = out[j * tq:(j + 1) * tq].astype(o_ref.dtype)


def attention(dm, qt, kn, vt, latent):
    hd = ATT_HEAD_DIM
    grp = ATT_HEADS // ATT_KV_HEADS
    tk = dm.ctx
    assert dm.s % tk == 0
    ctx_blk0 = dm.n_lat // dm.ctx
    kc_spec = pl.BlockSpec((tk, hd), lambda b, h, i: (ctx_blk0 + b, h))
    vtc_spec = pl.BlockSpec((ATT_V_ROWS, tk), lambda b, h, i: (h, ctx_blk0 + b))
    if latent:
        tq = TQ_ATT
        nq = dm.s // tq
        q_spec = pl.BlockSpec((grp * hd, tq), lambda b, h, i: (h, b * nq + i))
        kl_spec = pl.BlockSpec((dm.s, hd), lambda b, h, i: (b, h))
        vtl_spec = pl.BlockSpec((ATT_V_ROWS, dm.s), lambda b, h, i: (h, b))
        in_specs = [q_spec, kl_spec, vtl_spec, kc_spec, vtc_spec]
        args = (qt, kn, vt, kn, vt)
        rows = dm.n_lat
    else:
        tq, nq = dm.ctx, 1
        q_spec = pl.BlockSpec((grp * hd, tq), lambda b, h, i: (h, ctx_blk0 + b))
        in_specs = [q_spec, kc_spec, vtc_spec]
        args = (qt, kn, vt)
        rows = dm.b * dm.ctx
    return pl.pallas_call(
        functools.partial(_attn_kernel, with_latent_keys=latent),
        grid=(dm.b, ATT_KV_HEADS, nq),
        in_specs=in_specs,
        out_specs=pl.BlockSpec((tq, grp * hd), lambda b, h, i: (b * nq + i, h)),
        out_shape=jax.ShapeDtypeStruct((rows, ATT_HEADS * hd), BF16),
        scratch_shapes=([pltpu.VMEM((ATT_V_ROWS, grp * tq), F32)]
                        + [pltpu.VMEM((tk, grp * tq), F32)] * ATT_STAGES
                        + [pltpu.VMEM((tk, grp * tq), BF16)] * ATT_STAGES),
        compiler_params=_cparams(("parallel", "parallel", "arbitrary")),
        name="attention_latent" if latent else "attention_ctx",
    )(*args)


def _pool_kernel(prev_ref, u_ref, next_ref, w_ref, sc_ref, o_ref, a_ref, b_ref, *, dm):
    tm = u_ref.shape[0]
    H = POOL_HALO
    i = pl.program_id(0)
    lat_tiles = dm.n_lat // tm
    is_lat = i < lat_tiles
    n_seg = jnp.where(is_lat, dm.s, dm.ctx)
    base = jnp.where(is_lat, (i * tm) % dm.s, 0)
    r = lax.broadcasted_iota(jnp.int32, (tm + 2 * H, 1), 0) - H
    pos = jnp.where(is_lat, base + r, r % dm.ctx)
    win = pl.ds(H, tm + 2 * H)

    def shifted(ref, d):
        return ref[pl.ds(H + d, tm + 2 * H), :]

    zeros = jnp.zeros((H, POOL_GROUP_DIM), F32)
    for ref in (a_ref, b_ref):
        ref[pl.ds(0, H), :] = zeros
        ref[pl.ds(tm + 3 * H, H), :] = zeros

    for gi, w in enumerate(POOL_WINDOWS):
        cols = slice(gi * POOL_GROUP_DIM, (gi + 1) * POOL_GROUP_DIM)
        half = w // 2
        a_ref[pl.ds(H, H), :] = prev_ref[:, cols]
        a_ref[pl.ds(2 * H, tm), :] = u_ref[:, cols]
        a_ref[pl.ds(2 * H + tm, H), :] = next_ref[:, cols]
        b_ref[win, :] = jnp.where(pos >= 1, shifted(a_ref, -1), 0.0)
        step = 1
        while step < half:
            b_ref[win, :] = b_ref[win, :] + jnp.where(pos >= step, shifted(b_ref, -step), 0.0)
            step *= 2
        step = 1
        while step < half:
            a_ref[win, :] = a_ref[win, :] + jnp.where(pos + step < n_seg, shifted(a_ref, step), 0.0)
            step *= 2
        tile = pl.ds(2 * H, tm)
        pt = pos[H:H + tm]
        cnt = (jnp.minimum(pt + half, n_seg) - jnp.maximum(pt - half, 0)).astype(F32)
        u = u_ref[:, cols]
        diff = (a_ref[tile, :] + b_ref[tile, :]) / cnt - u
        o_ref[:, cols] = (_dot(diff, w_ref[gi]) * sc_ref[:, cols]).astype(o_ref.dtype)


def pool_mixer(dm, p, pool_w, pool_scale):
    tm = dm.tile(TM_POOL)
    n = dm.n_all
    H = POOL_HALO
    width = len(POOL_WINDOWS) * POOL_GROUP_DIM
    col = 2560 // width
    per = tm // H
    last = n // H - 1
    return pl.pallas_call(
        functools.partial(_pool_kernel, dm=dm),
        grid=(n // tm,),
        in_specs=[pl.BlockSpec((H, width), lambda i: (jnp.maximum(i * per - 1, 0), col)),
                  pl.BlockSpec((tm, width), lambda i: (i, col)),
                  pl.BlockSpec((H, width), lambda i: (jnp.minimum((i + 1) * per, last), col)),
                  pl.BlockSpec(pool_w.shape, lambda i: (0, 0, 0)),
                  pl.BlockSpec((1, width), lambda i: (0, 0))],
        out_specs=pl.BlockSpec((tm, width), lambda i: (i, 0)),
        out_shape=jax.ShapeDtypeStruct((n, width), BF16),
        scratch_shapes=[pltpu.VMEM((tm + 4 * H, POOL_GROUP_DIM), F32),
                        pltpu.VMEM((tm + 4 * H, POOL_GROUP_DIM), F32)],
        compiler_params=_cparams(("parallel",)),
        name="pool_mixer",
    )(p, p, p, pool_w, pool_scale.reshape(1, width))


def _merge_kernel(yr_ref, ya_ref, yp_ref, gl_ref, x_ref, mod_ref, wb_ref, wo_ref, o_ref):
    d = x_ref.shape[1]
    mixed = None
    for j, y_ref in enumerate((yr_ref, ya_ref, yp_ref)):
        gate = _sigmoid(gl_ref[:, j * d:(j + 1) * d])
        term = gate * jnp.dot(y_ref[...], wb_ref[j], preferred_element_type=F32)
        mixed = term if mixed is None else mixed + term
    y = _dot(mixed, wo_ref[...])
    o_ref[...] = x_ref[...] + mod_ref[0][2:3] * y


def merge(dm, n_rows, yr, ya, yp, p, x, mod, wb, wo):
    tm = dm.tile(TM_MERGE)
    d = dm.d
    bw = yr.shape[1]
    row = dm.mod_row(tm)
    return pl.pallas_call(
        _merge_kernel,
        grid=(n_rows // tm,),
        in_specs=[pl.BlockSpec((tm, bw), lambda i: (i, 0)),
                  pl.BlockSpec((tm, bw), lambda i: (i, 0)),
                  pl.BlockSpec((tm, bw), lambda i: (i, 0)),
                  pl.BlockSpec((tm, 3 * d), lambda i: (i, 1)),
                  pl.BlockSpec((tm, d), lambda i: (i, 0)),
                  pl.BlockSpec((1, 6, d), lambda i: (row(i), 0, 0)),
                  pl.BlockSpec(wb.shape, lambda i: (0, 0, 0)),
                  pl.BlockSpec(wo.shape, lambda i: (0, 0))],
        out_specs=pl.BlockSpec((tm, d), lambda i: (i, 0)),
        out_shape=jax.ShapeDtypeStruct((n_rows, d), F32),
        compiler_params=_cparams(("parallel",)),
        name="merge",
    )(yr, ya, yp, p, x, mod, wb, wo)


def _ffn_kernel(x_ref, mod_ref, gain_ref, w1_ref, w3_ref, w2_ref, o_ref, h_ref, acc_ref):
    f = pl.program_id(1)
    m = mod_ref[0]

    @pl.when(f == 0)
    def _():
        h_ref[...] = _norm_mod(x_ref[...], gain_ref[...], m[3:4], m[4:5]).astype(BF16)
        acc_ref[...] = jnp.zeros_like(acc_ref)

    h = h_ref[...]
    a = jnp.dot(h, w1_ref[...], preferred_element_type=F32)
    b = jnp.dot(h, w3_ref[...], preferred_element_type=F32)
    acc_ref[...] += _dot(_silu(a) * b, w2_ref[...])

    @pl.when(f == pl.num_programs(1) - 1)
    def _():
        o_ref[...] = x_ref[...] + m[5:6] * acc_ref[...]


def ffn_dense(dm, x, mod, gain, w1, w3, w2):
    n, d = x.shape
    fdim = w1.shape[1]
    tm, tf = dm.tile(TM_FFN), TF_FFN
    row = dm.mod_row(tm)
    return pl.pallas_call(
        _ffn_kernel,
        grid=(n // tm, fdim // tf),
        in_specs=[pl.BlockSpec((tm, d), lambda i, f: (i, 0)),
                  pl.BlockSpec((1, 6, d), lambda i, f: (row(i), 0, 0)),
                  pl.BlockSpec((1, d), lambda i, f: (0, 0)),
                  pl.BlockSpec((d, tf), lambda i, f: (0, f)),
                  pl.BlockSpec((d, tf), lambda i, f: (0, f)),
                  pl.BlockSpec((tf, d), lambda i, f: (f, 0))],
        out_specs=pl.BlockSpec((tm, d), lambda i, f: (i, 0)),
        out_shape=jax.ShapeDtypeStruct((n, d), F32),
        scratch_shapes=[pltpu.VMEM((tm, d), BF16), pltpu.VMEM((tm, d), F32)],
        compiler_params=_cparams(("parallel", "arbitrary")),
        name="ffn_dense",
    )(x, mod, gain.reshape(1, d), w1, w3, w2)


def _router_kernel(x_ref, mod_ref, gain_ref, wr_ref, br_ref, h_ref, slot_ref, slot_t_ref, wt_ref, cum_ref):
    tm = x_ref.shape[0]
    m = mod_ref[0]
    h = _norm_mod(x_ref[...], gain_ref[...], m[3:4], m[4:5])
    h_ref[...] = h.astype(BF16)
    logits = _dot3(h, wr_ref[...]) + br_ref[...]
    lane = lax.broadcasted_iota(jnp.int32, logits.shape, 1)
    m1 = jnp.max(logits, axis=-1, keepdims=True)
    i1 = jnp.min(jnp.where(logits == m1, lane, LANES), axis=-1, keepdims=True)
    rest = jnp.where(lane == i1, -jnp.inf, logits)
    m2 = jnp.max(rest, axis=-1, keepdims=True)
    i2 = jnp.min(jnp.where(rest == m2, lane, LANES), axis=-1, keepdims=True)
    e = jnp.exp(m2 - m1)
    w1 = 1.0 / (1.0 + e)
    w2 = e / (1.0 + e)
    wt_ref[...] = jnp.where(lane == i1, w1, jnp.where(lane == i2, w2, 0.0))
    routed = jnp.where((lane == i1) | (lane == i2), 1.0, 0.0)
    sub = SUB_MOE
    lower = (lax.broadcasted_iota(jnp.int32, (sub, sub), 0) > lax.broadcasted_iota(jnp.int32, (sub, sub), 1))
    lower = lower.astype(BF16)
    offset = jnp.zeros((1, LANES), F32)
    cum_ref[...] = jnp.zeros_like(cum_ref)
    for blk in range(tm // sub):
        rows = slice(blk * sub, (blk + 1) * sub)
        mask = routed[rows]
        cum_ref[0, blk:blk + 1, :] = offset.astype(jnp.int32)
        rank = jnp.dot(lower, mask.astype(BF16), preferred_element_type=F32) + offset
        slot_ref[rows, :] = jnp.where(mask > 0.0, rank, -1.0)
        count = jnp.sum(mask, axis=0, keepdims=True)
        offset = offset + jnp.ceil(count * (1.0 / SLOT_ALIGN)) * SLOT_ALIGN
    cum_ref[0, tm // sub:tm // sub + 1, :] = offset.astype(jnp.int32)
    slot_t_ref[...] = slot_ref[...].T[:SUBLANES, :]


def moe_route(dm, n_rows, x, mod, gain, wr_pad, br_pad):
    tm = dm.tile(TM_MOE)
    d = dm.d
    row = dm.mod_row(tm)
    nt = n_rows // tm
    cum_rows = -(-(tm // SUB_MOE + 1) // SUBLANES) * SUBLANES
    return pl.pallas_call(
        _router_kernel,
        grid=(nt,),
        in_specs=[pl.BlockSpec((tm, d), lambda i: (i, 0)),
                  pl.BlockSpec((1, 6, d), lambda i: (row(i), 0, 0)),
                  pl.BlockSpec((1, d), lambda i: (0, 0)),
                  pl.BlockSpec((d, LANES), lambda i: (0, 0)),
                  pl.BlockSpec((1, LANES), lambda i: (0, 0))],
        out_specs=[pl.BlockSpec((tm, d), lambda i: (i, 0)),
                   pl.BlockSpec((tm, LANES), lambda i: (i, 0)),
                   pl.BlockSpec((SUBLANES, tm), lambda i: (0, i)),
                   pl.BlockSpec((tm, LANES), lambda i: (i, 0)),
                   pl.BlockSpec((1, cum_rows, LANES), lambda i: (i, 0, 0))],
        out_shape=[jax.ShapeDtypeStruct((n_rows, d), BF16),
                   jax.ShapeDtypeStruct((n_rows, LANES), F32),
                   jax.ShapeDtypeStruct((SUBLANES, n_rows), F32),
                   jax.ShapeDtypeStruct((n_rows, LANES), F32),
                   jax.ShapeDtypeStruct((nt, cum_rows, LANES), jnp.int32)],
        compiler_params=_cparams(("parallel",)),
        name="moe_router",
    )(x, mod, gain.reshape(1, d), wr_pad, br_pad)


def _moe_kernel(cum_ref, h_ref, slot_ref, slot_t_ref, wt_ref, w1_ref, w3_ref, w2_ref, o_ref,
                xs_ref, acc_ref):
    i, e, f = pl.program_id(0), pl.program_id(1), pl.program_id(2)
    tm, d = h_ref.shape
    rb, win, sub = RB_MOE, WIN_MOE, SUB_MOE
    nsub = tm // sub

    def base(j):
        return cum_ref[i, j * N_EXPERTS + e]

    nblk = (base(nsub) + rb - 1) // rb

    def block_rows(r):
        return pl.ds(pl.multiple_of(r * rb, 32), rb)

    def windows(j):
        return base(j), (base(j + 1) - base(j) + win - 1) // win

    def window_rows(start, w):
        return pl.ds(pl.multiple_of(start + w * win, SLOT_ALIGN), win)

    @pl.when((e == 0) & (f == 0))
    def _():
        o_ref[...] = jnp.zeros_like(o_ref)

    @pl.when(f == 0)
    def _():
        def clear(r, carry):
            xs_ref[block_rows(r), :] = jnp.zeros((rb, d), BF16)
            acc_ref[block_rows(r), :] = jnp.zeros((rb, d), F32)
            return carry

        lax.fori_loop(0, nblk, clear, 0)
        acc_ref[pl.ds(pl.multiple_of(nblk * rb, 32), win), :] = jnp.zeros((win, d), F32)

        for j in range(nsub):
            tok = slice(j * sub, (j + 1) * sub)
            start, nwin = windows(j)

            def gather(w, carry, tok=tok, start=start):
                rows = window_rows(start, w)
                want = (lax.broadcasted_iota(jnp.int32, (win, sub), 0) + (start + w * win)).astype(F32)
                onehot = (slot_t_ref[pl.ds(e, 1), tok] == want).astype(BF16)
                xs_ref[rows, :] = jnp.dot(onehot, h_ref[tok, :], preferred_element_type=F32).astype(BF16)
                return carry

            lax.fori_loop(0, nwin, gather, 0)

    def expert(r, carry):
        xs = xs_ref[block_rows(r), :]
        a = jnp.dot(xs, w1_ref[0], preferred_element_type=F32)
        b = jnp.dot(xs, w3_ref[0], preferred_element_type=F32)
        acc_ref[block_rows(r), :] += _dot(_silu(a) * b, w2_ref[0])
        return carry

    lax.fori_loop(0, nblk, expert, 0)

    @pl.when(f == pl.num_programs(2) - 1)
    def _():
        pick = lax.broadcasted_iota(jnp.int32, (sub, LANES), 1) == e
        for j in range(nsub):
            tok = slice(j * sub, (j + 1) * sub)
            start, nwin = windows(j)
            slot_col = jnp.sum(jnp.where(pick, slot_ref[tok, :], 0.0), axis=-1, keepdims=True)
            wt_col = jnp.sum(jnp.where(pick, wt_ref[tok, :], 0.0), axis=-1, keepdims=True)

            def scatter(w, carry, tok=tok, start=start, slot_col=slot_col, wt_col=wt_col):
                ys = acc_ref[window_rows(start, w), :].astype(BF16)
                want = (lax.broadcasted_iota(jnp.int32, (sub, win), 1) + (start + w * win)).astype(F32)
                onehot = (slot_col == want).astype(BF16)
                o_ref[tok, :] += wt_col * jnp.dot(onehot, ys, preferred_element_type=F32)
                return carry

            lax.fori_loop(0, nwin, scatter, 0)


def moe_experts(dm, n_rows, h, slot, slot_t, wt, counts, w1, w3, w2):
    tm = dm.tile(TM_MOE)
    d = dm.d
    ne, _, fdim = w1.shape
    tf = TF_MOE if fdim % TF_MOE == 0 else TF_FFN
    assert SUB_MOE % SLOT_ALIGN == 0 and tm % SUB_MOE == 0
    cap = tm + RB_MOE + WIN_MOE
    once = dict(pipeline_mode=pl.Buffered(1))
    grid_spec = pltpu.PrefetchScalarGridSpec(
        num_scalar_prefetch=1,
        grid=(n_rows // tm, ne, fdim // tf),
        in_specs=[pl.BlockSpec((tm, d), lambda i, e, f, c: (i, 0), **once),
                  pl.BlockSpec((tm, LANES), lambda i, e, f, c: (i, 0), **once),
                  pl.BlockSpec((SUBLANES, tm), lambda i, e, f, c: (0, i), **once),
                  pl.BlockSpec((tm, LANES), lambda i, e, f, c: (i, 0), **once),
                  pl.BlockSpec((1, d, tf), lambda i, e, f, c: (e, 0, f)),
                  pl.BlockSpec((1, d, tf), lambda i, e, f, c: (e, 0, f)),
                  pl.BlockSpec((1, tf, d), lambda i, e, f, c: (e, f, 0))],
        out_specs=pl.BlockSpec((tm, d), lambda i, e, f, c: (i, 0), **once),
        scratch_shapes=[pltpu.VMEM((cap, d), BF16), pltpu.VMEM((cap, d), F32)],
    )
    return pl.pallas_call(
        _moe_kernel,
        grid_spec=grid_spec,
        out_shape=jax.ShapeDtypeStruct((n_rows, d), F32),
        compiler_params=_cparams(("parallel", "arbitrary", "arbitrary")),
        name="moe_experts",
    )(counts, h, slot, slot_t, wt, w1, w3, w2)


def _residual_kernel(x_ref, y_ref, mod_ref, o_ref):
    o_ref[...] = x_ref[...] + mod_ref[0][5:6] * y_ref[...]


def _final_kernel(x_ref, y_ref, mod_ref, gain_ref, o_ref):
    x = x_ref[...] + mod_ref[0][5:6] * y_ref[...]
    o_ref[...] = x * lax.rsqrt(jnp.mean(x * x, axis=-1, keepdims=True) + NORM_EPS) * gain_ref[...]


def residual(dm, n_rows, x, y, mod, final_gain=None):
    tm = dm.tile(TM_FFN)
    d = dm.d
    row = dm.mod_row(tm)
    in_specs = [pl.BlockSpec((tm, d), lambda i: (i, 0)),
                pl.BlockSpec((tm, d), lambda i: (i, 0)),
                pl.BlockSpec((1, 6, d), lambda i: (row(i), 0, 0))]
    args = [x, y, mod]
    if final_gain is not None:
        in_specs.append(pl.BlockSpec((1, d), lambda i: (0, 0)))
        args.append(final_gain.reshape(1, d))
    return pl.pallas_call(
        _residual_kernel if final_gain is None else _final_kernel,
        grid=(n_rows // tm,),
        in_specs=in_specs,
        out_specs=pl.BlockSpec((tm, d), lambda i: (i, 0)),
        out_shape=jax.ShapeDtypeStruct((n_rows, d), F32),
        compiler_params=_cparams(("parallel",)),
        name="residual" if final_gain is None else "residual_final_norm",
    )(*args)


def _final_norm_kernel(x_ref, gain_ref, o_ref):
    x = x_ref[...]
    o_ref[...] = x * lax.rsqrt(jnp.mean(x * x, axis=-1, keepdims=True) + NORM_EPS) * gain_ref[...]


def final_rmsnorm(dm, n_rows, x, gain):
    tm = dm.tile(TM_FFN)
    d = dm.d
    return pl.pallas_call(
        _final_norm_kernel,
        grid=(n_rows // tm,),
        in_specs=[pl.BlockSpec((tm, d), lambda i: (i, 0)), pl.BlockSpec((1, d), lambda i: (0, 0))],
        out_specs=pl.BlockSpec((tm, d), lambda i: (i, 0)),
        out_shape=jax.ShapeDtypeStruct((n_rows, d), F32),
        compiler_params=_cparams(("parallel",)),
        name="final_norm",
    )(x, gain.reshape(1, d))


def _rope_tables(s, ctx):
    def angles(pos, dim):
        freqs = 1.0 / (ROPE_THETA ** (jnp.arange(0, dim, 2, dtype=F32) / dim))
        return pos[:, None] * freqs[None, :]

    rows = s // GRID_W
    row = jnp.repeat(jnp.arange(rows, dtype=F32), GRID_W)
    col = jnp.tile(jnp.arange(GRID_W, dtype=F32), rows)
    half = ATT_HEAD_DIM // 2
    ang_att = jnp.concatenate([angles(row, half), angles(col, half)], axis=-1)
    sign = jnp.tile(jnp.array([-1.0, 1.0], F32), LANES // 2)

    def tables(ang):
        reps = LANES // (2 * ang.shape[1])
        cos = jnp.tile(jnp.repeat(jnp.cos(ang), 2, axis=-1), (1, reps))
        sin = jnp.tile(jnp.repeat(jnp.sin(ang), 2, axis=-1), (1, reps)) * sign
        return cos, sin

    ang_ret = angles(jnp.arange(s, dtype=F32), RET_QK_DIM)
    ang_ret = jnp.concatenate([ang_ret, jnp.zeros((ctx, RET_QK_DIM // 2), F32)], axis=0)
    return tables(ang_att) + tables(ang_ret)


def kernel(x, c, ctx, c_ctx, w_ada, b_ada, norm_mix, norm_ffn, w_in, ret_decay, ret_gn, attn_qn, attn_kn, pool_w, pool_scale, w_branch, w_out, ffn_w1, ffn_w3, ffn_w2, moe_router, moe_router_b, moe_w1, moe_w3, moe_w2, final_norm):
    b, s, d = x.shape
    n_ctx = ctx.shape[1]
    depth = w_in.shape[0]
    dm = Dims(b, s, n_ctx, d)

    xa = jnp.concatenate([x.reshape(b * s, d), ctx.reshape(b * n_ctx, d)], axis=0)
    mod_rows = 2 * SUBLANES
    cc = jnp.concatenate([c, c_ctx[None, :], jnp.zeros((mod_rows - b - 1, d), F32)], axis=0)
    mods = ada_table(cc, w_ada, b_ada).reshape(depth, mod_rows, 6, d)

    att_cos, att_sin, ret_cos, ret_sin = _rope_tables(s, n_ctx)

    for i in range(depth):
        need_ctx = i < depth - 1
        n_rows = dm.n_all if need_ctx else dm.n_lat
        mod = mods[i]
        p = proj_in(dm, xa, mod, norm_mix[i], w_in[i].astype(BF16))
        y_ret = retention(dm, p, ret_decay[i], ret_cos, ret_sin, ret_gn[i])
        y_pool = pool_mixer(dm, p, pool_w[i].astype(BF16), pool_scale[i])
        qt, kn, vt = att_prep(dm, p, att_cos, att_sin, attn_qn[i], attn_kn[i])
        y_att = attention(dm, qt, kn, vt, latent=True)
        if need_ctx:
            y_att = jnp.concatenate([y_att, attention(dm, qt, kn, vt, latent=False)], axis=0)
        xa = merge(dm, n_rows, y_ret, y_att, y_pool, p, xa, mod,
                   w_branch[i].astype(BF16), w_out[i].astype(BF16))
        j = i // 2
        last = i == depth - 1
        if i % 2 == 0:
            xa = ffn_dense(dm, xa, mod, norm_ffn[i], ffn_w1[j].astype(BF16), ffn_w3[j].astype(BF16),
                           ffn_w2[j].astype(BF16))
            if last:
                xa = final_rmsnorm(dm, n_rows, xa, final_norm)
        else:
            wr = jnp.zeros((d, LANES), F32).at[:, :N_EXPERTS].set(moe_router[j])
            br = jnp.full((1, LANES), -1e30, F32).at[0, :N_EXPERTS].set(moe_router_b[j])
            h, slot, slot_t, wt, cnt = moe_route(dm, n_rows, xa, mod, norm_ffn[i], wr, br)
            nsub = dm.tile(TM_MOE) // SUB_MOE
            cum = cnt[:, :nsub + 1, :N_EXPERTS].reshape(cnt.shape[0], (nsub + 1) * N_EXPERTS)
            y = moe_experts(dm, n_rows, h, slot, slot_t, wt, cum,
                            moe_w1[j].astype(BF16), moe_w3[j].astype(BF16), moe_w2[j].astype(BF16))
            xa = residual(dm, n_rows, xa, y, mod, final_norm if last else None)
    return xa[:dm.n_lat].reshape(b, s, d)
```

```python
import functools
import math

import numpy as np
import jax
import jax.numpy as jnp
from jax import lax
from jax.experimental import pallas as pl
from jax.experimental.pallas import tpu as pltpu

F32 = jnp.float32
BF16 = jnp.bfloat16

GRID_W = 64
RET_HEADS = 4
RET_QK_DIM = 64
RET_V_DIM = 128
ATT_HEADS = 4
ATT_KV_HEADS = 2
ATT_HEAD_DIM = 128
ROPE_THETA = 10000.0
POOL_WINDOWS = (2, 4, 8, 16)
POOL_GROUP_DIM = 128
N_EXPERTS = 8
NORM_EPS = 1e-6
GN_EPS = 1e-5

LANES = 128
SUBLANES = 8
VMEM_LIMIT_BYTES = 56 * 1024 * 1024

TM_PROJ = 1024
TN_PROJ = 1536
TM_PREP = 1024
TM_POOL = 1024
TM_MERGE = 512
TM_FFN = 1024
TF_FFN = 512
TM_MOE = 2048
TF_MOE = 896
RB_MOE = 256
WIN_MOE = 256
SUB_MOE = 512
SLOT_ALIGN = 16
assert RB_MOE % 64 == 0
RET_L = 256
TQ_ATT = 512
ATT_SCORE_SCALE = (ATT_HEAD_DIM ** -0.5) * math.log2(math.e)
ATT_V_ROWS = ATT_HEAD_DIM + 16
ATT_STAGES = 3
ATT_UNROLL = 4
ATT_REDUCE_GROUPS = 8
POOL_HALO = 8


def _cparams(sem):
    return pltpu.CompilerParams(dimension_semantics=sem, vmem_limit_bytes=VMEM_LIMIT_BYTES)


def _dot(a, b):
    return jnp.dot(a.astype(BF16), b.astype(BF16), preferred_element_type=F32)


def _dot_nt(a, b):
    return lax.dot_general(a.astype(BF16), b.astype(BF16), (((1,), (1,)), ((), ())),
                           preferred_element_type=F32)


def _dot_tn(a, b):
    return lax.dot_general(a.astype(BF16), b.astype(BF16), (((0,), (0,)), ((), ())),
                           preferred_element_type=F32)


def _split_bf16(a):
    hi = a.astype(BF16)
    lo = (a - hi.astype(F32)).astype(BF16)
    return hi, lo


def _dot3(a, b):
    ah, al = _split_bf16(a)
    bh, bl = _split_bf16(b)
    d = functools.partial(jnp.dot, preferred_element_type=F32)
    return d(ah, bh) + (d(al, bh) + d(ah, bl))


def _norm_mod(x, gain, shift, scale):
    y = x * lax.rsqrt(jnp.mean(x * x, axis=-1, keepdims=True) + NORM_EPS)
    return (y * gain) * (1.0 + scale) + shift


def _sigmoid(x):
    return 1.0 / (1.0 + jnp.exp(-x))


def _silu(x):
    return x * _sigmoid(x)


class Dims:
    def __init__(self, b, s, ctx, d):
        self.b, self.s, self.ctx, self.d = b, s, ctx, d
        self.n_lat = b * s
        self.n_all = b * s + b * ctx

    def tile(self, preferred):
        t = preferred
        while self.s % t or (self.b * self.ctx) % t:
            t //= 2
        return t

    def mod_row(self, tm):
        lat_tiles, per_batch, b = self.n_lat // tm, self.s // tm, self.b
        return lambda i: jnp.where(i < lat_tiles, i // per_batch, b)


def _ada_kernel(c_ref, w_ref, b_ref, o_ref):
    o_ref[0] = _dot3(_silu(c_ref[...]), w_ref[0]) + b_ref[0]


def ada_table(cc, w_ada, b_ada):
    depth, d, n6 = w_ada.shape
    tn = n6 // 4
    rows = cc.shape[0]
    return pl.pallas_call(
        _ada_kernel,
        grid=(depth, n6 // tn),
        in_specs=[pl.BlockSpec((rows, d), lambda l, j: (0, 0)),
                  pl.BlockSpec((1, d, tn), lambda l, j: (l, 0, j)),
                  pl.BlockSpec((1, 1, tn), lambda l, j: (l, 0, j))],
        out_specs=pl.BlockSpec((1, rows, tn), lambda l, j: (l, 0, j)),
        out_shape=jax.ShapeDtypeStruct((depth, rows, n6), F32),
        compiler_params=_cparams(("arbitrary", "arbitrary")),
        name="ada_table",
    )(cc, w_ada, b_ada.reshape(depth, 1, n6))


def _row_sources(dm, tm, parts):
    d = dm.d
    if len(parts) == 1:
        return [pl.BlockSpec((tm, d), lambda i, *_: (i, 0))]
    lat_tiles = dm.n_lat // tm
    return [pl.BlockSpec((tm, d), lambda i, *_: (jnp.minimum(i, lat_tiles - 1), 0)),
            pl.BlockSpec((tm, d), lambda i, *_: (jnp.maximum(i - lat_tiles, 0), 0))]


def _for_row_source(x_refs, lat_tiles, fn):
    if len(x_refs) == 1:
        fn(x_refs[0])
        return
    is_lat = pl.program_id(0) < lat_tiles
    pl.when(is_lat)(lambda: fn(x_refs[0]))
    pl.when(jnp.logical_not(is_lat))(lambda: fn(x_refs[1]))


def _proj_in_kernel(*refs, mixer_blocks, lat_tiles):
    mod_ref, gain_ref, w_ref, o_ref, g_ref, h_ref = refs[-6:]
    x_refs = refs[:-6]
    j = pl.program_id(1)

    @pl.when(j == 0)
    def _():
        def fill(x_ref):
            m = mod_ref[0]
            h_ref[...] = _norm_mod(x_ref[...], gain_ref[...], m[0:1], m[1:2]).astype(BF16)

        _for_row_source(x_refs, lat_tiles, fill)

    y = jnp.dot(h_ref[...], w_ref[...], preferred_element_type=F32)

    @pl.when(j < mixer_blocks)
    def _():
        o_ref[...] = y

    @pl.when(j >= mixer_blocks)
    def _():
        g_ref[...] = _sigmoid(y).astype(g_ref.dtype)


def proj_in(dm, x_parts, mod, gain, w):
    n, d = dm.n_all, dm.d
    nout = w.shape[1]
    gate_w = 3 * d
    tm, tn = dm.tile(TM_PROJ), TN_PROJ
    assert (nout - gate_w) % tn == 0 and gate_w % tn == 0
    mixer_blocks = (nout - gate_w) // tn
    row = dm.mod_row(tm)
    return pl.pallas_call(
        functools.partial(_proj_in_kernel, mixer_blocks=mixer_blocks, lat_tiles=dm.n_lat // tm),
        grid=(n // tm, nout // tn),
        in_specs=_row_sources(dm, tm, x_parts) + [
            pl.BlockSpec((1, 6, d), lambda i, j: (row(i), 0, 0)),
            pl.BlockSpec((1, d), lambda i, j: (0, 0)),
            pl.BlockSpec((d, tn), lambda i, j: (0, j))],
        out_specs=[pl.BlockSpec((tm, tn), lambda i, j: (i, jnp.minimum(j, mixer_blocks - 1))),
                   pl.BlockSpec((tm, tn), lambda i, j: (i, jnp.maximum(j - mixer_blocks, 0)))],
        out_shape=[jax.ShapeDtypeStruct((n, nout - gate_w), F32),
                   jax.ShapeDtypeStruct((n, gate_w), BF16)],
        scratch_shapes=[pltpu.VMEM((tm, d), BF16)],
        compiler_params=_cparams(("parallel", "arbitrary")),
        name="proj_in",
    )(*x_parts, mod, gain.reshape(1, d), w)


def _rope_pairs(x, cos, sin):
    lane = lax.broadcasted_iota(jnp.int32, x.shape, 1)
    partner = jnp.where(lane % 2 == 0, pltpu.roll(x, LANES - 1, 1), pltpu.roll(x, 1, 1))
    return x * cos + partner * sin


def _ret_rope(x, cos, sin):
    return jnp.concatenate([_rope_pairs(x[:, j * LANES:(j + 1) * LANES], cos, sin)
                            for j in range(x.shape[1] // LANES)], axis=1)


def _ret_kernel(dec_ref, q_ref, k_ref, v_ref, g_ref, cos_ref, sin_ref, gn_ref, o_ref,
                sf_ref, sb_ref, sball_ref, decay_ref, *, ncc, nlc):
    phase = pl.program_id(1)
    s = pl.program_id(2)
    L = RET_L
    pos = lax.broadcasted_iota(jnp.int32, (L, 1), 0).astype(F32)
    cos = cos_ref[...]
    sin = sin_ref[...]
    k = _ret_rope(k_ref[...], cos, sin) * (RET_QK_DIM ** -0.5)
    v = v_ref[...]

    def log_g(direction, h):
        return -jnp.exp(jnp.full((1, 1), dec_ref[direction, h], F32))

    @pl.when(phase == 0)
    def _():
        @pl.when(s == 0)
        def _():
            sb_ref[...] = jnp.zeros_like(sb_ref)
            diff = (lax.broadcasted_iota(jnp.int32, (L, L), 0)
                    - lax.broadcasted_iota(jnp.int32, (L, L), 1)).astype(F32)
            for h in range(RET_HEADS):
                decay_ref[h] = (jnp.where(diff >= 0, jnp.exp(jnp.maximum(diff, 0.0) * log_g(0, h)), 0.0)
                                + jnp.where(diff <= 0, jnp.exp(jnp.maximum(-diff, 0.0) * log_g(1, h)), 0.0))

        for h in range(RET_HEADS):
            lgb = log_g(1, h)
            kh = k[:, h * RET_QK_DIM:(h + 1) * RET_QK_DIM]
            vh = v[:, h * RET_V_DIM:(h + 1) * RET_V_DIM]
            st = sb_ref[h]
            sball_ref[s, h] = st
            sb_ref[h] = jnp.exp(L * lgb) * st + _dot_tn(kh * jnp.exp(pos * lgb), vh)

    @pl.when(phase == 1)
    def _():
        @pl.when(s == 0)
        def _():
            sf_ref[...] = jnp.zeros_like(sf_ref)

        slot = jnp.where(s < ncc, ncc - 1 - s, ncc + nlc - 1 - (s - ncc))
        q = _ret_rope(q_ref[...], cos, sin)
        g = g_ref[...]
        for h in range(RET_HEADS):
            lgf = log_g(0, h)
            lgb = log_g(1, h)
            qh = q[:, h * RET_QK_DIM:(h + 1) * RET_QK_DIM]
            kh = k[:, h * RET_QK_DIM:(h + 1) * RET_QK_DIM]
            vh = v[:, h * RET_V_DIM:(h + 1) * RET_V_DIM]
            st_f = sf_ref[h]
            st_b = sball_ref[slot, h]
            o = _dot(_dot_nt(qh, kh) * decay_ref[h], vh)
            o = o + _dot(qh, st_f) * jnp.exp((pos + 1.0) * lgf)
            o = o + _dot(qh, st_b) * jnp.exp((L - pos) * lgb)
            sf_ref[h] = jnp.exp(L * lgf) * st_f + _dot_tn(kh * jnp.exp((L - 1.0 - pos) * lgf), vh)
            mu = jnp.mean(o, axis=-1, keepdims=True)
            var = jnp.mean(jnp.square(o - mu), axis=-1, keepdims=True)
            o = (o - mu) * lax.rsqrt(var + GN_EPS) * gn_ref[:, h * RET_V_DIM:(h + 1) * RET_V_DIM]
            gh = g[:, h * RET_V_DIM:(h + 1) * RET_V_DIM]
            o_ref[:, h * RET_V_DIM:(h + 1) * RET_V_DIM] = (_silu(gh) * o).astype(o_ref.dtype)


def retention(dm, p, dec, cos_t, sin_t, gn):
    L = RET_L
    ncc, nlc = dm.ctx // L, dm.s // L
    nc = ncc + nlc
    lat_blocks = dm.n_lat // L
    qk_w = RET_HEADS * RET_QK_DIM
    v_w = RET_HEADS * RET_V_DIM

    def order(ph, s):
        is_ctx = s < ncc
        c_ctx = jnp.where(ph == 0, ncc - 1 - s, s)
        c_lat = jnp.where(ph == 0, nlc - 1 - (s - ncc), s - ncc)
        return is_ctx, c_ctx, c_lat

    def rows(b, ph, s):
        is_ctx, c_ctx, c_lat = order(ph, s)
        return jnp.where(is_ctx, lat_blocks + b * ncc + c_ctx, b * nlc + c_lat)

    def table_rows(b, ph, s):
        is_ctx, c_ctx, c_lat = order(ph, s)
        return jnp.where(is_ctx, nlc + c_ctx, c_lat)

    def out_rows(b, ph, s):
        return jnp.where(ph == 0, rows(b, 1, 0), rows(b, 1, s))

    grid_spec = pltpu.PrefetchScalarGridSpec(
        num_scalar_prefetch=1,
        grid=(dm.b, 2, nc),
        in_specs=[pl.BlockSpec((L, qk_w), lambda b, ph, s, d: (rows(b, ph, s), 0)),
                  pl.BlockSpec((L, qk_w), lambda b, ph, s, d: (rows(b, ph, s), 1)),
                  pl.BlockSpec((L, v_w), lambda b, ph, s, d: (rows(b, ph, s), 1)),
                  pl.BlockSpec((L, v_w), lambda b, ph, s, d: (rows(b, ph, s), 2)),
                  pl.BlockSpec((L, LANES), lambda b, ph, s, d: (table_rows(b, ph, s), 0)),
                  pl.BlockSpec((L, LANES), lambda b, ph, s, d: (table_rows(b, ph, s), 0)),
                  pl.BlockSpec((1, v_w), lambda b, ph, s, d: (0, 0))],
        out_specs=pl.BlockSpec((L, v_w), lambda b, ph, s, d: (out_rows(b, ph, s), 0)),
        scratch_shapes=[pltpu.VMEM((RET_HEADS, RET_QK_DIM, RET_V_DIM), F32),
                        pltpu.VMEM((RET_HEADS, RET_QK_DIM, RET_V_DIM), F32),
                        pltpu.VMEM((nc, RET_HEADS, RET_QK_DIM, RET_V_DIM), F32),
                        pltpu.VMEM((RET_HEADS, L, L), F32)],
    )
    return pl.pallas_call(
        functools.partial(_ret_kernel, ncc=ncc, nlc=nlc),
        grid_spec=grid_spec,
        out_shape=jax.ShapeDtypeStruct((dm.n_all, v_w), BF16),
        compiler_params=_cparams(("parallel", "arbitrary", "arbitrary")),
        name="retention",
    )(dec, p, p, p, p, cos_t, sin_t, gn.reshape(1, v_w))


def _att_prep_kernel(q_ref, k_ref, v_ref, cos_ref, sin_ref, gq_ref, gk_ref, qo_ref, ko_ref, vo_ref,
                     *, lat_tiles):
    is_lat = pl.program_id(0) < lat_tiles
    cos = jnp.where(is_lat, cos_ref[...], 1.0)
    sin = jnp.where(is_lat, sin_ref[...], 0.0)

    hd = ATT_HEAD_DIM

    def prep(x_ref, gain_ref, h):
        xh = x_ref[:, h * hd:(h + 1) * hd]
        xn = xh * lax.rsqrt(jnp.mean(xh * xh, axis=-1, keepdims=True) + NORM_EPS) * gain_ref[...]
        return _rope_pairs(xn, cos, sin)

    for h in range(ATT_HEADS):
        qo_ref[h * hd:(h + 1) * hd, :] = (prep(q_ref, gq_ref, h) * ATT_SCORE_SCALE).T.astype(qo_ref.dtype)
    ones = jnp.ones((ATT_V_ROWS - hd, vo_ref.shape[1]), vo_ref.dtype)
    for h in range(ATT_KV_HEADS):
        ko_ref[:, h * hd:(h + 1) * hd] = prep(k_ref, gk_ref, h).astype(ko_ref.dtype)
        vo_ref[h * ATT_V_ROWS:h * ATT_V_ROWS + hd, :] = v_ref[:, h * hd:(h + 1) * hd].T.astype(vo_ref.dtype)
        vo_ref[h * ATT_V_ROWS + hd:(h + 1) * ATT_V_ROWS, :] = ones


def att_prep(dm, p, cos_t, sin_t, gq, gk):
    tm = dm.tile(TM_PREP)
    n = dm.n_all
    lat_tiles, per_batch = dm.n_lat // tm, dm.s // tm
    qw = ATT_HEADS * ATT_HEAD_DIM
    kw = ATT_KV_HEADS * ATT_HEAD_DIM
    q_blk, k_blk, v_blk = 1536 // qw, 2048 // kw, 2304 // kw
    tab = lambda i: (jnp.where(i < lat_tiles, i % per_batch, 0), 0)
    return pl.pallas_call(
        functools.partial(_att_prep_kernel, lat_tiles=lat_tiles),
        grid=(n // tm,),
        in_specs=[pl.BlockSpec((tm, qw), lambda i: (i, q_blk)),
                  pl.BlockSpec((tm, kw), lambda i: (i, k_blk)),
                  pl.BlockSpec((tm, kw), lambda i: (i, v_blk)),
                  pl.BlockSpec((tm, ATT_HEAD_DIM), tab),
                  pl.BlockSpec((tm, ATT_HEAD_DIM), tab),
                  pl.BlockSpec((1, ATT_HEAD_DIM), lambda i: (0, 0)),
                  pl.BlockSpec((1, ATT_HEAD_DIM), lambda i: (0, 0))],
        out_specs=[pl.BlockSpec((qw, tm), lambda i: (0, i)),
                   pl.BlockSpec((tm, kw), lambda i: (i, 0)),
                   pl.BlockSpec((ATT_KV_HEADS * ATT_V_ROWS, tm), lambda i: (0, i))],
        out_shape=[jax.ShapeDtypeStruct((qw, n), BF16),
                   jax.ShapeDtypeStruct((n, kw), BF16),
                   jax.ShapeDtypeStruct((ATT_KV_HEADS * ATT_V_ROWS, n), BF16)],
        compiler_params=_cparams(("parallel",)),
        name="att_prep",
    )(p, p, p, cos_t, sin_t, gq.reshape(1, -1), gk.reshape(1, -1))


def _attn_kernel(*refs, with_latent_keys):
    ns = ATT_STAGES
    if with_latent_keys:
        qt_ref, kl_ref, vtl_ref, kc_ref, vtc_ref, o_ref, acc_ref = refs[:7]
    else:
        qt_ref, kc_ref, vtc_ref, o_ref, acc_ref = refs[:5]
    s_refs, p_refs = refs[-2 * ns:-ns], refs[-ns:]
    hd = ATT_HEAD_DIM
    grp = ATT_HEADS // ATT_KV_HEADS
    tq = qt_ref.shape[1]
    nq = grp * tq
    tk = kc_ref.shape[0]
    n = 1 + (kl_ref.shape[0] // tk if with_latent_keys else 0)
    qt = jnp.concatenate([qt_ref[j * hd:(j + 1) * hd, :] for j in range(grp)], axis=1)

    def key_chunk(j):
        if isinstance(j, int) and j == 0:
            return kc_ref[...]
        return kl_ref[pl.ds(pl.multiple_of((j - 1) * tk, tk), tk), :]

    def value_chunk(j):
        if isinstance(j, int) and j == 0:
            return vtc_ref[...]
        return vtl_ref[:, pl.ds(pl.multiple_of((j - 1) * tk, tk), tk)]

    def max_keys(x):
        groups = math.gcd(tk // SUBLANES, ATT_REDUCE_GROUPS)
        part = jnp.max(x.reshape(groups, tk // groups, nq), axis=0)
        return jnp.max(part, axis=0, keepdims=True)

    def time_step(t, slot, carry, do_scores=True, do_softmax=True, do_values=True):
        m, alpha = carry
        if do_scores:
            s_refs[slot][...] = jnp.dot(key_chunk(t), qt, preferred_element_type=F32)
        if do_values:
            pv = jnp.dot(value_chunk(t - 2), p_refs[(slot - 2) % ns][...], preferred_element_type=F32)
            acc_ref[...] = acc_ref[...] * alpha + pv
        if do_softmax:
            st = s_refs[(slot - 1) % ns][...]
            m_new = jnp.maximum(m, max_keys(st))
            p_refs[(slot - 1) % ns][...] = jnp.exp2(st - m_new).astype(BF16)
            alpha = jnp.exp2(m - m_new)
            m = m_new
        return m, alpha

    acc_ref[...] = jnp.zeros_like(acc_ref)
    carry = (jnp.full((1, nq), -1e30, F32), jnp.zeros((1, nq), F32))
    fill = min(3, n + 2)
    for t in range(fill):
        carry = time_step(t, t % ns, carry, t < n, 0 <= t - 1 < n, 0 <= t - 2 < n)
    steady = max(n - fill, 0)
    body_steps = ns * ATT_UNROLL

    def unrolled(i, cr):
        for k in range(body_steps):
            cr = time_step(fill + body_steps * i + k, (fill + k) % ns, cr)
        return cr

    if steady // body_steps:
        carry = lax.fori_loop(0, steady // body_steps, unrolled, carry)
    for t in range(fill + body_steps * (steady // body_steps), max(n, fill)):
        carry = time_step(t, t % ns, carry)
    for t in range(max(n, fill), n + 2):
        carry = time_step(t, t % ns, carry, False, t - 1 < n, True)
    out = (acc_ref[0:hd, :] / acc_ref[hd:hd + 1, :]).T
    for j in range(grp):
        o_ref[:, j * hd:(j + 1) * hd] = out[j * tq:(j + 1) * tq].astype(o_ref.dtype)


def attention(dm, qt, kn, vt, latent):
    hd = ATT_HEAD_DIM
    grp = ATT_HEADS // ATT_KV_HEADS
    tk = dm.ctx
    assert dm.s % tk == 0
    ctx_blk0 = dm.n_lat // dm.ctx
    kc_spec = pl.BlockSpec((tk, hd), lambda b, h, i: (ctx_blk0 + b, h))
    vtc_spec = pl.BlockSpec((ATT_V_ROWS, tk), lambda b, h, i: (h, ctx_blk0 + b))
    if latent:
        tq = TQ_ATT
        nq = dm.s // tq
        q_spec = pl.BlockSpec((grp * hd, tq), lambda b, h, i: (h, b * nq + i))
        kl_spec = pl.BlockSpec((dm.s, hd), lambda b, h, i: (b, h))
        vtl_spec = pl.BlockSpec((ATT_V_ROWS, dm.s), lambda b, h, i: (h, b))
        in_specs = [q_spec, kl_spec, vtl_spec, kc_spec, vtc_spec]
        args = (qt, kn, vt, kn, vt)
        rows = dm.n_lat
    else:
        tq, nq = dm.ctx, 1
        q_spec = pl.BlockSpec((grp * hd, tq), lambda b, h, i: (h, ctx_blk0 + b))
        in_specs = [q_spec, kc_spec, vtc_spec]
        args = (qt, kn, vt)
        rows = dm.b * dm.ctx
    return pl.pallas_call(
        functools.partial(_attn_kernel, with_latent_keys=latent),
        grid=(dm.b, ATT_KV_HEADS, nq),
        in_specs=in_specs,
        out_specs=pl.BlockSpec((tq, grp * hd), lambda b, h, i: (b * nq + i, h)),
        out_shape=jax.ShapeDtypeStruct((rows, ATT_HEADS * hd), BF16),
        scratch_shapes=([pltpu.VMEM((ATT_V_ROWS, grp * tq), F32)]
                        + [pltpu.VMEM((tk, grp * tq), F32)] * ATT_STAGES
                        + [pltpu.VMEM((tk, grp * tq), BF16)] * ATT_STAGES),
        compiler_params=_cparams(("parallel", "parallel", "arbitrary")),
        name="attention_latent" if latent else "attention_ctx",
    )(*args)


def _pool_kernel(prev_ref, u_ref, next_ref, w_ref, sc_ref, o_ref, a_ref, b_ref, *, dm):
    tm = u_ref.shape[0]
    H = POOL_HALO
    i = pl.program_id(0)
    lat_tiles = dm.n_lat // tm
    is_lat = i < lat_tiles
    n_seg = jnp.where(is_lat, dm.s, dm.ctx)
    base = jnp.where(is_lat, (i * tm) % dm.s, 0)
    r = lax.broadcasted_iota(jnp.int32, (tm + 2 * H, 1), 0) - H
    pos = jnp.where(is_lat, base + r, r % dm.ctx)
    win = pl.ds(H, tm + 2 * H)

    def shifted(ref, d):
        return ref[pl.ds(H + d, tm + 2 * H), :]

    zeros = jnp.zeros((H, POOL_GROUP_DIM), F32)
    for ref in (a_ref, b_ref):
        ref[pl.ds(0, H), :] = zeros
        ref[pl.ds(tm + 3 * H, H), :] = zeros

    for gi, w in enumerate(POOL_WINDOWS):
        cols = slice(gi * POOL_GROUP_DIM, (gi + 1) * POOL_GROUP_DIM)
        half = w // 2
        a_ref[pl.ds(H, H), :] = prev_ref[:, cols]
        a_ref[pl.ds(2 * H, tm), :] = u_ref[:, cols]
        a_ref[pl.ds(2 * H + tm, H), :] = next_ref[:, cols]
        b_ref[win, :] = jnp.where(pos >= 1, shifted(a_ref, -1), 0.0)
        step = 1
        while step < half:
            b_ref[win, :] = b_ref[win, :] + jnp.where(pos >= step, shifted(b_ref, -step), 0.0)
            step *= 2
        step = 1
        while step < half:
            a_ref[win, :] = a_ref[win, :] + jnp.where(pos + step < n_seg, shifted(a_ref, step), 0.0)
            step *= 2
        tile = pl.ds(2 * H, tm)
        pt = pos[H:H + tm]
        cnt = (jnp.minimum(pt + half, n_seg) - jnp.maximum(pt - half, 0)).astype(F32)
        u = u_ref[:, cols]
        diff = (a_ref[tile, :] + b_ref[tile, :]) / cnt - u
        o_ref[:, cols] = (_dot(diff, w_ref[gi]) * sc_ref[:, cols]).astype(o_ref.dtype)


def pool_mixer(dm, p, pool_w, pool_scale):
    tm = dm.tile(TM_POOL)
    n = dm.n_all
    H = POOL_HALO
    width = len(POOL_WINDOWS) * POOL_GROUP_DIM
    col = 2560 // width
    per = tm // H
    last = n // H - 1
    return pl.pallas_call(
        functools.partial(_pool_kernel, dm=dm),
        grid=(n // tm,),
        in_specs=[pl.BlockSpec((H, width), lambda i: (jnp.maximum(i * per - 1, 0), col)),
                  pl.BlockSpec((tm, width), lambda i: (i, col)),
                  pl.BlockSpec((H, width), lambda i: (jnp.minimum((i + 1) * per, last), col)),
                  pl.BlockSpec(pool_w.shape, lambda i: (0, 0, 0)),
                  pl.BlockSpec((1, width), lambda i: (0, 0))],
        out_specs=pl.BlockSpec((tm, width), lambda i: (i, 0)),
        out_shape=jax.ShapeDtypeStruct((n, width), BF16),
        scratch_shapes=[pltpu.VMEM((tm + 4 * H, POOL_GROUP_DIM), F32),
                        pltpu.VMEM((tm + 4 * H, POOL_GROUP_DIM), F32)],
        compiler_params=_cparams(("parallel",)),
        name="pool_mixer",
    )(p, p, p, pool_w, pool_scale.reshape(1, width))


def _merge_kernel(yr_ref, ya_ref, yp_ref, gl_ref, *refs, lat_tiles):
    mod_ref, wb_ref, wo_ref, o_ref = refs[-4:]
    x_refs = refs[:-4]
    d = o_ref.shape[1]
    mixed = None
    for j, y_ref in enumerate((yr_ref, ya_ref, yp_ref)):
        gate = gl_ref[:, j * d:(j + 1) * d].astype(F32)
        term = gate * jnp.dot(y_ref[...], wb_ref[j], preferred_element_type=F32)
        mixed = term if mixed is None else mixed + term
    y = mod_ref[0][2:3] * _dot(mixed, wo_ref[...])

    def finish(x_ref):
        o_ref[...] = x_ref[...] + y

    _for_row_source(x_refs, lat_tiles, finish)


def merge(dm, n_rows, yr, ya, yp, gates, x_parts, mod, wb, wo):
    tm = dm.tile(TM_MERGE)
    d = dm.d
    bw = yr.shape[1]
    row = dm.mod_row(tm)
    return pl.pallas_call(
        functools.partial(_merge_kernel, lat_tiles=dm.n_lat // tm),
        grid=(n_rows // tm,),
        in_specs=[pl.BlockSpec((tm, bw), lambda i: (i, 0)),
                  pl.BlockSpec((tm, bw), lambda i: (i, 0)),
                  pl.BlockSpec((tm, bw), lambda i: (i, 0)),
                  pl.BlockSpec((tm, 3 * d), lambda i: (i, 0))]
        + _row_sources(dm, tm, x_parts) + [
            pl.BlockSpec((1, 6, d), lambda i: (row(i), 0, 0)),
            pl.BlockSpec(wb.shape, lambda i: (0, 0, 0)),
            pl.BlockSpec(wo.shape, lambda i: (0, 0))],
        out_specs=pl.BlockSpec((tm, d), lambda i: (i, 0)),
        out_shape=jax.ShapeDtypeStruct((n_rows, d), F32),
        compiler_params=_cparams(("parallel",)),
        name="merge",
    )(yr, ya, yp, gates, *x_parts, mod, wb, wo)


def _ffn_kernel(x_ref, mod_ref, gain_ref, w1_ref, w3_ref, w2_ref, o_ref, h_ref, acc_ref):
    f = pl.program_id(1)
    m = mod_ref[0]

    @pl.when(f == 0)
    def _():
        h_ref[...] = _norm_mod(x_ref[...], gain_ref[...], m[3:4], m[4:5]).astype(BF16)
        acc_ref[...] = jnp.zeros_like(acc_ref)

    h = h_ref[...]
    a = jnp.dot(h, w1_ref[...], preferred_element_type=F32)
    b = jnp.dot(h, w3_ref[...], preferred_element_type=F32)
    acc_ref[...] += _dot(_silu(a) * b, w2_ref[...])

    @pl.when(f == pl.num_programs(1) - 1)
    def _():
        o_ref[...] = x_ref[...] + m[5:6] * acc_ref[...]


def ffn_dense(dm, x, mod, gain, w1, w3, w2):
    n, d = x.shape
    fdim = w1.shape[1]
    tm, tf = dm.tile(TM_FFN), TF_FFN
    row = dm.mod_row(tm)
    return pl.pallas_call(
        _ffn_kernel,
        grid=(n // tm, fdim // tf),
        in_specs=[pl.BlockSpec((tm, d), lambda i, f: (i, 0)),
                  pl.BlockSpec((1, 6, d), lambda i, f: (row(i), 0, 0)),
                  pl.BlockSpec((1, d), lambda i, f: (0, 0)),
                  pl.BlockSpec((d, tf), lambda i, f: (0, f)),
                  pl.BlockSpec((d, tf), lambda i, f: (0, f)),
                  pl.BlockSpec((tf, d), lambda i, f: (f, 0))],
        out_specs=pl.BlockSpec((tm, d), lambda i, f: (i, 0)),
        out_shape=jax.ShapeDtypeStruct((n, d), F32),
        scratch_shapes=[pltpu.VMEM((tm, d), BF16), pltpu.VMEM((tm, d), F32)],
        compiler_params=_cparams(("parallel", "arbitrary")),
        name="ffn_dense",
    )(x, mod, gain.reshape(1, d), w1, w3, w2)


def _router_kernel(x_ref, mod_ref, gain_ref, wr_ref, br_ref, h_ref, slot_ref, slot_t_ref, wt_ref, cum_ref):
    tm = x_ref.shape[0]
    m = mod_ref[0]
    h = _norm_mod(x_ref[...], gain_ref[...], m[3:4], m[4:5])
    h_ref[...] = h.astype(BF16)
    logits = _dot3(h, wr_ref[...]) + br_ref[...]
    lane = lax.broadcasted_iota(jnp.int32, logits.shape, 1)
    m1 = jnp.max(logits, axis=-1, keepdims=True)
    i1 = jnp.min(jnp.where(logits == m1, lane, LANES), axis=-1, keepdims=True)
    rest = jnp.where(lane == i1, -jnp.inf, logits)
    m2 = jnp.max(rest, axis=-1, keepdims=True)
    i2 = jnp.min(jnp.where(rest == m2, lane, LANES), axis=-1, keepdims=True)
    e = jnp.exp(m2 - m1)
    w1 = 1.0 / (1.0 + e)
    w2 = e / (1.0 + e)
    wt_ref[...] = jnp.where(lane == i1, w1, jnp.where(lane == i2, w2, 0.0))
    routed = jnp.where((lane == i1) | (lane == i2), 1.0, 0.0)
    sub = SUB_MOE
    lower = (lax.broadcasted_iota(jnp.int32, (sub, sub), 0) > lax.broadcasted_iota(jnp.int32, (sub, sub), 1))
    lower = lower.astype(BF16)
    offset = jnp.zeros((1, LANES), F32)
    cum_ref[...] = jnp.zeros_like(cum_ref)
    for blk in range(tm // sub):
        rows = slice(blk * sub, (blk + 1) * sub)
        mask = routed[rows]
        cum_ref[0, blk:blk + 1, :] = offset.astype(jnp.int32)
        rank = jnp.dot(lower, mask.astype(BF16), preferred_element_type=F32) + offset
        slot_ref[rows, :] = jnp.where(mask > 0.0, rank, -1.0)
        count = jnp.sum(mask, axis=0, keepdims=True)
        offset = offset + jnp.ceil(count * (1.0 / SLOT_ALIGN)) * SLOT_ALIGN
    cum_ref[0, tm // sub:tm // sub + 1, :] = offset.astype(jnp.int32)
    slot_t_ref[...] = slot_ref[...].T[:SUBLANES, :]


def moe_route(dm, n_rows, x, mod, gain, wr_pad, br_pad):
    tm = dm.tile(TM_MOE)
    d = dm.d
    row = dm.mod_row(tm)
    nt = n_rows // tm
    cum_rows = -(-(tm // SUB_MOE + 1) // SUBLANES) * SUBLANES
    return pl.pallas_call(
        _router_kernel,
        grid=(nt,),
        in_specs=[pl.BlockSpec((tm, d), lambda i: (i, 0)),
                  pl.BlockSpec((1, 6, d), lambda i: (row(i), 0, 0)),
                  pl.BlockSpec((1, d), lambda i: (0, 0)),
                  pl.BlockSpec((d, LANES), lambda i: (0, 0)),
                  pl.BlockSpec((1, LANES), lambda i: (0, 0))],
        out_specs=[pl.BlockSpec((tm, d), lambda i: (i, 0)),
                   pl.BlockSpec((tm, LANES), lambda i: (i, 0)),
                   pl.BlockSpec((SUBLANES, tm), lambda i: (0, i)),
                   pl.BlockSpec((tm, LANES), lambda i: (i, 0)),
                   pl.BlockSpec((1, cum_rows, LANES), lambda i: (i, 0, 0))],
        out_shape=[jax.ShapeDtypeStruct((n_rows, d), BF16),
                   jax.ShapeDtypeStruct((n_rows, LANES), F32),
                   jax.ShapeDtypeStruct((SUBLANES, n_rows), F32),
                   jax.ShapeDtypeStruct((n_rows, LANES), F32),
                   jax.ShapeDtypeStruct((nt, cum_rows, LANES), jnp.int32)],
        compiler_params=_cparams(("parallel",)),
        name="moe_router",
    )(x, mod, gain.reshape(1, d), wr_pad, br_pad)


def _moe_kernel(cum_ref, h_ref, slot_ref, slot_t_ref, wt_ref, w1_ref, w3_ref, w2_ref, o_ref,
                xs_ref, acc_ref):
    i, e, f = pl.program_id(0), pl.program_id(1), pl.program_id(2)
    tm, d = h_ref.shape
    rb, win, sub = RB_MOE, WIN_MOE, SUB_MOE
    nsub = tm // sub

    def base(j):
        return cum_ref[i, j * N_EXPERTS + e]

    halves = (base(nsub) + rb // 2 - 1) // (rb // 2)
    nblk, has_tail = halves // 2, halves % 2 == 1

    def block_rows(r, size=rb):
        return pl.ds(pl.multiple_of(r * rb, 32), size)

    def windows(j):
        return base(j), (base(j + 1) - base(j) + win - 1) // win

    def window_rows(start, w):
        return pl.ds(pl.multiple_of(start + w * win, SLOT_ALIGN), win)

    @pl.when((e == 0) & (f == 0))
    def _():
        o_ref[...] = jnp.zeros_like(o_ref)

    @pl.when(f == 0)
    def _():
        def clear(r, carry):
            xs_ref[block_rows(r), :] = jnp.zeros((rb, d), BF16)
            acc_ref[block_rows(r), :] = jnp.zeros((rb, d), F32)
            return carry

        lax.fori_loop(0, nblk + 1, clear, 0)
        acc_ref[pl.ds(pl.multiple_of((nblk + 1) * rb, 32), win), :] = jnp.zeros((win, d), F32)

        for j in range(nsub):
            tok = slice(j * sub, (j + 1) * sub)
            start, nwin = windows(j)

            def gather(w, carry, tok=tok, start=start):
                rows = window_rows(start, w)
                want = (lax.broadcasted_iota(jnp.int32, (win, sub), 0) + (start + w * win)).astype(F32)
                onehot = (slot_t_ref[pl.ds(e, 1), tok] == want).astype(BF16)
                xs_ref[rows, :] = jnp.dot(onehot, h_ref[tok, :], preferred_element_type=F32).astype(BF16)
                return carry

            lax.fori_loop(0, nwin, gather, 0)

    def expert(r, carry, size=rb):
        rows = block_rows(r, size)
        xs = xs_ref[rows, :]
        a = jnp.dot(xs, w1_ref[0], preferred_element_type=F32)
        b = jnp.dot(xs, w3_ref[0], preferred_element_type=F32)
        acc_ref[rows, :] += _dot(_silu(a) * b, w2_ref[0])
        return carry

    lax.fori_loop(0, nblk, expert, 0)

    @pl.when(has_tail)
    def _():
        expert(nblk, 0, rb // 2)

    @pl.when(f == pl.num_programs(2) - 1)
    def _():
        pick = lax.broadcasted_iota(jnp.int32, (sub, LANES), 1) == e
        for j in range(nsub):
            tok = slice(j * sub, (j + 1) * sub)
            start, nwin = windows(j)
            slot_col = jnp.sum(jnp.where(pick, slot_ref[tok, :], 0.0), axis=-1, keepdims=True)
            wt_col = jnp.sum(jnp.where(pick, wt_ref[tok, :], 0.0), axis=-1, keepdims=True)

            def scatter(w, carry, tok=tok, start=start, slot_col=slot_col, wt_col=wt_col):
                ys = acc_ref[window_rows(start, w), :].astype(BF16)
                want = (lax.broadcasted_iota(jnp.int32, (sub, win), 1) + (start + w * win)).astype(F32)
                onehot = (slot_col == want).astype(BF16)
                o_ref[tok, :] += wt_col * jnp.dot(onehot, ys, preferred_element_type=F32)
                return carry

            lax.fori_loop(0, nwin, scatter, 0)


def moe_experts(dm, n_rows, h, slot, slot_t, wt, counts, w1, w3, w2):
    tm = dm.tile(TM_MOE)
    d = dm.d
    ne, _, fdim = w1.shape
    tf = TF_MOE if fdim % TF_MOE == 0 else TF_FFN
    assert SUB_MOE % SLOT_ALIGN == 0 and tm % SUB_MOE == 0
    cap = tm + RB_MOE + WIN_MOE
    once = dict(pipeline_mode=pl.Buffered(1))
    grid_spec = pltpu.PrefetchScalarGridSpec(
        num_scalar_prefetch=1,
        grid=(n_rows // tm, ne, fdim // tf),
        in_specs=[pl.BlockSpec((tm, d), lambda i, e, f, c: (i, 0), **once),
                  pl.BlockSpec((tm, LANES), lambda i, e, f, c: (i, 0), **once),
                  pl.BlockSpec((SUBLANES, tm), lambda i, e, f, c: (0, i), **once),
                  pl.BlockSpec((tm, LANES), lambda i, e, f, c: (i, 0), **once),
                  pl.BlockSpec((1, d, tf), lambda i, e, f, c: (e, 0, f)),
                  pl.BlockSpec((1, d, tf), lambda i, e, f, c: (e, 0, f)),
                  pl.BlockSpec((1, tf, d), lambda i, e, f, c: (e, f, 0))],
        out_specs=pl.BlockSpec((tm, d), lambda i, e, f, c: (i, 0), **once),
        scratch_shapes=[pltpu.VMEM((cap, d), BF16), pltpu.VMEM((cap, d), F32)],
    )
    return pl.pallas_call(
        _moe_kernel,
        grid_spec=grid_spec,
        out_shape=jax.ShapeDtypeStruct((n_rows, d), F32),
        compiler_params=_cparams(("parallel", "arbitrary", "arbitrary")),
        name="moe_experts",
    )(counts, h, slot, slot_t, wt, w1, w3, w2)


def _residual_kernel(x_ref, y_ref, mod_ref, o_ref):
    o_ref[...] = x_ref[...] + mod_ref[0][5:6] * y_ref[...]


def _final_kernel(x_ref, y_ref, mod_ref, gain_ref, o_ref):
    x = x_ref[...] + mod_ref[0][5:6] * y_ref[...]
    o_ref[...] = x * lax.rsqrt(jnp.mean(x * x, axis=-1, keepdims=True) + NORM_EPS) * gain_ref[...]


def residual(dm, n_rows, x, y, mod, final_gain=None):
    tm = dm.tile(TM_FFN)
    d = dm.d
    row = dm.mod_row(tm)
    in_specs = [pl.BlockSpec((tm, d), lambda i: (i, 0)),
                pl.BlockSpec((tm, d), lambda i: (i, 0)),
                pl.BlockSpec((1, 6, d), lambda i: (row(i), 0, 0))]
    args = [x, y, mod]
    if final_gain is not None:
        in_specs.append(pl.BlockSpec((1, d), lambda i: (0, 0)))
        args.append(final_gain.reshape(1, d))
    return pl.pallas_call(
        _residual_kernel if final_gain is None else _final_kernel,
        grid=(n_rows // tm,),
        in_specs=in_specs,
        out_specs=pl.BlockSpec((tm, d), lambda i: (i, 0)),
        out_shape=jax.ShapeDtypeStruct((n_rows, d), F32),
        compiler_params=_cparams(("parallel",)),
        name="residual" if final_gain is None else "residual_final_norm",
    )(*args)


def _final_norm_kernel(x_ref, gain_ref, o_ref):
    x = x_ref[...]
    o_ref[...] = x * lax.rsqrt(jnp.mean(x * x, axis=-1, keepdims=True) + NORM_EPS) * gain_ref[...]


def final_rmsnorm(dm, n_rows, x, gain):
    tm = dm.tile(TM_FFN)
    d = dm.d
    return pl.pallas_call(
        _final_norm_kernel,
        grid=(n_rows // tm,),
        in_specs=[pl.BlockSpec((tm, d), lambda i: (i, 0)), pl.BlockSpec((1, d), lambda i: (0, 0))],
        out_specs=pl.BlockSpec((tm, d), lambda i: (i, 0)),
        out_shape=jax.ShapeDtypeStruct((n_rows, d), F32),
        compiler_params=_cparams(("parallel",)),
        name="final_norm",
    )(x, gain.reshape(1, d))


def _rope_tables(s, ctx):
    def angles(pos, dim):
        freqs = 1.0 / (ROPE_THETA ** (jnp.arange(0, dim, 2, dtype=F32) / dim))
        return pos[:, None] * freqs[None, :]

    rows = s // GRID_W
    row = jnp.repeat(jnp.arange(rows, dtype=F32), GRID_W)
    col = jnp.tile(jnp.arange(GRID_W, dtype=F32), rows)
    half = ATT_HEAD_DIM // 2
    ang_att = jnp.concatenate([angles(row, half), angles(col, half)], axis=-1)
    sign = jnp.tile(jnp.array([-1.0, 1.0], F32), LANES // 2)

    def tables(ang):
        reps = LANES // (2 * ang.shape[1])
        cos = jnp.tile(jnp.repeat(jnp.cos(ang), 2, axis=-1), (1, reps))
        sin = jnp.tile(jnp.repeat(jnp.sin(ang), 2, axis=-1), (1, reps)) * sign
        return cos, sin

    ang_ret = angles(jnp.arange(s, dtype=F32), RET_QK_DIM)
    ang_ret = jnp.concatenate([ang_ret, jnp.zeros((ctx, RET_QK_DIM // 2), F32)], axis=0)
    return tables(ang_att) + tables(ang_ret)


def kernel(x, c, ctx, c_ctx, w_ada, b_ada, norm_mix, norm_ffn, w_in, ret_decay, ret_gn, attn_qn, attn_kn, pool_w, pool_scale, w_branch, w_out, ffn_w1, ffn_w3, ffn_w2, moe_router, moe_router_b, moe_w1, moe_w3, moe_w2, final_norm):
    b, s, d = x.shape
    n_ctx = ctx.shape[1]
    depth = w_in.shape[0]
    dm = Dims(b, s, n_ctx, d)

    x_parts = (x.reshape(b * s, d), ctx.reshape(b * n_ctx, d))
    mod_rows = 2 * SUBLANES
    cc = jnp.concatenate([c, c_ctx[None, :], jnp.zeros((mod_rows - b - 1, d), F32)], axis=0)
    mods = ada_table(cc, w_ada, b_ada).reshape(depth, mod_rows, 6, d)

    att_cos, att_sin, ret_cos, ret_sin = _rope_tables(s, n_ctx)

    for i in range(depth):
        need_ctx = i < depth - 1
        n_rows = dm.n_all if need_ctx else dm.n_lat
        mod = mods[i]
        p, gates = proj_in(dm, x_parts, mod, norm_mix[i], w_in[i].astype(BF16))
        y_ret = retention(dm, p, ret_decay[i], ret_cos, ret_sin, ret_gn[i])
        y_pool = pool_mixer(dm, p, pool_w[i].astype(BF16), pool_scale[i])
        qt, kn, vt = att_prep(dm, p, att_cos, att_sin, attn_qn[i], attn_kn[i])
        y_att = attention(dm, qt, kn, vt, latent=True)
        if need_ctx:
            y_att = jnp.concatenate([y_att, attention(dm, qt, kn, vt, latent=False)], axis=0)
        xa = merge(dm, n_rows, y_ret, y_att, y_pool, gates, x_parts, mod,
                   w_branch[i].astype(BF16), w_out[i].astype(BF16))
        j = i // 2
        last = i == depth - 1
        if i % 2 == 0:
            xa = ffn_dense(dm, xa, mod, norm_ffn[i], ffn_w1[j].astype(BF16), ffn_w3[j].astype(BF16),
                           ffn_w2[j].astype(BF16))
            if last:
                xa = final_rmsnorm(dm, n_rows, xa, final_norm)
        else:
            wr = jnp.zeros((d, LANES), F32).at[:, :N_EXPERTS].set(moe_router[j])
            br = jnp.full((1, LANES), -1e30, F32).at[0, :N_EXPERTS].set(moe_router_b[j])
            h, slot, slot_t, wt, cnt = moe_route(dm, n_rows, xa, mod, norm_ffn[i], wr, br)
            nsub = dm.tile(TM_MOE) // SUB_MOE
            cum = cnt[:, :nsub + 1, :N_EXPERTS].reshape(cnt.shape[0], (nsub + 1) * N_EXPERTS)
            y = moe_experts(dm, n_rows, h, slot, slot_t, wt, cum,
                            moe_w1[j].astype(BF16), moe_w3[j].astype(BF16), moe_w2[j].astype(BF16))
            xa = residual(dm, n_rows, xa, y, mod, final_norm if last else None)
        x_parts = (xa,)
    return xa[:dm.n_lat].reshape(b, s, d)
```

```python
import functools
import math

import numpy as np
import jax
import jax.numpy as jnp
from jax import lax
from jax.experimental import pallas as pl
from jax.experimental.pallas import tpu as pltpu

F32 = jnp.float32
BF16 = jnp.bfloat16

GRID_W = 64
RET_HEADS = 4
RET_QK_DIM = 64
RET_V_DIM = 128
ATT_HEADS = 4
ATT_KV_HEADS = 2
ATT_HEAD_DIM = 128
ROPE_THETA = 10000.0
POOL_WINDOWS = (2, 4, 8, 16)
POOL_GROUP_DIM = 128
N_EXPERTS = 8
NORM_EPS = 1e-6
GN_EPS = 1e-5

LANES = 128
SUBLANES = 8
VMEM_LIMIT_BYTES = 56 * 1024 * 1024

TM_PROJ = 1024
TN_PROJ = 1536
TM_PREP = 1024
TM_POOL = 1024
TM_MERGE = 512
TM_FFN = 1024
TF_FFN = 512
TM_MOE = 2048
TF_MOE = 1792
MOE_COL_CHUNK = 512
RB_MOE = 256
WIN_MOE = 256
SUB_MOE = 512
SLOT_ALIGN = 16
assert RB_MOE % 64 == 0
RET_L = 256
TQ_ATT = 512
ATT_SCORE_SCALE = (ATT_HEAD_DIM ** -0.5) * math.log2(math.e)
ATT_V_ROWS = ATT_HEAD_DIM + 16
ATT_STAGES = 3
ATT_UNROLL = 4
ATT_REDUCE_GROUPS = 8
POOL_HALO = 8


def _cparams(sem):
    return pltpu.CompilerParams(dimension_semantics=sem, vmem_limit_bytes=VMEM_LIMIT_BYTES)


def _dot(a, b):
    return jnp.dot(a.astype(BF16), b.astype(BF16), preferred_element_type=F32)


def _dot_nt(a, b):
    return lax.dot_general(a.astype(BF16), b.astype(BF16), (((1,), (1,)), ((), ())),
                           preferred_element_type=F32)


def _dot_tn(a, b):
    return lax.dot_general(a.astype(BF16), b.astype(BF16), (((0,), (0,)), ((), ())),
                           preferred_element_type=F32)


def _split_bf16(a):
    hi = a.astype(BF16)
    lo = (a - hi.astype(F32)).astype(BF16)
    return hi, lo


def _dot3(a, b):
    ah, al = _split_bf16(a)
    bh, bl = _split_bf16(b)
    d = functools.partial(jnp.dot, preferred_element_type=F32)
    return d(ah, bh) + (d(al, bh) + d(ah, bl))


def _norm_mod(x, gain, shift, scale):
    y = x * lax.rsqrt(jnp.mean(x * x, axis=-1, keepdims=True) + NORM_EPS)
    return (y * gain) * (1.0 + scale) + shift


def _sigmoid(x):
    return 1.0 / (1.0 + jnp.exp(-x))


def _silu(x):
    return x * _sigmoid(x)


class Dims:
    def __init__(self, b, s, ctx, d):
        self.b, self.s, self.ctx, self.d = b, s, ctx, d
        self.n_lat = b * s
        self.n_all = b * s + b * ctx

    def tile(self, preferred):
        t = preferred
        while self.s % t or (self.b * self.ctx) % t:
            t //= 2
        return t

    def mod_row(self, tm):
        lat_tiles, per_batch, b = self.n_lat // tm, self.s // tm, self.b
        return lambda i: jnp.where(i < lat_tiles, i // per_batch, b)


def _ada_kernel(c_ref, w_ref, b_ref, o_ref):
    o_ref[0] = _dot3(_silu(c_ref[...]), w_ref[0]) + b_ref[0]


def ada_table(cc, w_ada, b_ada):
    depth, d, n6 = w_ada.shape
    tn = n6 // 4
    rows = cc.shape[0]
    return pl.pallas_call(
        _ada_kernel,
        grid=(depth, n6 // tn),
        in_specs=[pl.BlockSpec((rows, d), lambda l, j: (0, 0)),
                  pl.BlockSpec((1, d, tn), lambda l, j: (l, 0, j)),
                  pl.BlockSpec((1, 1, tn), lambda l, j: (l, 0, j))],
        out_specs=pl.BlockSpec((1, rows, tn), lambda l, j: (l, 0, j)),
        out_shape=jax.ShapeDtypeStruct((depth, rows, n6), F32),
        compiler_params=_cparams(("arbitrary", "arbitrary")),
        name="ada_table",
    )(cc, w_ada, b_ada.reshape(depth, 1, n6))


def _row_sources(dm, tm, parts):
    d = dm.d
    if len(parts) == 1:
        return [pl.BlockSpec((tm, d), lambda i, *_: (i, 0))]
    lat_tiles = dm.n_lat // tm
    return [pl.BlockSpec((tm, d), lambda i, *_: (jnp.minimum(i, lat_tiles - 1), 0)),
            pl.BlockSpec((tm, d), lambda i, *_: (jnp.maximum(i - lat_tiles, 0), 0))]


def _for_row_source(x_refs, lat_tiles, fn):
    if len(x_refs) == 1:
        fn(x_refs[0])
        return
    is_lat = pl.program_id(0) < lat_tiles
    pl.when(is_lat)(lambda: fn(x_refs[0]))
    pl.when(jnp.logical_not(is_lat))(lambda: fn(x_refs[1]))


def _proj_in_kernel(*refs, lat_tiles):
    mod_ref, gain_ref, wm_ref, wg_ref, o_ref, g_ref, h_ref = refs[-7:]
    x_refs = refs[:-7]

    @pl.when(pl.program_id(1) == 0)
    def _():
        def fill(x_ref):
            m = mod_ref[0]
            h_ref[...] = _norm_mod(x_ref[...], gain_ref[...], m[0:1], m[1:2]).astype(BF16)

        _for_row_source(x_refs, lat_tiles, fill)

    h = h_ref[...]
    o_ref[...] = jnp.dot(h, wm_ref[...], preferred_element_type=F32)
    logits = jnp.dot(h, wg_ref[...], preferred_element_type=F32)
    g_ref[...] = (0.5 * jnp.tanh(0.5 * logits) + 0.5).astype(g_ref.dtype)


def proj_in(dm, x_parts, mod, gain, w):
    n, d = dm.n_all, dm.d
    nout = w.shape[1]
    gate_w = 3 * d
    tm, tn = dm.tile(TM_PROJ), TN_PROJ
    assert nout - gate_w == gate_w and gate_w % tn == 0
    gate_blk0 = gate_w // tn
    row = dm.mod_row(tm)
    return pl.pallas_call(
        functools.partial(_proj_in_kernel, lat_tiles=dm.n_lat // tm),
        grid=(n // tm, gate_w // tn),
        in_specs=_row_sources(dm, tm, x_parts) + [
            pl.BlockSpec((1, 6, d), lambda i, j: (row(i), 0, 0)),
            pl.BlockSpec((1, d), lambda i, j: (0, 0)),
            pl.BlockSpec((d, tn), lambda i, j: (0, j)),
            pl.BlockSpec((d, tn), lambda i, j: (0, gate_blk0 + j))],
        out_specs=[pl.BlockSpec((tm, tn), lambda i, j: (i, j)),
                   pl.BlockSpec((tm, tn), lambda i, j: (i, j))],
        out_shape=[jax.ShapeDtypeStruct((n, nout - gate_w), F32),
                   jax.ShapeDtypeStruct((n, gate_w), BF16)],
        scratch_shapes=[pltpu.VMEM((tm, d), BF16)],
        compiler_params=_cparams(("parallel", "arbitrary")),
        name="proj_in",
    )(*x_parts, mod, gain.reshape(1, d), w, w)


def _rope_pairs(x, cos, sin):
    lane = lax.broadcasted_iota(jnp.int32, x.shape, 1)
    partner = jnp.where(lane % 2 == 0, pltpu.roll(x, LANES - 1, 1), pltpu.roll(x, 1, 1))
    return x * cos + partner * sin


def _ret_rope(x, cos, sin):
    return jnp.concatenate([_rope_pairs(x[:, j * LANES:(j + 1) * LANES], cos, sin)
                            for j in range(x.shape[1] // LANES)], axis=1)


def _ret_kernel(dec_ref, q_ref, k_ref, v_ref, g_ref, cos_ref, sin_ref, gn_ref, o_ref,
                sf_ref, sb_ref, sball_ref, decay_ref, *, ncc, nlc):
    phase = pl.program_id(1)
    s = pl.program_id(2)
    L = RET_L
    pos = lax.broadcasted_iota(jnp.int32, (L, 1), 0).astype(F32)
    cos = cos_ref[...]
    sin = sin_ref[...]
    k = _ret_rope(k_ref[...], cos, sin) * (RET_QK_DIM ** -0.5)
    v = v_ref[...]

    def log_g(direction, h):
        return -jnp.exp(jnp.full((1, 1), dec_ref[direction, h], F32))

    @pl.when(phase == 0)
    def _():
        @pl.when(s == 0)
        def _():
            sb_ref[...] = jnp.zeros_like(sb_ref)
            diff = (lax.broadcasted_iota(jnp.int32, (L, L), 0)
                    - lax.broadcasted_iota(jnp.int32, (L, L), 1)).astype(F32)
            for h in range(RET_HEADS):
                decay_ref[h] = (jnp.where(diff >= 0, jnp.exp(jnp.maximum(diff, 0.0) * log_g(0, h)), 0.0)
                                + jnp.where(diff <= 0, jnp.exp(jnp.maximum(-diff, 0.0) * log_g(1, h)), 0.0))

        for h in range(RET_HEADS):
            lgb = log_g(1, h)
            kh = k[:, h * RET_QK_DIM:(h + 1) * RET_QK_DIM]
            vh = v[:, h * RET_V_DIM:(h + 1) * RET_V_DIM]
            st = sb_ref[h]
            sball_ref[s, h] = st
            sb_ref[h] = jnp.exp(L * lgb) * st + _dot_tn(kh * jnp.exp(pos * lgb), vh)

    @pl.when(phase == 1)
    def _():
        @pl.when(s == 0)
        def _():
            sf_ref[...] = jnp.zeros_like(sf_ref)

        slot = jnp.where(s < ncc, ncc - 1 - s, ncc + nlc - 1 - (s - ncc))
        q = _ret_rope(q_ref[...], cos, sin)
        g = g_ref[...]
        for h in range(RET_HEADS):
            lgf = log_g(0, h)
            lgb = log_g(1, h)
            qh = q[:, h * RET_QK_DIM:(h + 1) * RET_QK_DIM]
            kh = k[:, h * RET_QK_DIM:(h + 1) * RET_QK_DIM]
            vh = v[:, h * RET_V_DIM:(h + 1) * RET_V_DIM]
            st_f = sf_ref[h]
            st_b = sball_ref[slot, h]
            o = _dot(_dot_nt(qh, kh) * decay_ref[h], vh)
            o = o + _dot(qh, st_f) * jnp.exp((pos + 1.0) * lgf)
            o = o + _dot(qh, st_b) * jnp.exp((L - pos) * lgb)
            sf_ref[h] = jnp.exp(L * lgf) * st_f + _dot_tn(kh * jnp.exp((L - 1.0 - pos) * lgf), vh)
            mu = jnp.mean(o, axis=-1, keepdims=True)
            var = jnp.mean(jnp.square(o - mu), axis=-1, keepdims=True)
            o = (o - mu) * lax.rsqrt(var + GN_EPS) * gn_ref[:, h * RET_V_DIM:(h + 1) * RET_V_DIM]
            gh = g[:, h * RET_V_DIM:(h + 1) * RET_V_DIM]
            o_ref[:, h * RET_V_DIM:(h + 1) * RET_V_DIM] = (_silu(gh) * o).astype(o_ref.dtype)


def retention(dm, p, dec, cos_t, sin_t, gn):
    L = RET_L
    ncc, nlc = dm.ctx // L, dm.s // L
    nc = ncc + nlc
    lat_blocks = dm.n_lat // L
    qk_w = RET_HEADS * RET_QK_DIM
    v_w = RET_HEADS * RET_V_DIM

    def order(ph, s):
        is_ctx = s < ncc
        c_ctx = jnp.where(ph == 0, ncc - 1 - s, s)
        c_lat = jnp.where(ph == 0, nlc - 1 - (s - ncc), s - ncc)
        return is_ctx, c_ctx, c_lat

    def rows(b, ph, s):
        is_ctx, c_ctx, c_lat = order(ph, s)
        return jnp.where(is_ctx, lat_blocks + b * ncc + c_ctx, b * nlc + c_lat)

    def table_rows(b, ph, s):
        is_ctx, c_ctx, c_lat = order(ph, s)
        return jnp.where(is_ctx, nlc + c_ctx, c_lat)

    def out_rows(b, ph, s):
        return jnp.where(ph == 0, rows(b, 1, 0), rows(b, 1, s))

    grid_spec = pltpu.PrefetchScalarGridSpec(
        num_scalar_prefetch=1,
        grid=(dm.b, 2, nc),
        in_specs=[pl.BlockSpec((L, qk_w), lambda b, ph, s, d: (rows(b, ph, s), 0)),
                  pl.BlockSpec((L, qk_w), lambda b, ph, s, d: (rows(b, ph, s), 1)),
                  pl.BlockSpec((L, v_w), lambda b, ph, s, d: (rows(b, ph, s), 1)),
                  pl.BlockSpec((L, v_w), lambda b, ph, s, d: (rows(b, ph, s), 2)),
                  pl.BlockSpec((L, LANES), lambda b, ph, s, d: (table_rows(b, ph, s), 0)),
                  pl.BlockSpec((L, LANES), lambda b, ph, s, d: (table_rows(b, ph, s), 0)),
                  pl.BlockSpec((1, v_w), lambda b, ph, s, d: (0, 0))],
        out_specs=pl.BlockSpec((L, v_w), lambda b, ph, s, d: (out_rows(b, ph, s), 0)),
        scratch_shapes=[pltpu.VMEM((RET_HEADS, RET_QK_DIM, RET_V_DIM), F32),
                        pltpu.VMEM((RET_HEADS, RET_QK_DIM, RET_V_DIM), F32),
                        pltpu.VMEM((nc, RET_HEADS, RET_QK_DIM, RET_V_DIM), F32),
                        pltpu.VMEM((RET_HEADS, L, L), F32)],
    )
    return pl.pallas_call(
        functools.partial(_ret_kernel, ncc=ncc, nlc=nlc),
        grid_spec=grid_spec,
        out_shape=jax.ShapeDtypeStruct((dm.n_all, v_w), BF16),
        compiler_params=_cparams(("parallel", "arbitrary", "arbitrary")),
        name="retention",
    )(dec, p, p, p, p, cos_t, sin_t, gn.reshape(1, v_w))


def _att_prep_kernel(q_ref, k_ref, v_ref, cos_ref, sin_ref, gq_ref, gk_ref, qo_ref, ko_ref, vo_ref,
                     *, lat_tiles):
    is_lat = pl.program_id(0) < lat_tiles
    cos = jnp.where(is_lat, cos_ref[...], 1.0)
    sin = jnp.where(is_lat, sin_ref[...], 0.0)

    hd = ATT_HEAD_DIM

    def prep(x_ref, gain_ref, h):
        xh = x_ref[:, h * hd:(h + 1) * hd]
        xn = xh * lax.rsqrt(jnp.mean(xh * xh, axis=-1, keepdims=True) + NORM_EPS) * gain_ref[...]
        return _rope_pairs(xn, cos, sin)

    for h in range(ATT_HEADS):
        qo_ref[h * hd:(h + 1) * hd, :] = (prep(q_ref, gq_ref, h) * ATT_SCORE_SCALE).T.astype(qo_ref.dtype)
    ones = jnp.ones((ATT_V_ROWS - hd, vo_ref.shape[1]), vo_ref.dtype)
    for h in range(ATT_KV_HEADS):
        ko_ref[:, h * hd:(h + 1) * hd] = prep(k_ref, gk_ref, h).astype(ko_ref.dtype)
        vo_ref[h * ATT_V_ROWS:h * ATT_V_ROWS + hd, :] = v_ref[:, h * hd:(h + 1) * hd].T.astype(vo_ref.dtype)
        vo_ref[h * ATT_V_ROWS + hd:(h + 1) * ATT_V_ROWS, :] = ones


def att_prep(dm, p, cos_t, sin_t, gq, gk):
    tm = dm.tile(TM_PREP)
    n = dm.n_all
    lat_tiles, per_batch = dm.n_lat // tm, dm.s // tm
    qw = ATT_HEADS * ATT_HEAD_DIM
    kw = ATT_KV_HEADS * ATT_HEAD_DIM
    q_blk, k_blk, v_blk = 1536 // qw, 2048 // kw, 2304 // kw
    tab = lambda i: (jnp.where(i < lat_tiles, i % per_batch, 0), 0)
    return pl.pallas_call(
        functools.partial(_att_prep_kernel, lat_tiles=lat_tiles),
        grid=(n // tm,),
        in_specs=[pl.BlockSpec((tm, qw), lambda i: (i, q_blk)),
                  pl.BlockSpec((tm, kw), lambda i: (i, k_blk)),
                  pl.BlockSpec((tm, kw), lambda i: (i, v_blk)),
                  pl.BlockSpec((tm, ATT_HEAD_DIM), tab),
                  pl.BlockSpec((tm, ATT_HEAD_DIM), tab),
                  pl.BlockSpec((1, ATT_HEAD_DIM), lambda i: (0, 0)),
                  pl.BlockSpec((1, ATT_HEAD_DIM), lambda i: (0, 0))],
        out_specs=[pl.BlockSpec((qw, tm), lambda i: (0, i)),
                   pl.BlockSpec((tm, kw), lambda i: (i, 0)),
                   pl.BlockSpec((ATT_KV_HEADS * ATT_V_ROWS, tm), lambda i: (0, i))],
        out_shape=[jax.ShapeDtypeStruct((qw, n), BF16),
                   jax.ShapeDtypeStruct((n, kw), BF16),
                   jax.ShapeDtypeStruct((ATT_KV_HEADS * ATT_V_ROWS, n), BF16)],
        compiler_params=_cparams(("parallel",)),
        name="att_prep",
    )(p, p, p, cos_t, sin_t, gq.reshape(1, -1), gk.reshape(1, -1))


def _attn_kernel(*refs, with_latent_keys):
    ns = ATT_STAGES
    if with_latent_keys:
        qt_ref, kl_ref, vtl_ref, kc_ref, vtc_ref, o_ref, acc_ref = refs[:7]
    else:
        qt_ref, kc_ref, vtc_ref, o_ref, acc_ref = refs[:5]
    s_refs, p_refs = refs[-2 * ns:-ns], refs[-ns:]
    hd = ATT_HEAD_DIM
    grp = ATT_HEADS // ATT_KV_HEADS
    tq = qt_ref.shape[1]
    nq = grp * tq
    tk = kc_ref.shape[0]
    n = 1 + (kl_ref.shape[0] // tk if with_latent_keys else 0)
    qt = jnp.concatenate([qt_ref[j * hd:(j + 1) * hd, :] for j in range(grp)], axis=1)

    def key_chunk(j):
        if isinstance(j, int) and j == 0:
            return kc_ref[...]
        return kl_ref[pl.ds(pl.multiple_of((j - 1) * tk, tk), tk), :]

    def value_chunk(j):
        if isinstance(j, int) and j == 0:
            return vtc_ref[...]
        return vtl_ref[:, pl.ds(pl.multiple_of((j - 1) * tk, tk), tk)]

    def max_keys(x):
        groups = math.gcd(tk // SUBLANES, ATT_REDUCE_GROUPS)
        part = jnp.max(x.reshape(groups, tk // groups, nq), axis=0)
        return jnp.max(part, axis=0, keepdims=True)

    def time_step(t, slot, carry, do_scores=True, do_softmax=True, do_values=True):
        m, alpha = carry
        if do_scores:
            s_refs[slot][...] = jnp.dot(key_chunk(t), qt, preferred_element_type=F32)
        if do_values:
            pv = jnp.dot(value_chunk(t - 2), p_refs[(slot - 2) % ns][...], preferred_element_type=F32)
            acc_ref[...] = acc_ref[...] * alpha + pv
        if do_softmax:
            st = s_refs[(slot - 1) % ns][...]
            m_new = jnp.maximum(m, max_keys(st))
            p_refs[(slot - 1) % ns][...] = jnp.exp2(st - m_new).astype(BF16)
            alpha = jnp.exp2(m - m_new)
            m = m_new
        return m, alpha

    acc_ref[...] = jnp.zeros_like(acc_ref)
    carry = (jnp.full((1, nq), -1e30, F32), jnp.zeros((1, nq), F32))
    fill = min(3, n + 2)
    for t in range(fill):
        carry = time_step(t, t % ns, carry, t < n, 0 <= t - 1 < n, 0 <= t - 2 < n)
    steady = max(n - fill, 0)
    body_steps = ns * ATT_UNROLL

    def unrolled(i, cr):
        for k in range(body_steps):
            cr = time_step(fill + body_steps * i + k, (fill + k) % ns, cr)
        return cr

    if steady // body_steps:
        carry = lax.fori_loop(0, steady // body_steps, unrolled, carry)
    for t in range(fill + body_steps * (steady // body_steps), max(n, fill)):
        carry = time_step(t, t % ns, carry)
    for t in range(max(n, fill), n + 2):
        carry = time_step(t, t % ns, carry, False, t - 1 < n, True)
    out = (acc_ref[0:hd, :] / acc_ref[hd:hd + 1, :]).T
    for j in range(grp):
        o_ref[:, j * hd:(j + 1) * hd] = out[j * tq:(j + 1) * tq].astype(o_ref.dtype)


def attention(dm, qt, kn, vt, latent):
    hd = ATT_HEAD_DIM
    grp = ATT_HEADS // ATT_KV_HEADS
    tk = dm.ctx
    assert dm.s % tk == 0
    ctx_blk0 = dm.n_lat // dm.ctx
    kc_spec = pl.BlockSpec((tk, hd), lambda b, h, i: (ctx_blk0 + b, h))
    vtc_spec = pl.BlockSpec((ATT_V_ROWS, tk), lambda b, h, i: (h, ctx_blk0 + b))
    if latent:
        tq = TQ_ATT
        nq = dm.s // tq
        q_spec = pl.BlockSpec((grp * hd, tq), lambda b, h, i: (h, b * nq + i))
        kl_spec = pl.BlockSpec((dm.s, hd), lambda b, h, i: (b, h))
        vtl_spec = pl.BlockSpec((ATT_V_ROWS, dm.s), lambda b, h, i: (h, b))
        in_specs = [q_spec, kl_spec, vtl_spec, kc_spec, vtc_spec]
        args = (qt, kn, vt, kn, vt)
        rows = dm.n_lat
    else:
        tq, nq = dm.ctx, 1
        q_spec = pl.BlockSpec((grp * hd, tq), lambda b, h, i: (h, ctx_blk0 + b))
        in_specs = [q_spec, kc_spec, vtc_spec]
        args = (qt, kn, vt)
        rows = dm.b * dm.ctx
    return pl.pallas_call(
        functools.partial(_attn_kernel, with_latent_keys=latent),
        grid=(dm.b, ATT_KV_HEADS, nq),
        in_specs=in_specs,
        out_specs=pl.BlockSpec((tq, grp * hd), lambda b, h, i: (b * nq + i, h)),
        out_shape=jax.ShapeDtypeStruct((rows, ATT_HEADS * hd), BF16),
        scratch_shapes=([pltpu.VMEM((ATT_V_ROWS, grp * tq), F32)]
                        + [pltpu.VMEM((tk, grp * tq), F32)] * ATT_STAGES
                        + [pltpu.VMEM((tk, grp * tq), BF16)] * ATT_STAGES),
        compiler_params=_cparams(("parallel", "parallel", "arbitrary")),
        name="attention_latent" if latent else "attention_ctx",
    )(*args)


def _pool_kernel(prev_ref, u_ref, next_ref, w_ref, sc_ref, o_ref, a_ref, b_ref, *, dm):
    tm = u_ref.shape[0]
    H = POOL_HALO
    i = pl.program_id(0)
    lat_tiles = dm.n_lat // tm
    is_lat = i < lat_tiles
    n_seg = jnp.where(is_lat, dm.s, dm.ctx)
    base = jnp.where(is_lat, (i * tm) % dm.s, 0)
    r = lax.broadcasted_iota(jnp.int32, (tm + 2 * H, 1), 0) - H
    pos = jnp.where(is_lat, base + r, r % dm.ctx)
    win = pl.ds(H, tm + 2 * H)

    def shifted(ref, d):
        return ref[pl.ds(H + d, tm + 2 * H), :]

    zeros = jnp.zeros((H, POOL_GROUP_DIM), F32)
    for ref in (a_ref, b_ref):
        ref[pl.ds(0, H), :] = zeros
        ref[pl.ds(tm + 3 * H, H), :] = zeros

    for gi, w in enumerate(POOL_WINDOWS):
        cols = slice(gi * POOL_GROUP_DIM, (gi + 1) * POOL_GROUP_DIM)
        half = w // 2
        a_ref[pl.ds(H, H), :] = prev_ref[:, cols]
        a_ref[pl.ds(2 * H, tm), :] = u_ref[:, cols]
        a_ref[pl.ds(2 * H + tm, H), :] = next_ref[:, cols]
        b_ref[win, :] = jnp.where(pos >= 1, shifted(a_ref, -1), 0.0)
        step = 1
        while step < half:
            b_ref[win, :] = b_ref[win, :] + jnp.where(pos >= step, shifted(b_ref, -step), 0.0)
            step *= 2
        step = 1
        while step < half:
            a_ref[win, :] = a_ref[win, :] + jnp.where(pos + step < n_seg, shifted(a_ref, step), 0.0)
            step *= 2
        tile = pl.ds(2 * H, tm)
        pt = pos[H:H + tm]
        cnt = (jnp.minimum(pt + half, n_seg) - jnp.maximum(pt - half, 0)).astype(F32)
        u = u_ref[:, cols]
        diff = (a_ref[tile, :] + b_ref[tile, :]) / cnt - u
        o_ref[:, cols] = (_dot(diff, w_ref[gi]) * sc_ref[:, cols]).astype(o_ref.dtype)


def pool_mixer(dm, p, pool_w, pool_scale):
    tm = dm.tile(TM_POOL)
    n = dm.n_all
    H = POOL_HALO
    width = len(POOL_WINDOWS) * POOL_GROUP_DIM
    col = 2560 // width
    per = tm // H
    last = n // H - 1
    return pl.pallas_call(
        functools.partial(_pool_kernel, dm=dm),
        grid=(n // tm,),
        in_specs=[pl.BlockSpec((H, width), lambda i: (jnp.maximum(i * per - 1, 0), col)),
                  pl.BlockSpec((tm, width), lambda i: (i, col)),
                  pl.BlockSpec((H, width), lambda i: (jnp.minimum((i + 1) * per, last), col)),
                  pl.BlockSpec(pool_w.shape, lambda i: (0, 0, 0)),
                  pl.BlockSpec((1, width), lambda i: (0, 0))],
        out_specs=pl.BlockSpec((tm, width), lambda i: (i, 0)),
        out_shape=jax.ShapeDtypeStruct((n, width), BF16),
        scratch_shapes=[pltpu.VMEM((tm + 4 * H, POOL_GROUP_DIM), F32),
                        pltpu.VMEM((tm + 4 * H, POOL_GROUP_DIM), F32)],
        compiler_params=_cparams(("parallel",)),
        name="pool_mixer",
    )(p, p, p, pool_w, pool_scale.reshape(1, width))


def _merge_kernel(yr_ref, ya_ref, yp_ref, gl_ref, *refs, lat_tiles):
    mod_ref, wb_ref, wo_ref, o_ref = refs[-4:]
    x_refs = refs[:-4]
    d = o_ref.shape[1]
    mixed = None
    for j, y_ref in enumerate((yr_ref, ya_ref, yp_ref)):
        gate = gl_ref[:, j * d:(j + 1) * d].astype(F32)
        term = gate * jnp.dot(y_ref[...], wb_ref[j], preferred_element_type=F32)
        mixed = term if mixed is None else mixed + term
    y = mod_ref[0][2:3] * _dot(mixed, wo_ref[...])

    def finish(x_ref):
        o_ref[...] = x_ref[...] + y

    _for_row_source(x_refs, lat_tiles, finish)


def merge(dm, n_rows, yr, ya, yp, gates, x_parts, mod, wb, wo):
    tm = dm.tile(TM_MERGE)
    d = dm.d
    bw = yr.shape[1]
    row = dm.mod_row(tm)
    return pl.pallas_call(
        functools.partial(_merge_kernel, lat_tiles=dm.n_lat // tm),
        grid=(n_rows // tm,),
        in_specs=[pl.BlockSpec((tm, bw), lambda i: (i, 0)),
                  pl.BlockSpec((tm, bw), lambda i: (i, 0)),
                  pl.BlockSpec((tm, bw), lambda i: (i, 0)),
                  pl.BlockSpec((tm, 3 * d), lambda i: (i, 0))]
        + _row_sources(dm, tm, x_parts) + [
            pl.BlockSpec((1, 6, d), lambda i: (row(i), 0, 0)),
            pl.BlockSpec(wb.shape, lambda i: (0, 0, 0)),
            pl.BlockSpec(wo.shape, lambda i: (0, 0))],
        out_specs=pl.BlockSpec((tm, d), lambda i: (i, 0)),
        out_shape=jax.ShapeDtypeStruct((n_rows, d), F32),
        compiler_params=_cparams(("parallel",)),
        name="merge",
    )(yr, ya, yp, gates, *x_parts, mod, wb, wo)


def _ffn_kernel(x_ref, mod_ref, gain_ref, w1_ref, w3_ref, w2_ref, o_ref, h_ref, acc_ref):
    f = pl.program_id(1)
    m = mod_ref[0]

    @pl.when(f == 0)
    def _():
        h_ref[...] = _norm_mod(x_ref[...], gain_ref[...], m[3:4], m[4:5]).astype(BF16)
        acc_ref[...] = jnp.zeros_like(acc_ref)

    h = h_ref[...]
    a = jnp.dot(h, w1_ref[...], preferred_element_type=F32)
    b = jnp.dot(h, w3_ref[...], preferred_element_type=F32)
    acc_ref[...] += _dot(_silu(a) * b, w2_ref[...])

    @pl.when(f == pl.num_programs(1) - 1)
    def _():
        o_ref[...] = x_ref[...] + m[5:6] * acc_ref[...]


def ffn_dense(dm, x, mod, gain, w1, w3, w2):
    n, d = x.shape
    fdim = w1.shape[1]
    tm, tf = dm.tile(TM_FFN), TF_FFN
    row = dm.mod_row(tm)
    return pl.pallas_call(
        _ffn_kernel,
        grid=(n // tm, fdim // tf),
        in_specs=[pl.BlockSpec((tm, d), lambda i, f: (i, 0)),
                  pl.BlockSpec((1, 6, d), lambda i, f: (row(i), 0, 0)),
                  pl.BlockSpec((1, d), lambda i, f: (0, 0)),
                  pl.BlockSpec((d, tf), lambda i, f: (0, f)),
                  pl.BlockSpec((d, tf), lambda i, f: (0, f)),
                  pl.BlockSpec((tf, d), lambda i, f: (f, 0))],
        out_specs=pl.BlockSpec((tm, d), lambda i, f: (i, 0)),
        out_shape=jax.ShapeDtypeStruct((n, d), F32),
        scratch_shapes=[pltpu.VMEM((tm, d), BF16), pltpu.VMEM((tm, d), F32)],
        compiler_params=_cparams(("parallel", "arbitrary")),
        name="ffn_dense",
    )(x, mod, gain.reshape(1, d), w1, w3, w2)


def _router_kernel(x_ref, mod_ref, gain_ref, wr_ref, br_ref, h_ref, slot_ref, slot_t_ref, wt_ref, cum_ref):
    tm = x_ref.shape[0]
    m = mod_ref[0]
    h = _norm_mod(x_ref[...], gain_ref[...], m[3:4], m[4:5])
    h_ref[...] = h.astype(BF16)
    logits = _dot3(h, wr_ref[...]) + br_ref[...]
    lane = lax.broadcasted_iota(jnp.int32, logits.shape, 1)
    m1 = jnp.max(logits, axis=-1, keepdims=True)
    i1 = jnp.min(jnp.where(logits == m1, lane, LANES), axis=-1, keepdims=True)
    rest = jnp.where(lane == i1, -jnp.inf, logits)
    m2 = jnp.max(rest, axis=-1, keepdims=True)
    i2 = jnp.min(jnp.where(rest == m2, lane, LANES), axis=-1, keepdims=True)
    e = jnp.exp(m2 - m1)
    w1 = 1.0 / (1.0 + e)
    w2 = e / (1.0 + e)
    wt_ref[...] = jnp.where(lane == i1, w1, jnp.where(lane == i2, w2, 0.0))
    routed = jnp.where((lane == i1) | (lane == i2), 1.0, 0.0)
    sub = SUB_MOE
    lower = (lax.broadcasted_iota(jnp.int32, (sub, sub), 0) > lax.broadcasted_iota(jnp.int32, (sub, sub), 1))
    lower = lower.astype(BF16)
    offset = jnp.zeros((1, LANES), F32)
    cum_ref[...] = jnp.zeros_like(cum_ref)
    for blk in range(tm // sub):
        rows = slice(blk * sub, (blk + 1) * sub)
        mask = routed[rows]
        cum_ref[0, blk:blk + 1, :] = offset.astype(jnp.int32)
        rank = jnp.dot(lower, mask.astype(BF16), preferred_element_type=F32) + offset
        slot_ref[rows, :] = jnp.where(mask > 0.0, rank, -1.0)
        count = jnp.sum(mask, axis=0, keepdims=True)
        offset = offset + jnp.ceil(count * (1.0 / SLOT_ALIGN)) * SLOT_ALIGN
    cum_ref[0, tm // sub:tm // sub + 1, :] = offset.astype(jnp.int32)
    slot_t_ref[...] = slot_ref[...].T[:SUBLANES, :]


def moe_route(dm, n_rows, x, mod, gain, wr_pad, br_pad):
    tm = dm.tile(TM_MOE)
    d = dm.d
    row = dm.mod_row(tm)
    nt = n_rows // tm
    cum_rows = -(-(tm // SUB_MOE + 1) // SUBLANES) * SUBLANES
    return pl.pallas_call(
        _router_kernel,
        grid=(nt,),
        in_specs=[pl.BlockSpec((tm, d), lambda i: (i, 0)),
                  pl.BlockSpec((1, 6, d), lambda i: (row(i), 0, 0)),
                  pl.BlockSpec((1, d), lambda i: (0, 0)),
                  pl.BlockSpec((d, LANES), lambda i: (0, 0)),
                  pl.BlockSpec((1, LANES), lambda i: (0, 0))],
        out_specs=[pl.BlockSpec((tm, d), lambda i: (i, 0)),
                   pl.BlockSpec((tm, LANES), lambda i: (i, 0)),
                   pl.BlockSpec((SUBLANES, tm), lambda i: (0, i)),
                   pl.BlockSpec((tm, LANES), lambda i: (i, 0)),
                   pl.BlockSpec((1, cum_rows, LANES), lambda i: (i, 0, 0))],
        out_shape=[jax.ShapeDtypeStruct((n_rows, d), BF16),
                   jax.ShapeDtypeStruct((n_rows, LANES), F32),
                   jax.ShapeDtypeStruct((SUBLANES, n_rows), F32),
                   jax.ShapeDtypeStruct((n_rows, LANES), F32),
                   jax.ShapeDtypeStruct((nt, cum_rows, LANES), jnp.int32)],
        compiler_params=_cparams(("parallel",)),
        name="moe_router",
    )(x, mod, gain.reshape(1, d), wr_pad, br_pad)


def _moe_kernel(cum_ref, h_ref, slot_ref, slot_t_ref, wt_ref, w1_ref, w3_ref, w2_ref, o_ref,
                xs_ref, acc_ref):
    i, e, f = pl.program_id(0), pl.program_id(1), pl.program_id(2)
    tm, d = h_ref.shape
    rb, win, sub = RB_MOE, WIN_MOE, SUB_MOE
    nsub = tm // sub

    def base(j):
        return cum_ref[i, j * N_EXPERTS + e]

    halves = (base(nsub) + rb // 2 - 1) // (rb // 2)
    nblk, has_tail = halves // 2, halves % 2 == 1

    def block_rows(r, size=rb):
        return pl.ds(pl.multiple_of(r * rb, 32), size)

    def windows(j):
        return base(j), (base(j + 1) - base(j) + win - 1) // win

    def window_rows(start, w):
        return pl.ds(pl.multiple_of(start + w * win, SLOT_ALIGN), win)

    @pl.when((e == 0) & (f == 0))
    def _():
        o_ref[...] = jnp.zeros_like(o_ref)

    @pl.when(f == 0)
    def _():
        def clear(r, carry):
            xs_ref[block_rows(r), :] = jnp.zeros((rb, d), BF16)
            acc_ref[block_rows(r), :] = jnp.zeros((rb, d), F32)
            return carry

        lax.fori_loop(0, nblk + 1, clear, 0)
        acc_ref[pl.ds(pl.multiple_of((nblk + 1) * rb, 32), win), :] = jnp.zeros((win, d), F32)

        for j in range(nsub):
            tok = slice(j * sub, (j + 1) * sub)
            start, nwin = windows(j)

            def gather(w, carry, tok=tok, start=start):
                rows = window_rows(start, w)
                want = (lax.broadcasted_iota(jnp.int32, (win, sub), 0) + (start + w * win)).astype(F32)
                onehot = (slot_t_ref[pl.ds(e, 1), tok] == want).astype(BF16)
                xs_ref[rows, :] = jnp.dot(onehot, h_ref[tok, :], preferred_element_type=F32).astype(BF16)
                return carry

            lax.fori_loop(0, nwin, gather, 0)

    def expert(r, carry, size=rb):
        rows = block_rows(r, size)
        xs = xs_ref[rows, :]
        tf = w1_ref.shape[2]
        y = None
        for lo in range(0, tf, MOE_COL_CHUNK):
            cols = slice(lo, min(lo + MOE_COL_CHUNK, tf))
            a = jnp.dot(xs, w1_ref[0, :, cols], preferred_element_type=F32)
            b = jnp.dot(xs, w3_ref[0, :, cols], preferred_element_type=F32)
            part = _dot(_silu(a) * b, w2_ref[0, cols, :])
            y = part if y is None else y + part
        acc_ref[rows, :] += y
        return carry

    lax.fori_loop(0, nblk, expert, 0)

    @pl.when(has_tail)
    def _():
        expert(nblk, 0, rb // 2)

    @pl.when(f == pl.num_programs(2) - 1)
    def _():
        pick = lax.broadcasted_iota(jnp.int32, (sub, LANES), 1) == e
        for j in range(nsub):
            tok = slice(j * sub, (j + 1) * sub)
            start, nwin = windows(j)
            slot_col = jnp.sum(jnp.where(pick, slot_ref[tok, :], 0.0), axis=-1, keepdims=True)
            wt_col = jnp.sum(jnp.where(pick, wt_ref[tok, :], 0.0), axis=-1, keepdims=True)

            def scatter(w, carry, tok=tok, start=start, slot_col=slot_col, wt_col=wt_col):
                ys = acc_ref[window_rows(start, w), :].astype(BF16)
                want = (lax.broadcasted_iota(jnp.int32, (sub, win), 1) + (start + w * win)).astype(F32)
                onehot = (slot_col == want).astype(BF16)
                o_ref[tok, :] += wt_col * jnp.dot(onehot, ys, preferred_element_type=F32)
                return carry

            lax.fori_loop(0, nwin, scatter, 0)


def moe_experts(dm, n_rows, h, slot, slot_t, wt, counts, w1, w3, w2):
    tm = dm.tile(TM_MOE)
    d = dm.d
    ne, _, fdim = w1.shape
    tf = TF_MOE if fdim % TF_MOE == 0 else TF_FFN
    assert SUB_MOE % SLOT_ALIGN == 0 and tm % SUB_MOE == 0
    cap = tm + RB_MOE + WIN_MOE
    once = dict(pipeline_mode=pl.Buffered(1))
    grid_spec = pltpu.PrefetchScalarGridSpec(
        num_scalar_prefetch=1,
        grid=(n_rows // tm, ne, fdim // tf),
        in_specs=[pl.BlockSpec((tm, d), lambda i, e, f, c: (i, 0), **once),
                  pl.BlockSpec((tm, LANES), lambda i, e, f, c: (i, 0), **once),
                  pl.BlockSpec((SUBLANES, tm), lambda i, e, f, c: (0, i), **once),
                  pl.BlockSpec((tm, LANES), lambda i, e, f, c: (i, 0), **once),
                  pl.BlockSpec((1, d, tf), lambda i, e, f, c: (e, 0, f)),
                  pl.BlockSpec((1, d, tf), lambda i, e, f, c: (e, 0, f)),
                  pl.BlockSpec((1, tf, d), lambda i, e, f, c: (e, f, 0))],
        out_specs=pl.BlockSpec((tm, d), lambda i, e, f, c: (i, 0), **once),
        scratch_shapes=[pltpu.VMEM((cap, d), BF16), pltpu.VMEM((cap, d), F32)],
    )
    return pl.pallas_call(
        _moe_kernel,
        grid_spec=grid_spec,
        out_shape=jax.ShapeDtypeStruct((n_rows, d), F32),
        compiler_params=_cparams(("parallel", "arbitrary", "arbitrary")),
        name="moe_experts",
    )(counts, h, slot, slot_t, wt, w1, w3, w2)


def _residual_kernel(x_ref, y_ref, mod_ref, o_ref):
    o_ref[...] = x_ref[...] + mod_ref[0][5:6] * y_ref[...]


def _final_kernel(x_ref, y_ref, mod_ref, gain_ref, o_ref):
    x = x_ref[...] + mod_ref[0][5:6] * y_ref[...]
    o_ref[...] = x * lax.rsqrt(jnp.mean(x * x, axis=-1, keepdims=True) + NORM_EPS) * gain_ref[...]


def residual(dm, n_rows, x, y, mod, final_gain=None):
    tm = dm.tile(TM_FFN)
    d = dm.d
    row = dm.mod_row(tm)
    in_specs = [pl.BlockSpec((tm, d), lambda i: (i, 0)),
                pl.BlockSpec((tm, d), lambda i: (i, 0)),
                pl.BlockSpec((1, 6, d), lambda i: (row(i), 0, 0))]
    args = [x, y, mod]
    if final_gain is not None:
        in_specs.append(pl.BlockSpec((1, d), lambda i: (0, 0)))
        args.append(final_gain.reshape(1, d))
    return pl.pallas_call(
        _residual_kernel if final_gain is None else _final_kernel,
        grid=(n_rows // tm,),
        in_specs=in_specs,
        out_specs=pl.BlockSpec((tm, d), lambda i: (i, 0)),
        out_shape=jax.ShapeDtypeStruct((n_rows, d), F32),
        compiler_params=_cparams(("parallel",)),
        name="residual" if final_gain is None else "residual_final_norm",
    )(*args)


def _final_norm_kernel(x_ref, gain_ref, o_ref):
    x = x_ref[...]
    o_ref[...] = x * lax.rsqrt(jnp.mean(x * x, axis=-1, keepdims=True) + NORM_EPS) * gain_ref[...]


def final_rmsnorm(dm, n_rows, x, gain):
    tm = dm.tile(TM_FFN)
    d = dm.d
    return pl.pallas_call(
        _final_norm_kernel,
        grid=(n_rows // tm,),
        in_specs=[pl.BlockSpec((tm, d), lambda i: (i, 0)), pl.BlockSpec((1, d), lambda i: (0, 0))],
        out_specs=pl.BlockSpec((tm, d), lambda i: (i, 0)),
        out_shape=jax.ShapeDtypeStruct((n_rows, d), F32),
        compiler_params=_cparams(("parallel",)),
        name="final_norm",
    )(x, gain.reshape(1, d))


def _rope_tables(s, ctx):
    def angles(pos, dim):
        freqs = 1.0 / (ROPE_THETA ** (jnp.arange(0, dim, 2, dtype=F32) / dim))
        return pos[:, None] * freqs[None, :]

    rows = s // GRID_W
    row = jnp.repeat(jnp.arange(rows, dtype=F32), GRID_W)
    col = jnp.tile(jnp.arange(GRID_W, dtype=F32), rows)
    half = ATT_HEAD_DIM // 2
    ang_att = jnp.concatenate([angles(row, half), angles(col, half)], axis=-1)
    sign = jnp.tile(jnp.array([-1.0, 1.0], F32), LANES // 2)

    def tables(ang):
        reps = LANES // (2 * ang.shape[1])
        cos = jnp.tile(jnp.repeat(jnp.cos(ang), 2, axis=-1), (1, reps))
        sin = jnp.tile(jnp.repeat(jnp.sin(ang), 2, axis=-1), (1, reps)) * sign
        return cos, sin

    ang_ret = angles(jnp.arange(s, dtype=F32), RET_QK_DIM)
    ang_ret = jnp.concatenate([ang_ret, jnp.zeros((ctx, RET_QK_DIM // 2), F32)], axis=0)
    return tables(ang_att) + tables(ang_ret)


def kernel(x, c, ctx, c_ctx, w_ada, b_ada, norm_mix, norm_ffn, w_in, ret_decay, ret_gn, attn_qn, attn_kn, pool_w, pool_scale, w_branch, w_out, ffn_w1, ffn_w3, ffn_w2, moe_router, moe_router_b, moe_w1, moe_w3, moe_w2, final_norm):
    b, s, d = x.shape
    n_ctx = ctx.shape[1]
    depth = w_in.shape[0]
    dm = Dims(b, s, n_ctx, d)

    x_parts = (x.reshape(b * s, d), ctx.reshape(b * n_ctx, d))
    mod_rows = 2 * SUBLANES
    cc = jnp.concatenate([c, c_ctx[None, :], jnp.zeros((mod_rows - b - 1, d), F32)], axis=0)
    mods = ada_table(cc, w_ada, b_ada).reshape(depth, mod_rows, 6, d)

    att_cos, att_sin, ret_cos, ret_sin = _rope_tables(s, n_ctx)

    for i in range(depth):
        need_ctx = i < depth - 1
        n_rows = dm.n_all if need_ctx else dm.n_lat
        mod = mods[i]
        p, gates = proj_in(dm, x_parts, mod, norm_mix[i], w_in[i].astype(BF16))
        y_ret = retention(dm, p, ret_decay[i], ret_cos, ret_sin, ret_gn[i])
        y_pool = pool_mixer(dm, p, pool_w[i].astype(BF16), pool_scale[i])
        qt, kn, vt = att_prep(dm, p, att_cos, att_sin, attn_qn[i], attn_kn[i])
        y_att = attention(dm, qt, kn, vt, latent=True)
        if need_ctx:
            y_att = jnp.concatenate([y_att, attention(dm, qt, kn, vt, latent=False)], axis=0)
        xa = merge(dm, n_rows, y_ret, y_att, y_pool, gates, x_parts, mod,
                   w_branch[i].astype(BF16), w_out[i].astype(BF16))
        j = i // 2
        last = i == depth - 1
        if i % 2 == 0:
            xa = ffn_dense(dm, xa, mod, norm_ffn[i], ffn_w1[j].astype(BF16), ffn_w3[j].astype(BF16),
                           ffn_w2[j].astype(BF16))
            if last:
                xa = final_rmsnorm(dm, n_rows, xa, final_norm)
        else:
            wr = jnp.zeros((d, LANES), F32).at[:, :N_EXPERTS].set(moe_router[j])
            br = jnp.full((1, LANES), -1e30, F32).at[0, :N_EXPERTS].set(moe_router_b[j])
            h, slot, slot_t, wt, cnt = moe_route(dm, n_rows, xa, mod, norm_ffn[i], wr, br)
            nsub = dm.tile(TM_MOE) // SUB_MOE
            cum = cnt[:, :nsub + 1, :N_EXPERTS].reshape(cnt.shape[0], (nsub + 1) * N_EXPERTS)
            y = moe_experts(dm, n_rows, h, slot, slot_t, wt, cum,
                            moe_w1[j].astype(BF16), moe_w3[j].astype(BF16), moe_w2[j].astype(BF16))
            xa = residual(dm, n_rows, xa, y, mod, final_norm if last else None)
        x_parts = (xa,)
    return xa[:dm.n_lat].reshape(b, s, d)
```

```python
import functools
import math

import numpy as np
import jax
import jax.numpy as jnp
from jax import lax
from jax.experimental import pallas as pl
from jax.experimental.pallas import tpu as pltpu

F32 = jnp.float32
BF16 = jnp.bfloat16

GRID_W = 64
RET_HEADS = 4
RET_QK_DIM = 64
RET_V_DIM = 128
ATT_HEADS = 4
ATT_KV_HEADS = 2
ATT_HEAD_DIM = 128
ROPE_THETA = 10000.0
POOL_WINDOWS = (2, 4, 8, 16)
POOL_GROUP_DIM = 128
N_EXPERTS = 8
NORM_EPS = 1e-6
GN_EPS = 1e-5

LANES = 128
SUBLANES = 8
VMEM_LIMIT_BYTES = 56 * 1024 * 1024

TM_PROJ = 1024
TN_PROJ = 1536
TM_PREP = 1024
TM_POOL = 1024
TM_MERGE = 1024
TM_FFN = 1024
TF_FFN = 512
TM_MOE = 2048
TF_MOE = 1792
MOE_COL_CHUNK = 512
RB_MOE = 256
WIN_MOE = 256
SUB_MOE = 512
SLOT_ALIGN = 16
assert RB_MOE % 64 == 0
RET_L = 256
TQ_ATT = 512
ATT_SCORE_SCALE = (ATT_HEAD_DIM ** -0.5) * math.log2(math.e)
ATT_V_ROWS = ATT_HEAD_DIM + 16
ATT_STAGES = 3
ATT_UNROLL = 4
ATT_REDUCE_GROUPS = 8
POOL_HALO = 8


def _cparams(sem):
    return pltpu.CompilerParams(dimension_semantics=sem, vmem_limit_bytes=VMEM_LIMIT_BYTES)


def _dot(a, b):
    return jnp.dot(a.astype(BF16), b.astype(BF16), preferred_element_type=F32)


def _dot_nt(a, b):
    return lax.dot_general(a.astype(BF16), b.astype(BF16), (((1,), (1,)), ((), ())),
                           preferred_element_type=F32)


def _dot_tn(a, b):
    return lax.dot_general(a.astype(BF16), b.astype(BF16), (((0,), (0,)), ((), ())),
                           preferred_element_type=F32)


def _split_bf16(a):
    hi = a.astype(BF16)
    lo = (a - hi.astype(F32)).astype(BF16)
    return hi, lo


def _dot3(a, b):
    ah, al = _split_bf16(a)
    bh, bl = _split_bf16(b)
    d = functools.partial(jnp.dot, preferred_element_type=F32)
    return d(ah, bh) + (d(al, bh) + d(ah, bl))


def _norm_mod(x, gain, shift, scale):
    y = x * lax.rsqrt(jnp.mean(x * x, axis=-1, keepdims=True) + NORM_EPS)
    return (y * gain) * (1.0 + scale) + shift


def _sigmoid(x):
    return 1.0 / (1.0 + jnp.exp(-x))


def _silu(x):
    return x * _sigmoid(x)


class Dims:
    def __init__(self, b, s, ctx, d):
        self.b, self.s, self.ctx, self.d = b, s, ctx, d
        self.n_lat = b * s
        self.n_all = b * s + b * ctx

    def tile(self, preferred):
        t = preferred
        while self.s % t or (self.b * self.ctx) % t:
            t //= 2
        return t

    def mod_row(self, tm):
        lat_tiles, per_batch, b = self.n_lat // tm, self.s // tm, self.b
        return lambda i: jnp.where(i < lat_tiles, i // per_batch, b)


def _ada_kernel(c_ref, w_ref, b_ref, o_ref):
    o_ref[0] = _dot3(_silu(c_ref[...]), w_ref[0]) + b_ref[0]


def ada_table(cc, w_ada, b_ada):
    depth, d, n6 = w_ada.shape
    tn = n6 // 4
    rows = cc.shape[0]
    return pl.pallas_call(
        _ada_kernel,
        grid=(depth, n6 // tn),
        in_specs=[pl.BlockSpec((rows, d), lambda l, j: (0, 0)),
                  pl.BlockSpec((1, d, tn), lambda l, j: (l, 0, j)),
                  pl.BlockSpec((1, 1, tn), lambda l, j: (l, 0, j))],
        out_specs=pl.BlockSpec((1, rows, tn), lambda l, j: (l, 0, j)),
        out_shape=jax.ShapeDtypeStruct((depth, rows, n6), F32),
        compiler_params=_cparams(("arbitrary", "arbitrary")),
        name="ada_table",
    )(cc, w_ada, b_ada.reshape(depth, 1, n6))


def _row_sources(dm, tm, parts):
    d = dm.d
    if len(parts) == 1:
        return [pl.BlockSpec((tm, d), lambda i, *_: (i, 0))]
    lat_tiles = dm.n_lat // tm
    return [pl.BlockSpec((tm, d), lambda i, *_: (jnp.minimum(i, lat_tiles - 1), 0)),
            pl.BlockSpec((tm, d), lambda i, *_: (jnp.maximum(i - lat_tiles, 0), 0))]


def _for_row_source(x_refs, lat_tiles, fn):
    if len(x_refs) == 1:
        fn(x_refs[0])
        return
    is_lat = pl.program_id(0) < lat_tiles
    pl.when(is_lat)(lambda: fn(x_refs[0]))
    pl.when(jnp.logical_not(is_lat))(lambda: fn(x_refs[1]))


def _proj_in_kernel(*refs, lat_tiles):
    mod_ref, gain_ref, wm_ref, wg_ref, o_ref, g_ref, h_ref = refs[-7:]
    x_refs = refs[:-7]

    @pl.when(pl.program_id(1) == 0)
    def _():
        def fill(x_ref):
            m = mod_ref[0]
            h_ref[...] = _norm_mod(x_ref[...], gain_ref[...], m[0:1], m[1:2]).astype(BF16)

        _for_row_source(x_refs, lat_tiles, fill)

    h = h_ref[...]
    o_ref[...] = jnp.dot(h, wm_ref[...], preferred_element_type=F32)
    logits = jnp.dot(h, wg_ref[...], preferred_element_type=F32)
    g_ref[...] = (0.5 * jnp.tanh(0.5 * logits) + 0.5).astype(g_ref.dtype)


def proj_in(dm, x_parts, mod, gain, w):
    n, d = dm.n_all, dm.d
    nout = w.shape[1]
    gate_w = 3 * d
    tm, tn = dm.tile(TM_PROJ), TN_PROJ
    assert nout - gate_w == gate_w and gate_w % tn == 0
    gate_blk0 = gate_w // tn
    row = dm.mod_row(tm)
    return pl.pallas_call(
        functools.partial(_proj_in_kernel, lat_tiles=dm.n_lat // tm),
        grid=(n // tm, gate_w // tn),
        in_specs=_row_sources(dm, tm, x_parts) + [
            pl.BlockSpec((1, 6, d), lambda i, j: (row(i), 0, 0)),
            pl.BlockSpec((1, d), lambda i, j: (0, 0)),
            pl.BlockSpec((d, tn), lambda i, j: (0, j)),
            pl.BlockSpec((d, tn), lambda i, j: (0, gate_blk0 + j))],
        out_specs=[pl.BlockSpec((tm, tn), lambda i, j: (i, j)),
                   pl.BlockSpec((tm, tn), lambda i, j: (i, j))],
        out_shape=[jax.ShapeDtypeStruct((n, nout - gate_w), F32),
                   jax.ShapeDtypeStruct((n, gate_w), BF16)],
        scratch_shapes=[pltpu.VMEM((tm, d), BF16)],
        compiler_params=_cparams(("parallel", "arbitrary")),
        name="proj_in",
    )(*x_parts, mod, gain.reshape(1, d), w, w)


def _rope_pairs(x, cos, sin):
    lane = lax.broadcasted_iota(jnp.int32, x.shape, 1)
    partner = jnp.where(lane % 2 == 0, pltpu.roll(x, LANES - 1, 1), pltpu.roll(x, 1, 1))
    return x * cos + partner * sin


def _ret_rope(x, cos, sin):
    return jnp.concatenate([_rope_pairs(x[:, j * LANES:(j + 1) * LANES], cos, sin)
                            for j in range(x.shape[1] // LANES)], axis=1)


def _ret_kernel(dec_ref, q_ref, g_ref, kl_ref, kc_ref, vl_ref, vc_ref, cos_ref, sin_ref, gn_ref, o_ref,
                kr_ref, vs_ref, sf_ref, sb_ref, sball_ref, decay_ref, *, ncc, nlc):
    s = pl.program_id(1)
    L = RET_L
    nc = ncc + nlc
    pos = lax.broadcasted_iota(jnp.int32, (L, 1), 0).astype(F32)

    def log_g(direction, h):
        return -jnp.exp(jnp.full((1, 1), dec_ref[direction, h], F32))

    def chunk(c):
        if isinstance(c, int):
            return pl.ds(c * L, L)
        return pl.ds(pl.multiple_of(c * L, L), L)

    def table(c):
        if isinstance(c, int):
            return chunk(nlc + c if c < ncc else c - ncc)
        return chunk(jnp.where(c < ncc, nlc + c, c - ncc))

    def head(x, h, width):
        return x[:, h * width:(h + 1) * width]

    @pl.when(s == 0)
    def _():
        def stage(c, k, v):
            kr_ref[chunk(c), :] = _ret_rope(k, cos_ref[table(c), :], sin_ref[table(c), :]) * (RET_QK_DIM ** -0.5)
            vs_ref[chunk(c), :] = v.astype(vs_ref.dtype)

        for c in range(ncc):
            stage(c, kc_ref[c * L:(c + 1) * L, :], vc_ref[c * L:(c + 1) * L, :])

        def stage_latent(j, carry):
            stage(ncc + j, kl_ref[chunk(j), :], vl_ref[chunk(j), :])
            return carry

        lax.fori_loop(0, nlc, stage_latent, 0)

        diff = (lax.broadcasted_iota(jnp.int32, (L, L), 0)
                - lax.broadcasted_iota(jnp.int32, (L, L), 1)).astype(F32)
        for h in range(RET_HEADS):
            decay_ref[h] = (jnp.where(diff >= 0, jnp.exp(jnp.maximum(diff, 0.0) * log_g(0, h)), 0.0)
                            + jnp.where(diff <= 0, jnp.exp(jnp.maximum(-diff, 0.0) * log_g(1, h)), 0.0))

        sf_ref[...] = jnp.zeros_like(sf_ref)
        sb_ref[...] = jnp.zeros_like(sb_ref)

        def visit(c, carry):
            k = kr_ref[chunk(c), :]
            v = vs_ref[chunk(c), :]
            for h in range(RET_HEADS):
                lgb = log_g(1, h)
                st = sb_ref[h]
                sball_ref[c, h] = st
                kz = head(k, h, RET_QK_DIM) * jnp.exp(pos * lgb)
                sb_ref[h] = jnp.exp(L * lgb) * st + _dot_tn(kz, head(v, h, RET_V_DIM))
            return carry

        for c in reversed(range(ncc)):
            visit(c, 0)
        lax.fori_loop(0, nlc, lambda j, carry: visit(nc - 1 - j, carry), 0)

    q = _ret_rope(q_ref[...], cos_ref[table(s), :], sin_ref[table(s), :])
    k = kr_ref[chunk(s), :]
    v = vs_ref[chunk(s), :]
    g = g_ref[...]
    for h in range(RET_HEADS):
        lgf = log_g(0, h)
        lgb = log_g(1, h)
        qh = head(q, h, RET_QK_DIM)
        kh = head(k, h, RET_QK_DIM)
        vh = head(v, h, RET_V_DIM)
        st_f = sf_ref[h]
        st_b = sball_ref[s, h]
        o = _dot(_dot_nt(qh, kh) * decay_ref[h], vh)
        o = o + _dot(qh, st_f) * jnp.exp((pos + 1.0) * lgf)
        o = o + _dot(qh, st_b) * jnp.exp((L - pos) * lgb)
        sf_ref[h] = jnp.exp(L * lgf) * st_f + _dot_tn(kh * jnp.exp((L - 1.0 - pos) * lgf), vh)
        mu = jnp.mean(o, axis=-1, keepdims=True)
        var = jnp.mean(jnp.square(o - mu), axis=-1, keepdims=True)
        o = (o - mu) * lax.rsqrt(var + GN_EPS) * head(gn_ref[...], h, RET_V_DIM)
        o_ref[:, h * RET_V_DIM:(h + 1) * RET_V_DIM] = (_silu(head(g, h, RET_V_DIM)) * o).astype(o_ref.dtype)


def retention(dm, p, dec, cos_t, sin_t, gn):
    L = RET_L
    ncc, nlc = dm.ctx // L, dm.s // L
    nc = ncc + nlc
    lat_blocks = dm.n_lat // L
    ctx_blk0 = dm.n_lat // dm.ctx
    qk_w = RET_HEADS * RET_QK_DIM
    v_w = RET_HEADS * RET_V_DIM

    def rows(b, s):
        return jnp.where(s < ncc, lat_blocks + b * ncc + s, b * nlc + s - ncc)

    whole = lambda shape: pl.BlockSpec(shape, lambda b, s, d: (0, 0))
    grid_spec = pltpu.PrefetchScalarGridSpec(
        num_scalar_prefetch=1,
        grid=(dm.b, nc),
        in_specs=[pl.BlockSpec((L, qk_w), lambda b, s, d: (rows(b, s), 0)),
                  pl.BlockSpec((L, v_w), lambda b, s, d: (rows(b, s), 2)),
                  pl.BlockSpec((dm.s, qk_w), lambda b, s, d: (b, 1)),
                  pl.BlockSpec((dm.ctx, qk_w), lambda b, s, d: (ctx_blk0 + b, 1)),
                  pl.BlockSpec((dm.s, v_w), lambda b, s, d: (b, 1)),
                  pl.BlockSpec((dm.ctx, v_w), lambda b, s, d: (ctx_blk0 + b, 1)),
                  whole(cos_t.shape), whole(sin_t.shape), whole((1, v_w))],
        out_specs=pl.BlockSpec((L, v_w), lambda b, s, d: (rows(b, s), 0)),
        scratch_shapes=[pltpu.VMEM((nc * L, qk_w), F32),
                        pltpu.VMEM((nc * L, v_w), BF16),
                        pltpu.VMEM((RET_HEADS, RET_QK_DIM, RET_V_DIM), F32),
                        pltpu.VMEM((RET_HEADS, RET_QK_DIM, RET_V_DIM), F32),
                        pltpu.VMEM((nc, RET_HEADS, RET_QK_DIM, RET_V_DIM), F32),
                        pltpu.VMEM((RET_HEADS, L, L), F32)],
    )
    return pl.pallas_call(
        functools.partial(_ret_kernel, ncc=ncc, nlc=nlc),
        grid_spec=grid_spec,
        out_shape=jax.ShapeDtypeStruct((dm.n_all, v_w), BF16),
        compiler_params=_cparams(("parallel", "arbitrary")),
        name="retention",
    )(dec, p, p, p, p, p, p, cos_t, sin_t, gn.reshape(1, v_w))


def _att_prep_kernel(q_ref, k_ref, v_ref, cos_ref, sin_ref, gq_ref, gk_ref, qo_ref, ko_ref, vo_ref,
                     *, lat_tiles):
    is_lat = pl.program_id(0) < lat_tiles
    cos = jnp.where(is_lat, cos_ref[...], 1.0)
    sin = jnp.where(is_lat, sin_ref[...], 0.0)

    hd = ATT_HEAD_DIM

    def prep(x_ref, gain_ref, h):
        xh = x_ref[:, h * hd:(h + 1) * hd]
        xn = xh * lax.rsqrt(jnp.mean(xh * xh, axis=-1, keepdims=True) + NORM_EPS) * gain_ref[...]
        return _rope_pairs(xn, cos, sin)

    for h in range(ATT_HEADS):
        qo_ref[h * hd:(h + 1) * hd, :] = (prep(q_ref, gq_ref, h) * ATT_SCORE_SCALE).T.astype(qo_ref.dtype)
    ones = jnp.ones((ATT_V_ROWS - hd, vo_ref.shape[1]), vo_ref.dtype)
    for h in range(ATT_KV_HEADS):
        ko_ref[:, h * hd:(h + 1) * hd] = prep(k_ref, gk_ref, h).astype(ko_ref.dtype)
        vo_ref[h * ATT_V_ROWS:h * ATT_V_ROWS + hd, :] = v_ref[:, h * hd:(h + 1) * hd].T.astype(vo_ref.dtype)
        vo_ref[h * ATT_V_ROWS + hd:(h + 1) * ATT_V_ROWS, :] = ones


def att_prep(dm, p, cos_t, sin_t, gq, gk):
    tm = dm.tile(TM_PREP)
    n = dm.n_all
    lat_tiles, per_batch = dm.n_lat // tm, dm.s // tm
    qw = ATT_HEADS * ATT_HEAD_DIM
    kw = ATT_KV_HEADS * ATT_HEAD_DIM
    q_blk, k_blk, v_blk = 1536 // qw, 2048 // kw, 2304 // kw
    tab = lambda i: (jnp.where(i < lat_tiles, i % per_batch, 0), 0)
    return pl.pallas_call(
        functools.partial(_att_prep_kernel, lat_tiles=lat_tiles),
        grid=(n // tm,),
        in_specs=[pl.BlockSpec((tm, qw), lambda i: (i, q_blk)),
                  pl.BlockSpec((tm, kw), lambda i: (i, k_blk)),
                  pl.BlockSpec((tm, kw), lambda i: (i, v_blk)),
                  pl.BlockSpec((tm, ATT_HEAD_DIM), tab),
                  pl.BlockSpec((tm, ATT_HEAD_DIM), tab),
                  pl.BlockSpec((1, ATT_HEAD_DIM), lambda i: (0, 0)),
                  pl.BlockSpec((1, ATT_HEAD_DIM), lambda i: (0, 0))],
        out_specs=[pl.BlockSpec((qw, tm), lambda i: (0, i)),
                   pl.BlockSpec((tm, kw), lambda i: (i, 0)),
                   pl.BlockSpec((ATT_KV_HEADS * ATT_V_ROWS, tm), lambda i: (0, i))],
        out_shape=[jax.ShapeDtypeStruct((qw, n), BF16),
                   jax.ShapeDtypeStruct((n, kw), BF16),
                   jax.ShapeDtypeStruct((ATT_KV_HEADS * ATT_V_ROWS, n), BF16)],
        compiler_params=_cparams(("parallel",)),
        name="att_prep",
    )(p, p, p, cos_t, sin_t, gq.reshape(1, -1), gk.reshape(1, -1))


def _attn_kernel(*refs, with_latent_keys):
    ns = ATT_STAGES
    if with_latent_keys:
        qt_ref, kl_ref, vtl_ref, kc_ref, vtc_ref, o_ref, acc_ref = refs[:7]
    else:
        qt_ref, kc_ref, vtc_ref, o_ref, acc_ref = refs[:5]
    s_refs, p_refs = refs[-2 * ns:-ns], refs[-ns:]
    hd = ATT_HEAD_DIM
    grp = ATT_HEADS // ATT_KV_HEADS
    tq = qt_ref.shape[1]
    nq = grp * tq
    tk = kc_ref.shape[0]
    n = 1 + (kl_ref.shape[0] // tk if with_latent_keys else 0)
    qt = jnp.concatenate([qt_ref[j * hd:(j + 1) * hd, :] for j in range(grp)], axis=1)

    def key_chunk(j):
        if isinstance(j, int) and j == 0:
            return kc_ref[...]
        return kl_ref[pl.ds(pl.multiple_of((j - 1) * tk, tk), tk), :]

    def value_chunk(j):
        if isinstance(j, int) and j == 0:
            return vtc_ref[...]
        return vtl_ref[:, pl.ds(pl.multiple_of((j - 1) * tk, tk), tk)]

    def max_keys(x):
        groups = math.gcd(tk // SUBLANES, ATT_REDUCE_GROUPS)
        part = jnp.max(x.reshape(groups, tk // groups, nq), axis=0)
        return jnp.max(part, axis=0, keepdims=True)

    def time_step(t, slot, carry, do_scores=True, do_softmax=True, do_values=True):
        m, alpha = carry
        if do_scores:
            s_refs[slot][...] = jnp.dot(key_chunk(t), qt, preferred_element_type=F32)
        if do_values:
            pv = jnp.dot(value_chunk(t - 2), p_refs[(slot - 2) % ns][...], preferred_element_type=F32)
            acc_ref[...] = acc_ref[...] * alpha + pv
        if do_softmax:
            st = s_refs[(slot - 1) % ns][...]
            m_new = jnp.maximum(m, max_keys(st))
            p_refs[(slot - 1) % ns][...] = jnp.exp2(st - m_new).astype(BF16)
            alpha = jnp.exp2(m - m_new)
            m = m_new
        return m, alpha

    acc_ref[...] = jnp.zeros_like(acc_ref)
    carry = (jnp.full((1, nq), -1e30, F32), jnp.zeros((1, nq), F32))
    fill = min(3, n + 2)
    for t in range(fill):
        carry = time_step(t, t % ns, carry, t < n, 0 <= t - 1 < n, 0 <= t - 2 < n)
    steady = max(n - fill, 0)
    body_steps = ns * ATT_UNROLL

    def unrolled(i, cr):
        for k in range(body_steps):
            cr = time_step(fill + body_steps * i + k, (fill + k) % ns, cr)
        return cr

    if steady // body_steps:
        carry = lax.fori_loop(0, steady // body_steps, unrolled, carry)
    for t in range(fill + body_steps * (steady // body_steps), max(n, fill)):
        carry = time_step(t, t % ns, carry)
    for t in range(max(n, fill), n + 2):
        carry = time_step(t, t % ns, carry, False, t - 1 < n, True)
    out = (acc_ref[0:hd, :] / acc_ref[hd:hd + 1, :]).T
    for j in range(grp):
        o_ref[:, j * hd:(j + 1) * hd] = out[j * tq:(j + 1) * tq].astype(o_ref.dtype)


def attention(dm, qt, kn, vt, latent):
    hd = ATT_HEAD_DIM
    grp = ATT_HEADS // ATT_KV_HEADS
    tk = dm.ctx
    assert dm.s % tk == 0
    ctx_blk0 = dm.n_lat // dm.ctx
    kc_spec = pl.BlockSpec((tk, hd), lambda b, h, i: (ctx_blk0 + b, h))
    vtc_spec = pl.BlockSpec((ATT_V_ROWS, tk), lambda b, h, i: (h, ctx_blk0 + b))
    if latent:
        tq = TQ_ATT
        nq = dm.s // tq
        q_spec = pl.BlockSpec((grp * hd, tq), lambda b, h, i: (h, b * nq + i))
        kl_spec = pl.BlockSpec((dm.s, hd), lambda b, h, i: (b, h))
        vtl_spec = pl.BlockSpec((ATT_V_ROWS, dm.s), lambda b, h, i: (h, b))
        in_specs = [q_spec, kl_spec, vtl_spec, kc_spec, vtc_spec]
        args = (qt, kn, vt, kn, vt)
        rows = dm.n_lat
    else:
        tq, nq = dm.ctx, 1
        q_spec = pl.BlockSpec((grp * hd, tq), lambda b, h, i: (h, ctx_blk0 + b))
        in_specs = [q_spec, kc_spec, vtc_spec]
        args = (qt, kn, vt)
        rows = dm.b * dm.ctx
    return pl.pallas_call(
        functools.partial(_attn_kernel, with_latent_keys=latent),
        grid=(dm.b, ATT_KV_HEADS, nq),
        in_specs=in_specs,
        out_specs=pl.BlockSpec((tq, grp * hd), lambda b, h, i: (b * nq + i, h)),
        out_shape=jax.ShapeDtypeStruct((rows, ATT_HEADS * hd), BF16),
        scratch_shapes=([pltpu.VMEM((ATT_V_ROWS, grp * tq), F32)]
                        + [pltpu.VMEM((tk, grp * tq), F32)] * ATT_STAGES
                        + [pltpu.VMEM((tk, grp * tq), BF16)] * ATT_STAGES),
        compiler_params=_cparams(("parallel", "parallel", "arbitrary")),
        name="attention_latent" if latent else "attention_ctx",
    )(*args)


def _pool_kernel(prev_ref, u_ref, next_ref, w_ref, sc_ref, o_ref, a_ref, b_ref, *, dm):
    tm = u_ref.shape[0]
    H = POOL_HALO
    i = pl.program_id(0)
    lat_tiles = dm.n_lat // tm
    is_lat = i < lat_tiles
    n_seg = jnp.where(is_lat, dm.s, dm.ctx)
    base = jnp.where(is_lat, (i * tm) % dm.s, 0)
    r = lax.broadcasted_iota(jnp.int32, (tm + 2 * H, 1), 0) - H
    pos = jnp.where(is_lat, base + r, r % dm.ctx)
    win = pl.ds(H, tm + 2 * H)

    def shifted(ref, d):
        return ref[pl.ds(H + d, tm + 2 * H), :]

    zeros = jnp.zeros((H, POOL_GROUP_DIM), F32)
    for ref in (a_ref, b_ref):
        ref[pl.ds(0, H), :] = zeros
        ref[pl.ds(tm + 3 * H, H), :] = zeros

    for gi, w in enumerate(POOL_WINDOWS):
        cols = slice(gi * POOL_GROUP_DIM, (gi + 1) * POOL_GROUP_DIM)
        half = w // 2
        a_ref[pl.ds(H, H), :] = prev_ref[:, cols]
        a_ref[pl.ds(2 * H, tm), :] = u_ref[:, cols]
        a_ref[pl.ds(2 * H + tm, H), :] = next_ref[:, cols]
        b_ref[win, :] = jnp.where(pos >= 1, shifted(a_ref, -1), 0.0)
        step = 1
        while step < half:
            b_ref[win, :] = b_ref[win, :] + jnp.where(pos >= step, shifted(b_ref, -step), 0.0)
            step *= 2
        step = 1
        while step < half:
            a_ref[win, :] = a_ref[win, :] + jnp.where(pos + step < n_seg, shifted(a_ref, step), 0.0)
            step *= 2
        tile = pl.ds(2 * H, tm)
        pt = pos[H:H + tm]
        cnt = (jnp.minimum(pt + half, n_seg) - jnp.maximum(pt - half, 0)).astype(F32)
        u = u_ref[:, cols]
        diff = (a_ref[tile, :] + b_ref[tile, :]) / cnt - u
        o_ref[:, cols] = (_dot(diff, w_ref[gi]) * sc_ref[:, cols]).astype(o_ref.dtype)


def pool_mixer(dm, p, pool_w, pool_scale):
    tm = dm.tile(TM_POOL)
    n = dm.n_all
    H = POOL_HALO
    width = len(POOL_WINDOWS) * POOL_GROUP_DIM
    col = 2560 // width
    per = tm // H
    last = n // H - 1
    return pl.pallas_call(
        functools.partial(_pool_kernel, dm=dm),
        grid=(n // tm,),
        in_specs=[pl.BlockSpec((H, width), lambda i: (jnp.maximum(i * per - 1, 0), col)),
                  pl.BlockSpec((tm, width), lambda i: (i, col)),
                  pl.BlockSpec((H, width), lambda i: (jnp.minimum((i + 1) * per, last), col)),
                  pl.BlockSpec(pool_w.shape, lambda i: (0, 0, 0)),
                  pl.BlockSpec((1, width), lambda i: (0, 0))],
        out_specs=pl.BlockSpec((tm, width), lambda i: (i, 0)),
        out_shape=jax.ShapeDtypeStruct((n, width), BF16),
        scratch_shapes=[pltpu.VMEM((tm + 4 * H, POOL_GROUP_DIM), F32),
                        pltpu.VMEM((tm + 4 * H, POOL_GROUP_DIM), F32)],
        compiler_params=_cparams(("parallel",)),
        name="pool_mixer",
    )(p, p, p, pool_w, pool_scale.reshape(1, width))


def _merge_kernel(yr_ref, ya_ref, yp_ref, gl_ref, *refs, lat_tiles):
    mod_ref, wb_ref, wo_ref, o_ref = refs[-4:]
    x_refs = refs[:-4]
    d = o_ref.shape[1]
    mixed = None
    for j, y_ref in enumerate((yr_ref, ya_ref, yp_ref)):
        gate = gl_ref[:, j * d:(j + 1) * d].astype(F32)
        term = gate * jnp.dot(y_ref[...], wb_ref[j], preferred_element_type=F32)
        mixed = term if mixed is None else mixed + term
    y = mod_ref[0][2:3] * _dot(mixed, wo_ref[...])

    def finish(x_ref):
        o_ref[...] = x_ref[...] + y

    _for_row_source(x_refs, lat_tiles, finish)


def merge(dm, n_rows, yr, ya, yp, gates, x_parts, mod, wb, wo):
    tm = dm.tile(TM_MERGE)
    d = dm.d
    bw = yr.shape[1]
    row = dm.mod_row(tm)
    return pl.pallas_call(
        functools.partial(_merge_kernel, lat_tiles=dm.n_lat // tm),
        grid=(n_rows // tm,),
        in_specs=[pl.BlockSpec((tm, bw), lambda i: (i, 0)),
                  pl.BlockSpec((tm, bw), lambda i: (i, 0)),
                  pl.BlockSpec((tm, bw), lambda i: (i, 0)),
                  pl.BlockSpec((tm, 3 * d), lambda i: (i, 0))]
        + _row_sources(dm, tm, x_parts) + [
            pl.BlockSpec((1, 6, d), lambda i: (row(i), 0, 0)),
            pl.BlockSpec(wb.shape, lambda i: (0, 0, 0)),
            pl.BlockSpec(wo.shape, lambda i: (0, 0))],
        out_specs=pl.BlockSpec((tm, d), lambda i: (i, 0)),
        out_shape=jax.ShapeDtypeStruct((n_rows, d), F32),
        compiler_params=_cparams(("parallel",)),
        name="merge",
    )(yr, ya, yp, gates, *x_parts, mod, wb, wo)


def _ffn_kernel(x_ref, mod_ref, gain_ref, w1_ref, w3_ref, w2_ref, o_ref, h_ref, acc_ref):
    f = pl.program_id(1)
    m = mod_ref[0]

    @pl.when(f == 0)
    def _():
        h_ref[...] = _norm_mod(x_ref[...], gain_ref[...], m[3:4], m[4:5]).astype(BF16)
        acc_ref[...] = jnp.zeros_like(acc_ref)

    h = h_ref[...]
    a = jnp.dot(h, w1_ref[...], preferred_element_type=F32)
    b = jnp.dot(h, w3_ref[...], preferred_element_type=F32)
    acc_ref[...] += _dot(_silu(a) * b, w2_ref[...])

    @pl.when(f == pl.num_programs(1) - 1)
    def _():
        o_ref[...] = x_ref[...] + m[5:6] * acc_ref[...]


def ffn_dense(dm, x, mod, gain, w1, w3, w2):
    n, d = x.shape
    fdim = w1.shape[1]
    tm, tf = dm.tile(TM_FFN), TF_FFN
    row = dm.mod_row(tm)
    return pl.pallas_call(
        _ffn_kernel,
        grid=(n // tm, fdim // tf),
        in_specs=[pl.BlockSpec((tm, d), lambda i, f: (i, 0)),
                  pl.BlockSpec((1, 6, d), lambda i, f: (row(i), 0, 0)),
                  pl.BlockSpec((1, d), lambda i, f: (0, 0)),
                  pl.BlockSpec((d, tf), lambda i, f: (0, f)),
                  pl.BlockSpec((d, tf), lambda i, f: (0, f)),
                  pl.BlockSpec((tf, d), lambda i, f: (f, 0))],
        out_specs=pl.BlockSpec((tm, d), lambda i, f: (i, 0)),
        out_shape=jax.ShapeDtypeStruct((n, d), F32),
        scratch_shapes=[pltpu.VMEM((tm, d), BF16), pltpu.VMEM((tm, d), F32)],
        compiler_params=_cparams(("parallel", "arbitrary")),
        name="ffn_dense",
    )(x, mod, gain.reshape(1, d), w1, w3, w2)


def _router_kernel(x_ref, mod_ref, gain_ref, wr_ref, br_ref, h_ref, slot_ref, slot_t_ref, wt_ref, cum_ref):
    tm = x_ref.shape[0]
    m = mod_ref[0]
    h = _norm_mod(x_ref[...], gain_ref[...], m[3:4], m[4:5])
    h_ref[...] = h.astype(BF16)
    logits = _dot3(h, wr_ref[...]) + br_ref[...]
    lane = lax.broadcasted_iota(jnp.int32, logits.shape, 1)
    m1 = jnp.max(logits, axis=-1, keepdims=True)
    i1 = jnp.min(jnp.where(logits == m1, lane, LANES), axis=-1, keepdims=True)
    rest = jnp.where(lane == i1, -jnp.inf, logits)
    m2 = jnp.max(rest, axis=-1, keepdims=True)
    i2 = jnp.min(jnp.where(rest == m2, lane, LANES), axis=-1, keepdims=True)
    e = jnp.exp(m2 - m1)
    w1 = 1.0 / (1.0 + e)
    w2 = e / (1.0 + e)
    wt_ref[...] = jnp.where(lane == i1, w1, jnp.where(lane == i2, w2, 0.0))
    routed = jnp.where((lane == i1) | (lane == i2), 1.0, 0.0)
    sub = SUB_MOE
    lower = (lax.broadcasted_iota(jnp.int32, (sub, sub), 0) > lax.broadcasted_iota(jnp.int32, (sub, sub), 1))
    lower = lower.astype(BF16)
    offset = jnp.zeros((1, LANES), F32)
    cum_ref[...] = jnp.zeros_like(cum_ref)
    for blk in range(tm // sub):
        rows = slice(blk * sub, (blk + 1) * sub)
        mask = routed[rows]
        cum_ref[0, blk:blk + 1, :] = offset.astype(jnp.int32)
        rank = jnp.dot(lower, mask.astype(BF16), preferred_element_type=F32) + offset
        slot_ref[rows, :] = jnp.where(mask > 0.0, rank, -1.0)
        count = jnp.sum(mask, axis=0, keepdims=True)
        offset = offset + jnp.ceil(count * (1.0 / SLOT_ALIGN)) * SLOT_ALIGN
    cum_ref[0, tm // sub:tm // sub + 1, :] = offset.astype(jnp.int32)
    slot_t_ref[...] = slot_ref[...].T[:SUBLANES, :]


def moe_route(dm, n_rows, x, mod, gain, wr_pad, br_pad):
    tm = dm.tile(TM_MOE)
    d = dm.d
    row = dm.mod_row(tm)
    nt = n_rows // tm
    cum_rows = -(-(tm // SUB_MOE + 1) // SUBLANES) * SUBLANES
    return pl.pallas_call(
        _router_kernel,
        grid=(nt,),
        in_specs=[pl.BlockSpec((tm, d), lambda i: (i, 0)),
                  pl.BlockSpec((1, 6, d), lambda i: (row(i), 0, 0)),
                  pl.BlockSpec((1, d), lambda i: (0, 0)),
                  pl.BlockSpec((d, LANES), lambda i: (0, 0)),
                  pl.BlockSpec((1, LANES), lambda i: (0, 0))],
        out_specs=[pl.BlockSpec((tm, d), lambda i: (i, 0)),
                   pl.BlockSpec((tm, LANES), lambda i: (i, 0)),
                   pl.BlockSpec((SUBLANES, tm), lambda i: (0, i)),
                   pl.BlockSpec((tm, LANES), lambda i: (i, 0)),
                   pl.BlockSpec((1, cum_rows, LANES), lambda i: (i, 0, 0))],
        out_shape=[jax.ShapeDtypeStruct((n_rows, d), BF16),
                   jax.ShapeDtypeStruct((n_rows, LANES), F32),
                   jax.ShapeDtypeStruct((SUBLANES, n_rows), F32),
                   jax.ShapeDtypeStruct((n_rows, LANES), F32),
                   jax.ShapeDtypeStruct((nt, cum_rows, LANES), jnp.int32)],
        compiler_params=_cparams(("parallel",)),
        name="moe_router",
    )(x, mod, gain.reshape(1, d), wr_pad, br_pad)


def _moe_kernel(cum_ref, h_ref, slot_ref, slot_t_ref, wt_ref, w1_ref, w3_ref, w2_ref, o_ref,
                xs_ref, acc_ref):
    i, e, f = pl.program_id(0), pl.program_id(1), pl.program_id(2)
    tm, d = h_ref.shape
    rb, win, sub = RB_MOE, WIN_MOE, SUB_MOE
    nsub = tm // sub

    def base(j):
        return cum_ref[i, j * N_EXPERTS + e]

    halves = (base(nsub) + rb // 2 - 1) // (rb // 2)
    nblk, has_tail = halves // 2, halves % 2 == 1

    def block_rows(r, size=rb):
        return pl.ds(pl.multiple_of(r * rb, 32), size)

    def windows(j):
        return base(j), (base(j + 1) - base(j) + win - 1) // win

    def window_rows(start, w):
        return pl.ds(pl.multiple_of(start + w * win, SLOT_ALIGN), win)

    @pl.when((e == 0) & (f == 0))
    def _():
        o_ref[...] = jnp.zeros_like(o_ref)

    @pl.when(f == 0)
    def _():
        def clear(r, carry):
            xs_ref[block_rows(r), :] = jnp.zeros((rb, d), BF16)
            acc_ref[block_rows(r), :] = jnp.zeros((rb, d), F32)
            return carry

        lax.fori_loop(0, nblk + 1, clear, 0)
        acc_ref[pl.ds(pl.multiple_of((nblk + 1) * rb, 32), win), :] = jnp.zeros((win, d), F32)

        for j in range(nsub):
            tok = slice(j * sub, (j + 1) * sub)
            start, nwin = windows(j)

            def gather(w, carry, tok=tok, start=start):
                rows = window_rows(start, w)
                want = (lax.broadcasted_iota(jnp.int32, (win, sub), 0) + (start + w * win)).astype(F32)
                onehot = (slot_t_ref[pl.ds(e, 1), tok] == want).astype(BF16)
                xs_ref[rows, :] = jnp.dot(onehot, h_ref[tok, :], preferred_element_type=F32).astype(BF16)
                return carry

            lax.fori_loop(0, nwin, gather, 0)

    def expert(r, carry, size=rb):
        rows = block_rows(r, size)
        xs = xs_ref[rows, :]
        tf = w1_ref.shape[2]
        y = None
        for lo in range(0, tf, MOE_COL_CHUNK):
            cols = slice(lo, min(lo + MOE_COL_CHUNK, tf))
            a = jnp.dot(xs, w1_ref[0, :, cols], preferred_element_type=F32)
            b = jnp.dot(xs, w3_ref[0, :, cols], preferred_element_type=F32)
            part = _dot(_silu(a) * b, w2_ref[0, cols, :])
            y = part if y is None else y + part
        acc_ref[rows, :] += y
        return carry

    lax.fori_loop(0, nblk, expert, 0)

    @pl.when(has_tail)
    def _():
        expert(nblk, 0, rb // 2)

    @pl.when(f == pl.num_programs(2) - 1)
    def _():
        pick = lax.broadcasted_iota(jnp.int32, (sub, LANES), 1) == e
        for j in range(nsub):
            tok = slice(j * sub, (j + 1) * sub)
            start, nwin = windows(j)
            slot_col = jnp.sum(jnp.where(pick, slot_ref[tok, :], 0.0), axis=-1, keepdims=True)
            wt_col = jnp.sum(jnp.where(pick, wt_ref[tok, :], 0.0), axis=-1, keepdims=True)

            def scatter(w, carry, tok=tok, start=start, slot_col=slot_col, wt_col=wt_col):
                ys = acc_ref[window_rows(start, w), :].astype(BF16)
                want = (lax.broadcasted_iota(jnp.int32, (sub, win), 1) + (start + w * win)).astype(F32)
                onehot = (slot_col == want).astype(BF16)
                o_ref[tok, :] += wt_col * jnp.dot(onehot, ys, preferred_element_type=F32)
                return carry

            lax.fori_loop(0, nwin, scatter, 0)


def moe_experts(dm, n_rows, h, slot, slot_t, wt, counts, w1, w3, w2):
    tm = dm.tile(TM_MOE)
    d = dm.d
    ne, _, fdim = w1.shape
    tf = TF_MOE if fdim % TF_MOE == 0 else TF_FFN
    assert SUB_MOE % SLOT_ALIGN == 0 and tm % SUB_MOE == 0
    cap = tm + RB_MOE + WIN_MOE
    once = dict(pipeline_mode=pl.Buffered(1))
    grid_spec = pltpu.PrefetchScalarGridSpec(
        num_scalar_prefetch=1,
        grid=(n_rows // tm, ne, fdim // tf),
        in_specs=[pl.BlockSpec((tm, d), lambda i, e, f, c: (i, 0), **once),
                  pl.BlockSpec((tm, LANES), lambda i, e, f, c: (i, 0), **once),
                  pl.BlockSpec((SUBLANES, tm), lambda i, e, f, c: (0, i), **once),
                  pl.BlockSpec((tm, LANES), lambda i, e, f, c: (i, 0), **once),
                  pl.BlockSpec((1, d, tf), lambda i, e, f, c: (e, 0, f)),
                  pl.BlockSpec((1, d, tf), lambda i, e, f, c: (e, 0, f)),
                  pl.BlockSpec((1, tf, d), lambda i, e, f, c: (e, f, 0))],
        out_specs=pl.BlockSpec((tm, d), lambda i, e, f, c: (i, 0), **once),
        scratch_shapes=[pltpu.VMEM((cap, d), BF16), pltpu.VMEM((cap, d), F32)],
    )
    return pl.pallas_call(
        _moe_kernel,
        grid_spec=grid_spec,
        out_shape=jax.ShapeDtypeStruct((n_rows, d), F32),
        compiler_params=_cparams(("parallel", "arbitrary", "arbitrary")),
        name="moe_experts",
    )(counts, h, slot, slot_t, wt, w1, w3, w2)


def _residual_kernel(x_ref, y_ref, mod_ref, o_ref):
    o_ref[...] = x_ref[...] + mod_ref[0][5:6] * y_ref[...]


def _final_kernel(x_ref, y_ref, mod_ref, gain_ref, o_ref):
    x = x_ref[...] + mod_ref[0][5:6] * y_ref[...]
    o_ref[...] = x * lax.rsqrt(jnp.mean(x * x, axis=-1, keepdims=True) + NORM_EPS) * gain_ref[...]


def residual(dm, n_rows, x, y, mod, final_gain=None):
    tm = dm.tile(TM_FFN)
    d = dm.d
    row = dm.mod_row(tm)
    in_specs = [pl.BlockSpec((tm, d), lambda i: (i, 0)),
                pl.BlockSpec((tm, d), lambda i: (i, 0)),
                pl.BlockSpec((1, 6, d), lambda i: (row(i), 0, 0))]
    args = [x, y, mod]
    if final_gain is not None:
        in_specs.append(pl.BlockSpec((1, d), lambda i: (0, 0)))
        args.append(final_gain.reshape(1, d))
    return pl.pallas_call(
        _residual_kernel if final_gain is None else _final_kernel,
        grid=(n_rows // tm,),
        in_specs=in_specs,
        out_specs=pl.BlockSpec((tm, d), lambda i: (i, 0)),
        out_shape=jax.ShapeDtypeStruct((n_rows, d), F32),
        compiler_params=_cparams(("parallel",)),
        name="residual" if final_gain is None else "residual_final_norm",
    )(*args)


def _final_norm_kernel(x_ref, gain_ref, o_ref):
    x = x_ref[...]
    o_ref[...] = x * lax.rsqrt(jnp.mean(x * x, axis=-1, keepdims=True) + NORM_EPS) * gain_ref[...]


def final_rmsnorm(dm, n_rows, x, gain):
    tm = dm.tile(TM_FFN)
    d = dm.d
    return pl.pallas_call(
        _final_norm_kernel,
        grid=(n_rows // tm,),
        in_specs=[pl.BlockSpec((tm, d), lambda i: (i, 0)), pl.BlockSpec((1, d), lambda i: (0, 0))],
        out_specs=pl.BlockSpec((tm, d), lambda i: (i, 0)),
        out_shape=jax.ShapeDtypeStruct((n_rows, d), F32),
        compiler_params=_cparams(("parallel",)),
        name="final_norm",
    )(x, gain.reshape(1, d))


def _rope_tables(s, ctx):
    def angles(pos, dim):
        freqs = 1.0 / (ROPE_THETA ** (jnp.arange(0, dim, 2, dtype=F32) / dim))
        return pos[:, None] * freqs[None, :]

    rows = s // GRID_W
    row = jnp.repeat(jnp.arange(rows, dtype=F32), GRID_W)
    col = jnp.tile(jnp.arange(GRID_W, dtype=F32), rows)
    half = ATT_HEAD_DIM // 2
    ang_att = jnp.concatenate([angles(row, half), angles(col, half)], axis=-1)
    sign = jnp.tile(jnp.array([-1.0, 1.0], F32), LANES // 2)

    def tables(ang):
        reps = LANES // (2 * ang.shape[1])
        cos = jnp.tile(jnp.repeat(jnp.cos(ang), 2, axis=-1), (1, reps))
        sin = jnp.tile(jnp.repeat(jnp.sin(ang), 2, axis=-1), (1, reps)) * sign
        return cos, sin

    ang_ret = angles(jnp.arange(s, dtype=F32), RET_QK_DIM)
    ang_ret = jnp.concatenate([ang_ret, jnp.zeros((ctx, RET_QK_DIM // 2), F32)], axis=0)
    return tables(ang_att) + tables(ang_ret)


def kernel(x, c, ctx, c_ctx, w_ada, b_ada, norm_mix, norm_ffn, w_in, ret_decay, ret_gn, attn_qn, attn_kn, pool_w, pool_scale, w_branch, w_out, ffn_w1, ffn_w3, ffn_w2, moe_router, moe_router_b, moe_w1, moe_w3, moe_w2, final_norm):
    b, s, d = x.shape
    n_ctx = ctx.shape[1]
    depth = w_in.shape[0]
    dm = Dims(b, s, n_ctx, d)

    x_parts = (x.reshape(b * s, d), ctx.reshape(b * n_ctx, d))
    mod_rows = 2 * SUBLANES
    cc = jnp.concatenate([c, c_ctx[None, :], jnp.zeros((mod_rows - b - 1, d), F32)], axis=0)
    mods = ada_table(cc, w_ada, b_ada).reshape(depth, mod_rows, 6, d)

    att_cos, att_sin, ret_cos, ret_sin = _rope_tables(s, n_ctx)

    for i in range(depth):
        need_ctx = i < depth - 1
        n_rows = dm.n_all if need_ctx else dm.n_lat
        mod = mods[i]
        p, gates = proj_in(dm, x_parts, mod, norm_mix[i], w_in[i].astype(BF16))
        y_ret = retention(dm, p, ret_decay[i], ret_cos, ret_sin, ret_gn[i])
        y_pool = pool_mixer(dm, p, pool_w[i].astype(BF16), pool_scale[i])
        qt, kn, vt = att_prep(dm, p, att_cos, att_sin, attn_qn[i], attn_kn[i])
        y_att = attention(dm, qt, kn, vt, latent=True)
        if need_ctx:
            y_att = jnp.concatenate([y_att, attention(dm, qt, kn, vt, latent=False)], axis=0)
        xa = merge(dm, n_rows, y_ret, y_att, y_pool, gates, x_parts, mod,
                   w_branch[i].astype(BF16), w_out[i].astype(BF16))
        j = i // 2
        last = i == depth - 1
        if i % 2 == 0:
            xa = ffn_dense(dm, xa, mod, norm_ffn[i], ffn_w1[j].astype(BF16), ffn_w3[j].astype(BF16),
                           ffn_w2[j].astype(BF16))
            if last:
                xa = final_rmsnorm(dm, n_rows, xa, final_norm)
        else:
            wr = jnp.zeros((d, LANES), F32).at[:, :N_EXPERTS].set(moe_router[j])
            br = jnp.full((1, LANES), -1e30, F32).at[0, :N_EXPERTS].set(moe_router_b[j])
            h, slot, slot_t, wt, cnt = moe_route(dm, n_rows, xa, mod, norm_ffn[i], wr, br)
            nsub = dm.tile(TM_MOE) // SUB_MOE
            cum = cnt[:, :nsub + 1, :N_EXPERTS].reshape(cnt.shape[0], (nsub + 1) * N_EXPERTS)
            y = moe_experts(dm, n_rows, h, slot, slot_t, wt, cum,
                            moe_w1[j].astype(BF16), moe_w3[j].astype(BF16), moe_w2[j].astype(BF16))
            xa = residual(dm, n_rows, xa, y, mod, final_norm if last else None)
        x_parts = (xa,)
    return xa[:dm.n_lat].reshape(b, s, d)
```

```python
import functools
import math

import numpy as np
import jax
import jax.numpy as jnp
from jax import lax
from jax.experimental import pallas as pl
from jax.experimental.pallas import tpu as pltpu

F32 = jnp.float32
BF16 = jnp.bfloat16

GRID_W = 64
RET_HEADS = 4
RET_QK_DIM = 64
RET_V_DIM = 128
ATT_HEADS = 4
ATT_KV_HEADS = 2
ATT_HEAD_DIM = 128
ROPE_THETA = 10000.0
POOL_WINDOWS = (2, 4, 8, 16)
POOL_GROUP_DIM = 128
N_EXPERTS = 8
NORM_EPS = 1e-6
GN_EPS = 1e-5

LANES = 128
SUBLANES = 8
VMEM_LIMIT_BYTES = 56 * 1024 * 1024

TM_PROJ = 1024
TN_PROJ = 1536
TM_PREP = 1024
TM_POOL = 1024
TM_MERGE = 1024
TM_FFN = 1024
TF_FFN = 512
TM_MOE = 2048
TF_MOE = 1792
MOE_COL_CHUNK = 512
RB_MOE = 256
WIN_MOE = 256
SUB_MOE = 512
SLOT_ALIGN = 16
assert RB_MOE % 64 == 0
RET_L = 256
TQ_ATT = 512
ATT_SCORE_SCALE = (ATT_HEAD_DIM ** -0.5) * math.log2(math.e)
ATT_V_ROWS = ATT_HEAD_DIM + 16
ATT_STAGES = 3
ATT_UNROLL = 4
ATT_REDUCE_GROUPS = 8
POOL_HALO = 8


def _cparams(sem):
    return pltpu.CompilerParams(dimension_semantics=sem, vmem_limit_bytes=VMEM_LIMIT_BYTES)


def _dot(a, b):
    return jnp.dot(a.astype(BF16), b.astype(BF16), preferred_element_type=F32)


def _dot_nt(a, b):
    return lax.dot_general(a.astype(BF16), b.astype(BF16), (((1,), (1,)), ((), ())),
                           preferred_element_type=F32)


def _dot_tn(a, b):
    return lax.dot_general(a.astype(BF16), b.astype(BF16), (((0,), (0,)), ((), ())),
                           preferred_element_type=F32)


def _split_bf16(a):
    hi = a.astype(BF16)
    lo = (a - hi.astype(F32)).astype(BF16)
    return hi, lo


def _dot3(a, b):
    ah, al = _split_bf16(a)
    bh, bl = _split_bf16(b)
    d = functools.partial(jnp.dot, preferred_element_type=F32)
    return d(ah, bh) + (d(al, bh) + d(ah, bl))


def _norm_mod(x, gain, shift, scale):
    y = x * lax.rsqrt(jnp.mean(x * x, axis=-1, keepdims=True) + NORM_EPS)
    return (y * gain) * (1.0 + scale) + shift


def _sigmoid(x):
    return 1.0 / (1.0 + jnp.exp(-x))


def _silu(x):
    return x * _sigmoid(x)


class Dims:
    def __init__(self, b, s, ctx, d):
        self.b, self.s, self.ctx, self.d = b, s, ctx, d
        self.n_lat = b * s
        self.n_all = b * s + b * ctx

    def tile(self, preferred):
        t = preferred
        while self.s % t or (self.b * self.ctx) % t:
            t //= 2
        return t

    def mod_row(self, tm):
        lat_tiles, per_batch, b = self.n_lat // tm, self.s // tm, self.b
        return lambda i: jnp.where(i < lat_tiles, i // per_batch, b)


def _ada_kernel(c_ref, w_ref, b_ref, o_ref):
    o_ref[0] = _dot3(_silu(c_ref[...]), w_ref[0]) + b_ref[0]


def ada_table(cc, w_ada, b_ada):
    depth, d, n6 = w_ada.shape
    tn = n6 // 4
    rows = cc.shape[0]
    return pl.pallas_call(
        _ada_kernel,
        grid=(depth, n6 // tn),
        in_specs=[pl.BlockSpec((rows, d), lambda l, j: (0, 0)),
                  pl.BlockSpec((1, d, tn), lambda l, j: (l, 0, j)),
                  pl.BlockSpec((1, 1, tn), lambda l, j: (l, 0, j))],
        out_specs=pl.BlockSpec((1, rows, tn), lambda l, j: (l, 0, j)),
        out_shape=jax.ShapeDtypeStruct((depth, rows, n6), F32),
        compiler_params=_cparams(("arbitrary", "arbitrary")),
        name="ada_table",
    )(cc, w_ada, b_ada.reshape(depth, 1, n6))


def _row_sources(dm, tm, parts):
    d = dm.d
    if len(parts) == 1:
        return [pl.BlockSpec((tm, d), lambda i, *_: (i, 0))]
    lat_tiles = dm.n_lat // tm
    return [pl.BlockSpec((tm, d), lambda i, *_: (jnp.minimum(i, lat_tiles - 1), 0)),
            pl.BlockSpec((tm, d), lambda i, *_: (jnp.maximum(i - lat_tiles, 0), 0))]


def _for_row_source(x_refs, lat_tiles, fn):
    if len(x_refs) == 1:
        fn(x_refs[0])
        return
    is_lat = pl.program_id(0) < lat_tiles
    pl.when(is_lat)(lambda: fn(x_refs[0]))
    pl.when(jnp.logical_not(is_lat))(lambda: fn(x_refs[1]))


def _proj_in_kernel(*refs, lat_tiles):
    mod_ref, gain_ref, wm_ref, wg_ref, o_ref, g_ref, h_ref = refs[-7:]
    x_refs = refs[:-7]

    @pl.when(pl.program_id(1) == 0)
    def _():
        def fill(x_ref):
            m = mod_ref[0]
            h_ref[...] = _norm_mod(x_ref[...], gain_ref[...], m[0:1], m[1:2]).astype(BF16)

        _for_row_source(x_refs, lat_tiles, fill)

    h = h_ref[...]
    o_ref[...] = jnp.dot(h, wm_ref[...], preferred_element_type=F32)
    logits = jnp.dot(h, wg_ref[...], preferred_element_type=F32)
    g_ref[...] = (0.5 * jnp.tanh(0.5 * logits) + 0.5).astype(g_ref.dtype)


def proj_in(dm, x_parts, mod, gain, w):
    n, d = dm.n_all, dm.d
    nout = w.shape[1]
    gate_w = 3 * d
    tm, tn = dm.tile(TM_PROJ), TN_PROJ
    assert nout - gate_w == gate_w and gate_w % tn == 0
    gate_blk0 = gate_w // tn
    row = dm.mod_row(tm)
    return pl.pallas_call(
        functools.partial(_proj_in_kernel, lat_tiles=dm.n_lat // tm),
        grid=(n // tm, gate_w // tn),
        in_specs=_row_sources(dm, tm, x_parts) + [
            pl.BlockSpec((1, 6, d), lambda i, j: (row(i), 0, 0)),
            pl.BlockSpec((1, d), lambda i, j: (0, 0)),
            pl.BlockSpec((d, tn), lambda i, j: (0, j)),
            pl.BlockSpec((d, tn), lambda i, j: (0, gate_blk0 + j))],
        out_specs=[pl.BlockSpec((tm, tn), lambda i, j: (i, j)),
                   pl.BlockSpec((tm, tn), lambda i, j: (i, j))],
        out_shape=[jax.ShapeDtypeStruct((n, nout - gate_w), F32),
                   jax.ShapeDtypeStruct((n, gate_w), BF16)],
        scratch_shapes=[pltpu.VMEM((tm, d), BF16)],
        compiler_params=_cparams(("parallel", "arbitrary")),
        name="proj_in",
    )(*x_parts, mod, gain.reshape(1, d), w, w)


def _rope_pairs(x, cos, sin):
    lane = lax.broadcasted_iota(jnp.int32, x.shape, 1)
    partner = jnp.where(lane % 2 == 0, pltpu.roll(x, LANES - 1, 1), pltpu.roll(x, 1, 1))
    return x * cos + partner * sin


def _ret_rope(x, cos, sin):
    return jnp.concatenate([_rope_pairs(x[:, j * LANES:(j + 1) * LANES], cos, sin)
                            for j in range(x.shape[1] // LANES)], axis=1)


def _ret_kernel(dec_ref, q_ref, g_ref, kl_ref, kc_ref, vl_ref, vc_ref, cos_ref, sin_ref, gn_ref, o_ref,
                kr_ref, vs_ref, sf_ref, sb_ref, sball_ref, decay_ref, *, ncc, nlc):
    s = pl.program_id(1)
    L = RET_L
    nc = ncc + nlc
    pos = lax.broadcasted_iota(jnp.int32, (L, 1), 0).astype(F32)

    def log_g(direction, h):
        return -jnp.exp(jnp.full((1, 1), dec_ref[direction, h], F32))

    def chunk(c):
        if isinstance(c, int):
            return pl.ds(c * L, L)
        return pl.ds(pl.multiple_of(c * L, L), L)

    def table(c):
        if isinstance(c, int):
            return chunk(nlc + c if c < ncc else c - ncc)
        return chunk(jnp.where(c < ncc, nlc + c, c - ncc))

    def head(x, h, width):
        return x[:, h * width:(h + 1) * width]

    @pl.when(s == 0)
    def _():
        def stage(c, k, v):
            kr_ref[chunk(c), :] = _ret_rope(k, cos_ref[table(c), :], sin_ref[table(c), :]) * (RET_QK_DIM ** -0.5)
            vs_ref[chunk(c), :] = v.astype(vs_ref.dtype)

        for c in range(ncc):
            stage(c, kc_ref[c * L:(c + 1) * L, :], vc_ref[c * L:(c + 1) * L, :])

        def stage_latent(j, carry):
            stage(ncc + j, kl_ref[chunk(j), :], vl_ref[chunk(j), :])
            return carry

        lax.fori_loop(0, nlc, stage_latent, 0)

        diff = (lax.broadcasted_iota(jnp.int32, (L, L), 0)
                - lax.broadcasted_iota(jnp.int32, (L, L), 1)).astype(F32)
        for h in range(RET_HEADS):
            decay_ref[h] = (jnp.where(diff >= 0, jnp.exp(jnp.maximum(diff, 0.0) * log_g(0, h)), 0.0)
                            + jnp.where(diff <= 0, jnp.exp(jnp.maximum(-diff, 0.0) * log_g(1, h)), 0.0))

        sf_ref[...] = jnp.zeros_like(sf_ref)
        sb_ref[...] = jnp.zeros_like(sb_ref)

        def visit(c, carry):
            k = kr_ref[chunk(c), :]
            v = vs_ref[chunk(c), :]
            for h in range(RET_HEADS):
                lgb = log_g(1, h)
                st = sb_ref[h]
                sball_ref[c, h] = st
                kz = head(k, h, RET_QK_DIM) * jnp.exp(pos * lgb)
                sb_ref[h] = jnp.exp(L * lgb) * st + _dot_tn(kz, head(v, h, RET_V_DIM))
            return carry

        for c in reversed(range(ncc)):
            visit(c, 0)
        lax.fori_loop(0, nlc, lambda j, carry: visit(nc - 1 - j, carry), 0)

    q = _ret_rope(q_ref[...], cos_ref[table(s), :], sin_ref[table(s), :])
    k = kr_ref[chunk(s), :]
    v = vs_ref[chunk(s), :]
    g = g_ref[...]
    for h in range(RET_HEADS):
        lgf = log_g(0, h)
        lgb = log_g(1, h)
        qh = head(q, h, RET_QK_DIM)
        kh = head(k, h, RET_QK_DIM)
        vh = head(v, h, RET_V_DIM)
        st_f = sf_ref[h]
        st_b = sball_ref[s, h]
        o = _dot(_dot_nt(qh, kh) * decay_ref[h], vh)
        o = o + _dot(qh, st_f) * jnp.exp((pos + 1.0) * lgf)
        o = o + _dot(qh, st_b) * jnp.exp((L - pos) * lgb)
        sf_ref[h] = jnp.exp(L * lgf) * st_f + _dot_tn(kh * jnp.exp((L - 1.0 - pos) * lgf), vh)
        mu = jnp.mean(o, axis=-1, keepdims=True)
        var = jnp.mean(jnp.square(o - mu), axis=-1, keepdims=True)
        o = (o - mu) * lax.rsqrt(var + GN_EPS) * head(gn_ref[...], h, RET_V_DIM)
        o_ref[:, h * RET_V_DIM:(h + 1) * RET_V_DIM] = (_silu(head(g, h, RET_V_DIM)) * o).astype(o_ref.dtype)


def retention(dm, p, dec, cos_t, sin_t, gn):
    L = RET_L
    ncc, nlc = dm.ctx // L, dm.s // L
    nc = ncc + nlc
    lat_blocks = dm.n_lat // L
    ctx_blk0 = dm.n_lat // dm.ctx
    qk_w = RET_HEADS * RET_QK_DIM
    v_w = RET_HEADS * RET_V_DIM

    def rows(b, s):
        return jnp.where(s < ncc, lat_blocks + b * ncc + s, b * nlc + s - ncc)

    whole = lambda shape: pl.BlockSpec(shape, lambda b, s, d: (0, 0))
    grid_spec = pltpu.PrefetchScalarGridSpec(
        num_scalar_prefetch=1,
        grid=(dm.b, nc),
        in_specs=[pl.BlockSpec((L, qk_w), lambda b, s, d: (rows(b, s), 0)),
                  pl.BlockSpec((L, v_w), lambda b, s, d: (rows(b, s), 2)),
                  pl.BlockSpec((dm.s, qk_w), lambda b, s, d: (b, 1)),
                  pl.BlockSpec((dm.ctx, qk_w), lambda b, s, d: (ctx_blk0 + b, 1)),
                  pl.BlockSpec((dm.s, v_w), lambda b, s, d: (b, 1)),
                  pl.BlockSpec((dm.ctx, v_w), lambda b, s, d: (ctx_blk0 + b, 1)),
                  whole(cos_t.shape), whole(sin_t.shape), whole((1, v_w))],
        out_specs=pl.BlockSpec((L, v_w), lambda b, s, d: (rows(b, s), 0)),
        scratch_shapes=[pltpu.VMEM((nc * L, qk_w), F32),
                        pltpu.VMEM((nc * L, v_w), BF16),
                        pltpu.VMEM((RET_HEADS, RET_QK_DIM, RET_V_DIM), F32),
                        pltpu.VMEM((RET_HEADS, RET_QK_DIM, RET_V_DIM), F32),
                        pltpu.VMEM((nc, RET_HEADS, RET_QK_DIM, RET_V_DIM), F32),
                        pltpu.VMEM((RET_HEADS, L, L), F32)],
    )
    return pl.pallas_call(
        functools.partial(_ret_kernel, ncc=ncc, nlc=nlc),
        grid_spec=grid_spec,
        out_shape=jax.ShapeDtypeStruct((dm.n_all, v_w), BF16),
        compiler_params=_cparams(("parallel", "arbitrary")),
        name="retention",
    )(dec, p, p, p, p, p, p, cos_t, sin_t, gn.reshape(1, v_w))


def _att_prep_kernel(q_ref, k_ref, v_ref, cos_ref, sin_ref, gq_ref, gk_ref, qo_ref, ko_ref, vo_ref,
                     *, lat_tiles):
    is_lat = pl.program_id(0) < lat_tiles
    cos = jnp.where(is_lat, cos_ref[...], 1.0)
    sin = jnp.where(is_lat, sin_ref[...], 0.0)

    hd = ATT_HEAD_DIM

    def prep(x_ref, gain_ref, h):
        xh = x_ref[:, h * hd:(h + 1) * hd]
        xn = xh * lax.rsqrt(jnp.mean(xh * xh, axis=-1, keepdims=True) + NORM_EPS) * gain_ref[...]
        return _rope_pairs(xn, cos, sin)

    for h in range(ATT_HEADS):
        qo_ref[:, h * hd:(h + 1) * hd] = (prep(q_ref, gq_ref, h) * ATT_SCORE_SCALE).astype(qo_ref.dtype)
    ones = jnp.ones((ATT_V_ROWS - hd, vo_ref.shape[1]), vo_ref.dtype)
    for h in range(ATT_KV_HEADS):
        ko_ref[:, h * hd:(h + 1) * hd] = prep(k_ref, gk_ref, h).astype(ko_ref.dtype)
        vo_ref[h * ATT_V_ROWS:h * ATT_V_ROWS + hd, :] = v_ref[:, h * hd:(h + 1) * hd].T.astype(vo_ref.dtype)
        vo_ref[h * ATT_V_ROWS + hd:(h + 1) * ATT_V_ROWS, :] = ones


def att_prep(dm, p, cos_t, sin_t, gq, gk):
    tm = dm.tile(TM_PREP)
    n = dm.n_all
    lat_tiles, per_batch = dm.n_lat // tm, dm.s // tm
    qw = ATT_HEADS * ATT_HEAD_DIM
    kw = ATT_KV_HEADS * ATT_HEAD_DIM
    q_blk, k_blk, v_blk = 1536 // qw, 2048 // kw, 2304 // kw
    tab = lambda i: (jnp.where(i < lat_tiles, i % per_batch, 0), 0)
    return pl.pallas_call(
        functools.partial(_att_prep_kernel, lat_tiles=lat_tiles),
        grid=(n // tm,),
        in_specs=[pl.BlockSpec((tm, qw), lambda i: (i, q_blk)),
                  pl.BlockSpec((tm, kw), lambda i: (i, k_blk)),
                  pl.BlockSpec((tm, kw), lambda i: (i, v_blk)),
                  pl.BlockSpec((tm, ATT_HEAD_DIM), tab),
                  pl.BlockSpec((tm, ATT_HEAD_DIM), tab),
                  pl.BlockSpec((1, ATT_HEAD_DIM), lambda i: (0, 0)),
                  pl.BlockSpec((1, ATT_HEAD_DIM), lambda i: (0, 0))],
        out_specs=[pl.BlockSpec((tm, qw), lambda i: (i, 0)),
                   pl.BlockSpec((tm, kw), lambda i: (i, 0)),
                   pl.BlockSpec((ATT_KV_HEADS * ATT_V_ROWS, tm), lambda i: (0, i))],
        out_shape=[jax.ShapeDtypeStruct((n, qw), BF16),
                   jax.ShapeDtypeStruct((n, kw), BF16),
                   jax.ShapeDtypeStruct((ATT_KV_HEADS * ATT_V_ROWS, n), BF16)],
        compiler_params=_cparams(("parallel",)),
        name="att_prep",
    )(p, p, p, cos_t, sin_t, gq.reshape(1, -1), gk.reshape(1, -1))


def _attn_kernel(*refs, with_latent_keys):
    ns = ATT_STAGES
    if with_latent_keys:
        qt_ref, kl_ref, vtl_ref, kc_ref, vtc_ref, o_ref, acc_ref = refs[:7]
    else:
        qt_ref, kc_ref, vtc_ref, o_ref, acc_ref = refs[:5]
    s_refs, p_refs = refs[-2 * ns:-ns], refs[-ns:]
    hd = ATT_HEAD_DIM
    grp = ATT_HEADS // ATT_KV_HEADS
    tq = qt_ref.shape[0]
    nq = grp * tq
    tk = kc_ref.shape[0]
    n = 1 + (kl_ref.shape[0] // tk if with_latent_keys else 0)
    q2 = jnp.concatenate([qt_ref[:, j * hd:(j + 1) * hd] for j in range(grp)], axis=0)

    def key_chunk(j):
        if isinstance(j, int) and j == 0:
            return kc_ref[...]
        return kl_ref[pl.ds(pl.multiple_of((j - 1) * tk, tk), tk), :]

    def value_chunk(j):
        if isinstance(j, int) and j == 0:
            return vtc_ref[...]
        return vtl_ref[:, pl.ds(pl.multiple_of((j - 1) * tk, tk), tk)]

    def max_keys(x):
        groups = math.gcd(tk // SUBLANES, ATT_REDUCE_GROUPS)
        part = jnp.max(x.reshape(groups, tk // groups, nq), axis=0)
        return jnp.max(part, axis=0, keepdims=True)

    def time_step(t, slot, carry, do_scores=True, do_softmax=True, do_values=True):
        m, alpha = carry
        if do_scores:
            s_refs[slot][...] = _dot_nt(key_chunk(t), q2)
        if do_values:
            pv = jnp.dot(value_chunk(t - 2), p_refs[(slot - 2) % ns][...], preferred_element_type=F32)
            acc_ref[...] = acc_ref[...] * alpha + pv
        if do_softmax:
            st = s_refs[(slot - 1) % ns][...]
            m_new = jnp.maximum(m, max_keys(st))
            p_refs[(slot - 1) % ns][...] = jnp.exp2(st - m_new).astype(BF16)
            alpha = jnp.exp2(m - m_new)
            m = m_new
        return m, alpha

    acc_ref[...] = jnp.zeros_like(acc_ref)
    carry = (jnp.full((1, nq), -1e30, F32), jnp.zeros((1, nq), F32))
    fill = min(3, n + 2)
    for t in range(fill):
        carry = time_step(t, t % ns, carry, t < n, 0 <= t - 1 < n, 0 <= t - 2 < n)
    steady = max(n - fill, 0)
    body_steps = ns * ATT_UNROLL

    def unrolled(i, cr):
        for k in range(body_steps):
            cr = time_step(fill + body_steps * i + k, (fill + k) % ns, cr)
        return cr

    if steady // body_steps:
        carry = lax.fori_loop(0, steady // body_steps, unrolled, carry)
    for t in range(fill + body_steps * (steady // body_steps), max(n, fill)):
        carry = time_step(t, t % ns, carry)
    for t in range(max(n, fill), n + 2):
        carry = time_step(t, t % ns, carry, False, t - 1 < n, True)
    out = (acc_ref[0:hd, :] / acc_ref[hd:hd + 1, :]).T
    for j in range(grp):
        o_ref[:, j * hd:(j + 1) * hd] = out[j * tq:(j + 1) * tq].astype(o_ref.dtype)


def attention(dm, qt, kn, vt, latent):
    hd = ATT_HEAD_DIM
    grp = ATT_HEADS // ATT_KV_HEADS
    tk = dm.ctx
    assert dm.s % tk == 0
    ctx_blk0 = dm.n_lat // dm.ctx
    kc_spec = pl.BlockSpec((tk, hd), lambda b, h, i: (ctx_blk0 + b, h))
    vtc_spec = pl.BlockSpec((ATT_V_ROWS, tk), lambda b, h, i: (h, ctx_blk0 + b))
    if latent:
        tq = TQ_ATT
        nq = dm.s // tq
        q_spec = pl.BlockSpec((tq, grp * hd), lambda b, h, i: (b * nq + i, h))
        kl_spec = pl.BlockSpec((dm.s, hd), lambda b, h, i: (b, h))
        vtl_spec = pl.BlockSpec((ATT_V_ROWS, dm.s), lambda b, h, i: (h, b))
        in_specs = [q_spec, kl_spec, vtl_spec, kc_spec, vtc_spec]
        args = (qt, kn, vt, kn, vt)
        rows = dm.n_lat
    else:
        tq, nq = dm.ctx, 1
        q_spec = pl.BlockSpec((tq, grp * hd), lambda b, h, i: (ctx_blk0 + b, h))
        in_specs = [q_spec, kc_spec, vtc_spec]
        args = (qt, kn, vt)
        rows = dm.b * dm.ctx
    return pl.pallas_call(
        functools.partial(_attn_kernel, with_latent_keys=latent),
        grid=(dm.b, ATT_KV_HEADS, nq),
        in_specs=in_specs,
        out_specs=pl.BlockSpec((tq, grp * hd), lambda b, h, i: (b * nq + i, h)),
        out_shape=jax.ShapeDtypeStruct((rows, ATT_HEADS * hd), BF16),
        scratch_shapes=([pltpu.VMEM((ATT_V_ROWS, grp * tq), F32)]
                        + [pltpu.VMEM((tk, grp * tq), F32)] * ATT_STAGES
                        + [pltpu.VMEM((tk, grp * tq), BF16)] * ATT_STAGES),
        compiler_params=_cparams(("parallel", "parallel", "arbitrary")),
        name="attention_latent" if latent else "attention_ctx",
    )(*args)


def _pool_kernel(prev_ref, u_ref, next_ref, w_ref, sc_ref, o_ref, a_ref, b_ref, *, dm):
    tm = u_ref.shape[0]
    H = POOL_HALO
    i = pl.program_id(0)
    lat_tiles = dm.n_lat // tm
    is_lat = i < lat_tiles
    n_seg = jnp.where(is_lat, dm.s, dm.ctx)
    base = jnp.where(is_lat, (i * tm) % dm.s, 0)
    r = lax.broadcasted_iota(jnp.int32, (tm + 2 * H, 1), 0) - H
    pos = jnp.where(is_lat, base + r, r % dm.ctx)
    win = pl.ds(H, tm + 2 * H)

    def shifted(ref, d):
        return ref[pl.ds(H + d, tm + 2 * H), :]

    zeros = jnp.zeros((H, POOL_GROUP_DIM), F32)
    for ref in (a_ref, b_ref):
        ref[pl.ds(0, H), :] = zeros
        ref[pl.ds(tm + 3 * H, H), :] = zeros

    for gi, w in enumerate(POOL_WINDOWS):
        cols = slice(gi * POOL_GROUP_DIM, (gi + 1) * POOL_GROUP_DIM)
        half = w // 2
        a_ref[pl.ds(H, H), :] = prev_ref[:, cols]
        a_ref[pl.ds(2 * H, tm), :] = u_ref[:, cols]
        a_ref[pl.ds(2 * H + tm, H), :] = next_ref[:, cols]
        b_ref[win, :] = jnp.where(pos >= 1, shifted(a_ref, -1), 0.0)
        step = 1
        while step < half:
            b_ref[win, :] = b_ref[win, :] + jnp.where(pos >= step, shifted(b_ref, -step), 0.0)
            step *= 2
        step = 1
        while step < half:
            a_ref[win, :] = a_ref[win, :] + jnp.where(pos + step < n_seg, shifted(a_ref, step), 0.0)
            step *= 2
        tile = pl.ds(2 * H, tm)
        pt = pos[H:H + tm]
        cnt = (jnp.minimum(pt + half, n_seg) - jnp.maximum(pt - half, 0)).astype(F32)
        u = u_ref[:, cols]
        diff = (a_ref[tile, :] + b_ref[tile, :]) / cnt - u
        o_ref[:, cols] = (_dot(diff, w_ref[gi]) * sc_ref[:, cols]).astype(o_ref.dtype)


def pool_mixer(dm, p, pool_w, pool_scale):
    tm = dm.tile(TM_POOL)
    n = dm.n_all
    H = POOL_HALO
    width = len(POOL_WINDOWS) * POOL_GROUP_DIM
    col = 2560 // width
    per = tm // H
    last = n // H - 1
    return pl.pallas_call(
        functools.partial(_pool_kernel, dm=dm),
        grid=(n // tm,),
        in_specs=[pl.BlockSpec((H, width), lambda i: (jnp.maximum(i * per - 1, 0), col)),
                  pl.BlockSpec((tm, width), lambda i: (i, col)),
                  pl.BlockSpec((H, width), lambda i: (jnp.minimum((i + 1) * per, last), col)),
                  pl.BlockSpec(pool_w.shape, lambda i: (0, 0, 0)),
                  pl.BlockSpec((1, width), lambda i: (0, 0))],
        out_specs=pl.BlockSpec((tm, width), lambda i: (i, 0)),
        out_shape=jax.ShapeDtypeStruct((n, width), BF16),
        scratch_shapes=[pltpu.VMEM((tm + 4 * H, POOL_GROUP_DIM), F32),
                        pltpu.VMEM((tm + 4 * H, POOL_GROUP_DIM), F32)],
        compiler_params=_cparams(("parallel",)),
        name="pool_mixer",
    )(p, p, p, pool_w, pool_scale.reshape(1, width))


def _merge_kernel(yr_ref, ya_ref, yp_ref, gl_ref, *refs, lat_tiles):
    mod_ref, wb_ref, wo_ref, o_ref = refs[-4:]
    x_refs = refs[:-4]
    d = o_ref.shape[1]
    mixed = None
    for j, y_ref in enumerate((yr_ref, ya_ref, yp_ref)):
        gate = gl_ref[:, j * d:(j + 1) * d].astype(F32)
        term = gate * jnp.dot(y_ref[...], wb_ref[j], preferred_element_type=F32)
        mixed = term if mixed is None else mixed + term
    y = mod_ref[0][2:3] * _dot(mixed, wo_ref[...])

    def finish(x_ref):
        o_ref[...] = x_ref[...] + y

    _for_row_source(x_refs, lat_tiles, finish)


def merge(dm, n_rows, yr, ya, yp, gates, x_parts, mod, wb, wo):
    tm = dm.tile(TM_MERGE)
    d = dm.d
    bw = yr.shape[1]
    row = dm.mod_row(tm)
    return pl.pallas_call(
        functools.partial(_merge_kernel, lat_tiles=dm.n_lat // tm),
        grid=(n_rows // tm,),
        in_specs=[pl.BlockSpec((tm, bw), lambda i: (i, 0)),
                  pl.BlockSpec((tm, bw), lambda i: (i, 0)),
                  pl.BlockSpec((tm, bw), lambda i: (i, 0)),
                  pl.BlockSpec((tm, 3 * d), lambda i: (i, 0))]
        + _row_sources(dm, tm, x_parts) + [
            pl.BlockSpec((1, 6, d), lambda i: (row(i), 0, 0)),
            pl.BlockSpec(wb.shape, lambda i: (0, 0, 0)),
            pl.BlockSpec(wo.shape, lambda i: (0, 0))],
        out_specs=pl.BlockSpec((tm, d), lambda i: (i, 0)),
        out_shape=jax.ShapeDtypeStruct((n_rows, d), F32),
        compiler_params=_cparams(("parallel",)),
        name="merge",
    )(yr, ya, yp, gates, *x_parts, mod, wb, wo)


def _ffn_kernel(x_ref, mod_ref, gain_ref, w1_ref, w3_ref, w2_ref, o_ref, h_ref, acc_ref):
    f = pl.program_id(1)
    m = mod_ref[0]

    @pl.when(f == 0)
    def _():
        h_ref[...] = _norm_mod(x_ref[...], gain_ref[...], m[3:4], m[4:5]).astype(BF16)
        acc_ref[...] = jnp.zeros_like(acc_ref)

    h = h_ref[...]
    tf = w1_ref.shape[1]
    y = None
    for lo in range(0, tf, MOE_COL_CHUNK):
        cols = slice(lo, min(lo + MOE_COL_CHUNK, tf))
        a = jnp.dot(h, w1_ref[:, cols], preferred_element_type=F32)
        b = jnp.dot(h, w3_ref[:, cols], preferred_element_type=F32)
        part = _dot(_silu(a) * b, w2_ref[cols, :])
        y = part if y is None else y + part
    acc_ref[...] += y

    @pl.when(f == pl.num_programs(1) - 1)
    def _():
        o_ref[...] = x_ref[...] + m[5:6] * acc_ref[...]


def ffn_dense(dm, x, mod, gain, w1, w3, w2):
    n, d = x.shape
    fdim = w1.shape[1]
    tm = dm.tile(TM_FFN)
    tf = TF_MOE if fdim % TF_MOE == 0 else TF_FFN
    row = dm.mod_row(tm)
    return pl.pallas_call(
        _ffn_kernel,
        grid=(n // tm, fdim // tf),
        in_specs=[pl.BlockSpec((tm, d), lambda i, f: (i, 0)),
                  pl.BlockSpec((1, 6, d), lambda i, f: (row(i), 0, 0)),
                  pl.BlockSpec((1, d), lambda i, f: (0, 0)),
                  pl.BlockSpec((d, tf), lambda i, f: (0, f)),
                  pl.BlockSpec((d, tf), lambda i, f: (0, f)),
                  pl.BlockSpec((tf, d), lambda i, f: (f, 0))],
        out_specs=pl.BlockSpec((tm, d), lambda i, f: (i, 0)),
        out_shape=jax.ShapeDtypeStruct((n, d), F32),
        scratch_shapes=[pltpu.VMEM((tm, d), BF16), pltpu.VMEM((tm, d), F32)],
        compiler_params=_cparams(("parallel", "arbitrary")),
        name="ffn_dense",
    )(x, mod, gain.reshape(1, d), w1, w3, w2)


def _router_kernel(x_ref, mod_ref, gain_ref, wr_ref, br_ref, h_ref, slot_ref, slot_t_ref, wt_ref, cum_ref):
    tm = x_ref.shape[0]
    m = mod_ref[0]
    h = _norm_mod(x_ref[...], gain_ref[...], m[3:4], m[4:5])
    h_ref[...] = h.astype(BF16)
    logits = _dot3(h, wr_ref[...]) + br_ref[...]
    lane = lax.broadcasted_iota(jnp.int32, logits.shape, 1)
    m1 = jnp.max(logits, axis=-1, keepdims=True)
    i1 = jnp.min(jnp.where(logits == m1, lane, LANES), axis=-1, keepdims=True)
    rest = jnp.where(lane == i1, -jnp.inf, logits)
    m2 = jnp.max(rest, axis=-1, keepdims=True)
    i2 = jnp.min(jnp.where(rest == m2, lane, LANES), axis=-1, keepdims=True)
    e = jnp.exp(m2 - m1)
    w1 = 1.0 / (1.0 + e)
    w2 = e / (1.0 + e)
    wt_ref[...] = jnp.where(lane == i1, w1, jnp.where(lane == i2, w2, 0.0))
    routed = jnp.where((lane == i1) | (lane == i2), 1.0, 0.0)
    sub = SUB_MOE
    lower = (lax.broadcasted_iota(jnp.int32, (sub, sub), 0) > lax.broadcasted_iota(jnp.int32, (sub, sub), 1))
    lower = lower.astype(BF16)
    offset = jnp.zeros((1, LANES), F32)
    cum_ref[...] = jnp.zeros_like(cum_ref)
    for blk in range(tm // sub):
        rows = slice(blk * sub, (blk + 1) * sub)
        mask = routed[rows]
        cum_ref[0, blk:blk + 1, :] = offset.astype(jnp.int32)
        rank = jnp.dot(lower, mask.astype(BF16), preferred_element_type=F32) + offset
        slot_ref[rows, :] = jnp.where(mask > 0.0, rank, -1.0)
        count = jnp.sum(mask, axis=0, keepdims=True)
        offset = offset + jnp.ceil(count * (1.0 / SLOT_ALIGN)) * SLOT_ALIGN
    cum_ref[0, tm // sub:tm // sub + 1, :] = offset.astype(jnp.int32)
    slot_t_ref[...] = slot_ref[...].T[:SUBLANES, :]


def moe_route(dm, n_rows, x, mod, gain, wr_pad, br_pad):
    tm = dm.tile(TM_MOE)
    d = dm.d
    row = dm.mod_row(tm)
    nt = n_rows // tm
    cum_rows = -(-(tm // SUB_MOE + 1) // SUBLANES) * SUBLANES
    return pl.pallas_call(
        _router_kernel,
        grid=(nt,),
        in_specs=[pl.BlockSpec((tm, d), lambda i: (i, 0)),
                  pl.BlockSpec((1, 6, d), lambda i: (row(i), 0, 0)),
                  pl.BlockSpec((1, d), lambda i: (0, 0)),
                  pl.BlockSpec((d, LANES), lambda i: (0, 0)),
                  pl.BlockSpec((1, LANES), lambda i: (0, 0))],
        out_specs=[pl.BlockSpec((tm, d), lambda i: (i, 0)),
                   pl.BlockSpec((tm, LANES), lambda i: (i, 0)),
                   pl.BlockSpec((SUBLANES, tm), lambda i: (0, i)),
                   pl.BlockSpec((tm, LANES), lambda i: (i, 0)),
                   pl.BlockSpec((1, cum_rows, LANES), lambda i: (i, 0, 0))],
        out_shape=[jax.ShapeDtypeStruct((n_rows, d), BF16),
                   jax.ShapeDtypeStruct((n_rows, LANES), F32),
                   jax.ShapeDtypeStruct((SUBLANES, n_rows), F32),
                   jax.ShapeDtypeStruct((n_rows, LANES), F32),
                   jax.ShapeDtypeStruct((nt, cum_rows, LANES), jnp.int32)],
        compiler_params=_cparams(("parallel",)),
        name="moe_router",
    )(x, mod, gain.reshape(1, d), wr_pad, br_pad)


def _moe_kernel(cum_ref, h_ref, slot_ref, slot_t_ref, wt_ref, w1_ref, w3_ref, w2_ref, o_ref,
                xs_ref, acc_ref):
    i, e, f = pl.program_id(0), pl.program_id(1), pl.program_id(2)
    tm, d = h_ref.shape
    rb, win, sub = RB_MOE, WIN_MOE, SUB_MOE
    nsub = tm // sub

    def base(j):
        return cum_ref[i, j * N_EXPERTS + e]

    halves = (base(nsub) + rb // 2 - 1) // (rb // 2)
    nblk, has_tail = halves // 2, halves % 2 == 1

    def block_rows(r, size=rb):
        return pl.ds(pl.multiple_of(r * rb, 32), size)

    def windows(j):
        return base(j), (base(j + 1) - base(j) + win - 1) // win

    def window_rows(start, w):
        return pl.ds(pl.multiple_of(start + w * win, SLOT_ALIGN), win)

    @pl.when((e == 0) & (f == 0))
    def _():
        o_ref[...] = jnp.zeros_like(o_ref)

    @pl.when(f == 0)
    def _():
        def clear(r, carry):
            xs_ref[block_rows(r), :] = jnp.zeros((rb, d), BF16)
            acc_ref[block_rows(r), :] = jnp.zeros((rb, d), F32)
            return carry

        lax.fori_loop(0, nblk + 1, clear, 0)
        acc_ref[pl.ds(pl.multiple_of((nblk + 1) * rb, 32), win), :] = jnp.zeros((win, d), F32)

        for j in range(nsub):
            tok = slice(j * sub, (j + 1) * sub)
            start, nwin = windows(j)

            def gather(w, carry, tok=tok, start=start):
                rows = window_rows(start, w)
                want = (lax.broadcasted_iota(jnp.int32, (win, sub), 0) + (start + w * win)).astype(F32)
                onehot = (slot_t_ref[pl.ds(e, 1), tok] == want).astype(BF16)
                xs_ref[rows, :] = jnp.dot(onehot, h_ref[tok, :], preferred_element_type=F32).astype(BF16)
                return carry

            lax.fori_loop(0, nwin, gather, 0)

    def expert(r, carry, size=rb):
        rows = block_rows(r, size)
        xs = xs_ref[rows, :]
        tf = w1_ref.shape[2]
        y = None
        for lo in range(0, tf, MOE_COL_CHUNK):
            cols = slice(lo, min(lo + MOE_COL_CHUNK, tf))
            a = jnp.dot(xs, w1_ref[0, :, cols], preferred_element_type=F32)
            b = jnp.dot(xs, w3_ref[0, :, cols], preferred_element_type=F32)
            part = _dot(_silu(a) * b, w2_ref[0, cols, :])
            y = part if y is None else y + part
        acc_ref[rows, :] += y
        return carry

    lax.fori_loop(0, nblk, expert, 0)

    @pl.when(has_tail)
    def _():
        expert(nblk, 0, rb // 2)

    @pl.when(f == pl.num_programs(2) - 1)
    def _():
        pick = lax.broadcasted_iota(jnp.int32, (sub, LANES), 1) == e
        for j in range(nsub):
            tok = slice(j * sub, (j + 1) * sub)
            start, nwin = windows(j)
            slot_col = jnp.sum(jnp.where(pick, slot_ref[tok, :], 0.0), axis=-1, keepdims=True)
            wt_col = jnp.sum(jnp.where(pick, wt_ref[tok, :], 0.0), axis=-1, keepdims=True)

            def scatter(w, carry, tok=tok, start=start, slot_col=slot_col, wt_col=wt_col):
                ys = acc_ref[window_rows(start, w), :].astype(BF16)
                want = (lax.broadcasted_iota(jnp.int32, (sub, win), 1) + (start + w * win)).astype(F32)
                onehot = (slot_col == want).astype(BF16)
                o_ref[tok, :] += wt_col * jnp.dot(onehot, ys, preferred_element_type=F32)
                return carry

            lax.fori_loop(0, nwin, scatter, 0)


def moe_experts(dm, n_rows, h, slot, slot_t, wt, counts, w1, w3, w2):
    tm = dm.tile(TM_MOE)
    d = dm.d
    ne, _, fdim = w1.shape
    tf = TF_MOE if fdim % TF_MOE == 0 else TF_FFN
    assert SUB_MOE % SLOT_ALIGN == 0 and tm % SUB_MOE == 0
    cap = tm + RB_MOE + WIN_MOE
    once = dict(pipeline_mode=pl.Buffered(1))
    grid_spec = pltpu.PrefetchScalarGridSpec(
        num_scalar_prefetch=1,
        grid=(n_rows // tm, ne, fdim // tf),
        in_specs=[pl.BlockSpec((tm, d), lambda i, e, f, c: (i, 0), **once),
                  pl.BlockSpec((tm, LANES), lambda i, e, f, c: (i, 0), **once),
                  pl.BlockSpec((SUBLANES, tm), lambda i, e, f, c: (0, i), **once),
                  pl.BlockSpec((tm, LANES), lambda i, e, f, c: (i, 0), **once),
                  pl.BlockSpec((1, d, tf), lambda i, e, f, c: (e, 0, f)),
                  pl.BlockSpec((1, d, tf), lambda i, e, f, c: (e, 0, f)),
                  pl.BlockSpec((1, tf, d), lambda i, e, f, c: (e, f, 0))],
        out_specs=pl.BlockSpec((tm, d), lambda i, e, f, c: (i, 0), **once),
        scratch_shapes=[pltpu.VMEM((cap, d), BF16), pltpu.VMEM((cap, d), F32)],
    )
    return pl.pallas_call(
        _moe_kernel,
        grid_spec=grid_spec,
        out_shape=jax.ShapeDtypeStruct((n_rows, d), F32),
        compiler_params=_cparams(("parallel", "arbitrary", "arbitrary")),
        name="moe_experts",
    )(counts, h, slot, slot_t, wt, w1, w3, w2)


def _residual_kernel(x_ref, y_ref, mod_ref, o_ref):
    o_ref[...] = x_ref[...] + mod_ref[0][5:6] * y_ref[...]


def _final_kernel(x_ref, y_ref, mod_ref, gain_ref, o_ref):
    x = x_ref[...] + mod_ref[0][5:6] * y_ref[...]
    o_ref[...] = x * lax.rsqrt(jnp.mean(x * x, axis=-1, keepdims=True) + NORM_EPS) * gain_ref[...]


def residual(dm, n_rows, x, y, mod, final_gain=None):
    tm = dm.tile(TM_FFN)
    d = dm.d
    row = dm.mod_row(tm)
    in_specs = [pl.BlockSpec((tm, d), lambda i: (i, 0)),
                pl.BlockSpec((tm, d), lambda i: (i, 0)),
                pl.BlockSpec((1, 6, d), lambda i: (row(i), 0, 0))]
    args = [x, y, mod]
    if final_gain is not None:
        in_specs.append(pl.BlockSpec((1, d), lambda i: (0, 0)))
        args.append(final_gain.reshape(1, d))
    return pl.pallas_call(
        _residual_kernel if final_gain is None else _final_kernel,
        grid=(n_rows // tm,),
        in_specs=in_specs,
        out_specs=pl.BlockSpec((tm, d), lambda i: (i, 0)),
        out_shape=jax.ShapeDtypeStruct((n_rows, d), F32),
        compiler_params=_cparams(("parallel",)),
        name="residual" if final_gain is None else "residual_final_norm",
    )(*args)


def _final_norm_kernel(x_ref, gain_ref, o_ref):
    x = x_ref[...]
    o_ref[...] = x * lax.rsqrt(jnp.mean(x * x, axis=-1, keepdims=True) + NORM_EPS) * gain_ref[...]


def final_rmsnorm(dm, n_rows, x, gain):
    tm = dm.tile(TM_FFN)
    d = dm.d
    return pl.pallas_call(
        _final_norm_kernel,
        grid=(n_rows // tm,),
        in_specs=[pl.BlockSpec((tm, d), lambda i: (i, 0)), pl.BlockSpec((1, d), lambda i: (0, 0))],
        out_specs=pl.BlockSpec((tm, d), lambda i: (i, 0)),
        out_shape=jax.ShapeDtypeStruct((n_rows, d), F32),
        compiler_params=_cparams(("parallel",)),
        name="final_norm",
    )(x, gain.reshape(1, d))


def _rope_tables(s, ctx):
    def angles(pos, dim):
        freqs = 1.0 / (ROPE_THETA ** (jnp.arange(0, dim, 2, dtype=F32) / dim))
        return pos[:, None] * freqs[None, :]

    rows = s // GRID_W
    row = jnp.repeat(jnp.arange(rows, dtype=F32), GRID_W)
    col = jnp.tile(jnp.arange(GRID_W, dtype=F32), rows)
    half = ATT_HEAD_DIM // 2
    ang_att = jnp.concatenate([angles(row, half), angles(col, half)], axis=-1)
    sign = jnp.tile(jnp.array([-1.0, 1.0], F32), LANES // 2)

    def tables(ang):
        reps = LANES // (2 * ang.shape[1])
        cos = jnp.tile(jnp.repeat(jnp.cos(ang), 2, axis=-1), (1, reps))
        sin = jnp.tile(jnp.repeat(jnp.sin(ang), 2, axis=-1), (1, reps)) * sign
        return cos, sin

    ang_ret = angles(jnp.arange(s, dtype=F32), RET_QK_DIM)
    ang_ret = jnp.concatenate([ang_ret, jnp.zeros((ctx, RET_QK_DIM // 2), F32)], axis=0)
    return tables(ang_att) + tables(ang_ret)


def kernel(x, c, ctx, c_ctx, w_ada, b_ada, norm_mix, norm_ffn, w_in, ret_decay, ret_gn, attn_qn, attn_kn, pool_w, pool_scale, w_branch, w_out, ffn_w1, ffn_w3, ffn_w2, moe_router, moe_router_b, moe_w1, moe_w3, moe_w2, final_norm):
    b, s, d = x.shape
    n_ctx = ctx.shape[1]
    depth = w_in.shape[0]
    dm = Dims(b, s, n_ctx, d)

    x_parts = (x.reshape(b * s, d), ctx.reshape(b * n_ctx, d))
    mod_rows = 2 * SUBLANES
    cc = jnp.concatenate([c, c_ctx[None, :], jnp.zeros((mod_rows - b - 1, d), F32)], axis=0)
    mods = ada_table(cc, w_ada, b_ada).reshape(depth, mod_rows, 6, d)

    att_cos, att_sin, ret_cos, ret_sin = _rope_tables(s, n_ctx)

    for i in range(depth):
        need_ctx = i < depth - 1
        n_rows = dm.n_all if need_ctx else dm.n_lat
        mod = mods[i]
        p, gates = proj_in(dm, x_parts, mod, norm_mix[i], w_in[i].astype(BF16))
        y_ret = retention(dm, p, ret_decay[i], ret_cos, ret_sin, ret_gn[i])
        y_pool = pool_mixer(dm, p, pool_w[i].astype(BF16), pool_scale[i])
        qt, kn, vt = att_prep(dm, p, att_cos, att_sin, attn_qn[i], attn_kn[i])
        y_att = attention(dm, qt, kn, vt, latent=True)
        if need_ctx:
            y_att = jnp.concatenate([y_att, attention(dm, qt, kn, vt, latent=False)], axis=0)
        xa = merge(dm, n_rows, y_ret, y_att, y_pool, gates, x_parts, mod,
                   w_branch[i].astype(BF16), w_out[i].astype(BF16))
        j = i // 2
        last = i == depth - 1
        if i % 2 == 0:
            xa = ffn_dense(dm, xa, mod, norm_ffn[i], ffn_w1[j].astype(BF16), ffn_w3[j].astype(BF16),
                           ffn_w2[j].astype(BF16))
            if last:
                xa = final_rmsnorm(dm, n_rows, xa, final_norm)
        else:
            wr = jnp.zeros((d, LANES), F32).at[:, :N_EXPERTS].set(moe_router[j])
            br = jnp.full((1, LANES), -1e30, F32).at[0, :N_EXPERTS].set(moe_router_b[j])
            h, slot, slot_t, wt, cnt = moe_route(dm, n_rows, xa, mod, norm_ffn[i], wr, br)
            nsub = dm.tile(TM_MOE) // SUB_MOE
            cum = cnt[:, :nsub + 1, :N_EXPERTS].reshape(cnt.shape[0], (nsub + 1) * N_EXPERTS)
            y = moe_experts(dm, n_rows, h, slot, slot_t, wt, cum,
                            moe_w1[j].astype(BF16), moe_w3[j].astype(BF16), moe_w2[j].astype(BF16))
            xa = residual(dm, n_rows, xa, y, mod, final_norm if last else None)
        x_parts = (xa,)
    return xa[:dm.n_lat].reshape(b, s, d)
```

```python
import functools
import math

import numpy as np
import jax
import jax.numpy as jnp
from jax import lax
from jax.experimental import pallas as pl
from jax.experimental.pallas import tpu as pltpu

F32 = jnp.float32
BF16 = jnp.bfloat16

GRID_W = 64
RET_HEADS = 4
RET_QK_DIM = 64
RET_V_DIM = 128
ATT_HEADS = 4
ATT_KV_HEADS = 2
ATT_HEAD_DIM = 128
ROPE_THETA = 10000.0
POOL_WINDOWS = (2, 4, 8, 16)
POOL_GROUP_DIM = 128
N_EXPERTS = 8
NORM_EPS = 1e-6
GN_EPS = 1e-5

LANES = 128
SUBLANES = 8
VMEM_LIMIT_BYTES = 56 * 1024 * 1024

TM_PROJ = 1024
TN_PROJ = 1536
TM_PREP = 1024
TM_POOL = 1024
TM_MERGE = 1024
TM_FFN = 1024
TF_FFN = 512
TM_MOE = 2048
TF_MOE = 1792
MOE_COL_CHUNK = 512
RB_MOE = 256
BLK_MOE = 512
WIN_MOE = 256
SUB_MOE = 512
SLOT_ALIGN = 16
assert RB_MOE % 64 == 0
RET_L = 256
TQ_ATT = 512
ATT_Q_COL = 2 * RET_HEADS * RET_QK_DIM + 2 * RET_HEADS * RET_V_DIM
ATT_K_COL = ATT_Q_COL + ATT_HEADS * ATT_HEAD_DIM
ATT_V_COL = ATT_K_COL + ATT_KV_HEADS * ATT_HEAD_DIM
ATT_SCORE_SCALE = (ATT_HEAD_DIM ** -0.5) * math.log2(math.e)
ATT_V_ROWS = ATT_HEAD_DIM + 16
ATT_STAGES = 3
ATT_UNROLL = 4
ATT_REDUCE_GROUPS = 8
POOL_HALO = 8


def _cparams(sem):
    return pltpu.CompilerParams(dimension_semantics=sem, vmem_limit_bytes=VMEM_LIMIT_BYTES)


def _dot(a, b):
    return jnp.dot(a.astype(BF16), b.astype(BF16), preferred_element_type=F32)


def _dot_nt(a, b):
    return lax.dot_general(a.astype(BF16), b.astype(BF16), (((1,), (1,)), ((), ())),
                           preferred_element_type=F32)


def _dot_tn(a, b):
    return lax.dot_general(a.astype(BF16), b.astype(BF16), (((0,), (0,)), ((), ())),
                           preferred_element_type=F32)


def _split_bf16(a):
    hi = a.astype(BF16)
    lo = (a - hi.astype(F32)).astype(BF16)
    return hi, lo


def _dot3(a, b):
    ah, al = _split_bf16(a)
    bh, bl = _split_bf16(b)
    d = functools.partial(jnp.dot, preferred_element_type=F32)
    return d(ah, bh) + (d(al, bh) + d(ah, bl))


def _norm_mod(x, gain, shift, scale):
    y = x * lax.rsqrt(jnp.mean(x * x, axis=-1, keepdims=True) + NORM_EPS)
    return (y * gain) * (1.0 + scale) + shift


def _sigmoid(x):
    return 1.0 / (1.0 + jnp.exp(-x))


def _silu(x):
    return x * _sigmoid(x)


class Dims:
    def __init__(self, b, s, ctx, d):
        self.b, self.s, self.ctx, self.d = b, s, ctx, d
        self.n_lat = b * s
        self.n_all = b * s + b * ctx

    def tile(self, preferred):
        t = preferred
        while self.s % t or (self.b * self.ctx) % t:
            t //= 2
        return t

    def mod_row(self, tm):
        lat_tiles, per_batch, b = self.n_lat // tm, self.s // tm, self.b
        return lambda i: jnp.where(i < lat_tiles, i // per_batch, b)


def _ada_kernel(c_ref, w_ref, b_ref, o_ref):
    o_ref[0] = _dot3(_silu(c_ref[...]), w_ref[0]) + b_ref[0]


def ada_table(cc, w_ada, b_ada):
    depth, d, n6 = w_ada.shape
    tn = n6 // 4
    rows = cc.shape[0]
    return pl.pallas_call(
        _ada_kernel,
        grid=(depth, n6 // tn),
        in_specs=[pl.BlockSpec((rows, d), lambda l, j: (0, 0)),
                  pl.BlockSpec((1, d, tn), lambda l, j: (l, 0, j)),
                  pl.BlockSpec((1, 1, tn), lambda l, j: (l, 0, j))],
        out_specs=pl.BlockSpec((1, rows, tn), lambda l, j: (l, 0, j)),
        out_shape=jax.ShapeDtypeStruct((depth, rows, n6), F32),
        compiler_params=_cparams(("arbitrary", "arbitrary")),
        name="ada_table",
    )(cc, w_ada, b_ada.reshape(depth, 1, n6))


def _row_sources(dm, tm, parts):
    d = dm.d
    if len(parts) == 1:
        return [pl.BlockSpec((tm, d), lambda i, *_: (i, 0))]
    lat_tiles = dm.n_lat // tm
    return [pl.BlockSpec((tm, d), lambda i, *_: (jnp.minimum(i, lat_tiles - 1), 0)),
            pl.BlockSpec((tm, d), lambda i, *_: (jnp.maximum(i - lat_tiles, 0), 0))]


def _for_row_source(x_refs, lat_tiles, fn):
    if len(x_refs) == 1:
        fn(x_refs[0])
        return
    is_lat = pl.program_id(0) < lat_tiles
    pl.when(is_lat)(lambda: fn(x_refs[0]))
    pl.when(jnp.logical_not(is_lat))(lambda: fn(x_refs[1]))


def _proj_in_kernel(*refs, lat_tiles):
    mod_ref, gain_ref, wm_ref, wg_ref, o_ref, g_ref, h_ref = refs[-7:]
    x_refs = refs[:-7]

    @pl.when(pl.program_id(1) == 0)
    def _():
        def fill(x_ref):
            m = mod_ref[0]
            h_ref[...] = _norm_mod(x_ref[...], gain_ref[...], m[0:1], m[1:2]).astype(BF16)

        _for_row_source(x_refs, lat_tiles, fill)

    h = h_ref[...]
    o_ref[...] = jnp.dot(h, wm_ref[...], preferred_element_type=F32)
    logits = jnp.dot(h, wg_ref[...], preferred_element_type=F32)
    g_ref[...] = (0.5 * jnp.tanh(0.5 * logits) + 0.5).astype(g_ref.dtype)


def proj_in(dm, x_parts, mod, gain, w):
    n, d = dm.n_all, dm.d
    nout = w.shape[1]
    gate_w = 3 * d
    tm, tn = dm.tile(TM_PROJ), TN_PROJ
    assert nout - gate_w == gate_w and gate_w % tn == 0
    gate_blk0 = gate_w // tn
    row = dm.mod_row(tm)
    return pl.pallas_call(
        functools.partial(_proj_in_kernel, lat_tiles=dm.n_lat // tm),
        grid=(n // tm, gate_w // tn),
        in_specs=_row_sources(dm, tm, x_parts) + [
            pl.BlockSpec((1, 6, d), lambda i, j: (row(i), 0, 0)),
            pl.BlockSpec((1, d), lambda i, j: (0, 0)),
            pl.BlockSpec((d, tn), lambda i, j: (0, j)),
            pl.BlockSpec((d, tn), lambda i, j: (0, gate_blk0 + j))],
        out_specs=[pl.BlockSpec((tm, tn), lambda i, j: (i, j)),
                   pl.BlockSpec((tm, tn), lambda i, j: (i, j))],
        out_shape=[jax.ShapeDtypeStruct((n, nout - gate_w), F32),
                   jax.ShapeDtypeStruct((n, gate_w), BF16)],
        scratch_shapes=[pltpu.VMEM((tm, d), BF16)],
        compiler_params=_cparams(("parallel", "arbitrary")),
        name="proj_in",
    )(*x_parts, mod, gain.reshape(1, d), w, w)


def _rope_pairs(x, cos, sin):
    lane = lax.broadcasted_iota(jnp.int32, x.shape, 1)
    partner = jnp.where(lane % 2 == 0, pltpu.roll(x, LANES - 1, 1), pltpu.roll(x, 1, 1))
    return x * cos + partner * sin


def _ret_rope(x, cos, sin):
    return jnp.concatenate([_rope_pairs(x[:, j * LANES:(j + 1) * LANES], cos, sin)
                            for j in range(x.shape[1] // LANES)], axis=1)


def _ret_kernel(dec_ref, q_ref, g_ref, kl_ref, kc_ref, vl_ref, vc_ref, cos_ref, sin_ref, gn_ref, o_ref,
                kr_ref, vs_ref, sf_ref, sb_ref, sball_ref, decay_ref, *, ncc, nlc):
    s = pl.program_id(1)
    L = RET_L
    nc = ncc + nlc
    pos = lax.broadcasted_iota(jnp.int32, (L, 1), 0).astype(F32)

    def log_g(direction, h):
        return -jnp.exp(jnp.full((1, 1), dec_ref[direction, h], F32))

    def chunk(c):
        if isinstance(c, int):
            return pl.ds(c * L, L)
        return pl.ds(pl.multiple_of(c * L, L), L)

    def table(c):
        if isinstance(c, int):
            return chunk(nlc + c if c < ncc else c - ncc)
        return chunk(jnp.where(c < ncc, nlc + c, c - ncc))

    def head(x, h, width):
        return x[:, h * width:(h + 1) * width]

    @pl.when(s == 0)
    def _():
        def stage(c, k, v):
            kr_ref[chunk(c), :] = _ret_rope(k, cos_ref[table(c), :], sin_ref[table(c), :]) * (RET_QK_DIM ** -0.5)
            vs_ref[chunk(c), :] = v.astype(vs_ref.dtype)

        for c in range(ncc):
            stage(c, kc_ref[c * L:(c + 1) * L, :], vc_ref[c * L:(c + 1) * L, :])

        def stage_latent(j, carry):
            stage(ncc + j, kl_ref[chunk(j), :], vl_ref[chunk(j), :])
            return carry

        lax.fori_loop(0, nlc, stage_latent, 0)

        diff = (lax.broadcasted_iota(jnp.int32, (L, L), 0)
                - lax.broadcasted_iota(jnp.int32, (L, L), 1)).astype(F32)
        for h in range(RET_HEADS):
            decay_ref[h] = (jnp.where(diff >= 0, jnp.exp(jnp.maximum(diff, 0.0) * log_g(0, h)), 0.0)
                            + jnp.where(diff <= 0, jnp.exp(jnp.maximum(-diff, 0.0) * log_g(1, h)), 0.0))

        sf_ref[...] = jnp.zeros_like(sf_ref)
        sb_ref[...] = jnp.zeros_like(sb_ref)

        def visit(c, carry):
            k = kr_ref[chunk(c), :]
            v = vs_ref[chunk(c), :]
            for h in range(RET_HEADS):
                lgb = log_g(1, h)
                st = sb_ref[h]
                sball_ref[c, h] = st
                kz = head(k, h, RET_QK_DIM) * jnp.exp(pos * lgb)
                sb_ref[h] = jnp.exp(L * lgb) * st + _dot_tn(kz, head(v, h, RET_V_DIM))
            return carry

        for c in reversed(range(ncc)):
            visit(c, 0)
        lax.fori_loop(0, nlc, lambda j, carry: visit(nc - 1 - j, carry), 0)

    q = _ret_rope(q_ref[...], cos_ref[table(s), :], sin_ref[table(s), :])
    k = kr_ref[chunk(s), :]
    v = vs_ref[chunk(s), :]
    g = g_ref[...]
    for h in range(RET_HEADS):
        lgf = log_g(0, h)
        lgb = log_g(1, h)
        qh = head(q, h, RET_QK_DIM)
        kh = head(k, h, RET_QK_DIM)
        vh = head(v, h, RET_V_DIM)
        st_f = sf_ref[h]
        st_b = sball_ref[s, h]
        o = _dot(_dot_nt(qh, kh) * decay_ref[h], vh)
        o = o + _dot(qh, st_f) * jnp.exp((pos + 1.0) * lgf)
        o = o + _dot(qh, st_b) * jnp.exp((L - pos) * lgb)
        sf_ref[h] = jnp.exp(L * lgf) * st_f + _dot_tn(kh * jnp.exp((L - 1.0 - pos) * lgf), vh)
        mu = jnp.mean(o, axis=-1, keepdims=True)
        var = jnp.mean(jnp.square(o - mu), axis=-1, keepdims=True)
        o = (o - mu) * lax.rsqrt(var + GN_EPS) * head(gn_ref[...], h, RET_V_DIM)
        o_ref[:, h * RET_V_DIM:(h + 1) * RET_V_DIM] = (_silu(head(g, h, RET_V_DIM)) * o).astype(o_ref.dtype)


def retention(dm, p, dec, cos_t, sin_t, gn):
    L = RET_L
    ncc, nlc = dm.ctx // L, dm.s // L
    nc = ncc + nlc
    lat_blocks = dm.n_lat // L
    ctx_blk0 = dm.n_lat // dm.ctx
    qk_w = RET_HEADS * RET_QK_DIM
    v_w = RET_HEADS * RET_V_DIM

    def rows(b, s):
        return jnp.where(s < ncc, lat_blocks + b * ncc + s, b * nlc + s - ncc)

    whole = lambda shape: pl.BlockSpec(shape, lambda b, s, d: (0, 0))
    grid_spec = pltpu.PrefetchScalarGridSpec(
        num_scalar_prefetch=1,
        grid=(dm.b, nc),
        in_specs=[pl.BlockSpec((L, qk_w), lambda b, s, d: (rows(b, s), 0)),
                  pl.BlockSpec((L, v_w), lambda b, s, d: (rows(b, s), 2)),
                  pl.BlockSpec((dm.s, qk_w), lambda b, s, d: (b, 1)),
                  pl.BlockSpec((dm.ctx, qk_w), lambda b, s, d: (ctx_blk0 + b, 1)),
                  pl.BlockSpec((dm.s, v_w), lambda b, s, d: (b, 1)),
                  pl.BlockSpec((dm.ctx, v_w), lambda b, s, d: (ctx_blk0 + b, 1)),
                  whole(cos_t.shape), whole(sin_t.shape), whole((1, v_w))],
        out_specs=pl.BlockSpec((L, v_w), lambda b, s, d: (rows(b, s), 0)),
        scratch_shapes=[pltpu.VMEM((nc * L, qk_w), F32),
                        pltpu.VMEM((nc * L, v_w), BF16),
                        pltpu.VMEM((RET_HEADS, RET_QK_DIM, RET_V_DIM), F32),
                        pltpu.VMEM((RET_HEADS, RET_QK_DIM, RET_V_DIM), F32),
                        pltpu.VMEM((nc, RET_HEADS, RET_QK_DIM, RET_V_DIM), F32),
                        pltpu.VMEM((RET_HEADS, L, L), F32)],
    )
    return pl.pallas_call(
        functools.partial(_ret_kernel, ncc=ncc, nlc=nlc),
        grid_spec=grid_spec,
        out_shape=jax.ShapeDtypeStruct((dm.n_all, v_w), BF16),
        compiler_params=_cparams(("parallel", "arbitrary")),
        name="retention",
    )(dec, p, p, p, p, p, p, cos_t, sin_t, gn.reshape(1, v_w))


def _att_prep_kernel(q_ref, k_ref, v_ref, cos_ref, sin_ref, gq_ref, gk_ref, qo_ref, ko_ref, vo_ref,
                     *, lat_tiles):
    is_lat = pl.program_id(0) < lat_tiles
    cos = jnp.where(is_lat, cos_ref[...], 1.0)
    sin = jnp.where(is_lat, sin_ref[...], 0.0)

    hd = ATT_HEAD_DIM

    def prep(x_ref, gain_ref, h):
        xh = x_ref[:, h * hd:(h + 1) * hd]
        xn = xh * lax.rsqrt(jnp.mean(xh * xh, axis=-1, keepdims=True) + NORM_EPS) * gain_ref[...]
        return _rope_pairs(xn, cos, sin)

    for h in range(ATT_HEADS):
        qo_ref[:, h * hd:(h + 1) * hd] = (prep(q_ref, gq_ref, h) * ATT_SCORE_SCALE).astype(qo_ref.dtype)
    ones = jnp.ones((ATT_V_ROWS - hd, vo_ref.shape[1]), vo_ref.dtype)
    for h in range(ATT_KV_HEADS):
        ko_ref[:, h * hd:(h + 1) * hd] = prep(k_ref, gk_ref, h).astype(ko_ref.dtype)
        vo_ref[h * ATT_V_ROWS:h * ATT_V_ROWS + hd, :] = v_ref[:, h * hd:(h + 1) * hd].T.astype(vo_ref.dtype)
        vo_ref[h * ATT_V_ROWS + hd:(h + 1) * ATT_V_ROWS, :] = ones


def att_prep(dm, p, cos_t, sin_t, gq, gk):
    tm = dm.tile(TM_PREP)
    n = dm.n_all
    lat_tiles, per_batch = dm.n_lat // tm, dm.s // tm
    qw = ATT_HEADS * ATT_HEAD_DIM
    kw = ATT_KV_HEADS * ATT_HEAD_DIM
    q_blk, k_blk, v_blk = ATT_Q_COL // qw, ATT_K_COL // kw, ATT_V_COL // kw
    tab = lambda i: (jnp.where(i < lat_tiles, i % per_batch, 0), 0)
    return pl.pallas_call(
        functools.partial(_att_prep_kernel, lat_tiles=lat_tiles),
        grid=(n // tm,),
        in_specs=[pl.BlockSpec((tm, qw), lambda i: (i, q_blk)),
                  pl.BlockSpec((tm, kw), lambda i: (i, k_blk)),
                  pl.BlockSpec((tm, kw), lambda i: (i, v_blk)),
                  pl.BlockSpec((tm, ATT_HEAD_DIM), tab),
                  pl.BlockSpec((tm, ATT_HEAD_DIM), tab),
                  pl.BlockSpec((1, ATT_HEAD_DIM), lambda i: (0, 0)),
                  pl.BlockSpec((1, ATT_HEAD_DIM), lambda i: (0, 0))],
        out_specs=[pl.BlockSpec((tm, qw), lambda i: (i, 0)),
                   pl.BlockSpec((tm, kw), lambda i: (i, 0)),
                   pl.BlockSpec((ATT_KV_HEADS * ATT_V_ROWS, tm), lambda i: (0, i))],
        out_shape=[jax.ShapeDtypeStruct((n, qw), BF16),
                   jax.ShapeDtypeStruct((n, kw), BF16),
                   jax.ShapeDtypeStruct((ATT_KV_HEADS * ATT_V_ROWS, n), BF16)],
        compiler_params=_cparams(("parallel",)),
        name="att_prep",
    )(p, p, p, cos_t, sin_t, gq.reshape(1, -1), gk.reshape(1, -1))


def _attn_kernel(*refs, with_latent_keys):
    ns = ATT_STAGES
    if with_latent_keys:
        qt_ref, kl_ref, vtl_ref, kc_ref, vtc_ref, o_ref, acc_ref = refs[:7]
    else:
        qt_ref, kc_ref, vtc_ref, o_ref, acc_ref = refs[:5]
    s_refs, p_refs = refs[-2 * ns:-ns], refs[-ns:]
    hd = ATT_HEAD_DIM
    grp = ATT_HEADS // ATT_KV_HEADS
    tq = qt_ref.shape[0]
    nq = grp * tq
    tk = kc_ref.shape[0]
    n = 1 + (kl_ref.shape[0] // tk if with_latent_keys else 0)
    q2 = jnp.concatenate([qt_ref[:, j * hd:(j + 1) * hd] for j in range(grp)], axis=0)

    def key_chunk(j):
        if isinstance(j, int) and j == 0:
            return kc_ref[...]
        return kl_ref[pl.ds(pl.multiple_of((j - 1) * tk, tk), tk), :]

    def value_chunk(j):
        if isinstance(j, int) and j == 0:
            return vtc_ref[...]
        return vtl_ref[:, pl.ds(pl.multiple_of((j - 1) * tk, tk), tk)]

    def max_keys(x):
        groups = math.gcd(tk // SUBLANES, ATT_REDUCE_GROUPS)
        part = jnp.max(x.reshape(groups, tk // groups, nq), axis=0)
        return jnp.max(part, axis=0, keepdims=True)

    def time_step(t, slot, carry, do_scores=True, do_softmax=True, do_values=True):
        m, alpha = carry
        if do_scores:
            s_refs[slot][...] = _dot_nt(key_chunk(t), q2)
        if do_values:
            pv = jnp.dot(value_chunk(t - 2), p_refs[(slot - 2) % ns][...], preferred_element_type=F32)
            acc_ref[...] = acc_ref[...] * alpha + pv
        if do_softmax:
            st = s_refs[(slot - 1) % ns][...]
            m_new = jnp.maximum(m, max_keys(st))
            p_refs[(slot - 1) % ns][...] = jnp.exp2(st - m_new).astype(BF16)
            alpha = jnp.exp2(m - m_new)
            m = m_new
        return m, alpha

    acc_ref[...] = jnp.zeros_like(acc_ref)
    carry = (jnp.full((1, nq), -1e30, F32), jnp.zeros((1, nq), F32))
    fill = min(3, n + 2)
    for t in range(fill):
        carry = time_step(t, t % ns, carry, t < n, 0 <= t - 1 < n, 0 <= t - 2 < n)
    steady = max(n - fill, 0)
    body_steps = ns * ATT_UNROLL

    def unrolled(i, cr):
        for k in range(body_steps):
            cr = time_step(fill + body_steps * i + k, (fill + k) % ns, cr)
        return cr

    if steady // body_steps:
        carry = lax.fori_loop(0, steady // body_steps, unrolled, carry)
    for t in range(fill + body_steps * (steady // body_steps), max(n, fill)):
        carry = time_step(t, t % ns, carry)
    for t in range(max(n, fill), n + 2):
        carry = time_step(t, t % ns, carry, False, t - 1 < n, True)
    out = (acc_ref[0:hd, :] / acc_ref[hd:hd + 1, :]).T
    for j in range(grp):
        o_ref[:, j * hd:(j + 1) * hd] = out[j * tq:(j + 1) * tq].astype(o_ref.dtype)


def attention(dm, qt, kn, vt, latent):
    hd = ATT_HEAD_DIM
    grp = ATT_HEADS // ATT_KV_HEADS
    tk = dm.ctx
    assert dm.s % tk == 0
    ctx_blk0 = dm.n_lat // dm.ctx
    kc_spec = pl.BlockSpec((tk, hd), lambda b, h, i: (ctx_blk0 + b, h))
    vtc_spec = pl.BlockSpec((ATT_V_ROWS, tk), lambda b, h, i: (h, ctx_blk0 + b))
    if latent:
        tq = TQ_ATT
        nq = dm.s // tq
        q_spec = pl.BlockSpec((tq, grp * hd), lambda b, h, i: (b * nq + i, h))
        kl_spec = pl.BlockSpec((dm.s, hd), lambda b, h, i: (b, h))
        vtl_spec = pl.BlockSpec((ATT_V_ROWS, dm.s), lambda b, h, i: (h, b))
        in_specs = [q_spec, kl_spec, vtl_spec, kc_spec, vtc_spec]
        args = (qt, kn, vt, kn, vt)
        rows = dm.n_lat
    else:
        tq, nq = dm.ctx, 1
        q_spec = pl.BlockSpec((tq, grp * hd), lambda b, h, i: (ctx_blk0 + b, h))
        in_specs = [q_spec, kc_spec, vtc_spec]
        args = (qt, kn, vt)
        rows = dm.b * dm.ctx
    return pl.pallas_call(
        functools.partial(_attn_kernel, with_latent_keys=latent),
        grid=(dm.b, ATT_KV_HEADS, nq),
        in_specs=in_specs,
        out_specs=pl.BlockSpec((tq, grp * hd), lambda b, h, i: (b * nq + i, h)),
        out_shape=jax.ShapeDtypeStruct((rows, ATT_HEADS * hd), BF16),
        scratch_shapes=([pltpu.VMEM((ATT_V_ROWS, grp * tq), F32)]
                        + [pltpu.VMEM((tk, grp * tq), F32)] * ATT_STAGES
                        + [pltpu.VMEM((tk, grp * tq), BF16)] * ATT_STAGES),
        compiler_params=_cparams(("parallel", "parallel", "arbitrary")),
        name="attention_latent" if latent else "attention_ctx",
    )(*args)


def _pool_kernel(prev_ref, u_ref, next_ref, w_ref, sc_ref, o_ref, a_ref, b_ref, *, dm):
    tm = u_ref.shape[0]
    H = POOL_HALO
    i = pl.program_id(0)
    lat_tiles = dm.n_lat // tm
    is_lat = i < lat_tiles
    n_seg = jnp.where(is_lat, dm.s, dm.ctx)
    base = jnp.where(is_lat, (i * tm) % dm.s, 0)
    r = lax.broadcasted_iota(jnp.int32, (tm + 2 * H, 1), 0) - H
    pos = jnp.where(is_lat, base + r, r % dm.ctx)
    win = pl.ds(H, tm + 2 * H)

    def shifted(ref, d):
        return ref[pl.ds(H + d, tm + 2 * H), :]

    zeros = jnp.zeros((H, POOL_GROUP_DIM), F32)
    for ref in (a_ref, b_ref):
        ref[pl.ds(0, H), :] = zeros
        ref[pl.ds(tm + 3 * H, H), :] = zeros

    for gi, w in enumerate(POOL_WINDOWS):
        cols = slice(gi * POOL_GROUP_DIM, (gi + 1) * POOL_GROUP_DIM)
        half = w // 2
        a_ref[pl.ds(H, H), :] = prev_ref[:, cols]
        a_ref[pl.ds(2 * H, tm), :] = u_ref[:, cols]
        a_ref[pl.ds(2 * H + tm, H), :] = next_ref[:, cols]
        b_ref[win, :] = jnp.where(pos >= 1, shifted(a_ref, -1), 0.0)
        step = 1
        while step < half:
            b_ref[win, :] = b_ref[win, :] + jnp.where(pos >= step, shifted(b_ref, -step), 0.0)
            step *= 2
        step = 1
        while step < half:
            a_ref[win, :] = a_ref[win, :] + jnp.where(pos + step < n_seg, shifted(a_ref, step), 0.0)
            step *= 2
        tile = pl.ds(2 * H, tm)
        pt = pos[H:H + tm]
        cnt = (jnp.minimum(pt + half, n_seg) - jnp.maximum(pt - half, 0)).astype(F32)
        u = u_ref[:, cols]
        diff = (a_ref[tile, :] + b_ref[tile, :]) / cnt - u
        o_ref[:, cols] = (_dot(diff, w_ref[gi]) * sc_ref[:, cols]).astype(o_ref.dtype)


def pool_mixer(dm, p, pool_w, pool_scale):
    tm = dm.tile(TM_POOL)
    n = dm.n_all
    H = POOL_HALO
    width = len(POOL_WINDOWS) * POOL_GROUP_DIM
    col = 2560 // width
    per = tm // H
    last = n // H - 1
    return pl.pallas_call(
        functools.partial(_pool_kernel, dm=dm),
        grid=(n // tm,),
        in_specs=[pl.BlockSpec((H, width), lambda i: (jnp.maximum(i * per - 1, 0), col)),
                  pl.BlockSpec((tm, width), lambda i: (i, col)),
                  pl.BlockSpec((H, width), lambda i: (jnp.minimum((i + 1) * per, last), col)),
                  pl.BlockSpec(pool_w.shape, lambda i: (0, 0, 0)),
                  pl.BlockSpec((1, width), lambda i: (0, 0))],
        out_specs=pl.BlockSpec((tm, width), lambda i: (i, 0)),
        out_shape=jax.ShapeDtypeStruct((n, width), BF16),
        scratch_shapes=[pltpu.VMEM((tm + 4 * H, POOL_GROUP_DIM), F32),
                        pltpu.VMEM((tm + 4 * H, POOL_GROUP_DIM), F32)],
        compiler_params=_cparams(("parallel",)),
        name="pool_mixer",
    )(p, p, p, pool_w, pool_scale.reshape(1, width))


def _merge_kernel(yr_ref, ya_ref, yp_ref, gl_ref, *refs, lat_tiles):
    mod_ref, wb_ref, wo_ref, o_ref = refs[-4:]
    x_refs = refs[:-4]
    d = o_ref.shape[1]
    mixed = None
    for j, y_ref in enumerate((yr_ref, ya_ref, yp_ref)):
        gate = gl_ref[:, j * d:(j + 1) * d].astype(F32)
        term = gate * jnp.dot(y_ref[...], wb_ref[j], preferred_element_type=F32)
        mixed = term if mixed is None else mixed + term
    y = mod_ref[0][2:3] * _dot(mixed, wo_ref[...])

    def finish(x_ref):
        o_ref[...] = x_ref[...] + y

    _for_row_source(x_refs, lat_tiles, finish)


def merge(dm, n_rows, yr, ya, yp, gates, x_parts, mod, wb, wo):
    tm = dm.tile(TM_MERGE)
    d = dm.d
    bw = yr.shape[1]
    row = dm.mod_row(tm)
    return pl.pallas_call(
        functools.partial(_merge_kernel, lat_tiles=dm.n_lat // tm),
        grid=(n_rows // tm,),
        in_specs=[pl.BlockSpec((tm, bw), lambda i: (i, 0)),
                  pl.BlockSpec((tm, bw), lambda i: (i, 0)),
                  pl.BlockSpec((tm, bw), lambda i: (i, 0)),
                  pl.BlockSpec((tm, 3 * d), lambda i: (i, 0))]
        + _row_sources(dm, tm, x_parts) + [
            pl.BlockSpec((1, 6, d), lambda i: (row(i), 0, 0)),
            pl.BlockSpec(wb.shape, lambda i: (0, 0, 0)),
            pl.BlockSpec(wo.shape, lambda i: (0, 0))],
        out_specs=pl.BlockSpec((tm, d), lambda i: (i, 0)),
        out_shape=jax.ShapeDtypeStruct((n_rows, d), F32),
        compiler_params=_cparams(("parallel",)),
        name="merge",
    )(yr, ya, yp, gates, *x_parts, mod, wb, wo)


def _ffn_kernel(x_ref, mod_ref, gain_ref, w1_ref, w3_ref, w2_ref, o_ref, h_ref, acc_ref):
    f = pl.program_id(1)
    m = mod_ref[0]

    @pl.when(f == 0)
    def _():
        h_ref[...] = _norm_mod(x_ref[...], gain_ref[...], m[3:4], m[4:5]).astype(BF16)
        acc_ref[...] = jnp.zeros_like(acc_ref)

    h = h_ref[...]
    tf = w1_ref.shape[1]
    y = None
    for lo in range(0, tf, MOE_COL_CHUNK):
        cols = slice(lo, min(lo + MOE_COL_CHUNK, tf))
        a = jnp.dot(h, w1_ref[:, cols], preferred_element_type=F32)
        b = jnp.dot(h, w3_ref[:, cols], preferred_element_type=F32)
        part = _dot(_silu(a) * b, w2_ref[cols, :])
        y = part if y is None else y + part
    acc_ref[...] += y

    @pl.when(f == pl.num_programs(1) - 1)
    def _():
        o_ref[...] = x_ref[...] + m[5:6] * acc_ref[...]


def ffn_dense(dm, x, mod, gain, w1, w3, w2):
    n, d = x.shape
    fdim = w1.shape[1]
    tm = dm.tile(TM_FFN)
    tf = TF_MOE if fdim % TF_MOE == 0 else TF_FFN
    row = dm.mod_row(tm)
    return pl.pallas_call(
        _ffn_kernel,
        grid=(n // tm, fdim // tf),
        in_specs=[pl.BlockSpec((tm, d), lambda i, f: (i, 0)),
                  pl.BlockSpec((1, 6, d), lambda i, f: (row(i), 0, 0)),
                  pl.BlockSpec((1, d), lambda i, f: (0, 0)),
                  pl.BlockSpec((d, tf), lambda i, f: (0, f)),
                  pl.BlockSpec((d, tf), lambda i, f: (0, f)),
                  pl.BlockSpec((tf, d), lambda i, f: (f, 0))],
        out_specs=pl.BlockSpec((tm, d), lambda i, f: (i, 0)),
        out_shape=jax.ShapeDtypeStruct((n, d), F32),
        scratch_shapes=[pltpu.VMEM((tm, d), BF16), pltpu.VMEM((tm, d), F32)],
        compiler_params=_cparams(("parallel", "arbitrary")),
        name="ffn_dense",
    )(x, mod, gain.reshape(1, d), w1, w3, w2)


def _router_kernel(x_ref, mod_ref, gain_ref, wr_ref, br_ref, h_ref, slot_ref, slot_t_ref, wt_ref, cum_ref):
    tm = x_ref.shape[0]
    m = mod_ref[0]
    h = _norm_mod(x_ref[...], gain_ref[...], m[3:4], m[4:5])
    h_ref[...] = h.astype(BF16)
    logits = _dot3(h, wr_ref[...]) + br_ref[...]
    lane = lax.broadcasted_iota(jnp.int32, logits.shape, 1)
    m1 = jnp.max(logits, axis=-1, keepdims=True)
    i1 = jnp.min(jnp.where(logits == m1, lane, LANES), axis=-1, keepdims=True)
    rest = jnp.where(lane == i1, -jnp.inf, logits)
    m2 = jnp.max(rest, axis=-1, keepdims=True)
    i2 = jnp.min(jnp.where(rest == m2, lane, LANES), axis=-1, keepdims=True)
    e = jnp.exp(m2 - m1)
    w1 = 1.0 / (1.0 + e)
    w2 = e / (1.0 + e)
    wt_ref[...] = jnp.where(lane == i1, w1, jnp.where(lane == i2, w2, 0.0))
    routed = jnp.where((lane == i1) | (lane == i2), 1.0, 0.0)
    sub = SUB_MOE
    lower = (lax.broadcasted_iota(jnp.int32, (sub, sub), 0) > lax.broadcasted_iota(jnp.int32, (sub, sub), 1))
    lower = lower.astype(BF16)
    offset = jnp.zeros((1, LANES), F32)
    cum_ref[...] = jnp.zeros_like(cum_ref)
    for blk in range(tm // sub):
        rows = slice(blk * sub, (blk + 1) * sub)
        mask = routed[rows]
        cum_ref[0, blk:blk + 1, :] = offset.astype(jnp.int32)
        rank = jnp.dot(lower, mask.astype(BF16), preferred_element_type=F32) + offset
        slot_ref[rows, :] = jnp.where(mask > 0.0, rank, -1.0)
        count = jnp.sum(mask, axis=0, keepdims=True)
        offset = offset + jnp.ceil(count * (1.0 / SLOT_ALIGN)) * SLOT_ALIGN
    cum_ref[0, tm // sub:tm // sub + 1, :] = offset.astype(jnp.int32)
    slot_t_ref[...] = slot_ref[...].T[:SUBLANES, :]


def moe_route(dm, n_rows, x, mod, gain, wr_pad, br_pad):
    tm = dm.tile(TM_MOE)
    d = dm.d
    row = dm.mod_row(tm)
    nt = n_rows // tm
    cum_rows = -(-(tm // SUB_MOE + 1) // SUBLANES) * SUBLANES
    return pl.pallas_call(
        _router_kernel,
        grid=(nt,),
        in_specs=[pl.BlockSpec((tm, d), lambda i: (i, 0)),
                  pl.BlockSpec((1, 6, d), lambda i: (row(i), 0, 0)),
                  pl.BlockSpec((1, d), lambda i: (0, 0)),
                  pl.BlockSpec((d, LANES), lambda i: (0, 0)),
                  pl.BlockSpec((1, LANES), lambda i: (0, 0))],
        out_specs=[pl.BlockSpec((tm, d), lambda i: (i, 0)),
                   pl.BlockSpec((tm, LANES), lambda i: (i, 0)),
                   pl.BlockSpec((SUBLANES, tm), lambda i: (0, i)),
                   pl.BlockSpec((tm, LANES), lambda i: (i, 0)),
                   pl.BlockSpec((1, cum_rows, LANES), lambda i: (i, 0, 0))],
        out_shape=[jax.ShapeDtypeStruct((n_rows, d), BF16),
                   jax.ShapeDtypeStruct((n_rows, LANES), F32),
                   jax.ShapeDtypeStruct((SUBLANES, n_rows), F32),
                   jax.ShapeDtypeStruct((n_rows, LANES), F32),
                   jax.ShapeDtypeStruct((nt, cum_rows, LANES), jnp.int32)],
        compiler_params=_cparams(("parallel",)),
        name="moe_router",
    )(x, mod, gain.reshape(1, d), wr_pad, br_pad)


def _moe_kernel(cum_ref, h_ref, slot_ref, slot_t_ref, wt_ref, w1_ref, w3_ref, w2_ref, o_ref,
                xs_ref, acc_ref):
    i, e, f = pl.program_id(0), pl.program_id(1), pl.program_id(2)
    tm, d = h_ref.shape
    rb, win, sub = RB_MOE, WIN_MOE, SUB_MOE
    nsub = tm // sub

    def base(j):
        return cum_ref[i, j * N_EXPERTS + e]

    halves = (base(nsub) + rb // 2 - 1) // (rb // 2)
    nblk, has_tail = halves // 2, halves % 2 == 1

    def block_rows(r, size=rb):
        return pl.ds(pl.multiple_of(r * rb, 32), size)

    def windows(j):
        return base(j), (base(j + 1) - base(j) + win - 1) // win

    def window_rows(start, w):
        return pl.ds(pl.multiple_of(start + w * win, SLOT_ALIGN), win)

    @pl.when((e == 0) & (f == 0))
    def _():
        o_ref[...] = jnp.zeros_like(o_ref)

    @pl.when(f == 0)
    def _():
        def clear(r, carry):
            xs_ref[block_rows(r), :] = jnp.zeros((rb, d), BF16)
            acc_ref[block_rows(r), :] = jnp.zeros((rb, d), F32)
            return carry

        lax.fori_loop(0, nblk + 1, clear, 0)
        acc_ref[pl.ds(pl.multiple_of((nblk + 1) * rb, 32), win), :] = jnp.zeros((win, d), F32)

        for j in range(nsub):
            tok = slice(j * sub, (j + 1) * sub)
            start, nwin = windows(j)

            def gather(w, carry, tok=tok, start=start):
                rows = window_rows(start, w)
                want = (lax.broadcasted_iota(jnp.int32, (win, sub), 0) + (start + w * win)).astype(F32)
                onehot = (slot_t_ref[pl.ds(e, 1), tok] == want).astype(BF16)
                xs_ref[rows, :] = jnp.dot(onehot, h_ref[tok, :], preferred_element_type=F32).astype(BF16)
                return carry

            lax.fori_loop(0, nwin, gather, 0)

    def expert(r, carry, size=rb):
        rows = block_rows(r, size)
        xs = xs_ref[rows, :]
        tf = w1_ref.shape[2]
        y = None
        for lo in range(0, tf, MOE_COL_CHUNK):
            cols = slice(lo, min(lo + MOE_COL_CHUNK, tf))
            a = jnp.dot(xs, w1_ref[0, :, cols], preferred_element_type=F32)
            b = jnp.dot(xs, w3_ref[0, :, cols], preferred_element_type=F32)
            part = _dot(_silu(a) * b, w2_ref[0, cols, :])
            y = part if y is None else y + part
        acc_ref[rows, :] += y
        return carry

    lax.fori_loop(0, nblk, expert, 0)

    @pl.when(has_tail)
    def _():
        expert(nblk, 0, rb // 2)

    @pl.when(f == pl.num_programs(2) - 1)
    def _():
        pick = lax.broadcasted_iota(jnp.int32, (sub, LANES), 1) == e
        for j in range(nsub):
            tok = slice(j * sub, (j + 1) * sub)
            start, nwin = windows(j)
            slot_col = jnp.sum(jnp.where(pick, slot_ref[tok, :], 0.0), axis=-1, keepdims=True)
            wt_col = jnp.sum(jnp.where(pick, wt_ref[tok, :], 0.0), axis=-1, keepdims=True)

            def scatter(w, carry, tok=tok, start=start, slot_col=slot_col, wt_col=wt_col):
                ys = acc_ref[window_rows(start, w), :].astype(BF16)
                want = (lax.broadcasted_iota(jnp.int32, (sub, win), 1) + (start + w * win)).astype(F32)
                onehot = (slot_col == want).astype(BF16)
                o_ref[tok, :] += wt_col * jnp.dot(onehot, ys, preferred_element_type=F32)
                return carry

            lax.fori_loop(0, nwin, scatter, 0)


def moe_experts(dm, n_rows, h, slot, slot_t, wt, counts, w1, w3, w2):
    tm = dm.tile(TM_MOE)
    d = dm.d
    ne, _, fdim = w1.shape
    tf = TF_MOE if fdim % TF_MOE == 0 else TF_FFN
    assert SUB_MOE % SLOT_ALIGN == 0 and tm % SUB_MOE == 0
    cap = tm + RB_MOE + WIN_MOE
    once = dict(pipeline_mode=pl.Buffered(1))
    grid_spec = pltpu.PrefetchScalarGridSpec(
        num_scalar_prefetch=1,
        grid=(n_rows // tm, ne, fdim // tf),
        in_specs=[pl.BlockSpec((tm, d), lambda i, e, f, c: (i, 0), **once),
                  pl.BlockSpec((tm, LANES), lambda i, e, f, c: (i, 0), **once),
                  pl.BlockSpec((SUBLANES, tm), lambda i, e, f, c: (0, i), **once),
                  pl.BlockSpec((tm, LANES), lambda i, e, f, c: (i, 0), **once),
                  pl.BlockSpec((1, d, tf), lambda i, e, f, c: (e, 0, f)),
                  pl.BlockSpec((1, d, tf), lambda i, e, f, c: (e, 0, f)),
                  pl.BlockSpec((1, tf, d), lambda i, e, f, c: (e, f, 0))],
        out_specs=pl.BlockSpec((tm, d), lambda i, e, f, c: (i, 0), **once),
        scratch_shapes=[pltpu.VMEM((cap, d), BF16), pltpu.VMEM((cap, d), F32)],
    )
    return pl.pallas_call(
        _moe_kernel,
        grid_spec=grid_spec,
        out_shape=jax.ShapeDtypeStruct((n_rows, d), F32),
        compiler_params=_cparams(("parallel", "arbitrary", "arbitrary")),
        name="moe_experts",
    )(counts, h, slot, slot_t, wt, w1, w3, w2)


def _moe_windows(base_ref, row0_ref, e):
    i, j = pl.program_id(0), pl.program_id(1)
    start = base_ref[i, j * N_EXPERTS + e]
    nwin = (base_ref[i, (j + 1) * N_EXPERTS + e] - start + WIN_MOE - 1) // WIN_MOE
    return start, nwin, row0_ref[i, e]


def _moe_gather_kernel(base_ref, row0_ref, h_ref, slot_t_ref, xs_in_ref, xs_ref, buf_ref, sem):
    del xs_in_ref
    win, sub = WIN_MOE, SUB_MOE
    spare = N_EXPERTS

    def gathered(e, start, w):
        want = (lax.broadcasted_iota(jnp.int32, (win, sub), 0) + (start + w * win)).astype(F32)
        onehot = (slot_t_ref[e:e + 1, :] == want).astype(BF16)
        return jnp.dot(onehot, h_ref[...], preferred_element_type=F32).astype(BF16)

    def copy(e, buf, w):
        start, _, row0 = _moe_windows(base_ref, row0_ref, e)
        rows = pl.ds(pl.multiple_of(row0 + start + w * win, SLOT_ALIGN), win)
        return pltpu.make_async_copy(buf_ref.at[buf], xs_ref.at[rows, :], sem.at[buf])

    for e in range(N_EXPERTS):
        start, nwin, _ = _moe_windows(base_ref, row0_ref, e)

        @pl.when(nwin > 0)
        def _(e=e, start=start):
            buf_ref[e] = gathered(e, start, 0)
            copy(e, e, 0).start()

        def extra(w, carry, e=e, start=start):
            buf_ref[spare] = gathered(e, start, w)
            cp = copy(e, spare, w)
            cp.start()
            cp.wait()
            return carry

        lax.fori_loop(1, jnp.maximum(nwin, 1), extra, 0)

    for e in range(N_EXPERTS):
        _, nwin, _ = _moe_windows(base_ref, row0_ref, e)
        pl.when(nwin > 0)(lambda e=e: copy(e, e, 0).wait())


def moe_gather(dm, n_rows, h, slot_t, base, row0, cap):
    tm, sub, d = dm.tile(TM_MOE), SUB_MOE, dm.d
    nsub = tm // sub
    grid_spec = pltpu.PrefetchScalarGridSpec(
        num_scalar_prefetch=2,
        grid=(n_rows // tm, nsub),
        in_specs=[pl.BlockSpec((sub, d), lambda i, j, b, r: (i * nsub + j, 0)),
                  pl.BlockSpec((SUBLANES, sub), lambda i, j, b, r: (0, i * nsub + j)),
                  pl.BlockSpec(memory_space=pl.ANY)],
        out_specs=pl.BlockSpec(memory_space=pl.ANY),
        scratch_shapes=[pltpu.VMEM((N_EXPERTS + 1, WIN_MOE, d), BF16),
                        pltpu.SemaphoreType.DMA((N_EXPERTS + 1,))],
    )
    return pl.pallas_call(
        _moe_gather_kernel,
        grid_spec=grid_spec,
        out_shape=jax.ShapeDtypeStruct((cap, d), BF16),
        input_output_aliases={4: 0},
        compiler_params=_cparams(("arbitrary", "arbitrary")),
        name="moe_gather",
    )(base, row0, h, slot_t, jnp.zeros((cap, d), BF16))


def _moe_ffn_kernel(bexp_ref, bvalid_ref, xs_ref, w1_ref, w3_ref, w2_ref, o_ref):
    del bexp_ref
    valid = bvalid_ref[pl.program_id(0)] == 1

    @pl.when(valid)
    def _():
        xs = xs_ref[...]
        fdim = w1_ref.shape[2]
        y = None
        for lo in range(0, fdim, MOE_COL_CHUNK):
            cols = slice(lo, min(lo + MOE_COL_CHUNK, fdim))
            a = jnp.dot(xs, w1_ref[0, :, cols], preferred_element_type=F32)
            b = jnp.dot(xs, w3_ref[0, :, cols], preferred_element_type=F32)
            part = _dot(_silu(a) * b, w2_ref[0, cols, :])
            y = part if y is None else y + part
        o_ref[...] = y

    @pl.when(jnp.logical_not(valid))
    def _():
        o_ref[...] = jnp.zeros_like(o_ref)


def moe_ffn(xs, blk_expert, blk_valid, w1, w3, w2):
    cap, d = xs.shape
    _, _, fdim = w1.shape
    blk = BLK_MOE
    once = dict(pipeline_mode=pl.Buffered(1))
    grid_spec = pltpu.PrefetchScalarGridSpec(
        num_scalar_prefetch=2,
        grid=(cap // blk,),
        in_specs=[pl.BlockSpec((blk, d), lambda k, be, bv: (k, 0)),
                  pl.BlockSpec((1, d, fdim), lambda k, be, bv: (be[k], 0, 0), **once),
                  pl.BlockSpec((1, d, fdim), lambda k, be, bv: (be[k], 0, 0), **once),
                  pl.BlockSpec((1, fdim, d), lambda k, be, bv: (be[k], 0, 0), **once)],
        out_specs=pl.BlockSpec((blk, d), lambda k, be, bv: (k, 0)),
    )
    return pl.pallas_call(
        _moe_ffn_kernel,
        grid_spec=grid_spec,
        out_shape=jax.ShapeDtypeStruct((cap, d), F32),
        compiler_params=_cparams(("arbitrary",)),
        name="moe_ffn",
    )(blk_expert, blk_valid, xs, w1, w3, w2)


def _moe_scatter_kernel(base_ref, row0_ref, slot_ref, wt_ref, x_ref, mod_ref, gain_ref, ys_ref, o_ref,
                        buf_ref, y_ref, sem, *, final):
    win, sub = WIN_MOE, SUB_MOE
    spare = N_EXPERTS

    def copy(e, buf, w):
        start, _, row0 = _moe_windows(base_ref, row0_ref, e)
        rows = pl.ds(pl.multiple_of(row0 + start + w * win, SLOT_ALIGN), win)
        return pltpu.make_async_copy(ys_ref.at[rows, :], buf_ref.at[buf], sem.at[buf])

    for e in range(N_EXPERTS):
        _, nwin, _ = _moe_windows(base_ref, row0_ref, e)
        pl.when(nwin > 0)(lambda e=e: copy(e, e, 0).start())

    y_ref[...] = jnp.zeros_like(y_ref)
    for e in range(N_EXPERTS):
        start, nwin, _ = _moe_windows(base_ref, row0_ref, e)
        slot_col = slot_ref[:, e:e + 1]
        wt_col = wt_ref[:, e:e + 1]

        def scatter(buf, w, start=start, slot_col=slot_col, wt_col=wt_col):
            want = (lax.broadcasted_iota(jnp.int32, (sub, win), 1) + (start + w * win)).astype(F32)
            onehot = (slot_col == want).astype(BF16)
            y_ref[...] += wt_col * _dot(onehot, buf_ref[buf])

        @pl.when(nwin > 0)
        def _(e=e, scatter=scatter):
            copy(e, e, 0).wait()
            scatter(e, 0)

        def extra(w, carry, e=e, scatter=scatter):
            cp = copy(e, spare, w)
            cp.start()
            cp.wait()
            scatter(spare, w)
            return carry

        lax.fori_loop(1, jnp.maximum(nwin, 1), extra, 0)

    x = x_ref[...] + mod_ref[0][5:6] * y_ref[...]
    if final:
        x = x * lax.rsqrt(jnp.mean(x * x, axis=-1, keepdims=True) + NORM_EPS) * gain_ref[...]
    o_ref[...] = x


def moe_scatter(dm, n_rows, ys, slot, wt, base, row0, x, mod, final_gain):
    tm, sub, d = dm.tile(TM_MOE), SUB_MOE, dm.d
    nsub = tm // sub
    row = dm.mod_row(sub)
    final = final_gain is not None
    gain = final_gain if final else jnp.ones((d,), F32)
    tok = lambda i, j, b, r: (i * nsub + j, 0)
    grid_spec = pltpu.PrefetchScalarGridSpec(
        num_scalar_prefetch=2,
        grid=(n_rows // tm, nsub),
        in_specs=[pl.BlockSpec((sub, LANES), tok),
                  pl.BlockSpec((sub, LANES), tok),
                  pl.BlockSpec((sub, d), tok),
                  pl.BlockSpec((1, 6, d), lambda i, j, b, r: (row(i * nsub + j), 0, 0)),
                  pl.BlockSpec((1, d), lambda i, j, b, r: (0, 0)),
                  pl.BlockSpec(memory_space=pl.ANY)],
        out_specs=pl.BlockSpec((sub, d), tok),
        scratch_shapes=[pltpu.VMEM((N_EXPERTS + 1, WIN_MOE, d), F32),
                        pltpu.VMEM((sub, d), F32),
                        pltpu.SemaphoreType.DMA((N_EXPERTS + 1,))],
    )
    return pl.pallas_call(
        functools.partial(_moe_scatter_kernel, final=final),
        grid_spec=grid_spec,
        out_shape=jax.ShapeDtypeStruct((n_rows, d), F32),
        compiler_params=_cparams(("arbitrary", "arbitrary")),
        name="moe_scatter",
    )(base, row0, slot, wt, x, mod, gain.reshape(1, d), ys)


def moe_layout(dm, n_rows, cum):
    tm = dm.tile(TM_MOE)
    nt, nsub = n_rows // tm, tm // SUB_MOE
    base = cum[:, :nsub + 1, :N_EXPERTS]
    total = base[:, nsub, :]
    per_expert = jnp.sum(total, axis=0)
    padded = (per_expert + WIN_MOE + BLK_MOE - 1) // BLK_MOE * BLK_MOE
    expert_row0 = jnp.cumsum(padded) - padded
    row0 = expert_row0[None, :] + jnp.cumsum(total, axis=0) - total
    slack = nt * nsub * N_EXPERTS * (SLOT_ALIGN - 1) + N_EXPERTS * (WIN_MOE + BLK_MOE)
    nblk = -(-(2 * n_rows + slack) // BLK_MOE)
    ends = jnp.cumsum(padded // BLK_MOE)
    k = jnp.arange(nblk, dtype=jnp.int32)
    blk_expert = jnp.minimum(jnp.searchsorted(ends, k, side="right"), N_EXPERTS - 1).astype(jnp.int32)
    blk_valid = (k < ends[-1]).astype(jnp.int32)
    return base.reshape(nt, (nsub + 1) * N_EXPERTS), row0.astype(jnp.int32), blk_expert, blk_valid, nblk * BLK_MOE


def _residual_kernel(x_ref, y_ref, mod_ref, o_ref):
    o_ref[...] = x_ref[...] + mod_ref[0][5:6] * y_ref[...]


def _final_kernel(x_ref, y_ref, mod_ref, gain_ref, o_ref):
    x = x_ref[...] + mod_ref[0][5:6] * y_ref[...]
    o_ref[...] = x * lax.rsqrt(jnp.mean(x * x, axis=-1, keepdims=True) + NORM_EPS) * gain_ref[...]


def residual(dm, n_rows, x, y, mod, final_gain=None):
    tm = dm.tile(TM_FFN)
    d = dm.d
    row = dm.mod_row(tm)
    in_specs = [pl.BlockSpec((tm, d), lambda i: (i, 0)),
                pl.BlockSpec((tm, d), lambda i: (i, 0)),
                pl.BlockSpec((1, 6, d), lambda i: (row(i), 0, 0))]
    args = [x, y, mod]
    if final_gain is not None:
        in_specs.append(pl.BlockSpec((1, d), lambda i: (0, 0)))
        args.append(final_gain.reshape(1, d))
    return pl.pallas_call(
        _residual_kernel if final_gain is None else _final_kernel,
        grid=(n_rows // tm,),
        in_specs=in_specs,
        out_specs=pl.BlockSpec((tm, d), lambda i: (i, 0)),
        out_shape=jax.ShapeDtypeStruct((n_rows, d), F32),
        compiler_params=_cparams(("parallel",)),
        name="residual" if final_gain is None else "residual_final_norm",
    )(*args)


def _final_norm_kernel(x_ref, gain_ref, o_ref):
    x = x_ref[...]
    o_ref[...] = x * lax.rsqrt(jnp.mean(x * x, axis=-1, keepdims=True) + NORM_EPS) * gain_ref[...]


def final_rmsnorm(dm, n_rows, x, gain):
    tm = dm.tile(TM_FFN)
    d = dm.d
    return pl.pallas_call(
        _final_norm_kernel,
        grid=(n_rows // tm,),
        in_specs=[pl.BlockSpec((tm, d), lambda i: (i, 0)), pl.BlockSpec((1, d), lambda i: (0, 0))],
        out_specs=pl.BlockSpec((tm, d), lambda i: (i, 0)),
        out_shape=jax.ShapeDtypeStruct((n_rows, d), F32),
        compiler_params=_cparams(("parallel",)),
        name="final_norm",
    )(x, gain.reshape(1, d))


def _rope_tables(s, ctx):
    def angles(pos, dim):
        freqs = 1.0 / (ROPE_THETA ** (jnp.arange(0, dim, 2, dtype=F32) / dim))
        return pos[:, None] * freqs[None, :]

    rows = s // GRID_W
    row = jnp.repeat(jnp.arange(rows, dtype=F32), GRID_W)
    col = jnp.tile(jnp.arange(GRID_W, dtype=F32), rows)
    half = ATT_HEAD_DIM // 2
    ang_att = jnp.concatenate([angles(row, half), angles(col, half)], axis=-1)
    sign = jnp.tile(jnp.array([-1.0, 1.0], F32), LANES // 2)

    def tables(ang):
        reps = LANES // (2 * ang.shape[1])
        cos = jnp.tile(jnp.repeat(jnp.cos(ang), 2, axis=-1), (1, reps))
        sin = jnp.tile(jnp.repeat(jnp.sin(ang), 2, axis=-1), (1, reps)) * sign
        return cos, sin

    ang_ret = angles(jnp.arange(s, dtype=F32), RET_QK_DIM)
    ang_ret = jnp.concatenate([ang_ret, jnp.zeros((ctx, RET_QK_DIM // 2), F32)], axis=0)
    return tables(ang_att) + tables(ang_ret)


def kernel(x, c, ctx, c_ctx, w_ada, b_ada, norm_mix, norm_ffn, w_in, ret_decay, ret_gn, attn_qn, attn_kn, pool_w, pool_scale, w_branch, w_out, ffn_w1, ffn_w3, ffn_w2, moe_router, moe_router_b, moe_w1, moe_w3, moe_w2, final_norm):
    b, s, d = x.shape
    n_ctx = ctx.shape[1]
    depth = w_in.shape[0]
    dm = Dims(b, s, n_ctx, d)

    x_parts = (x.reshape(b * s, d), ctx.reshape(b * n_ctx, d))
    mod_rows = 2 * SUBLANES
    cc = jnp.concatenate([c, c_ctx[None, :], jnp.zeros((mod_rows - b - 1, d), F32)], axis=0)
    mods = ada_table(cc, w_ada, b_ada).reshape(depth, mod_rows, 6, d)

    att_cos, att_sin, ret_cos, ret_sin = _rope_tables(s, n_ctx)

    for i in range(depth):
        need_ctx = i < depth - 1
        n_rows = dm.n_all if need_ctx else dm.n_lat
        mod = mods[i]
        p, gates = proj_in(dm, x_parts, mod, norm_mix[i], w_in[i].astype(BF16))
        y_ret = retention(dm, p, ret_decay[i], ret_cos, ret_sin, ret_gn[i])
        y_pool = pool_mixer(dm, p, pool_w[i].astype(BF16), pool_scale[i])
        qt, kn, vt = att_prep(dm, p, att_cos, att_sin, attn_qn[i], attn_kn[i])
        y_att = attention(dm, qt, kn, vt, latent=True)
        if need_ctx:
            y_att = jnp.concatenate([y_att, attention(dm, qt, kn, vt, latent=False)], axis=0)
        xa = merge(dm, n_rows, y_ret, y_att, y_pool, gates, x_parts, mod,
                   w_branch[i].astype(BF16), w_out[i].astype(BF16))
        j = i // 2
        last = i == depth - 1
        if i % 2 == 0:
            xa = ffn_dense(dm, xa, mod, norm_ffn[i], ffn_w1[j].astype(BF16), ffn_w3[j].astype(BF16),
                           ffn_w2[j].astype(BF16))
            if last:
                xa = final_rmsnorm(dm, n_rows, xa, final_norm)
        else:
            wr = jnp.zeros((d, LANES), F32).at[:, :N_EXPERTS].set(moe_router[j])
            br = jnp.full((1, LANES), -1e30, F32).at[0, :N_EXPERTS].set(moe_router_b[j])
            h, slot, slot_t, wt, cnt = moe_route(dm, n_rows, xa, mod, norm_ffn[i], wr, br)
            base, row0, blk_expert, blk_valid, cap = moe_layout(dm, n_rows, cnt)
            xs = moe_gather(dm, n_rows, h, slot_t, base, row0, cap)
            ys = moe_ffn(xs, blk_expert, blk_valid,
                         moe_w1[j].astype(BF16), moe_w3[j].astype(BF16), moe_w2[j].astype(BF16))
            xa = moe_scatter(dm, n_rows, ys, slot, wt, base, row0, xa, mod, final_norm if last else None)
        x_parts = (xa,)
    return xa[:dm.n_lat].reshape(b, s, d)
```

```python
import functools
import math

import numpy as np
import jax
import jax.numpy as jnp
from jax import lax
from jax.experimental import pallas as pl
from jax.experimental.pallas import tpu as pltpu

F32 = jnp.float32
BF16 = jnp.bfloat16

GRID_W = 64
RET_HEADS = 4
RET_QK_DIM = 64
RET_V_DIM = 128
ATT_HEADS = 4
ATT_KV_HEADS = 2
ATT_HEAD_DIM = 128
ROPE_THETA = 10000.0
POOL_WINDOWS = (2, 4, 8, 16)
POOL_GROUP_DIM = 128
N_EXPERTS = 8
NORM_EPS = 1e-6
GN_EPS = 1e-5

LANES = 128
SUBLANES = 8
VMEM_LIMIT_BYTES = 56 * 1024 * 1024

TM_PROJ = 1024
TN_PROJ = 1536
TM_PREP = 1024
TM_POOL = 1024
TM_MERGE = 1024
TM_FFN = 1024
TF_FFN = 512
TM_MOE = 2048
TF_MOE = 1792
MOE_COL_CHUNK = 512
RB_MOE = 256
BLK_MOE = 512
WIN_MOE = 256
SUB_MOE = 512
SLOT_ALIGN = 16
assert RB_MOE % 64 == 0
RET_L = 256
TQ_ATT = 512
ATT_Q_COL = 2 * RET_HEADS * RET_QK_DIM + 2 * RET_HEADS * RET_V_DIM
ATT_K_COL = ATT_Q_COL + ATT_HEADS * ATT_HEAD_DIM
ATT_V_COL = ATT_K_COL + ATT_KV_HEADS * ATT_HEAD_DIM
ATT_SCORE_SCALE = (ATT_HEAD_DIM ** -0.5) * math.log2(math.e)
ATT_V_ROWS = ATT_HEAD_DIM + 16
ATT_STAGES = 3
ATT_UNROLL = 4
ATT_REDUCE_GROUPS = 8
POOL_HALO = 8


def _cparams(sem):
    return pltpu.CompilerParams(dimension_semantics=sem, vmem_limit_bytes=VMEM_LIMIT_BYTES)


def _dot(a, b):
    return jnp.dot(a.astype(BF16), b.astype(BF16), preferred_element_type=F32)


def _dot_nt(a, b):
    return lax.dot_general(a.astype(BF16), b.astype(BF16), (((1,), (1,)), ((), ())),
                           preferred_element_type=F32)


def _dot_tn(a, b):
    return lax.dot_general(a.astype(BF16), b.astype(BF16), (((0,), (0,)), ((), ())),
                           preferred_element_type=F32)


def _split_bf16(a):
    hi = a.astype(BF16)
    lo = (a - hi.astype(F32)).astype(BF16)
    return hi, lo


def _dot3(a, b):
    ah, al = _split_bf16(a)
    bh, bl = _split_bf16(b)
    d = functools.partial(jnp.dot, preferred_element_type=F32)
    return d(ah, bh) + (d(al, bh) + d(ah, bl))


def _norm_mod(x, gain, shift, scale):
    y = x * lax.rsqrt(jnp.mean(x * x, axis=-1, keepdims=True) + NORM_EPS)
    return (y * gain) * (1.0 + scale) + shift


def _sigmoid(x):
    return 1.0 / (1.0 + jnp.exp(-x))


def _silu(x):
    return x * _sigmoid(x)


class Dims:
    def __init__(self, b, s, ctx, d):
        self.b, self.s, self.ctx, self.d = b, s, ctx, d
        self.n_lat = b * s
        self.n_all = b * s + b * ctx

    def tile(self, preferred):
        t = preferred
        while self.s % t or (self.b * self.ctx) % t:
            t //= 2
        return t

    def mod_row(self, tm):
        lat_tiles, per_batch, b = self.n_lat // tm, self.s // tm, self.b
        return lambda i: jnp.where(i < lat_tiles, i // per_batch, b)


def _ada_kernel(c_ref, w_ref, b_ref, o_ref):
    o_ref[0] = _dot3(_silu(c_ref[...]), w_ref[0]) + b_ref[0]


def ada_table(cc, w_ada, b_ada):
    depth, d, n6 = w_ada.shape
    tn = n6 // 4
    rows = cc.shape[0]
    return pl.pallas_call(
        _ada_kernel,
        grid=(depth, n6 // tn),
        in_specs=[pl.BlockSpec((rows, d), lambda l, j: (0, 0)),
                  pl.BlockSpec((1, d, tn), lambda l, j: (l, 0, j)),
                  pl.BlockSpec((1, 1, tn), lambda l, j: (l, 0, j))],
        out_specs=pl.BlockSpec((1, rows, tn), lambda l, j: (l, 0, j)),
        out_shape=jax.ShapeDtypeStruct((depth, rows, n6), F32),
        compiler_params=_cparams(("arbitrary", "arbitrary")),
        name="ada_table",
    )(cc, w_ada, b_ada.reshape(depth, 1, n6))


def _row_sources(dm, tm, parts):
    d = dm.d
    if len(parts) == 1:
        return [pl.BlockSpec((tm, d), lambda i, *_: (i, 0))]
    lat_tiles = dm.n_lat // tm
    return [pl.BlockSpec((tm, d), lambda i, *_: (jnp.minimum(i, lat_tiles - 1), 0)),
            pl.BlockSpec((tm, d), lambda i, *_: (jnp.maximum(i - lat_tiles, 0), 0))]


def _for_row_source(x_refs, lat_tiles, fn):
    if len(x_refs) == 1:
        fn(x_refs[0])
        return
    is_lat = pl.program_id(0) < lat_tiles
    pl.when(is_lat)(lambda: fn(x_refs[0]))
    pl.when(jnp.logical_not(is_lat))(lambda: fn(x_refs[1]))


def _proj_in_kernel(*refs, lat_tiles):
    mod_ref, gain_ref, wm_ref, wg_ref, o_ref, g_ref, h_ref = refs[-7:]
    x_refs = refs[:-7]

    @pl.when(pl.program_id(1) == 0)
    def _():
        def fill(x_ref):
            m = mod_ref[0]
            h_ref[...] = _norm_mod(x_ref[...], gain_ref[...], m[0:1], m[1:2]).astype(BF16)

        _for_row_source(x_refs, lat_tiles, fill)

    h = h_ref[...]
    o_ref[...] = jnp.dot(h, wm_ref[...], preferred_element_type=F32)
    logits = jnp.dot(h, wg_ref[...], preferred_element_type=F32)
    g_ref[...] = (0.5 * jnp.tanh(0.5 * logits) + 0.5).astype(g_ref.dtype)


def proj_in(dm, x_parts, mod, gain, w):
    n, d = dm.n_all, dm.d
    nout = w.shape[1]
    gate_w = 3 * d
    tm, tn = dm.tile(TM_PROJ), TN_PROJ
    assert nout - gate_w == gate_w and gate_w % tn == 0
    gate_blk0 = gate_w // tn
    row = dm.mod_row(tm)
    return pl.pallas_call(
        functools.partial(_proj_in_kernel, lat_tiles=dm.n_lat // tm),
        grid=(n // tm, gate_w // tn),
        in_specs=_row_sources(dm, tm, x_parts) + [
            pl.BlockSpec((1, 6, d), lambda i, j: (row(i), 0, 0)),
            pl.BlockSpec((1, d), lambda i, j: (0, 0)),
            pl.BlockSpec((d, tn), lambda i, j: (0, j)),
            pl.BlockSpec((d, tn), lambda i, j: (0, gate_blk0 + j))],
        out_specs=[pl.BlockSpec((tm, tn), lambda i, j: (i, j)),
                   pl.BlockSpec((tm, tn), lambda i, j: (i, j))],
        out_shape=[jax.ShapeDtypeStruct((n, nout - gate_w), F32),
                   jax.ShapeDtypeStruct((n, gate_w), BF16)],
        scratch_shapes=[pltpu.VMEM((tm, d), BF16)],
        compiler_params=_cparams(("parallel", "arbitrary")),
        name="proj_in",
    )(*x_parts, mod, gain.reshape(1, d), w, w)


def _rope_pairs(x, cos, sin):
    lane = lax.broadcasted_iota(jnp.int32, x.shape, 1)
    partner = jnp.where(lane % 2 == 0, pltpu.roll(x, LANES - 1, 1), pltpu.roll(x, 1, 1))
    return x * cos + partner * sin


def _ret_rope(x, cos, sin):
    return jnp.concatenate([_rope_pairs(x[:, j * LANES:(j + 1) * LANES], cos, sin)
                            for j in range(x.shape[1] // LANES)], axis=1)


def _ret_kernel(dec_ref, q_ref, g_ref, kl_ref, kc_ref, vl_ref, vc_ref, cos_ref, sin_ref, gn_ref, o_ref,
                kr_ref, vs_ref, sf_ref, sb_ref, sball_ref, decay_ref, *, ncc, nlc):
    s = pl.program_id(1)
    L = RET_L
    nc = ncc + nlc
    pos = lax.broadcasted_iota(jnp.int32, (L, 1), 0).astype(F32)

    def log_g(direction, h):
        return -jnp.exp(jnp.full((1, 1), dec_ref[direction, h], F32))

    def chunk(c):
        if isinstance(c, int):
            return pl.ds(c * L, L)
        return pl.ds(pl.multiple_of(c * L, L), L)

    def table(c):
        if isinstance(c, int):
            return chunk(nlc + c if c < ncc else c - ncc)
        return chunk(jnp.where(c < ncc, nlc + c, c - ncc))

    def head(x, h, width):
        return x[:, h * width:(h + 1) * width]

    @pl.when(s == 0)
    def _():
        def stage(c, k, v):
            kr_ref[chunk(c), :] = _ret_rope(k, cos_ref[table(c), :], sin_ref[table(c), :]) * (RET_QK_DIM ** -0.5)
            vs_ref[chunk(c), :] = v.astype(vs_ref.dtype)

        for c in range(ncc):
            stage(c, kc_ref[c * L:(c + 1) * L, :], vc_ref[c * L:(c + 1) * L, :])

        def stage_latent(j, carry):
            stage(ncc + j, kl_ref[chunk(j), :], vl_ref[chunk(j), :])
            return carry

        lax.fori_loop(0, nlc, stage_latent, 0)

        diff = (lax.broadcasted_iota(jnp.int32, (L, L), 0)
                - lax.broadcasted_iota(jnp.int32, (L, L), 1)).astype(F32)
        for h in range(RET_HEADS):
            decay_ref[h] = (jnp.where(diff >= 0, jnp.exp(jnp.maximum(diff, 0.0) * log_g(0, h)), 0.0)
                            + jnp.where(diff <= 0, jnp.exp(jnp.maximum(-diff, 0.0) * log_g(1, h)), 0.0))

        sf_ref[...] = jnp.zeros_like(sf_ref)
        sb_ref[...] = jnp.zeros_like(sb_ref)

        def visit(c, carry):
            k = kr_ref[chunk(c), :]
            v = vs_ref[chunk(c), :]
            for h in range(RET_HEADS):
                lgb = log_g(1, h)
                st = sb_ref[h]
                sball_ref[c, h] = st
                kz = head(k, h, RET_QK_DIM) * jnp.exp(pos * lgb)
                sb_ref[h] = jnp.exp(L * lgb) * st + _dot_tn(kz, head(v, h, RET_V_DIM))
            return carry

        for c in reversed(range(ncc)):
            visit(c, 0)
        lax.fori_loop(0, nlc, lambda j, carry: visit(nc - 1 - j, carry), 0)

    q = _ret_rope(q_ref[...], cos_ref[table(s), :], sin_ref[table(s), :])
    k = kr_ref[chunk(s), :]
    v = vs_ref[chunk(s), :]
    g = g_ref[...]
    for h in range(RET_HEADS):
        lgf = log_g(0, h)
        lgb = log_g(1, h)
        qh = head(q, h, RET_QK_DIM)
        kh = head(k, h, RET_QK_DIM)
        vh = head(v, h, RET_V_DIM)
        st_f = sf_ref[h]
        st_b = sball_ref[s, h]
        o = _dot(_dot_nt(qh, kh) * decay_ref[h], vh)
        o = o + _dot(qh, st_f) * jnp.exp((pos + 1.0) * lgf)
        o = o + _dot(qh, st_b) * jnp.exp((L - pos) * lgb)
        sf_ref[h] = jnp.exp(L * lgf) * st_f + _dot_tn(kh * jnp.exp((L - 1.0 - pos) * lgf), vh)
        mu = jnp.mean(o, axis=-1, keepdims=True)
        var = jnp.mean(jnp.square(o - mu), axis=-1, keepdims=True)
        o = (o - mu) * lax.rsqrt(var + GN_EPS) * head(gn_ref[...], h, RET_V_DIM)
        o_ref[:, h * RET_V_DIM:(h + 1) * RET_V_DIM] = (_silu(head(g, h, RET_V_DIM)) * o).astype(o_ref.dtype)


def retention(dm, p, dec, cos_t, sin_t, gn):
    L = RET_L
    ncc, nlc = dm.ctx // L, dm.s // L
    nc = ncc + nlc
    lat_blocks = dm.n_lat // L
    ctx_blk0 = dm.n_lat // dm.ctx
    qk_w = RET_HEADS * RET_QK_DIM
    v_w = RET_HEADS * RET_V_DIM

    def rows(b, s):
        return jnp.where(s < ncc, lat_blocks + b * ncc + s, b * nlc + s - ncc)

    whole = lambda shape: pl.BlockSpec(shape, lambda b, s, d: (0, 0))
    grid_spec = pltpu.PrefetchScalarGridSpec(
        num_scalar_prefetch=1,
        grid=(dm.b, nc),
        in_specs=[pl.BlockSpec((L, qk_w), lambda b, s, d: (rows(b, s), 0)),
                  pl.BlockSpec((L, v_w), lambda b, s, d: (rows(b, s), 2)),
                  pl.BlockSpec((dm.s, qk_w), lambda b, s, d: (b, 1)),
                  pl.BlockSpec((dm.ctx, qk_w), lambda b, s, d: (ctx_blk0 + b, 1)),
                  pl.BlockSpec((dm.s, v_w), lambda b, s, d: (b, 1)),
                  pl.BlockSpec((dm.ctx, v_w), lambda b, s, d: (ctx_blk0 + b, 1)),
                  whole(cos_t.shape), whole(sin_t.shape), whole((1, v_w))],
        out_specs=pl.BlockSpec((L, v_w), lambda b, s, d: (rows(b, s), 0)),
        scratch_shapes=[pltpu.VMEM((nc * L, qk_w), F32),
                        pltpu.VMEM((nc * L, v_w), BF16),
                        pltpu.VMEM((RET_HEADS, RET_QK_DIM, RET_V_DIM), F32),
                        pltpu.VMEM((RET_HEADS, RET_QK_DIM, RET_V_DIM), F32),
                        pltpu.VMEM((nc, RET_HEADS, RET_QK_DIM, RET_V_DIM), F32),
                        pltpu.VMEM((RET_HEADS, L, L), F32)],
    )
    return pl.pallas_call(
        functools.partial(_ret_kernel, ncc=ncc, nlc=nlc),
        grid_spec=grid_spec,
        out_shape=jax.ShapeDtypeStruct((dm.n_all, v_w), BF16),
        compiler_params=_cparams(("parallel", "arbitrary")),
        name="retention",
    )(dec, p, p, p, p, p, p, cos_t, sin_t, gn.reshape(1, v_w))


def _att_prep_kernel(q_ref, k_ref, v_ref, cos_ref, sin_ref, gq_ref, gk_ref, qo_ref, ko_ref, vo_ref,
                     *, lat_tiles):
    is_lat = pl.program_id(0) < lat_tiles
    cos = jnp.where(is_lat, cos_ref[...], 1.0)
    sin = jnp.where(is_lat, sin_ref[...], 0.0)

    hd = ATT_HEAD_DIM

    def prep(x_ref, gain_ref, h):
        xh = x_ref[:, h * hd:(h + 1) * hd]
        xn = xh * lax.rsqrt(jnp.mean(xh * xh, axis=-1, keepdims=True) + NORM_EPS) * gain_ref[...]
        return _rope_pairs(xn, cos, sin)

    for h in range(ATT_HEADS):
        qo_ref[:, h * hd:(h + 1) * hd] = (prep(q_ref, gq_ref, h) * ATT_SCORE_SCALE).astype(qo_ref.dtype)
    ones = jnp.ones((ATT_V_ROWS - hd, vo_ref.shape[1]), vo_ref.dtype)
    for h in range(ATT_KV_HEADS):
        ko_ref[:, h * hd:(h + 1) * hd] = prep(k_ref, gk_ref, h).astype(ko_ref.dtype)
        vo_ref[h * ATT_V_ROWS:h * ATT_V_ROWS + hd, :] = v_ref[:, h * hd:(h + 1) * hd].T.astype(vo_ref.dtype)
        vo_ref[h * ATT_V_ROWS + hd:(h + 1) * ATT_V_ROWS, :] = ones


def att_prep(dm, p, cos_t, sin_t, gq, gk):
    tm = dm.tile(TM_PREP)
    n = dm.n_all
    lat_tiles, per_batch = dm.n_lat // tm, dm.s // tm
    qw = ATT_HEADS * ATT_HEAD_DIM
    kw = ATT_KV_HEADS * ATT_HEAD_DIM
    q_blk, k_blk, v_blk = ATT_Q_COL // qw, ATT_K_COL // kw, ATT_V_COL // kw
    tab = lambda i: (jnp.where(i < lat_tiles, i % per_batch, 0), 0)
    return pl.pallas_call(
        functools.partial(_att_prep_kernel, lat_tiles=lat_tiles),
        grid=(n // tm,),
        in_specs=[pl.BlockSpec((tm, qw), lambda i: (i, q_blk)),
                  pl.BlockSpec((tm, kw), lambda i: (i, k_blk)),
                  pl.BlockSpec((tm, kw), lambda i: (i, v_blk)),
                  pl.BlockSpec((tm, ATT_HEAD_DIM), tab),
                  pl.BlockSpec((tm, ATT_HEAD_DIM), tab),
                  pl.BlockSpec((1, ATT_HEAD_DIM), lambda i: (0, 0)),
                  pl.BlockSpec((1, ATT_HEAD_DIM), lambda i: (0, 0))],
        out_specs=[pl.BlockSpec((tm, qw), lambda i: (i, 0)),
                   pl.BlockSpec((tm, kw), lambda i: (i, 0)),
                   pl.BlockSpec((ATT_KV_HEADS * ATT_V_ROWS, tm), lambda i: (0, i))],
        out_shape=[jax.ShapeDtypeStruct((n, qw), BF16),
                   jax.ShapeDtypeStruct((n, kw), BF16),
                   jax.ShapeDtypeStruct((ATT_KV_HEADS * ATT_V_ROWS, n), BF16)],
        compiler_params=_cparams(("parallel",)),
        name="att_prep",
    )(p, p, p, cos_t, sin_t, gq.reshape(1, -1), gk.reshape(1, -1))


def _attn_kernel(*refs, with_latent_keys):
    ns = ATT_STAGES
    if with_latent_keys:
        qt_ref, kl_ref, vtl_ref, kc_ref, vtc_ref, o_ref, acc_ref = refs[:7]
    else:
        qt_ref, kc_ref, vtc_ref, o_ref, acc_ref = refs[:5]
    s_refs, p_refs = refs[-2 * ns:-ns], refs[-ns:]
    hd = ATT_HEAD_DIM
    grp = ATT_HEADS // ATT_KV_HEADS
    tq = qt_ref.shape[0]
    nq = grp * tq
    tk = kc_ref.shape[0]
    n = 1 + (kl_ref.shape[0] // tk if with_latent_keys else 0)
    q2 = jnp.concatenate([qt_ref[:, j * hd:(j + 1) * hd] for j in range(grp)], axis=0)

    def key_chunk(j):
        if isinstance(j, int) and j == 0:
            return kc_ref[...]
        return kl_ref[pl.ds(pl.multiple_of((j - 1) * tk, tk), tk), :]

    def value_chunk(j):
        if isinstance(j, int) and j == 0:
            return vtc_ref[...]
        return vtl_ref[:, pl.ds(pl.multiple_of((j - 1) * tk, tk), tk)]

    def max_keys(x):
        groups = math.gcd(tk // SUBLANES, ATT_REDUCE_GROUPS)
        part = jnp.max(x.reshape(groups, tk // groups, nq), axis=0)
        return jnp.max(part, axis=0, keepdims=True)

    def time_step(t, slot, carry, do_scores=True, do_softmax=True, do_values=True):
        m, alpha = carry
        if do_scores:
            s_refs[slot][...] = _dot_nt(key_chunk(t), q2)
        if do_values:
            pv = jnp.dot(value_chunk(t - 2), p_refs[(slot - 2) % ns][...], preferred_element_type=F32)
            acc_ref[...] = acc_ref[...] * alpha + pv
        if do_softmax:
            st = s_refs[(slot - 1) % ns][...]
            m_new = jnp.maximum(m, max_keys(st))
            p_refs[(slot - 1) % ns][...] = jnp.exp2(st - m_new).astype(BF16)
            alpha = jnp.exp2(m - m_new)
            m = m_new
        return m, alpha

    acc_ref[...] = jnp.zeros_like(acc_ref)
    carry = (jnp.full((1, nq), -1e30, F32), jnp.zeros((1, nq), F32))
    fill = min(3, n + 2)
    for t in range(fill):
        carry = time_step(t, t % ns, carry, t < n, 0 <= t - 1 < n, 0 <= t - 2 < n)
    steady = max(n - fill, 0)
    body_steps = ns * ATT_UNROLL

    def unrolled(i, cr):
        for k in range(body_steps):
            cr = time_step(fill + body_steps * i + k, (fill + k) % ns, cr)
        return cr

    if steady // body_steps:
        carry = lax.fori_loop(0, steady // body_steps, unrolled, carry)
    for t in range(fill + body_steps * (steady // body_steps), max(n, fill)):
        carry = time_step(t, t % ns, carry)
    for t in range(max(n, fill), n + 2):
        carry = time_step(t, t % ns, carry, False, t - 1 < n, True)
    out = (acc_ref[0:hd, :] / acc_ref[hd:hd + 1, :]).T
    for j in range(grp):
        o_ref[:, j * hd:(j + 1) * hd] = out[j * tq:(j + 1) * tq].astype(o_ref.dtype)


def attention(dm, qt, kn, vt, latent):
    hd = ATT_HEAD_DIM
    grp = ATT_HEADS // ATT_KV_HEADS
    tk = dm.ctx
    assert dm.s % tk == 0
    ctx_blk0 = dm.n_lat // dm.ctx
    kc_spec = pl.BlockSpec((tk, hd), lambda b, h, i: (ctx_blk0 + b, h))
    vtc_spec = pl.BlockSpec((ATT_V_ROWS, tk), lambda b, h, i: (h, ctx_blk0 + b))
    if latent:
        tq = TQ_ATT
        nq = dm.s // tq
        q_spec = pl.BlockSpec((tq, grp * hd), lambda b, h, i: (b * nq + i, h))
        kl_spec = pl.BlockSpec((dm.s, hd), lambda b, h, i: (b, h))
        vtl_spec = pl.BlockSpec((ATT_V_ROWS, dm.s), lambda b, h, i: (h, b))
        in_specs = [q_spec, kl_spec, vtl_spec, kc_spec, vtc_spec]
        args = (qt, kn, vt, kn, vt)
        rows = dm.n_lat
    else:
        tq, nq = dm.ctx, 1
        q_spec = pl.BlockSpec((tq, grp * hd), lambda b, h, i: (ctx_blk0 + b, h))
        in_specs = [q_spec, kc_spec, vtc_spec]
        args = (qt, kn, vt)
        rows = dm.b * dm.ctx
    return pl.pallas_call(
        functools.partial(_attn_kernel, with_latent_keys=latent),
        grid=(dm.b, ATT_KV_HEADS, nq),
        in_specs=in_specs,
        out_specs=pl.BlockSpec((tq, grp * hd), lambda b, h, i: (b * nq + i, h)),
        out_shape=jax.ShapeDtypeStruct((rows, ATT_HEADS * hd), BF16),
        scratch_shapes=([pltpu.VMEM((ATT_V_ROWS, grp * tq), F32)]
                        + [pltpu.VMEM((tk, grp * tq), F32)] * ATT_STAGES
                        + [pltpu.VMEM((tk, grp * tq), BF16)] * ATT_STAGES),
        compiler_params=_cparams(("parallel", "parallel", "arbitrary")),
        name="attention_latent" if latent else "attention_ctx",
    )(*args)


def _pool_kernel(prev_ref, u_ref, next_ref, w_ref, sc_ref, o_ref, a_ref, b_ref, *, dm):
    tm = u_ref.shape[0]
    H = POOL_HALO
    i = pl.program_id(0)
    lat_tiles = dm.n_lat // tm
    is_lat = i < lat_tiles
    n_seg = jnp.where(is_lat, dm.s, dm.ctx)
    base = jnp.where(is_lat, (i * tm) % dm.s, 0)
    r = lax.broadcasted_iota(jnp.int32, (tm + 2 * H, 1), 0) - H
    pos = jnp.where(is_lat, base + r, r % dm.ctx)
    win = pl.ds(H, tm + 2 * H)

    def shifted(ref, d):
        return ref[pl.ds(H + d, tm + 2 * H), :]

    zeros = jnp.zeros((H, POOL_GROUP_DIM), F32)
    for ref in (a_ref, b_ref):
        ref[pl.ds(0, H), :] = zeros
        ref[pl.ds(tm + 3 * H, H), :] = zeros

    for gi, w in enumerate(POOL_WINDOWS):
        cols = slice(gi * POOL_GROUP_DIM, (gi + 1) * POOL_GROUP_DIM)
        half = w // 2
        a_ref[pl.ds(H, H), :] = prev_ref[:, cols]
        a_ref[pl.ds(2 * H, tm), :] = u_ref[:, cols]
        a_ref[pl.ds(2 * H + tm, H), :] = next_ref[:, cols]
        b_ref[win, :] = jnp.where(pos >= 1, shifted(a_ref, -1), 0.0)
        step = 1
        while step < half:
            b_ref[win, :] = b_ref[win, :] + jnp.where(pos >= step, shifted(b_ref, -step), 0.0)
            step *= 2
        step = 1
        while step < half:
            a_ref[win, :] = a_ref[win, :] + jnp.where(pos + step < n_seg, shifted(a_ref, step), 0.0)
            step *= 2
        tile = pl.ds(2 * H, tm)
        pt = pos[H:H + tm]
        cnt = (jnp.minimum(pt + half, n_seg) - jnp.maximum(pt - half, 0)).astype(F32)
        u = u_ref[:, cols]
        diff = (a_ref[tile, :] + b_ref[tile, :]) / cnt - u
        o_ref[:, cols] = (_dot(diff, w_ref[gi]) * sc_ref[:, cols]).astype(o_ref.dtype)


def pool_mixer(dm, p, pool_w, pool_scale):
    tm = dm.tile(TM_POOL)
    n = dm.n_all
    H = POOL_HALO
    width = len(POOL_WINDOWS) * POOL_GROUP_DIM
    col = 2560 // width
    per = tm // H
    last = n // H - 1
    return pl.pallas_call(
        functools.partial(_pool_kernel, dm=dm),
        grid=(n // tm,),
        in_specs=[pl.BlockSpec((H, width), lambda i: (jnp.maximum(i * per - 1, 0), col)),
                  pl.BlockSpec((tm, width), lambda i: (i, col)),
                  pl.BlockSpec((H, width), lambda i: (jnp.minimum((i + 1) * per, last), col)),
                  pl.BlockSpec(pool_w.shape, lambda i: (0, 0, 0)),
                  pl.BlockSpec((1, width), lambda i: (0, 0))],
        out_specs=pl.BlockSpec((tm, width), lambda i: (i, 0)),
        out_shape=jax.ShapeDtypeStruct((n, width), BF16),
        scratch_shapes=[pltpu.VMEM((tm + 4 * H, POOL_GROUP_DIM), F32),
                        pltpu.VMEM((tm + 4 * H, POOL_GROUP_DIM), F32)],
        compiler_params=_cparams(("parallel",)),
        name="pool_mixer",
    )(p, p, p, pool_w, pool_scale.reshape(1, width))


def _merge_kernel(yr_ref, ya_ref, yp_ref, gl_ref, *refs, lat_tiles):
    mod_ref, wb_ref, wo_ref, o_ref = refs[-4:]
    x_refs = refs[:-4]
    d = o_ref.shape[1]
    mixed = None
    for j, y_ref in enumerate((yr_ref, ya_ref, yp_ref)):
        gate = gl_ref[:, j * d:(j + 1) * d].astype(F32)
        term = gate * jnp.dot(y_ref[...], wb_ref[j], preferred_element_type=F32)
        mixed = term if mixed is None else mixed + term
    y = mod_ref[0][2:3] * _dot(mixed, wo_ref[...])

    def finish(x_ref):
        o_ref[...] = x_ref[...] + y

    _for_row_source(x_refs, lat_tiles, finish)


def merge(dm, n_rows, yr, ya, yp, gates, x_parts, mod, wb, wo):
    tm = dm.tile(TM_MERGE)
    d = dm.d
    bw = yr.shape[1]
    row = dm.mod_row(tm)
    return pl.pallas_call(
        functools.partial(_merge_kernel, lat_tiles=dm.n_lat // tm),
        grid=(n_rows // tm,),
        in_specs=[pl.BlockSpec((tm, bw), lambda i: (i, 0)),
                  pl.BlockSpec((tm, bw), lambda i: (i, 0)),
                  pl.BlockSpec((tm, bw), lambda i: (i, 0)),
                  pl.BlockSpec((tm, 3 * d), lambda i: (i, 0))]
        + _row_sources(dm, tm, x_parts) + [
            pl.BlockSpec((1, 6, d), lambda i: (row(i), 0, 0)),
            pl.BlockSpec(wb.shape, lambda i: (0, 0, 0)),
            pl.BlockSpec(wo.shape, lambda i: (0, 0))],
        out_specs=pl.BlockSpec((tm, d), lambda i: (i, 0)),
        out_shape=jax.ShapeDtypeStruct((n_rows, d), F32),
        compiler_params=_cparams(("parallel",)),
        name="merge",
    )(yr, ya, yp, gates, *x_parts, mod, wb, wo)


def _ffn_kernel(x_ref, mod_ref, gain_ref, w1_ref, w3_ref, w2_ref, o_ref):
    m = mod_ref[0]
    h = _norm_mod(x_ref[...], gain_ref[...], m[3:4], m[4:5]).astype(BF16)
    fdim = w1_ref.shape[1]
    y = None
    for lo in range(0, fdim, MOE_COL_CHUNK):
        cols = slice(lo, min(lo + MOE_COL_CHUNK, fdim))
        a = jnp.dot(h, w1_ref[:, cols], preferred_element_type=F32)
        b = jnp.dot(h, w3_ref[:, cols], preferred_element_type=F32)
        part = _dot(_silu(a) * b, w2_ref[cols, :])
        y = part if y is None else y + part
    o_ref[...] = x_ref[...] + m[5:6] * y


def ffn_dense(dm, x, mod, gain, w1, w3, w2):
    n, d = x.shape
    fdim = w1.shape[1]
    tm = dm.tile(TM_FFN)
    row = dm.mod_row(tm)
    once = dict(pipeline_mode=pl.Buffered(1))
    return pl.pallas_call(
        _ffn_kernel,
        grid=(n // tm,),
        in_specs=[pl.BlockSpec((tm, d), lambda i: (i, 0)),
                  pl.BlockSpec((1, 6, d), lambda i: (row(i), 0, 0)),
                  pl.BlockSpec((1, d), lambda i: (0, 0)),
                  pl.BlockSpec((d, fdim), lambda i: (0, 0), **once),
                  pl.BlockSpec((d, fdim), lambda i: (0, 0), **once),
                  pl.BlockSpec((fdim, d), lambda i: (0, 0), **once)],
        out_specs=pl.BlockSpec((tm, d), lambda i: (i, 0)),
        out_shape=jax.ShapeDtypeStruct((n, d), F32),
        compiler_params=_cparams(("parallel",)),
        name="ffn_dense",
    )(x, mod, gain.reshape(1, d), w1, w3, w2)


def _router_kernel(x_ref, mod_ref, gain_ref, wr_ref, br_ref, h_ref, slot_ref, slot_t_ref, wt_ref, cum_ref):
    tm = x_ref.shape[0]
    m = mod_ref[0]
    h = _norm_mod(x_ref[...], gain_ref[...], m[3:4], m[4:5])
    h_ref[...] = h.astype(BF16)
    logits = _dot3(h, wr_ref[...]) + br_ref[...]
    lane = lax.broadcasted_iota(jnp.int32, logits.shape, 1)
    m1 = jnp.max(logits, axis=-1, keepdims=True)
    i1 = jnp.min(jnp.where(logits == m1, lane, LANES), axis=-1, keepdims=True)
    rest = jnp.where(lane == i1, -jnp.inf, logits)
    m2 = jnp.max(rest, axis=-1, keepdims=True)
    i2 = jnp.min(jnp.where(rest == m2, lane, LANES), axis=-1, keepdims=True)
    e = jnp.exp(m2 - m1)
    w1 = 1.0 / (1.0 + e)
    w2 = e / (1.0 + e)
    wt_ref[...] = jnp.where(lane == i1, w1, jnp.where(lane == i2, w2, 0.0))
    routed = jnp.where((lane == i1) | (lane == i2), 1.0, 0.0)
    sub = SUB_MOE
    lower = (lax.broadcasted_iota(jnp.int32, (sub, sub), 0) > lax.broadcasted_iota(jnp.int32, (sub, sub), 1))
    lower = lower.astype(BF16)
    offset = jnp.zeros((1, LANES), F32)
    cum_ref[...] = jnp.zeros_like(cum_ref)
    for blk in range(tm // sub):
        rows = slice(blk * sub, (blk + 1) * sub)
        mask = routed[rows]
        cum_ref[0, blk:blk + 1, :] = offset.astype(jnp.int32)
        rank = jnp.dot(lower, mask.astype(BF16), preferred_element_type=F32) + offset
        slot_ref[rows, :] = jnp.where(mask > 0.0, rank, -1.0)
        count = jnp.sum(mask, axis=0, keepdims=True)
        offset = offset + jnp.ceil(count * (1.0 / SLOT_ALIGN)) * SLOT_ALIGN
    cum_ref[0, tm // sub:tm // sub + 1, :] = offset.astype(jnp.int32)
    slot_t_ref[...] = slot_ref[...].T[:SUBLANES, :]


def moe_route(dm, n_rows, x, mod, gain, wr_pad, br_pad):
    tm = dm.tile(TM_MOE)
    d = dm.d
    row = dm.mod_row(tm)
    nt = n_rows // tm
    cum_rows = -(-(tm // SUB_MOE + 1) // SUBLANES) * SUBLANES
    return pl.pallas_call(
        _router_kernel,
        grid=(nt,),
        in_specs=[pl.BlockSpec((tm, d), lambda i: (i, 0)),
                  pl.BlockSpec((1, 6, d), lambda i: (row(i), 0, 0)),
                  pl.BlockSpec((1, d), lambda i: (0, 0)),
                  pl.BlockSpec((d, LANES), lambda i: (0, 0)),
                  pl.BlockSpec((1, LANES), lambda i: (0, 0))],
        out_specs=[pl.BlockSpec((tm, d), lambda i: (i, 0)),
                   pl.BlockSpec((tm, LANES), lambda i: (i, 0)),
                   pl.BlockSpec((SUBLANES, tm), lambda i: (0, i)),
                   pl.BlockSpec((tm, LANES), lambda i: (i, 0)),
                   pl.BlockSpec((1, cum_rows, LANES), lambda i: (i, 0, 0))],
        out_shape=[jax.ShapeDtypeStruct((n_rows, d), BF16),
                   jax.ShapeDtypeStruct((n_rows, LANES), F32),
                   jax.ShapeDtypeStruct((SUBLANES, n_rows), F32),
                   jax.ShapeDtypeStruct((n_rows, LANES), F32),
                   jax.ShapeDtypeStruct((nt, cum_rows, LANES), jnp.int32)],
        compiler_params=_cparams(("parallel",)),
        name="moe_router",
    )(x, mod, gain.reshape(1, d), wr_pad, br_pad)


def _moe_kernel(cum_ref, h_ref, slot_ref, slot_t_ref, wt_ref, w1_ref, w3_ref, w2_ref, o_ref,
                xs_ref, acc_ref):
    i, e, f = pl.program_id(0), pl.program_id(1), pl.program_id(2)
    tm, d = h_ref.shape
    rb, win, sub = RB_MOE, WIN_MOE, SUB_MOE
    nsub = tm // sub

    def base(j):
        return cum_ref[i, j * N_EXPERTS + e]

    halves = (base(nsub) + rb // 2 - 1) // (rb // 2)
    nblk, has_tail = halves // 2, halves % 2 == 1

    def block_rows(r, size=rb):
        return pl.ds(pl.multiple_of(r * rb, 32), size)

    def windows(j):
        return base(j), (base(j + 1) - base(j) + win - 1) // win

    def window_rows(start, w):
        return pl.ds(pl.multiple_of(start + w * win, SLOT_ALIGN), win)

    @pl.when((e == 0) & (f == 0))
    def _():
        o_ref[...] = jnp.zeros_like(o_ref)

    @pl.when(f == 0)
    def _():
        def clear(r, carry):
            xs_ref[block_rows(r), :] = jnp.zeros((rb, d), BF16)
            acc_ref[block_rows(r), :] = jnp.zeros((rb, d), F32)
            return carry

        lax.fori_loop(0, nblk + 1, clear, 0)
        acc_ref[pl.ds(pl.multiple_of((nblk + 1) * rb, 32), win), :] = jnp.zeros((win, d), F32)

        for j in range(nsub):
            tok = slice(j * sub, (j + 1) * sub)
            start, nwin = windows(j)

            def gather(w, carry, tok=tok, start=start):
                rows = window_rows(start, w)
                want = (lax.broadcasted_iota(jnp.int32, (win, sub), 0) + (start + w * win)).astype(F32)
                onehot = (slot_t_ref[pl.ds(e, 1), tok] == want).astype(BF16)
                xs_ref[rows, :] = jnp.dot(onehot, h_ref[tok, :], preferred_element_type=F32).astype(BF16)
                return carry

            lax.fori_loop(0, nwin, gather, 0)

    def expert(r, carry, size=rb):
        rows = block_rows(r, size)
        xs = xs_ref[rows, :]
        tf = w1_ref.shape[2]
        y = None
        for lo in range(0, tf, MOE_COL_CHUNK):
            cols = slice(lo, min(lo + MOE_COL_CHUNK, tf))
            a = jnp.dot(xs, w1_ref[0, :, cols], preferred_element_type=F32)
            b = jnp.dot(xs, w3_ref[0, :, cols], preferred_element_type=F32)
            part = _dot(_silu(a) * b, w2_ref[0, cols, :])
            y = part if y is None else y + part
        acc_ref[rows, :] += y
        return carry

    lax.fori_loop(0, nblk, expert, 0)

    @pl.when(has_tail)
    def _():
        expert(nblk, 0, rb // 2)

    @pl.when(f == pl.num_programs(2) - 1)
    def _():
        pick = lax.broadcasted_iota(jnp.int32, (sub, LANES), 1) == e
        for j in range(nsub):
            tok = slice(j * sub, (j + 1) * sub)
            start, nwin = windows(j)
            slot_col = jnp.sum(jnp.where(pick, slot_ref[tok, :], 0.0), axis=-1, keepdims=True)
            wt_col = jnp.sum(jnp.where(pick, wt_ref[tok, :], 0.0), axis=-1, keepdims=True)

            def scatter(w, carry, tok=tok, start=start, slot_col=slot_col, wt_col=wt_col):
                ys = acc_ref[window_rows(start, w), :].astype(BF16)
                want = (lax.broadcasted_iota(jnp.int32, (sub, win), 1) + (start + w * win)).astype(F32)
                onehot = (slot_col == want).astype(BF16)
                o_ref[tok, :] += wt_col * jnp.dot(onehot, ys, preferred_element_type=F32)
                return carry

            lax.fori_loop(0, nwin, scatter, 0)


def moe_experts(dm, n_rows, h, slot, slot_t, wt, counts, w1, w3, w2):
    tm = dm.tile(TM_MOE)
    d = dm.d
    ne, _, fdim = w1.shape
    tf = TF_MOE if fdim % TF_MOE == 0 else TF_FFN
    assert SUB_MOE % SLOT_ALIGN == 0 and tm % SUB_MOE == 0
    cap = tm + RB_MOE + WIN_MOE
    once = dict(pipeline_mode=pl.Buffered(1))
    grid_spec = pltpu.PrefetchScalarGridSpec(
        num_scalar_prefetch=1,
        grid=(n_rows // tm, ne, fdim // tf),
        in_specs=[pl.BlockSpec((tm, d), lambda i, e, f, c: (i, 0), **once),
                  pl.BlockSpec((tm, LANES), lambda i, e, f, c: (i, 0), **once),
                  pl.BlockSpec((SUBLANES, tm), lambda i, e, f, c: (0, i), **once),
                  pl.BlockSpec((tm, LANES), lambda i, e, f, c: (i, 0), **once),
                  pl.BlockSpec((1, d, tf), lambda i, e, f, c: (e, 0, f)),
                  pl.BlockSpec((1, d, tf), lambda i, e, f, c: (e, 0, f)),
                  pl.BlockSpec((1, tf, d), lambda i, e, f, c: (e, f, 0))],
        out_specs=pl.BlockSpec((tm, d), lambda i, e, f, c: (i, 0), **once),
        scratch_shapes=[pltpu.VMEM((cap, d), BF16), pltpu.VMEM((cap, d), F32)],
    )
    return pl.pallas_call(
        _moe_kernel,
        grid_spec=grid_spec,
        out_shape=jax.ShapeDtypeStruct((n_rows, d), F32),
        compiler_params=_cparams(("parallel", "arbitrary", "arbitrary")),
        name="moe_experts",
    )(counts, h, slot, slot_t, wt, w1, w3, w2)


def _moe_windows(base_ref, row0_ref, e):
    i, j = pl.program_id(0), pl.program_id(1)
    start = base_ref[i, j * N_EXPERTS + e]
    nwin = (base_ref[i, (j + 1) * N_EXPERTS + e] - start + WIN_MOE - 1) // WIN_MOE
    return start, nwin, row0_ref[i, e]


def _moe_gather_kernel(base_ref, row0_ref, h_ref, slot_t_ref, xs_in_ref, xs_ref, buf_ref, sem):
    del xs_in_ref
    win, sub = WIN_MOE, SUB_MOE
    spare = N_EXPERTS

    def gathered(e, start, w):
        want = (lax.broadcasted_iota(jnp.int32, (win, sub), 0) + (start + w * win)).astype(F32)
        onehot = (slot_t_ref[e:e + 1, :] == want).astype(BF16)
        return jnp.dot(onehot, h_ref[...], preferred_element_type=F32).astype(BF16)

    def copy(e, buf, w):
        start, _, row0 = _moe_windows(base_ref, row0_ref, e)
        rows = pl.ds(pl.multiple_of(row0 + start + w * win, SLOT_ALIGN), win)
        return pltpu.make_async_copy(buf_ref.at[buf], xs_ref.at[rows, :], sem.at[buf])

    for e in range(N_EXPERTS):
        start, nwin, _ = _moe_windows(base_ref, row0_ref, e)

        @pl.when(nwin > 0)
        def _(e=e, start=start):
            buf_ref[e] = gathered(e, start, 0)
            copy(e, e, 0).start()

        def extra(w, carry, e=e, start=start):
            buf_ref[spare] = gathered(e, start, w)
            cp = copy(e, spare, w)
            cp.start()
            cp.wait()
            return carry

        lax.fori_loop(1, jnp.maximum(nwin, 1), extra, 0)

    for e in range(N_EXPERTS):
        _, nwin, _ = _moe_windows(base_ref, row0_ref, e)
        pl.when(nwin > 0)(lambda e=e: copy(e, e, 0).wait())


def moe_gather(dm, n_rows, h, slot_t, base, row0, cap):
    tm, sub, d = dm.tile(TM_MOE), SUB_MOE, dm.d
    nsub = tm // sub
    grid_spec = pltpu.PrefetchScalarGridSpec(
        num_scalar_prefetch=2,
        grid=(n_rows // tm, nsub),
        in_specs=[pl.BlockSpec((sub, d), lambda i, j, b, r: (i * nsub + j, 0)),
                  pl.BlockSpec((SUBLANES, sub), lambda i, j, b, r: (0, i * nsub + j)),
                  pl.BlockSpec(memory_space=pl.ANY)],
        out_specs=pl.BlockSpec(memory_space=pl.ANY),
        scratch_shapes=[pltpu.VMEM((N_EXPERTS + 1, WIN_MOE, d), BF16),
                        pltpu.SemaphoreType.DMA((N_EXPERTS + 1,))],
    )
    return pl.pallas_call(
        _moe_gather_kernel,
        grid_spec=grid_spec,
        out_shape=jax.ShapeDtypeStruct((cap, d), BF16),
        input_output_aliases={4: 0},
        compiler_params=_cparams(("arbitrary", "arbitrary")),
        name="moe_gather",
    )(base, row0, h, slot_t, jnp.zeros((cap, d), BF16))


def _moe_ffn_kernel(bexp_ref, bvalid_ref, xs_ref, w1_ref, w3_ref, w2_ref, o_ref):
    del bexp_ref
    valid = bvalid_ref[pl.program_id(0)] == 1

    @pl.when(valid)
    def _():
        xs = xs_ref[...]
        fdim = w1_ref.shape[2]
        y = None
        for lo in range(0, fdim, MOE_COL_CHUNK):
            cols = slice(lo, min(lo + MOE_COL_CHUNK, fdim))
            a = jnp.dot(xs, w1_ref[0, :, cols], preferred_element_type=F32)
            b = jnp.dot(xs, w3_ref[0, :, cols], preferred_element_type=F32)
            part = _dot(_silu(a) * b, w2_ref[0, cols, :])
            y = part if y is None else y + part
        o_ref[...] = y.astype(o_ref.dtype)

    @pl.when(jnp.logical_not(valid))
    def _():
        o_ref[...] = jnp.zeros_like(o_ref)


def moe_ffn(xs, blk_expert, blk_valid, w1, w3, w2):
    cap, d = xs.shape
    _, _, fdim = w1.shape
    blk = BLK_MOE
    once = dict(pipeline_mode=pl.Buffered(1))
    grid_spec = pltpu.PrefetchScalarGridSpec(
        num_scalar_prefetch=2,
        grid=(cap // blk,),
        in_specs=[pl.BlockSpec((blk, d), lambda k, be, bv: (k, 0)),
                  pl.BlockSpec((1, d, fdim), lambda k, be, bv: (be[k], 0, 0), **once),
                  pl.BlockSpec((1, d, fdim), lambda k, be, bv: (be[k], 0, 0), **once),
                  pl.BlockSpec((1, fdim, d), lambda k, be, bv: (be[k], 0, 0), **once)],
        out_specs=pl.BlockSpec((blk, d), lambda k, be, bv: (k, 0)),
    )
    return pl.pallas_call(
        _moe_ffn_kernel,
        grid_spec=grid_spec,
        out_shape=jax.ShapeDtypeStruct((cap, d), BF16),
        compiler_params=_cparams(("arbitrary",)),
        name="moe_ffn",
    )(blk_expert, blk_valid, xs, w1, w3, w2)


def _moe_scatter_kernel(base_ref, row0_ref, slot_ref, wt_ref, x_ref, mod_ref, gain_ref, ys_ref, o_ref,
                        buf_ref, y_ref, sem, *, final):
    win, sub = WIN_MOE, SUB_MOE
    spare = N_EXPERTS

    def copy(e, buf, w):
        start, _, row0 = _moe_windows(base_ref, row0_ref, e)
        rows = pl.ds(pl.multiple_of(row0 + start + w * win, SLOT_ALIGN), win)
        return pltpu.make_async_copy(ys_ref.at[rows, :], buf_ref.at[buf], sem.at[buf])

    for e in range(N_EXPERTS):
        _, nwin, _ = _moe_windows(base_ref, row0_ref, e)
        pl.when(nwin > 0)(lambda e=e: copy(e, e, 0).start())

    y_ref[...] = jnp.zeros_like(y_ref)
    for e in range(N_EXPERTS):
        start, nwin, _ = _moe_windows(base_ref, row0_ref, e)
        slot_col = slot_ref[:, e:e + 1]
        wt_col = wt_ref[:, e:e + 1]

        def scatter(buf, w, start=start, slot_col=slot_col, wt_col=wt_col):
            want = (lax.broadcasted_iota(jnp.int32, (sub, win), 1) + (start + w * win)).astype(F32)
            onehot = (slot_col == want).astype(BF16)
            y_ref[...] += wt_col * jnp.dot(onehot, buf_ref[buf], preferred_element_type=F32)

        @pl.when(nwin > 0)
        def _(e=e, scatter=scatter):
            copy(e, e, 0).wait()
            scatter(e, 0)

        def extra(w, carry, e=e, scatter=scatter):
            cp = copy(e, spare, w)
            cp.start()
            cp.wait()
            scatter(spare, w)
            return carry

        lax.fori_loop(1, jnp.maximum(nwin, 1), extra, 0)

    x = x_ref[...] + mod_ref[0][5:6] * y_ref[...]
    if final:
        x = x * lax.rsqrt(jnp.mean(x * x, axis=-1, keepdims=True) + NORM_EPS) * gain_ref[...]
    o_ref[...] = x


def moe_scatter(dm, n_rows, ys, slot, wt, base, row0, x, mod, final_gain):
    tm, sub, d = dm.tile(TM_MOE), SUB_MOE, dm.d
    nsub = tm // sub
    row = dm.mod_row(sub)
    final = final_gain is not None
    gain = final_gain if final else jnp.ones((d,), F32)
    tok = lambda i, j, b, r: (i * nsub + j, 0)
    grid_spec = pltpu.PrefetchScalarGridSpec(
        num_scalar_prefetch=2,
        grid=(n_rows // tm, nsub),
        in_specs=[pl.BlockSpec((sub, LANES), tok),
                  pl.BlockSpec((sub, LANES), tok),
                  pl.BlockSpec((sub, d), tok),
                  pl.BlockSpec((1, 6, d), lambda i, j, b, r: (row(i * nsub + j), 0, 0)),
                  pl.BlockSpec((1, d), lambda i, j, b, r: (0, 0)),
                  pl.BlockSpec(memory_space=pl.ANY)],
        out_specs=pl.BlockSpec((sub, d), tok),
        scratch_shapes=[pltpu.VMEM((N_EXPERTS + 1, WIN_MOE, d), BF16),
                        pltpu.VMEM((sub, d), F32),
                        pltpu.SemaphoreType.DMA((N_EXPERTS + 1,))],
    )
    return pl.pallas_call(
        functools.partial(_moe_scatter_kernel, final=final),
        grid_spec=grid_spec,
        out_shape=jax.ShapeDtypeStruct((n_rows, d), F32),
        compiler_params=_cparams(("arbitrary", "arbitrary")),
        name="moe_scatter",
    )(base, row0, slot, wt, x, mod, gain.reshape(1, d), ys)


def moe_layout(dm, n_rows, cum):
    tm = dm.tile(TM_MOE)
    nt, nsub = n_rows // tm, tm // SUB_MOE
    base = cum[:, :nsub + 1, :N_EXPERTS]
    total = base[:, nsub, :]
    per_expert = jnp.sum(total, axis=0)
    padded = (per_expert + WIN_MOE + BLK_MOE - 1) // BLK_MOE * BLK_MOE
    expert_row0 = jnp.cumsum(padded) - padded
    row0 = expert_row0[None, :] + jnp.cumsum(total, axis=0) - total
    slack = nt * nsub * N_EXPERTS * (SLOT_ALIGN - 1) + N_EXPERTS * (WIN_MOE + BLK_MOE)
    nblk = -(-(2 * n_rows + slack) // BLK_MOE)
    ends = jnp.cumsum(padded // BLK_MOE)
    k = jnp.arange(nblk, dtype=jnp.int32)
    blk_expert = jnp.minimum(jnp.searchsorted(ends, k, side="right"), N_EXPERTS - 1).astype(jnp.int32)
    blk_valid = (k < ends[-1]).astype(jnp.int32)
    return base.reshape(nt, (nsub + 1) * N_EXPERTS), row0.astype(jnp.int32), blk_expert, blk_valid, nblk * BLK_MOE


def _residual_kernel(x_ref, y_ref, mod_ref, o_ref):
    o_ref[...] = x_ref[...] + mod_ref[0][5:6] * y_ref[...]


def _final_kernel(x_ref, y_ref, mod_ref, gain_ref, o_ref):
    x = x_ref[...] + mod_ref[0][5:6] * y_ref[...]
    o_ref[...] = x * lax.rsqrt(jnp.mean(x * x, axis=-1, keepdims=True) + NORM_EPS) * gain_ref[...]


def residual(dm, n_rows, x, y, mod, final_gain=None):
    tm = dm.tile(TM_FFN)
    d = dm.d
    row = dm.mod_row(tm)
    in_specs = [pl.BlockSpec((tm, d), lambda i: (i, 0)),
                pl.BlockSpec((tm, d), lambda i: (i, 0)),
                pl.BlockSpec((1, 6, d), lambda i: (row(i), 0, 0))]
    args = [x, y, mod]
    if final_gain is not None:
        in_specs.append(pl.BlockSpec((1, d), lambda i: (0, 0)))
        args.append(final_gain.reshape(1, d))
    return pl.pallas_call(
        _residual_kernel if final_gain is None else _final_kernel,
        grid=(n_rows // tm,),
        in_specs=in_specs,
        out_specs=pl.BlockSpec((tm, d), lambda i: (i, 0)),
        out_shape=jax.ShapeDtypeStruct((n_rows, d), F32),
        compiler_params=_cparams(("parallel",)),
        name="residual" if final_gain is None else "residual_final_norm",
    )(*args)


def _final_norm_kernel(x_ref, gain_ref, o_ref):
    x = x_ref[...]
    o_ref[...] = x * lax.rsqrt(jnp.mean(x * x, axis=-1, keepdims=True) + NORM_EPS) * gain_ref[...]


def final_rmsnorm(dm, n_rows, x, gain):
    tm = dm.tile(TM_FFN)
    d = dm.d
    return pl.pallas_call(
        _final_norm_kernel,
        grid=(n_rows // tm,),
        in_specs=[pl.BlockSpec((tm, d), lambda i: (i, 0)), pl.BlockSpec((1, d), lambda i: (0, 0))],
        out_specs=pl.BlockSpec((tm, d), lambda i: (i, 0)),
        out_shape=jax.ShapeDtypeStruct((n_rows, d), F32),
        compiler_params=_cparams(("parallel",)),
        name="final_norm",
    )(x, gain.reshape(1, d))


def _rope_tables(s, ctx):
    def angles(pos, dim):
        freqs = 1.0 / (ROPE_THETA ** (jnp.arange(0, dim, 2, dtype=F32) / dim))
        return pos[:, None] * freqs[None, :]

    rows = s // GRID_W
    row = jnp.repeat(jnp.arange(rows, dtype=F32), GRID_W)
    col = jnp.tile(jnp.arange(GRID_W, dtype=F32), rows)
    half = ATT_HEAD_DIM // 2
    ang_att = jnp.concatenate([angles(row, half), angles(col, half)], axis=-1)
    sign = jnp.tile(jnp.array([-1.0, 1.0], F32), LANES // 2)

    def tables(ang):
        reps = LANES // (2 * ang.shape[1])
        cos = jnp.tile(jnp.repeat(jnp.cos(ang), 2, axis=-1), (1, reps))
        sin = jnp.tile(jnp.repeat(jnp.sin(ang), 2, axis=-1), (1, reps)) * sign
        return cos, sin

    ang_ret = angles(jnp.arange(s, dtype=F32), RET_QK_DIM)
    ang_ret = jnp.concatenate([ang_ret, jnp.zeros((ctx, RET_QK_DIM // 2), F32)], axis=0)
    return tables(ang_att) + tables(ang_ret)


def kernel(x, c, ctx, c_ctx, w_ada, b_ada, norm_mix, norm_ffn, w_in, ret_decay, ret_gn, attn_qn, attn_kn, pool_w, pool_scale, w_branch, w_out, ffn_w1, ffn_w3, ffn_w2, moe_router, moe_router_b, moe_w1, moe_w3, moe_w2, final_norm):
    b, s, d = x.shape
    n_ctx = ctx.shape[1]
    depth = w_in.shape[0]
    dm = Dims(b, s, n_ctx, d)

    x_parts = (x.reshape(b * s, d), ctx.reshape(b * n_ctx, d))
    mod_rows = 2 * SUBLANES
    cc = jnp.concatenate([c, c_ctx[None, :], jnp.zeros((mod_rows - b - 1, d), F32)], axis=0)
    mods = ada_table(cc, w_ada, b_ada).reshape(depth, mod_rows, 6, d)

    att_cos, att_sin, ret_cos, ret_sin = _rope_tables(s, n_ctx)

    for i in range(depth):
        need_ctx = i < depth - 1
        n_rows = dm.n_all if need_ctx else dm.n_lat
        mod = mods[i]
        p, gates = proj_in(dm, x_parts, mod, norm_mix[i], w_in[i].astype(BF16))
        y_ret = retention(dm, p, ret_decay[i], ret_cos, ret_sin, ret_gn[i])
        y_pool = pool_mixer(dm, p, pool_w[i].astype(BF16), pool_scale[i])
        qt, kn, vt = att_prep(dm, p, att_cos, att_sin, attn_qn[i], attn_kn[i])
        y_att = attention(dm, qt, kn, vt, latent=True)
        if need_ctx:
            y_att = jnp.concatenate([y_att, attention(dm, qt, kn, vt, latent=False)], axis=0)
        xa = merge(dm, n_rows, y_ret, y_att, y_pool, gates, x_parts, mod,
                   w_branch[i].astype(BF16), w_out[i].astype(BF16))
        j = i // 2
        last = i == depth - 1
        if i % 2 == 0:
            xa = ffn_dense(dm, xa, mod, norm_ffn[i], ffn_w1[j].astype(BF16), ffn_w3[j].astype(BF16),
                           ffn_w2[j].astype(BF16))
            if last:
                xa = final_rmsnorm(dm, n_rows, xa, final_norm)
        else:
            wr = jnp.zeros((d, LANES), F32).at[:, :N_EXPERTS].set(moe_router[j])
            br = jnp.full((1, LANES), -1e30, F32).at[0, :N_EXPERTS].set(moe_router_b[j])
            h, slot, slot_t, wt, cnt = moe_route(dm, n_rows, xa, mod, norm_ffn[i], wr, br)
            base, row0, blk_expert, blk_valid, cap = moe_layout(dm, n_rows, cnt)
            xs = moe_gather(dm, n_rows, h, slot_t, base, row0, cap)
            ys = moe_ffn(xs, blk_expert, blk_valid,
                         moe_w1[j].astype(BF16), moe_w3[j].astype(BF16), moe_w2[j].astype(BF16))
            xa = moe_scatter(dm, n_rows, ys, slot, wt, base, row0, xa, mod, final_norm if last else None)
        x_parts = (xa,)
    return xa[:dm.n_lat].reshape(b, s, d)
```

```python
import functools
import math

import numpy as np
import jax
import jax.numpy as jnp
from jax import lax
from jax.experimental import pallas as pl
from jax.experimental.pallas import tpu as pltpu

F32 = jnp.float32
BF16 = jnp.bfloat16

GRID_W = 64
RET_HEADS = 4
RET_QK_DIM = 64
RET_V_DIM = 128
ATT_HEADS = 4
ATT_KV_HEADS = 2
ATT_HEAD_DIM = 128
ROPE_THETA = 10000.0
POOL_WINDOWS = (2, 4, 8, 16)
POOL_GROUP_DIM = 128
N_EXPERTS = 8
NORM_EPS = 1e-6
GN_EPS = 1e-5

LANES = 128
SUBLANES = 8
VMEM_LIMIT_BYTES = 56 * 1024 * 1024

TM_PROJ = 1024
TN_PROJ = 1536
TM_PREP = 1024
TM_POOL = 1024
TM_MERGE = 1024
TM_FFN = 1024
TF_FFN = 512
TM_MOE = 2048
TF_MOE = 1792
MOE_COL_CHUNK = 512
RB_MOE = 256
BLK_MOE = 512
WIN_MOE = 256
SUB_MOE = 512
SLOT_ALIGN = 16
assert RB_MOE % 64 == 0
RET_L = 256
TQ_ATT = 512
ATT_Q_COL = 2 * RET_HEADS * RET_QK_DIM + 2 * RET_HEADS * RET_V_DIM
ATT_K_COL = ATT_Q_COL + ATT_HEADS * ATT_HEAD_DIM
ATT_V_COL = ATT_K_COL + ATT_KV_HEADS * ATT_HEAD_DIM
ATT_SCORE_SCALE = (ATT_HEAD_DIM ** -0.5) * math.log2(math.e)
ATT_V_ROWS = ATT_HEAD_DIM + 16
ATT_STAGES = 3
ATT_UNROLL = 4
ATT_REDUCE_GROUPS = 8
POOL_HALO = 8


def _cparams(sem):
    return pltpu.CompilerParams(dimension_semantics=sem, vmem_limit_bytes=VMEM_LIMIT_BYTES)


def _dot(a, b):
    return jnp.dot(a.astype(BF16), b.astype(BF16), preferred_element_type=F32)


def _dot_nt(a, b):
    return lax.dot_general(a.astype(BF16), b.astype(BF16), (((1,), (1,)), ((), ())),
                           preferred_element_type=F32)


def _dot_tn(a, b):
    return lax.dot_general(a.astype(BF16), b.astype(BF16), (((0,), (0,)), ((), ())),
                           preferred_element_type=F32)


def _split_bf16(a):
    hi = a.astype(BF16)
    lo = (a - hi.astype(F32)).astype(BF16)
    return hi, lo


def _dot3(a, b):
    ah, al = _split_bf16(a)
    bh, bl = _split_bf16(b)
    d = functools.partial(jnp.dot, preferred_element_type=F32)
    return d(ah, bh) + (d(al, bh) + d(ah, bl))


def _norm_mod(x, gain, shift, scale):
    y = x * lax.rsqrt(jnp.mean(x * x, axis=-1, keepdims=True) + NORM_EPS)
    return (y * gain) * (1.0 + scale) + shift


def _sigmoid(x):
    return 1.0 / (1.0 + jnp.exp(-x))


def _silu(x):
    return x * _sigmoid(x)


class Dims:
    def __init__(self, b, s, ctx, d):
        self.b, self.s, self.ctx, self.d = b, s, ctx, d
        self.n_lat = b * s
        self.n_all = b * s + b * ctx

    def tile(self, preferred):
        t = preferred
        while self.s % t or (self.b * self.ctx) % t:
            t //= 2
        return t

    def mod_row(self, tm):
        lat_tiles, per_batch, b = self.n_lat // tm, self.s // tm, self.b
        return lambda i: jnp.where(i < lat_tiles, i // per_batch, b)


def _ada_kernel(c_ref, w_ref, b_ref, o_ref):
    o_ref[0] = _dot3(_silu(c_ref[...]), w_ref[0]) + b_ref[0]


def ada_table(cc, w_ada, b_ada):
    depth, d, n6 = w_ada.shape
    tn = n6 // 4
    rows = cc.shape[0]
    return pl.pallas_call(
        _ada_kernel,
        grid=(depth, n6 // tn),
        in_specs=[pl.BlockSpec((rows, d), lambda l, j: (0, 0)),
                  pl.BlockSpec((1, d, tn), lambda l, j: (l, 0, j)),
                  pl.BlockSpec((1, 1, tn), lambda l, j: (l, 0, j))],
        out_specs=pl.BlockSpec((1, rows, tn), lambda l, j: (l, 0, j)),
        out_shape=jax.ShapeDtypeStruct((depth, rows, n6), F32),
        compiler_params=_cparams(("arbitrary", "arbitrary")),
        name="ada_table",
    )(cc, w_ada, b_ada.reshape(depth, 1, n6))


def _row_sources(dm, tm, parts):
    d = dm.d
    if len(parts) == 1:
        return [pl.BlockSpec((tm, d), lambda i, *_: (i, 0))]
    lat_tiles = dm.n_lat // tm
    return [pl.BlockSpec((tm, d), lambda i, *_: (jnp.minimum(i, lat_tiles - 1), 0)),
            pl.BlockSpec((tm, d), lambda i, *_: (jnp.maximum(i - lat_tiles, 0), 0))]


def _for_row_source(x_refs, lat_tiles, fn):
    if len(x_refs) == 1:
        fn(x_refs[0])
        return
    is_lat = pl.program_id(0) < lat_tiles
    pl.when(is_lat)(lambda: fn(x_refs[0]))
    pl.when(jnp.logical_not(is_lat))(lambda: fn(x_refs[1]))


def _proj_in_kernel(*refs, lat_tiles):
    mod_ref, gain_ref, wm_ref, wg_ref, o_ref, g_ref, h_ref = refs[-7:]
    x_refs = refs[:-7]

    @pl.when(pl.program_id(1) == 0)
    def _():
        def fill(x_ref):
            m = mod_ref[0]
            h_ref[...] = _norm_mod(x_ref[...], gain_ref[...], m[0:1], m[1:2]).astype(BF16)

        _for_row_source(x_refs, lat_tiles, fill)

    h = h_ref[...]
    o_ref[...] = jnp.dot(h, wm_ref[...], preferred_element_type=F32)
    logits = jnp.dot(h, wg_ref[...], preferred_element_type=F32)
    g_ref[...] = (0.5 * jnp.tanh(0.5 * logits) + 0.5).astype(g_ref.dtype)


def proj_in(dm, x_parts, mod, gain, w):
    n, d = dm.n_all, dm.d
    nout = w.shape[1]
    gate_w = 3 * d
    tm, tn = dm.tile(TM_PROJ), TN_PROJ
    assert nout - gate_w == gate_w and gate_w % tn == 0
    gate_blk0 = gate_w // tn
    row = dm.mod_row(tm)
    return pl.pallas_call(
        functools.partial(_proj_in_kernel, lat_tiles=dm.n_lat // tm),
        grid=(n // tm, gate_w // tn),
        in_specs=_row_sources(dm, tm, x_parts) + [
            pl.BlockSpec((1, 6, d), lambda i, j: (row(i), 0, 0)),
            pl.BlockSpec((1, d), lambda i, j: (0, 0)),
            pl.BlockSpec((d, tn), lambda i, j: (0, j)),
            pl.BlockSpec((d, tn), lambda i, j: (0, gate_blk0 + j))],
        out_specs=[pl.BlockSpec((tm, tn), lambda i, j: (i, j)),
                   pl.BlockSpec((tm, tn), lambda i, j: (i, j))],
        out_shape=[jax.ShapeDtypeStruct((n, nout - gate_w), F32),
                   jax.ShapeDtypeStruct((n, gate_w), BF16)],
        scratch_shapes=[pltpu.VMEM((tm, d), BF16)],
        compiler_params=_cparams(("parallel", "arbitrary")),
        name="proj_in",
    )(*x_parts, mod, gain.reshape(1, d), w, w)


def _rope_pairs(x, cos, sin):
    lane = lax.broadcasted_iota(jnp.int32, x.shape, 1)
    partner = jnp.where(lane % 2 == 0, pltpu.roll(x, LANES - 1, 1), pltpu.roll(x, 1, 1))
    return x * cos + partner * sin


def _ret_rope(x, cos, sin):
    return jnp.concatenate([_rope_pairs(x[:, j * LANES:(j + 1) * LANES], cos, sin)
                            for j in range(x.shape[1] // LANES)], axis=1)


def _ret_kernel(dec_ref, q_ref, g_ref, kl_ref, kc_ref, vl_ref, vc_ref, cos_ref, sin_ref, gn_ref, o_ref,
                kr_ref, vs_ref, sf_ref, sb_ref, sball_ref, decay_ref, *, ncc, nlc):
    s = pl.program_id(1)
    L = RET_L
    nc = ncc + nlc
    pos = lax.broadcasted_iota(jnp.int32, (L, 1), 0).astype(F32)

    def log_g(direction, h):
        return -jnp.exp(jnp.full((1, 1), dec_ref[direction, h], F32))

    def chunk(c):
        if isinstance(c, int):
            return pl.ds(c * L, L)
        return pl.ds(pl.multiple_of(c * L, L), L)

    def table(c):
        if isinstance(c, int):
            return chunk(nlc + c if c < ncc else c - ncc)
        return chunk(jnp.where(c < ncc, nlc + c, c - ncc))

    def head(x, h, width):
        return x[:, h * width:(h + 1) * width]

    @pl.when(s == 0)
    def _():
        def stage(c, k, v):
            kr_ref[chunk(c), :] = _ret_rope(k, cos_ref[table(c), :], sin_ref[table(c), :]) * (RET_QK_DIM ** -0.5)
            vs_ref[chunk(c), :] = v.astype(vs_ref.dtype)

        for c in range(ncc):
            stage(c, kc_ref[c * L:(c + 1) * L, :], vc_ref[c * L:(c + 1) * L, :])

        def stage_latent(j, carry):
            stage(ncc + j, kl_ref[chunk(j), :], vl_ref[chunk(j), :])
            return carry

        lax.fori_loop(0, nlc, stage_latent, 0)

        diff = (lax.broadcasted_iota(jnp.int32, (L, L), 0)
                - lax.broadcasted_iota(jnp.int32, (L, L), 1)).astype(F32)
        for h in range(RET_HEADS):
            decay_ref[h] = (jnp.where(diff >= 0, jnp.exp(jnp.maximum(diff, 0.0) * log_g(0, h)), 0.0)
                            + jnp.where(diff <= 0, jnp.exp(jnp.maximum(-diff, 0.0) * log_g(1, h)), 0.0))

        sf_ref[...] = jnp.zeros_like(sf_ref)
        sb_ref[...] = jnp.zeros_like(sb_ref)

        def visit(c, carry):
            k = kr_ref[chunk(c), :]
            v = vs_ref[chunk(c), :]
            for h in range(RET_HEADS):
                lgb = log_g(1, h)
                st = sb_ref[h]
                sball_ref[c, h] = st
                kz = head(k, h, RET_QK_DIM) * jnp.exp(pos * lgb)
                sb_ref[h] = jnp.exp(L * lgb) * st + _dot_tn(kz, head(v, h, RET_V_DIM))
            return carry

        for c in reversed(range(ncc)):
            visit(c, 0)
        lax.fori_loop(0, nlc, lambda j, carry: visit(nc - 1 - j, carry), 0)

    q = _ret_rope(q_ref[...], cos_ref[table(s), :], sin_ref[table(s), :])
    k = kr_ref[chunk(s), :]
    v = vs_ref[chunk(s), :]
    g = g_ref[...]
    for h in range(RET_HEADS):
        lgf = log_g(0, h)
        lgb = log_g(1, h)
        qh = head(q, h, RET_QK_DIM)
        kh = head(k, h, RET_QK_DIM)
        vh = head(v, h, RET_V_DIM)
        st_f = sf_ref[h]
        st_b = sball_ref[s, h]
        o = _dot(_dot_nt(qh, kh) * decay_ref[h], vh)
        o = o + _dot(qh, st_f) * jnp.exp((pos + 1.0) * lgf)
        o = o + _dot(qh, st_b) * jnp.exp((L - pos) * lgb)
        sf_ref[h] = jnp.exp(L * lgf) * st_f + _dot_tn(kh * jnp.exp((L - 1.0 - pos) * lgf), vh)
        mu = jnp.mean(o, axis=-1, keepdims=True)
        var = jnp.mean(jnp.square(o - mu), axis=-1, keepdims=True)
        o = (o - mu) * lax.rsqrt(var + GN_EPS) * head(gn_ref[...], h, RET_V_DIM)
        o_ref[:, h * RET_V_DIM:(h + 1) * RET_V_DIM] = (_silu(head(g, h, RET_V_DIM)) * o).astype(o_ref.dtype)


def retention(dm, p, dec, cos_t, sin_t, gn):
    L = RET_L
    ncc, nlc = dm.ctx // L, dm.s // L
    nc = ncc + nlc
    lat_blocks = dm.n_lat // L
    ctx_blk0 = dm.n_lat // dm.ctx
    qk_w = RET_HEADS * RET_QK_DIM
    v_w = RET_HEADS * RET_V_DIM

    def rows(b, s):
        return jnp.where(s < ncc, lat_blocks + b * ncc + s, b * nlc + s - ncc)

    whole = lambda shape: pl.BlockSpec(shape, lambda b, s, d: (0, 0))
    grid_spec = pltpu.PrefetchScalarGridSpec(
        num_scalar_prefetch=1,
        grid=(dm.b, nc),
        in_specs=[pl.BlockSpec((L, qk_w), lambda b, s, d: (rows(b, s), 0)),
                  pl.BlockSpec((L, v_w), lambda b, s, d: (rows(b, s), 2)),
                  pl.BlockSpec((dm.s, qk_w), lambda b, s, d: (b, 1)),
                  pl.BlockSpec((dm.ctx, qk_w), lambda b, s, d: (ctx_blk0 + b, 1)),
                  pl.BlockSpec((dm.s, v_w), lambda b, s, d: (b, 1)),
                  pl.BlockSpec((dm.ctx, v_w), lambda b, s, d: (ctx_blk0 + b, 1)),
                  whole(cos_t.shape), whole(sin_t.shape), whole((1, v_w))],
        out_specs=pl.BlockSpec((L, v_w), lambda b, s, d: (rows(b, s), 0)),
        scratch_shapes=[pltpu.VMEM((nc * L, qk_w), F32),
                        pltpu.VMEM((nc * L, v_w), BF16),
                        pltpu.VMEM((RET_HEADS, RET_QK_DIM, RET_V_DIM), F32),
                        pltpu.VMEM((RET_HEADS, RET_QK_DIM, RET_V_DIM), F32),
                        pltpu.VMEM((nc, RET_HEADS, RET_QK_DIM, RET_V_DIM), F32),
                        pltpu.VMEM((RET_HEADS, L, L), F32)],
    )
    return pl.pallas_call(
        functools.partial(_ret_kernel, ncc=ncc, nlc=nlc),
        grid_spec=grid_spec,
        out_shape=jax.ShapeDtypeStruct((dm.n_all, v_w), BF16),
        compiler_params=_cparams(("parallel", "arbitrary")),
        name="retention",
    )(dec, p, p, p, p, p, p, cos_t, sin_t, gn.reshape(1, v_w))


def _att_prep_kernel(q_ref, k_ref, v_ref, cos_ref, sin_ref, gq_ref, gk_ref, qo_ref, ko_ref, vo_ref,
                     *, lat_tiles):
    is_lat = pl.program_id(0) < lat_tiles
    cos = jnp.where(is_lat, cos_ref[...], 1.0)
    sin = jnp.where(is_lat, sin_ref[...], 0.0)

    hd = ATT_HEAD_DIM

    def prep(x_ref, gain_ref, h):
        xh = x_ref[:, h * hd:(h + 1) * hd]
        xn = xh * lax.rsqrt(jnp.mean(xh * xh, axis=-1, keepdims=True) + NORM_EPS) * gain_ref[...]
        return _rope_pairs(xn, cos, sin)

    for h in range(ATT_HEADS):
        qo_ref[:, h * hd:(h + 1) * hd] = (prep(q_ref, gq_ref, h) * ATT_SCORE_SCALE).astype(qo_ref.dtype)
    ones = jnp.ones((ATT_V_ROWS - hd, vo_ref.shape[1]), vo_ref.dtype)
    for h in range(ATT_KV_HEADS):
        ko_ref[:, h * hd:(h + 1) * hd] = prep(k_ref, gk_ref, h).astype(ko_ref.dtype)
        vo_ref[h * ATT_V_ROWS:h * ATT_V_ROWS + hd, :] = v_ref[:, h * hd:(h + 1) * hd].T.astype(vo_ref.dtype)
        vo_ref[h * ATT_V_ROWS + hd:(h + 1) * ATT_V_ROWS, :] = ones


def att_prep(dm, p, cos_t, sin_t, gq, gk):
    tm = dm.tile(TM_PREP)
    n = dm.n_all
    lat_tiles, per_batch = dm.n_lat // tm, dm.s // tm
    qw = ATT_HEADS * ATT_HEAD_DIM
    kw = ATT_KV_HEADS * ATT_HEAD_DIM
    q_blk, k_blk, v_blk = ATT_Q_COL // qw, ATT_K_COL // kw, ATT_V_COL // kw
    tab = lambda i: (jnp.where(i < lat_tiles, i % per_batch, 0), 0)
    return pl.pallas_call(
        functools.partial(_att_prep_kernel, lat_tiles=lat_tiles),
        grid=(n // tm,),
        in_specs=[pl.BlockSpec((tm, qw), lambda i: (i, q_blk)),
                  pl.BlockSpec((tm, kw), lambda i: (i, k_blk)),
                  pl.BlockSpec((tm, kw), lambda i: (i, v_blk)),
                  pl.BlockSpec((tm, ATT_HEAD_DIM), tab),
                  pl.BlockSpec((tm, ATT_HEAD_DIM), tab),
                  pl.BlockSpec((1, ATT_HEAD_DIM), lambda i: (0, 0)),
                  pl.BlockSpec((1, ATT_HEAD_DIM), lambda i: (0, 0))],
        out_specs=[pl.BlockSpec((tm, qw), lambda i: (i, 0)),
                   pl.BlockSpec((tm, kw), lambda i: (i, 0)),
                   pl.BlockSpec((ATT_KV_HEADS * ATT_V_ROWS, tm), lambda i: (0, i))],
        out_shape=[jax.ShapeDtypeStruct((n, qw), BF16),
                   jax.ShapeDtypeStruct((n, kw), BF16),
                   jax.ShapeDtypeStruct((ATT_KV_HEADS * ATT_V_ROWS, n), BF16)],
        compiler_params=_cparams(("parallel",)),
        name="att_prep",
    )(p, p, p, cos_t, sin_t, gq.reshape(1, -1), gk.reshape(1, -1))


def _attn_kernel(*refs, with_latent_keys):
    ns = ATT_STAGES
    if with_latent_keys:
        qt_ref, kl_ref, vtl_ref, kc_ref, vtc_ref, o_ref, acc_ref = refs[:7]
    else:
        qt_ref, kc_ref, vtc_ref, o_ref, acc_ref = refs[:5]
    s_refs, p_refs = refs[-2 * ns:-ns], refs[-ns:]
    hd = ATT_HEAD_DIM
    grp = ATT_HEADS // ATT_KV_HEADS
    tq = qt_ref.shape[0]
    nq = grp * tq
    tk = kc_ref.shape[0]
    n = 1 + (kl_ref.shape[0] // tk if with_latent_keys else 0)
    q2 = jnp.concatenate([qt_ref[:, j * hd:(j + 1) * hd] for j in range(grp)], axis=0)

    def key_chunk(j):
        if isinstance(j, int) and j == 0:
            return kc_ref[...]
        return kl_ref[pl.ds(pl.multiple_of((j - 1) * tk, tk), tk), :]

    def value_chunk(j):
        if isinstance(j, int) and j == 0:
            return vtc_ref[...]
        return vtl_ref[:, pl.ds(pl.multiple_of((j - 1) * tk, tk), tk)]

    def max_keys(x):
        groups = math.gcd(tk // SUBLANES, ATT_REDUCE_GROUPS)
        part = jnp.max(x.reshape(groups, tk // groups, nq), axis=0)
        return jnp.max(part, axis=0, keepdims=True)

    def time_step(t, slot, carry, do_scores=True, do_softmax=True, do_values=True):
        m, alpha = carry
        if do_scores:
            s_refs[slot][...] = _dot_nt(key_chunk(t), q2)
        if do_values:
            pv = jnp.dot(value_chunk(t - 2), p_refs[(slot - 2) % ns][...], preferred_element_type=F32)
            acc_ref[...] = acc_ref[...] * alpha + pv
        if do_softmax:
            st = s_refs[(slot - 1) % ns][...]
            m_new = jnp.maximum(m, max_keys(st))
            p_refs[(slot - 1) % ns][...] = jnp.exp2(st - m_new).astype(BF16)
            alpha = jnp.exp2(m - m_new)
            m = m_new
        return m, alpha

    acc_ref[...] = jnp.zeros_like(acc_ref)
    carry = (jnp.full((1, nq), -1e30, F32), jnp.zeros((1, nq), F32))
    fill = min(3, n + 2)
    for t in range(fill):
        carry = time_step(t, t % ns, carry, t < n, 0 <= t - 1 < n, 0 <= t - 2 < n)
    steady = max(n - fill, 0)
    body_steps = ns * ATT_UNROLL

    def unrolled(i, cr):
        for k in range(body_steps):
            cr = time_step(fill + body_steps * i + k, (fill + k) % ns, cr)
        return cr

    if steady // body_steps:
        carry = lax.fori_loop(0, steady // body_steps, unrolled, carry)
    for t in range(fill + body_steps * (steady // body_steps), max(n, fill)):
        carry = time_step(t, t % ns, carry)
    for t in range(max(n, fill), n + 2):
        carry = time_step(t, t % ns, carry, False, t - 1 < n, True)
    out = (acc_ref[0:hd, :] / acc_ref[hd:hd + 1, :]).T
    for j in range(grp):
        o_ref[:, j * hd:(j + 1) * hd] = out[j * tq:(j + 1) * tq].astype(o_ref.dtype)


def attention(dm, qt, kn, vt, latent):
    hd = ATT_HEAD_DIM
    grp = ATT_HEADS // ATT_KV_HEADS
    tk = dm.ctx
    assert dm.s % tk == 0
    ctx_blk0 = dm.n_lat // dm.ctx
    kc_spec = pl.BlockSpec((tk, hd), lambda b, h, i: (ctx_blk0 + b, h))
    vtc_spec = pl.BlockSpec((ATT_V_ROWS, tk), lambda b, h, i: (h, ctx_blk0 + b))
    if latent:
        tq = TQ_ATT
        nq = dm.s // tq
        q_spec = pl.BlockSpec((tq, grp * hd), lambda b, h, i: (b * nq + i, h))
        kl_spec = pl.BlockSpec((dm.s, hd), lambda b, h, i: (b, h))
        vtl_spec = pl.BlockSpec((ATT_V_ROWS, dm.s), lambda b, h, i: (h, b))
        in_specs = [q_spec, kl_spec, vtl_spec, kc_spec, vtc_spec]
        args = (qt, kn, vt, kn, vt)
        rows = dm.n_lat
    else:
        tq, nq = dm.ctx, 1
        q_spec = pl.BlockSpec((tq, grp * hd), lambda b, h, i: (ctx_blk0 + b, h))
        in_specs = [q_spec, kc_spec, vtc_spec]
        args = (qt, kn, vt)
        rows = dm.b * dm.ctx
    return pl.pallas_call(
        functools.partial(_attn_kernel, with_latent_keys=latent),
        grid=(dm.b, ATT_KV_HEADS, nq),
        in_specs=in_specs,
        out_specs=pl.BlockSpec((tq, grp * hd), lambda b, h, i: (b * nq + i, h)),
        out_shape=jax.ShapeDtypeStruct((rows, ATT_HEADS * hd), BF16),
        scratch_shapes=([pltpu.VMEM((ATT_V_ROWS, grp * tq), F32)]
                        + [pltpu.VMEM((tk, grp * tq), F32)] * ATT_STAGES
                        + [pltpu.VMEM((tk, grp * tq), BF16)] * ATT_STAGES),
        compiler_params=_cparams(("parallel", "parallel", "arbitrary")),
        name="attention_latent" if latent else "attention_ctx",
    )(*args)


def _pool_kernel(prev_ref, u_ref, next_ref, w_ref, sc_ref, o_ref, a_ref, b_ref, *, dm):
    tm = u_ref.shape[0]
    H = POOL_HALO
    i = pl.program_id(0)
    lat_tiles = dm.n_lat // tm
    is_lat = i < lat_tiles
    n_seg = jnp.where(is_lat, dm.s, dm.ctx)
    base = jnp.where(is_lat, (i * tm) % dm.s, 0)
    r = lax.broadcasted_iota(jnp.int32, (tm + 2 * H, 1), 0) - H
    pos = jnp.where(is_lat, base + r, r % dm.ctx)
    win = pl.ds(H, tm + 2 * H)

    def shifted(ref, d):
        return ref[pl.ds(H + d, tm + 2 * H), :]

    zeros = jnp.zeros((H, POOL_GROUP_DIM), F32)
    for ref in (a_ref, b_ref):
        ref[pl.ds(0, H), :] = zeros
        ref[pl.ds(tm + 3 * H, H), :] = zeros

    for gi, w in enumerate(POOL_WINDOWS):
        cols = slice(gi * POOL_GROUP_DIM, (gi + 1) * POOL_GROUP_DIM)
        half = w // 2
        a_ref[pl.ds(H, H), :] = prev_ref[:, cols]
        a_ref[pl.ds(2 * H, tm), :] = u_ref[:, cols]
        a_ref[pl.ds(2 * H + tm, H), :] = next_ref[:, cols]
        b_ref[win, :] = jnp.where(pos >= 1, shifted(a_ref, -1), 0.0)
        step = 1
        while step < half:
            b_ref[win, :] = b_ref[win, :] + jnp.where(pos >= step, shifted(b_ref, -step), 0.0)
            step *= 2
        step = 1
        while step < half:
            a_ref[win, :] = a_ref[win, :] + jnp.where(pos + step < n_seg, shifted(a_ref, step), 0.0)
            step *= 2
        tile = pl.ds(2 * H, tm)
        pt = pos[H:H + tm]
        cnt = (jnp.minimum(pt + half, n_seg) - jnp.maximum(pt - half, 0)).astype(F32)
        u = u_ref[:, cols]
        diff = (a_ref[tile, :] + b_ref[tile, :]) / cnt - u
        o_ref[:, cols] = (_dot(diff, w_ref[gi]) * sc_ref[:, cols]).astype(o_ref.dtype)


def pool_mixer(dm, p, pool_w, pool_scale):
    tm = dm.tile(TM_POOL)
    n = dm.n_all
    H = POOL_HALO
    width = len(POOL_WINDOWS) * POOL_GROUP_DIM
    col = 2560 // width
    per = tm // H
    last = n // H - 1
    return pl.pallas_call(
        functools.partial(_pool_kernel, dm=dm),
        grid=(n // tm,),
        in_specs=[pl.BlockSpec((H, width), lambda i: (jnp.maximum(i * per - 1, 0), col)),
                  pl.BlockSpec((tm, width), lambda i: (i, col)),
                  pl.BlockSpec((H, width), lambda i: (jnp.minimum((i + 1) * per, last), col)),
                  pl.BlockSpec(pool_w.shape, lambda i: (0, 0, 0)),
                  pl.BlockSpec((1, width), lambda i: (0, 0))],
        out_specs=pl.BlockSpec((tm, width), lambda i: (i, 0)),
        out_shape=jax.ShapeDtypeStruct((n, width), BF16),
        scratch_shapes=[pltpu.VMEM((tm + 4 * H, POOL_GROUP_DIM), F32),
                        pltpu.VMEM((tm + 4 * H, POOL_GROUP_DIM), F32)],
        compiler_params=_cparams(("parallel",)),
        name="pool_mixer",
    )(p, p, p, pool_w, pool_scale.reshape(1, width))


def _merge_kernel(yr_ref, ya_ref, yp_ref, gl_ref, *refs, lat_tiles):
    mod_ref, wb_ref, wo_ref, o_ref = refs[-4:]
    x_refs = refs[:-4]
    d = o_ref.shape[1]
    mixed = None
    for j, y_ref in enumerate((yr_ref, ya_ref, yp_ref)):
        gate = gl_ref[:, j * d:(j + 1) * d].astype(F32)
        term = gate * jnp.dot(y_ref[...], wb_ref[j], preferred_element_type=F32)
        mixed = term if mixed is None else mixed + term
    y = mod_ref[0][2:3] * _dot(mixed, wo_ref[...])

    def finish(x_ref):
        o_ref[...] = x_ref[...] + y

    _for_row_source(x_refs, lat_tiles, finish)


def merge(dm, n_rows, yr, ya, yp, gates, x_parts, mod, wb, wo):
    tm = dm.tile(TM_MERGE)
    d = dm.d
    bw = yr.shape[1]
    row = dm.mod_row(tm)
    return pl.pallas_call(
        functools.partial(_merge_kernel, lat_tiles=dm.n_lat // tm),
        grid=(n_rows // tm,),
        in_specs=[pl.BlockSpec((tm, bw), lambda i: (i, 0)),
                  pl.BlockSpec((tm, bw), lambda i: (i, 0)),
                  pl.BlockSpec((tm, bw), lambda i: (i, 0)),
                  pl.BlockSpec((tm, 3 * d), lambda i: (i, 0))]
        + _row_sources(dm, tm, x_parts) + [
            pl.BlockSpec((1, 6, d), lambda i: (row(i), 0, 0)),
            pl.BlockSpec(wb.shape, lambda i: (0, 0, 0)),
            pl.BlockSpec(wo.shape, lambda i: (0, 0))],
        out_specs=pl.BlockSpec((tm, d), lambda i: (i, 0)),
        out_shape=jax.ShapeDtypeStruct((n_rows, d), F32),
        compiler_params=_cparams(("parallel",)),
        name="merge",
    )(yr, ya, yp, gates, *x_parts, mod, wb, wo)


def _ffn_kernel(x_ref, mod_ref, gain_ref, w1_ref, w3_ref, w2_ref, o_ref):
    m = mod_ref[0]
    h = _norm_mod(x_ref[...], gain_ref[...], m[3:4], m[4:5]).astype(BF16)
    fdim = w1_ref.shape[1]
    y = None
    for lo in range(0, fdim, MOE_COL_CHUNK):
        cols = slice(lo, min(lo + MOE_COL_CHUNK, fdim))
        a = jnp.dot(h, w1_ref[:, cols], preferred_element_type=F32)
        b = jnp.dot(h, w3_ref[:, cols], preferred_element_type=F32)
        part = _dot(_silu(a) * b, w2_ref[cols, :])
        y = part if y is None else y + part
    o_ref[...] = x_ref[...] + m[5:6] * y


def ffn_dense(dm, x, mod, gain, w1, w3, w2):
    n, d = x.shape
    fdim = w1.shape[1]
    tm = dm.tile(TM_FFN)
    row = dm.mod_row(tm)
    once = dict(pipeline_mode=pl.Buffered(1))
    return pl.pallas_call(
        _ffn_kernel,
        grid=(n // tm,),
        in_specs=[pl.BlockSpec((tm, d), lambda i: (i, 0)),
                  pl.BlockSpec((1, 6, d), lambda i: (row(i), 0, 0)),
                  pl.BlockSpec((1, d), lambda i: (0, 0)),
                  pl.BlockSpec((d, fdim), lambda i: (0, 0), **once),
                  pl.BlockSpec((d, fdim), lambda i: (0, 0), **once),
                  pl.BlockSpec((fdim, d), lambda i: (0, 0), **once)],
        out_specs=pl.BlockSpec((tm, d), lambda i: (i, 0)),
        out_shape=jax.ShapeDtypeStruct((n, d), F32),
        compiler_params=_cparams(("parallel",)),
        name="ffn_dense",
    )(x, mod, gain.reshape(1, d), w1, w3, w2)


def _router_kernel(x_ref, mod_ref, gain_ref, wr_ref, br_ref, h_ref, slot_ref, slot_t_ref, wt_ref, cum_ref):
    tm = x_ref.shape[0]
    m = mod_ref[0]
    h = _norm_mod(x_ref[...], gain_ref[...], m[3:4], m[4:5])
    h_ref[...] = h.astype(BF16)
    logits = _dot3(h, wr_ref[...]) + br_ref[...]
    lane = lax.broadcasted_iota(jnp.int32, logits.shape, 1)
    m1 = jnp.max(logits, axis=-1, keepdims=True)
    i1 = jnp.min(jnp.where(logits == m1, lane, LANES), axis=-1, keepdims=True)
    rest = jnp.where(lane == i1, -jnp.inf, logits)
    m2 = jnp.max(rest, axis=-1, keepdims=True)
    i2 = jnp.min(jnp.where(rest == m2, lane, LANES), axis=-1, keepdims=True)
    e = jnp.exp(m2 - m1)
    w1 = 1.0 / (1.0 + e)
    w2 = e / (1.0 + e)
    wt_ref[...] = jnp.where(lane == i1, w1, jnp.where(lane == i2, w2, 0.0))
    routed = jnp.where((lane == i1) | (lane == i2), 1.0, 0.0)
    sub = SUB_MOE
    lower = (lax.broadcasted_iota(jnp.int32, (sub, sub), 0) > lax.broadcasted_iota(jnp.int32, (sub, sub), 1))
    lower = lower.astype(BF16)
    offset = jnp.zeros((1, LANES), F32)
    cum_ref[...] = jnp.zeros_like(cum_ref)
    for blk in range(tm // sub):
        rows = slice(blk * sub, (blk + 1) * sub)
        mask = routed[rows]
        cum_ref[0, blk:blk + 1, :] = offset.astype(jnp.int32)
        rank = jnp.dot(lower, mask.astype(BF16), preferred_element_type=F32) + offset
        slot_ref[rows, :] = jnp.where(mask > 0.0, rank, -1.0)
        count = jnp.sum(mask, axis=0, keepdims=True)
        offset = offset + jnp.ceil(count * (1.0 / SLOT_ALIGN)) * SLOT_ALIGN
    cum_ref[0, tm // sub:tm // sub + 1, :] = offset.astype(jnp.int32)
    slot_t_ref[...] = slot_ref[...].T[:SUBLANES, :]


def moe_route(dm, n_rows, x, mod, gain, wr_pad, br_pad):
    tm = dm.tile(TM_MOE)
    d = dm.d
    row = dm.mod_row(tm)
    nt = n_rows // tm
    cum_rows = -(-(tm // SUB_MOE + 1) // SUBLANES) * SUBLANES
    return pl.pallas_call(
        _router_kernel,
        grid=(nt,),
        in_specs=[pl.BlockSpec((tm, d), lambda i: (i, 0)),
                  pl.BlockSpec((1, 6, d), lambda i: (row(i), 0, 0)),
                  pl.BlockSpec((1, d), lambda i: (0, 0)),
                  pl.BlockSpec((d, LANES), lambda i: (0, 0)),
                  pl.BlockSpec((1, LANES), lambda i: (0, 0))],
        out_specs=[pl.BlockSpec((tm, d), lambda i: (i, 0)),
                   pl.BlockSpec((tm, LANES), lambda i: (i, 0)),
                   pl.BlockSpec((SUBLANES, tm), lambda i: (0, i)),
                   pl.BlockSpec((tm, LANES), lambda i: (i, 0)),
                   pl.BlockSpec((1, cum_rows, LANES), lambda i: (i, 0, 0))],
        out_shape=[jax.ShapeDtypeStruct((n_rows, d), BF16),
                   jax.ShapeDtypeStruct((n_rows, LANES), F32),
                   jax.ShapeDtypeStruct((SUBLANES, n_rows), F32),
                   jax.ShapeDtypeStruct((n_rows, LANES), F32),
                   jax.ShapeDtypeStruct((nt, cum_rows, LANES), jnp.int32)],
        compiler_params=_cparams(("parallel",)),
        name="moe_router",
    )(x, mod, gain.reshape(1, d), wr_pad, br_pad)


def _moe_kernel(cum_ref, h_ref, slot_ref, slot_t_ref, wt_ref, w1_ref, w3_ref, w2_ref, o_ref,
                xs_ref, acc_ref):
    i, e, f = pl.program_id(0), pl.program_id(1), pl.program_id(2)
    tm, d = h_ref.shape
    rb, win, sub = RB_MOE, WIN_MOE, SUB_MOE
    nsub = tm // sub

    def base(j):
        return cum_ref[i, j * N_EXPERTS + e]

    halves = (base(nsub) + rb // 2 - 1) // (rb // 2)
    nblk, has_tail = halves // 2, halves % 2 == 1

    def block_rows(r, size=rb):
        return pl.ds(pl.multiple_of(r * rb, 32), size)

    def windows(j):
        return base(j), (base(j + 1) - base(j) + win - 1) // win

    def window_rows(start, w):
        return pl.ds(pl.multiple_of(start + w * win, SLOT_ALIGN), win)

    @pl.when((e == 0) & (f == 0))
    def _():
        o_ref[...] = jnp.zeros_like(o_ref)

    @pl.when(f == 0)
    def _():
        def clear(r, carry):
            xs_ref[block_rows(r), :] = jnp.zeros((rb, d), BF16)
            acc_ref[block_rows(r), :] = jnp.zeros((rb, d), F32)
            return carry

        lax.fori_loop(0, nblk + 1, clear, 0)
        acc_ref[pl.ds(pl.multiple_of((nblk + 1) * rb, 32), win), :] = jnp.zeros((win, d), F32)

        for j in range(nsub):
            tok = slice(j * sub, (j + 1) * sub)
            start, nwin = windows(j)

            def gather(w, carry, tok=tok, start=start):
                rows = window_rows(start, w)
                want = (lax.broadcasted_iota(jnp.int32, (win, sub), 0) + (start + w * win)).astype(F32)
                onehot = (slot_t_ref[pl.ds(e, 1), tok] == want).astype(BF16)
                xs_ref[rows, :] = jnp.dot(onehot, h_ref[tok, :], preferred_element_type=F32).astype(BF16)
                return carry

            lax.fori_loop(0, nwin, gather, 0)

    def expert(r, carry, size=rb):
        rows = block_rows(r, size)
        xs = xs_ref[rows, :]
        tf = w1_ref.shape[2]
        y = None
        for lo in range(0, tf, MOE_COL_CHUNK):
            cols = slice(lo, min(lo + MOE_COL_CHUNK, tf))
            a = jnp.dot(xs, w1_ref[0, :, cols], preferred_element_type=F32)
            b = jnp.dot(xs, w3_ref[0, :, cols], preferred_element_type=F32)
            part = _dot(_silu(a) * b, w2_ref[0, cols, :])
            y = part if y is None else y + part
        acc_ref[rows, :] += y
        return carry

    lax.fori_loop(0, nblk, expert, 0)

    @pl.when(has_tail)
    def _():
        expert(nblk, 0, rb // 2)

    @pl.when(f == pl.num_programs(2) - 1)
    def _():
        pick = lax.broadcasted_iota(jnp.int32, (sub, LANES), 1) == e
        for j in range(nsub):
            tok = slice(j * sub, (j + 1) * sub)
            start, nwin = windows(j)
            slot_col = jnp.sum(jnp.where(pick, slot_ref[tok, :], 0.0), axis=-1, keepdims=True)
            wt_col = jnp.sum(jnp.where(pick, wt_ref[tok, :], 0.0), axis=-1, keepdims=True)

            def scatter(w, carry, tok=tok, start=start, slot_col=slot_col, wt_col=wt_col):
                ys = acc_ref[window_rows(start, w), :].astype(BF16)
                want = (lax.broadcasted_iota(jnp.int32, (sub, win), 1) + (start + w * win)).astype(F32)
                onehot = (slot_col == want).astype(BF16)
                o_ref[tok, :] += wt_col * jnp.dot(onehot, ys, preferred_element_type=F32)
                return carry

            lax.fori_loop(0, nwin, scatter, 0)


def moe_experts(dm, n_rows, h, slot, slot_t, wt, counts, w1, w3, w2):
    tm = dm.tile(TM_MOE)
    d = dm.d
    ne, _, fdim = w1.shape
    tf = TF_MOE if fdim % TF_MOE == 0 else TF_FFN
    assert SUB_MOE % SLOT_ALIGN == 0 and tm % SUB_MOE == 0
    cap = tm + RB_MOE + WIN_MOE
    once = dict(pipeline_mode=pl.Buffered(1))
    grid_spec = pltpu.PrefetchScalarGridSpec(
        num_scalar_prefetch=1,
        grid=(n_rows // tm, ne, fdim // tf),
        in_specs=[pl.BlockSpec((tm, d), lambda i, e, f, c: (i, 0), **once),
                  pl.BlockSpec((tm, LANES), lambda i, e, f, c: (i, 0), **once),
                  pl.BlockSpec((SUBLANES, tm), lambda i, e, f, c: (0, i), **once),
                  pl.BlockSpec((tm, LANES), lambda i, e, f, c: (i, 0), **once),
                  pl.BlockSpec((1, d, tf), lambda i, e, f, c: (e, 0, f)),
                  pl.BlockSpec((1, d, tf), lambda i, e, f, c: (e, 0, f)),
                  pl.BlockSpec((1, tf, d), lambda i, e, f, c: (e, f, 0))],
        out_specs=pl.BlockSpec((tm, d), lambda i, e, f, c: (i, 0), **once),
        scratch_shapes=[pltpu.VMEM((cap, d), BF16), pltpu.VMEM((cap, d), F32)],
    )
    return pl.pallas_call(
        _moe_kernel,
        grid_spec=grid_spec,
        out_shape=jax.ShapeDtypeStruct((n_rows, d), F32),
        compiler_params=_cparams(("parallel", "arbitrary", "arbitrary")),
        name="moe_experts",
    )(counts, h, slot, slot_t, wt, w1, w3, w2)


def _moe_windows(base_ref, row0_ref, e, step=None):
    i, j = pl.program_id(0), pl.program_id(1)
    if step is not None:
        i, j = step // pl.num_programs(1), step % pl.num_programs(1)
    start = base_ref[i, j * N_EXPERTS + e]
    nwin = (base_ref[i, (j + 1) * N_EXPERTS + e] - start + WIN_MOE - 1) // WIN_MOE
    return start, nwin, row0_ref[i, e]


def _moe_gather_kernel(base_ref, row0_ref, h_ref, slot_t_ref, xs_in_ref, xs_ref, buf_ref, sem):
    del xs_in_ref
    win, sub = WIN_MOE, SUB_MOE
    spare = N_EXPERTS

    def gathered(e, start, w):
        want = (lax.broadcasted_iota(jnp.int32, (win, sub), 0) + (start + w * win)).astype(F32)
        onehot = (slot_t_ref[e:e + 1, :] == want).astype(BF16)
        return jnp.dot(onehot, h_ref[...], preferred_element_type=F32).astype(BF16)

    def copy(e, buf, w):
        start, _, row0 = _moe_windows(base_ref, row0_ref, e)
        rows = pl.ds(pl.multiple_of(row0 + start + w * win, SLOT_ALIGN), win)
        return pltpu.make_async_copy(buf_ref.at[buf], xs_ref.at[rows, :], sem.at[buf])

    for e in range(N_EXPERTS):
        start, _, _ = _moe_windows(base_ref, row0_ref, e)
        buf_ref[e] = gathered(e, start, 0)
        copy(e, e, 0).start()

    for e in range(N_EXPERTS):
        start, nwin, _ = _moe_windows(base_ref, row0_ref, e)

        def extra(w, carry, e=e, start=start):
            buf_ref[spare] = gathered(e, start, w)
            cp = copy(e, spare, w)
            cp.start()
            cp.wait()
            return carry

        lax.fori_loop(1, jnp.maximum(nwin, 1), extra, 0)

    for e in range(N_EXPERTS):
        copy(e, e, 0).wait()


def moe_gather(dm, n_rows, h, slot_t, base, row0, cap):
    tm, sub, d = dm.tile(TM_MOE), SUB_MOE, dm.d
    nsub = tm // sub
    grid_spec = pltpu.PrefetchScalarGridSpec(
        num_scalar_prefetch=2,
        grid=(n_rows // tm, nsub),
        in_specs=[pl.BlockSpec((sub, d), lambda i, j, b, r: (i * nsub + j, 0)),
                  pl.BlockSpec((SUBLANES, sub), lambda i, j, b, r: (0, i * nsub + j)),
                  pl.BlockSpec(memory_space=pl.ANY)],
        out_specs=pl.BlockSpec(memory_space=pl.ANY),
        scratch_shapes=[pltpu.VMEM((N_EXPERTS + 1, WIN_MOE, d), BF16),
                        pltpu.SemaphoreType.DMA((N_EXPERTS + 1,))],
    )
    return pl.pallas_call(
        _moe_gather_kernel,
        grid_spec=grid_spec,
        out_shape=jax.ShapeDtypeStruct((cap, d), BF16),
        input_output_aliases={4: 0},
        compiler_params=_cparams(("arbitrary", "arbitrary")),
        name="moe_gather",
    )(base, row0, h, slot_t, jnp.zeros((cap, d), BF16))


def _moe_ffn_kernel(bexp_ref, bvalid_ref, xs_ref, w1_ref, w3_ref, w2_ref, o_ref):
    del bexp_ref
    valid = bvalid_ref[pl.program_id(0)] == 1

    @pl.when(valid)
    def _():
        xs = xs_ref[...]
        fdim = w1_ref.shape[2]
        y = None
        for lo in range(0, fdim, MOE_COL_CHUNK):
            cols = slice(lo, min(lo + MOE_COL_CHUNK, fdim))
            a = jnp.dot(xs, w1_ref[0, :, cols], preferred_element_type=F32)
            b = jnp.dot(xs, w3_ref[0, :, cols], preferred_element_type=F32)
            part = _dot(_silu(a) * b, w2_ref[0, cols, :])
            y = part if y is None else y + part
        o_ref[...] = y.astype(o_ref.dtype)

    @pl.when(jnp.logical_not(valid))
    def _():
        o_ref[...] = jnp.zeros_like(o_ref)


def moe_ffn(xs, blk_expert, blk_valid, w1, w3, w2):
    cap, d = xs.shape
    _, _, fdim = w1.shape
    blk = BLK_MOE
    once = dict(pipeline_mode=pl.Buffered(1))
    grid_spec = pltpu.PrefetchScalarGridSpec(
        num_scalar_prefetch=2,
        grid=(cap // blk,),
        in_specs=[pl.BlockSpec((blk, d), lambda k, be, bv: (k, 0)),
                  pl.BlockSpec((1, d, fdim), lambda k, be, bv: (be[k], 0, 0), **once),
                  pl.BlockSpec((1, d, fdim), lambda k, be, bv: (be[k], 0, 0), **once),
                  pl.BlockSpec((1, fdim, d), lambda k, be, bv: (be[k], 0, 0), **once)],
        out_specs=pl.BlockSpec((blk, d), lambda k, be, bv: (k, 0)),
    )
    return pl.pallas_call(
        _moe_ffn_kernel,
        grid_spec=grid_spec,
        out_shape=jax.ShapeDtypeStruct((cap, d), BF16),
        compiler_params=_cparams(("arbitrary",)),
        name="moe_ffn",
    )(blk_expert, blk_valid, xs, w1, w3, w2)


def _moe_scatter_kernel(base_ref, row0_ref, slot_ref, wt_ref, x_ref, mod_ref, gain_ref, ys_ref, o_ref,
                        buf_ref, y_ref, sem, *, final):
    win, sub = WIN_MOE, SUB_MOE
    nbuf = N_EXPERTS + 1
    step = pl.program_id(0) * pl.num_programs(1) + pl.program_id(1)
    last_step = pl.num_programs(0) * pl.num_programs(1) - 1
    mine = (step % 2) * nbuf
    spare = mine + N_EXPERTS

    def copy(e, buf, w, at=None):
        start, _, row0 = _moe_windows(base_ref, row0_ref, e, at)
        rows = pl.ds(pl.multiple_of(row0 + start + w * win, SLOT_ALIGN), win)
        return pltpu.make_async_copy(ys_ref.at[rows, :], buf_ref.at[buf], sem.at[buf])

    @pl.when(step == 0)
    def _():
        for e in range(N_EXPERTS):
            copy(e, e, 0).start()

    @pl.when(step < last_step)
    def _():
        for e in range(N_EXPERTS):
            copy(e, nbuf - mine + e, 0, at=step + 1).start()

    def scattered(e, buf, w):
        start, _, _ = _moe_windows(base_ref, row0_ref, e)
        want = (lax.broadcasted_iota(jnp.int32, (sub, win), 1) + (start + w * win)).astype(F32)
        onehot = (slot_ref[:, e:e + 1] == want).astype(BF16)
        return wt_ref[:, e:e + 1] * jnp.dot(onehot, buf_ref[buf], preferred_element_type=F32)

    y_ref[...] = jnp.zeros_like(y_ref)
    for e in range(N_EXPERTS):
        _, nwin, _ = _moe_windows(base_ref, row0_ref, e)

        def extra(w, carry, e=e):
            cp = copy(e, spare, w)
            cp.start()
            cp.wait()
            y_ref[...] += scattered(e, spare, w)
            return carry

        lax.fori_loop(1, jnp.maximum(nwin, 1), extra, 0)

    y = y_ref[...]
    for e in range(N_EXPERTS):
        copy(e, mine + e, 0).wait()
        y = y + scattered(e, mine + e, 0)
    x = x_ref[...] + mod_ref[0][5:6] * y
    if final:
        x = x * lax.rsqrt(jnp.mean(x * x, axis=-1, keepdims=True) + NORM_EPS) * gain_ref[...]
    o_ref[...] = x


def moe_scatter(dm, n_rows, ys, slot, wt, base, row0, x, mod, final_gain):
    tm, sub, d = dm.tile(TM_MOE), SUB_MOE, dm.d
    nsub = tm // sub
    row = dm.mod_row(sub)
    final = final_gain is not None
    gain = final_gain if final else jnp.ones((d,), F32)
    tok = lambda i, j, b, r: (i * nsub + j, 0)
    grid_spec = pltpu.PrefetchScalarGridSpec(
        num_scalar_prefetch=2,
        grid=(n_rows // tm, nsub),
        in_specs=[pl.BlockSpec((sub, LANES), tok),
                  pl.BlockSpec((sub, LANES), tok),
                  pl.BlockSpec((sub, d), tok),
                  pl.BlockSpec((1, 6, d), lambda i, j, b, r: (row(i * nsub + j), 0, 0)),
                  pl.BlockSpec((1, d), lambda i, j, b, r: (0, 0)),
                  pl.BlockSpec(memory_space=pl.ANY)],
        out_specs=pl.BlockSpec((sub, d), tok),
        scratch_shapes=[pltpu.VMEM((2 * (N_EXPERTS + 1), WIN_MOE, d), BF16),
                        pltpu.VMEM((sub, d), F32),
                        pltpu.SemaphoreType.DMA((2 * (N_EXPERTS + 1),))],
    )
    return pl.pallas_call(
        functools.partial(_moe_scatter_kernel, final=final),
        grid_spec=grid_spec,
        out_shape=jax.ShapeDtypeStruct((n_rows, d), F32),
        compiler_params=_cparams(("arbitrary", "arbitrary")),
        name="moe_scatter",
    )(base, row0, slot, wt, x, mod, gain.reshape(1, d), ys)


def moe_layout(dm, n_rows, cum):
    tm = dm.tile(TM_MOE)
    nt, nsub = n_rows // tm, tm // SUB_MOE
    base = cum[:, :nsub + 1, :N_EXPERTS]
    total = base[:, nsub, :]
    per_expert = jnp.sum(total, axis=0)
    padded = (per_expert + WIN_MOE + BLK_MOE - 1) // BLK_MOE * BLK_MOE
    expert_row0 = jnp.cumsum(padded) - padded
    row0 = expert_row0[None, :] + jnp.cumsum(total, axis=0) - total
    slack = nt * nsub * N_EXPERTS * (SLOT_ALIGN - 1) + N_EXPERTS * (WIN_MOE + BLK_MOE)
    nblk = -(-(2 * n_rows + slack) // BLK_MOE)
    ends = jnp.cumsum(padded // BLK_MOE)
    k = jnp.arange(nblk, dtype=jnp.int32)
    blk_expert = jnp.minimum(jnp.searchsorted(ends, k, side="right"), N_EXPERTS - 1).astype(jnp.int32)
    blk_valid = (k < ends[-1]).astype(jnp.int32)
    return base.reshape(nt, (nsub + 1) * N_EXPERTS), row0.astype(jnp.int32), blk_expert, blk_valid, nblk * BLK_MOE


def _residual_kernel(x_ref, y_ref, mod_ref, o_ref):
    o_ref[...] = x_ref[...] + mod_ref[0][5:6] * y_ref[...]


def _final_kernel(x_ref, y_ref, mod_ref, gain_ref, o_ref):
    x = x_ref[...] + mod_ref[0][5:6] * y_ref[...]
    o_ref[...] = x * lax.rsqrt(jnp.mean(x * x, axis=-1, keepdims=True) + NORM_EPS) * gain_ref[...]


def residual(dm, n_rows, x, y, mod, final_gain=None):
    tm = dm.tile(TM_FFN)
    d = dm.d
    row = dm.mod_row(tm)
    in_specs = [pl.BlockSpec((tm, d), lambda i: (i, 0)),
                pl.BlockSpec((tm, d), lambda i: (i, 0)),
                pl.BlockSpec((1, 6, d), lambda i: (row(i), 0, 0))]
    args = [x, y, mod]
    if final_gain is not None:
        in_specs.append(pl.BlockSpec((1, d), lambda i: (0, 0)))
        args.append(final_gain.reshape(1, d))
    return pl.pallas_call(
        _residual_kernel if final_gain is None else _final_kernel,
        grid=(n_rows // tm,),
        in_specs=in_specs,
        out_specs=pl.BlockSpec((tm, d), lambda i: (i, 0)),
        out_shape=jax.ShapeDtypeStruct((n_rows, d), F32),
        compiler_params=_cparams(("parallel",)),
        name="residual" if final_gain is None else "residual_final_norm",
    )(*args)


def _final_norm_kernel(x_ref, gain_ref, o_ref):
    x = x_ref[...]
    o_ref[...] = x * lax.rsqrt(jnp.mean(x * x, axis=-1, keepdims=True) + NORM_EPS) * gain_ref[...]


def final_rmsnorm(dm, n_rows, x, gain):
    tm = dm.tile(TM_FFN)
    d = dm.d
    return pl.pallas_call(
        _final_norm_kernel,
        grid=(n_rows // tm,),
        in_specs=[pl.BlockSpec((tm, d), lambda i: (i, 0)), pl.BlockSpec((1, d), lambda i: (0, 0))],
        out_specs=pl.BlockSpec((tm, d), lambda i: (i, 0)),
        out_shape=jax.ShapeDtypeStruct((n_rows, d), F32),
        compiler_params=_cparams(("parallel",)),
        name="final_norm",
    )(x, gain.reshape(1, d))


def _rope_tables(s, ctx):
    def angles(pos, dim):
        freqs = 1.0 / (ROPE_THETA ** (jnp.arange(0, dim, 2, dtype=F32) / dim))
        return pos[:, None] * freqs[None, :]

    rows = s // GRID_W
    row = jnp.repeat(jnp.arange(rows, dtype=F32), GRID_W)
    col = jnp.tile(jnp.arange(GRID_W, dtype=F32), rows)
    half = ATT_HEAD_DIM // 2
    ang_att = jnp.concatenate([angles(row, half), angles(col, half)], axis=-1)
    sign = jnp.tile(jnp.array([-1.0, 1.0], F32), LANES // 2)

    def tables(ang):
        reps = LANES // (2 * ang.shape[1])
        cos = jnp.tile(jnp.repeat(jnp.cos(ang), 2, axis=-1), (1, reps))
        sin = jnp.tile(jnp.repeat(jnp.sin(ang), 2, axis=-1), (1, reps)) * sign
        return cos, sin

    ang_ret = angles(jnp.arange(s, dtype=F32), RET_QK_DIM)
    ang_ret = jnp.concatenate([ang_ret, jnp.zeros((ctx, RET_QK_DIM // 2), F32)], axis=0)
    return tables(ang_att) + tables(ang_ret)


def kernel(x, c, ctx, c_ctx, w_ada, b_ada, norm_mix, norm_ffn, w_in, ret_decay, ret_gn, attn_qn, attn_kn, pool_w, pool_scale, w_branch, w_out, ffn_w1, ffn_w3, ffn_w2, moe_router, moe_router_b, moe_w1, moe_w3, moe_w2, final_norm):
    b, s, d = x.shape
    n_ctx = ctx.shape[1]
    depth = w_in.shape[0]
    dm = Dims(b, s, n_ctx, d)

    x_parts = (x.reshape(b * s, d), ctx.reshape(b * n_ctx, d))
    mod_rows = 2 * SUBLANES
    cc = jnp.concatenate([c, c_ctx[None, :], jnp.zeros((mod_rows - b - 1, d), F32)], axis=0)
    mods = ada_table(cc, w_ada, b_ada).reshape(depth, mod_rows, 6, d)

    att_cos, att_sin, ret_cos, ret_sin = _rope_tables(s, n_ctx)

    for i in range(depth):
        need_ctx = i < depth - 1
        n_rows = dm.n_all if need_ctx else dm.n_lat
        mod = mods[i]
        p, gates = proj_in(dm, x_parts, mod, norm_mix[i], w_in[i].astype(BF16))
        y_ret = retention(dm, p, ret_decay[i], ret_cos, ret_sin, ret_gn[i])
        y_pool = pool_mixer(dm, p, pool_w[i].astype(BF16), pool_scale[i])
        qt, kn, vt = att_prep(dm, p, att_cos, att_sin, attn_qn[i], attn_kn[i])
        y_att = attention(dm, qt, kn, vt, latent=True)
        if need_ctx:
            y_att = jnp.concatenate([y_att, attention(dm, qt, kn, vt, latent=False)], axis=0)
        xa = merge(dm, n_rows, y_ret, y_att, y_pool, gates, x_parts, mod,
                   w_branch[i].astype(BF16), w_out[i].astype(BF16))
        j = i // 2
        last = i == depth - 1
        if i % 2 == 0:
            xa = ffn_dense(dm, xa, mod, norm_ffn[i], ffn_w1[j].astype(BF16), ffn_w3[j].astype(BF16),
                           ffn_w2[j].astype(BF16))
            if last:
                xa = final_rmsnorm(dm, n_rows, xa, final_norm)
        else:
            wr = jnp.zeros((d, LANES), F32).at[:, :N_EXPERTS].set(moe_router[j])
            br = jnp.full((1, LANES), -1e30, F32).at[0, :N_EXPERTS].set(moe_router_b[j])
            h, slot, slot_t, wt, cnt = moe_route(dm, n_rows, xa, mod, norm_ffn[i], wr, br)
            base, row0, blk_expert, blk_valid, cap = moe_layout(dm, n_rows, cnt)
            xs = moe_gather(dm, n_rows, h, slot_t, base, row0, cap)
            ys = moe_ffn(xs, blk_expert, blk_valid,
                         moe_w1[j].astype(BF16), moe_w3[j].astype(BF16), moe_w2[j].astype(BF16))
            xa = moe_scatter(dm, n_rows, ys, slot, wt, base, row0, xa, mod, final_norm if last else None)
        x_parts = (xa,)
    return xa[:dm.n_lat].reshape(b, s, d)
```

```python
import functools
import math

import jax
import jax.numpy as jnp
from jax import lax
from jax.experimental import pallas as pl
from jax.experimental.pallas import tpu as pltpu

F32 = jnp.float32
BF16 = jnp.bfloat16

GRID_W = 64
RET_HEADS = 4
RET_QK_DIM = 64
RET_V_DIM = 128
ATT_HEADS = 4
ATT_KV_HEADS = 2
ATT_HEAD_DIM = 128
ROPE_THETA = 10000.0
POOL_WINDOWS = (2, 4, 8, 16)
POOL_GROUP_DIM = 128
N_EXPERTS = 8
NORM_EPS = 1e-6
GN_EPS = 1e-5

LANES = 128
SUBLANES = 8
VMEM_LIMIT_BYTES = 56 * 1024 * 1024

TM_PROJ = 1024
TN_PROJ = 1536
TM_PREP = 1024
TM_POOL = 1024
TM_MERGE = 1024
TM_FFN = 1024
TM_MOE = 2048
MOE_COL_CHUNK = 512
BLK_MOE = 512
WIN_MOE = 256
SUB_MOE = 512
SLOT_ALIGN = 16
RET_L = 256
TQ_ATT = 512
ATT_Q_COL = 2 * RET_HEADS * RET_QK_DIM + 2 * RET_HEADS * RET_V_DIM
ATT_K_COL = ATT_Q_COL + ATT_HEADS * ATT_HEAD_DIM
ATT_V_COL = ATT_K_COL + ATT_KV_HEADS * ATT_HEAD_DIM
ATT_SCORE_SCALE = (ATT_HEAD_DIM ** -0.5) * math.log2(math.e)
ATT_V_ROWS = ATT_HEAD_DIM + 16
ATT_STAGES = 3
ATT_UNROLL = 4
ATT_REDUCE_GROUPS = 8
POOL_HALO = 8


def _cparams(sem):
    return pltpu.CompilerParams(dimension_semantics=sem, vmem_limit_bytes=VMEM_LIMIT_BYTES)


def _dot(a, b):
    return jnp.dot(a.astype(BF16), b.astype(BF16), preferred_element_type=F32)


def _dot_nt(a, b):
    return lax.dot_general(a.astype(BF16), b.astype(BF16), (((1,), (1,)), ((), ())),
                           preferred_element_type=F32)


def _dot_tn(a, b):
    return lax.dot_general(a.astype(BF16), b.astype(BF16), (((0,), (0,)), ((), ())),
                           preferred_element_type=F32)


def _split_bf16(a):
    hi = a.astype(BF16)
    lo = (a - hi.astype(F32)).astype(BF16)
    return hi, lo


def _dot3(a, b):
    ah, al = _split_bf16(a)
    bh, bl = _split_bf16(b)
    d = functools.partial(jnp.dot, preferred_element_type=F32)
    return d(ah, bh) + (d(al, bh) + d(ah, bl))


def _norm_mod(x, gain, shift, scale):
    y = x * lax.rsqrt(jnp.mean(x * x, axis=-1, keepdims=True) + NORM_EPS)
    return (y * gain) * (1.0 + scale) + shift


def _sigmoid(x):
    return 1.0 / (1.0 + jnp.exp(-x))


def _silu(x):
    return x * _sigmoid(x)


class Dims:
    def __init__(self, b, s, ctx, d):
        self.b, self.s, self.ctx, self.d = b, s, ctx, d
        self.n_lat = b * s
        self.n_all = b * s + b * ctx

    def tile(self, preferred):
        t = preferred
        while self.s % t or (self.b * self.ctx) % t:
            t //= 2
        return t

    def mod_row(self, tm):
        lat_tiles, per_batch, b = self.n_lat // tm, self.s // tm, self.b
        return lambda i: jnp.where(i < lat_tiles, i // per_batch, b)


def _ada_kernel(c_ref, w_ref, b_ref, o_ref):
    o_ref[0] = _dot3(_silu(c_ref[...]), w_ref[0]) + b_ref[0]


def ada_table(cc, w_ada, b_ada):
    depth, d, n6 = w_ada.shape
    tn = n6 // 4
    rows = cc.shape[0]
    return pl.pallas_call(
        _ada_kernel,
        grid=(depth, n6 // tn),
        in_specs=[pl.BlockSpec((rows, d), lambda l, j: (0, 0)),
                  pl.BlockSpec((1, d, tn), lambda l, j: (l, 0, j)),
                  pl.BlockSpec((1, 1, tn), lambda l, j: (l, 0, j))],
        out_specs=pl.BlockSpec((1, rows, tn), lambda l, j: (l, 0, j)),
        out_shape=jax.ShapeDtypeStruct((depth, rows, n6), F32),
        compiler_params=_cparams(("arbitrary", "arbitrary")),
        name="ada_table",
    )(cc, w_ada, b_ada.reshape(depth, 1, n6))


def _row_sources(dm, tm, parts):
    d = dm.d
    if len(parts) == 1:
        return [pl.BlockSpec((tm, d), lambda i, *_: (i, 0))]
    lat_tiles = dm.n_lat // tm
    return [pl.BlockSpec((tm, d), lambda i, *_: (jnp.minimum(i, lat_tiles - 1), 0)),
            pl.BlockSpec((tm, d), lambda i, *_: (jnp.maximum(i - lat_tiles, 0), 0))]


def _for_row_source(x_refs, lat_tiles, fn):
    if len(x_refs) == 1:
        fn(x_refs[0])
        return
    is_lat = pl.program_id(0) < lat_tiles
    pl.when(is_lat)(lambda: fn(x_refs[0]))
    pl.when(jnp.logical_not(is_lat))(lambda: fn(x_refs[1]))


def _proj_in_kernel(*refs, lat_tiles):
    mod_ref, gain_ref, wm_ref, wg_ref, o_ref, g_ref = refs[-6:]
    x_refs = refs[:-6]
    x = x_refs[0][...]
    if len(x_refs) == 2:
        x = jnp.where(pl.program_id(0) < lat_tiles, x, x_refs[1][...])
    m = mod_ref[0]
    h = _norm_mod(x, gain_ref[...], m[0:1], m[1:2]).astype(BF16)
    o_ref[...] = jnp.dot(h, wm_ref[...], preferred_element_type=F32)
    logits = jnp.dot(h, wg_ref[...], preferred_element_type=F32)
    g_ref[...] = (0.5 * jnp.tanh(0.5 * logits) + 0.5).astype(g_ref.dtype)


def proj_in(dm, x_parts, mod, gain, w):
    n, d = dm.n_all, dm.d
    nout = w.shape[1]
    gate_w = 3 * d
    tm, tn = dm.tile(TM_PROJ), TN_PROJ
    assert nout - gate_w == gate_w and gate_w % tn == 0
    gate_blk0 = gate_w // tn
    row = dm.mod_row(tm)
    return pl.pallas_call(
        functools.partial(_proj_in_kernel, lat_tiles=dm.n_lat // tm),
        grid=(n // tm, gate_w // tn),
        in_specs=_row_sources(dm, tm, x_parts) + [
            pl.BlockSpec((1, 6, d), lambda i, j: (row(i), 0, 0)),
            pl.BlockSpec((1, d), lambda i, j: (0, 0)),
            pl.BlockSpec((d, tn), lambda i, j: (0, j)),
            pl.BlockSpec((d, tn), lambda i, j: (0, gate_blk0 + j))],
        out_specs=[pl.BlockSpec((tm, tn), lambda i, j: (i, j)),
                   pl.BlockSpec((tm, tn), lambda i, j: (i, j))],
        out_shape=[jax.ShapeDtypeStruct((n, nout - gate_w), F32),
                   jax.ShapeDtypeStruct((n, gate_w), BF16)],
        compiler_params=_cparams(("parallel", "arbitrary")),
        name="proj_in",
    )(*x_parts, mod, gain.reshape(1, d), w, w)


def _rope_pairs(x, cos, sin):
    lane = lax.broadcasted_iota(jnp.int32, x.shape, 1)
    partner = jnp.where(lane % 2 == 0, pltpu.roll(x, LANES - 1, 1), pltpu.roll(x, 1, 1))
    return x * cos + partner * sin


def _ret_rope(x, cos, sin):
    return jnp.concatenate([_rope_pairs(x[:, j * LANES:(j + 1) * LANES], cos, sin)
                            for j in range(x.shape[1] // LANES)], axis=1)


def _ret_kernel(dec_ref, q_ref, g_ref, kl_ref, kc_ref, vl_ref, vc_ref, cos_ref, sin_ref, gn_ref, o_ref,
                kr_ref, vs_ref, sf_ref, sb_ref, sball_ref, decay_ref, *, ncc, nlc):
    s = pl.program_id(1)
    L = RET_L
    nc = ncc + nlc
    pos = lax.broadcasted_iota(jnp.int32, (L, 1), 0).astype(F32)

    def log_g(direction, h):
        return -jnp.exp(jnp.full((1, 1), dec_ref[direction, h], F32))

    def chunk(c):
        if isinstance(c, int):
            return pl.ds(c * L, L)
        return pl.ds(pl.multiple_of(c * L, L), L)

    def table(c):
        if isinstance(c, int):
            return chunk(nlc + c if c < ncc else c - ncc)
        return chunk(jnp.where(c < ncc, nlc + c, c - ncc))

    def head(x, h, width):
        return x[:, h * width:(h + 1) * width]

    @pl.when(s == 0)
    def _():
        def stage(c, k, v):
            kr_ref[chunk(c), :] = _ret_rope(k, cos_ref[table(c), :], sin_ref[table(c), :]) * (RET_QK_DIM ** -0.5)
            vs_ref[chunk(c), :] = v.astype(vs_ref.dtype)

        for c in range(ncc):
            stage(c, kc_ref[c * L:(c + 1) * L, :], vc_ref[c * L:(c + 1) * L, :])

        def stage_latent(j, carry):
            stage(ncc + j, kl_ref[chunk(j), :], vl_ref[chunk(j), :])
            return carry

        lax.fori_loop(0, nlc, stage_latent, 0)

        diff = (lax.broadcasted_iota(jnp.int32, (L, L), 0)
                - lax.broadcasted_iota(jnp.int32, (L, L), 1)).astype(F32)
        for h in range(RET_HEADS):
            decay_ref[h] = (jnp.where(diff >= 0, jnp.exp(jnp.maximum(diff, 0.0) * log_g(0, h)), 0.0)
                            + jnp.where(diff <= 0, jnp.exp(jnp.maximum(-diff, 0.0) * log_g(1, h)), 0.0))

        sf_ref[...] = jnp.zeros_like(sf_ref)
        sb_ref[...] = jnp.zeros_like(sb_ref)

        def visit(c, carry):
            k = kr_ref[chunk(c), :]
            v = vs_ref[chunk(c), :]
            for h in range(RET_HEADS):
                lgb = log_g(1, h)
                st = sb_ref[h]
                sball_ref[c, h] = st
                kz = head(k, h, RET_QK_DIM) * jnp.exp(pos * lgb)
                sb_ref[h] = jnp.exp(L * lgb) * st + _dot_tn(kz, head(v, h, RET_V_DIM))
            return carry

        for c in reversed(range(ncc)):
            visit(c, 0)
        lax.fori_loop(0, nlc, lambda j, carry: visit(nc - 1 - j, carry), 0)

    q = _ret_rope(q_ref[...], cos_ref[table(s), :], sin_ref[table(s), :])
    k = kr_ref[chunk(s), :]
    v = vs_ref[chunk(s), :]
    g = g_ref[...]
    for h in range(RET_HEADS):
        lgf = log_g(0, h)
        lgb = log_g(1, h)
        qh = head(q, h, RET_QK_DIM)
        kh = head(k, h, RET_QK_DIM)
        vh = head(v, h, RET_V_DIM)
        st_f = sf_ref[h]
        st_b = sball_ref[s, h]
        o = _dot(_dot_nt(qh, kh) * decay_ref[h], vh)
        o = o + _dot(qh, st_f) * jnp.exp((pos + 1.0) * lgf)
        o = o + _dot(qh, st_b) * jnp.exp((L - pos) * lgb)
        sf_ref[h] = jnp.exp(L * lgf) * st_f + _dot_tn(kh * jnp.exp((L - 1.0 - pos) * lgf), vh)
        mu = jnp.mean(o, axis=-1, keepdims=True)
        var = jnp.mean(jnp.square(o - mu), axis=-1, keepdims=True)
        o = (o - mu) * lax.rsqrt(var + GN_EPS) * head(gn_ref[...], h, RET_V_DIM)
        o_ref[:, h * RET_V_DIM:(h + 1) * RET_V_DIM] = (_silu(head(g, h, RET_V_DIM)) * o).astype(o_ref.dtype)


def retention(dm, p, dec, cos_t, sin_t, gn):
    L = RET_L
    ncc, nlc = dm.ctx // L, dm.s // L
    nc = ncc + nlc
    lat_blocks = dm.n_lat // L
    ctx_blk0 = dm.n_lat // dm.ctx
    qk_w = RET_HEADS * RET_QK_DIM
    v_w = RET_HEADS * RET_V_DIM

    def rows(b, s):
        return jnp.where(s < ncc, lat_blocks + b * ncc + s, b * nlc + s - ncc)

    whole = lambda shape: pl.BlockSpec(shape, lambda b, s, d: (0, 0))
    grid_spec = pltpu.PrefetchScalarGridSpec(
        num_scalar_prefetch=1,
        grid=(dm.b, nc),
        in_specs=[pl.BlockSpec((L, qk_w), lambda b, s, d: (rows(b, s), 0)),
                  pl.BlockSpec((L, v_w), lambda b, s, d: (rows(b, s), 2)),
                  pl.BlockSpec((dm.s, qk_w), lambda b, s, d: (b, 1)),
                  pl.BlockSpec((dm.ctx, qk_w), lambda b, s, d: (ctx_blk0 + b, 1)),
                  pl.BlockSpec((dm.s, v_w), lambda b, s, d: (b, 1)),
                  pl.BlockSpec((dm.ctx, v_w), lambda b, s, d: (ctx_blk0 + b, 1)),
                  whole(cos_t.shape), whole(sin_t.shape), whole((1, v_w))],
        out_specs=pl.BlockSpec((L, v_w), lambda b, s, d: (rows(b, s), 0)),
        scratch_shapes=[pltpu.VMEM((nc * L, qk_w), F32),
                        pltpu.VMEM((nc * L, v_w), BF16),
                        pltpu.VMEM((RET_HEADS, RET_QK_DIM, RET_V_DIM), F32),
                        pltpu.VMEM((RET_HEADS, RET_QK_DIM, RET_V_DIM), F32),
                        pltpu.VMEM((nc, RET_HEADS, RET_QK_DIM, RET_V_DIM), F32),
                        pltpu.VMEM((RET_HEADS, L, L), F32)],
    )
    return pl.pallas_call(
        functools.partial(_ret_kernel, ncc=ncc, nlc=nlc),
        grid_spec=grid_spec,
        out_shape=jax.ShapeDtypeStruct((dm.n_all, v_w), BF16),
        compiler_params=_cparams(("parallel", "arbitrary")),
        name="retention",
    )(dec, p, p, p, p, p, p, cos_t, sin_t, gn.reshape(1, v_w))


def _att_prep_kernel(q_ref, k_ref, v_ref, cos_ref, sin_ref, gq_ref, gk_ref, qo_ref, ko_ref, vo_ref,
                     *, lat_tiles):
    is_lat = pl.program_id(0) < lat_tiles
    cos = jnp.where(is_lat, cos_ref[...], 1.0)
    sin = jnp.where(is_lat, sin_ref[...], 0.0)

    hd = ATT_HEAD_DIM

    def prep(x_ref, gain_ref, h):
        xh = x_ref[:, h * hd:(h + 1) * hd]
        xn = xh * lax.rsqrt(jnp.mean(xh * xh, axis=-1, keepdims=True) + NORM_EPS) * gain_ref[...]
        return _rope_pairs(xn, cos, sin)

    for h in range(ATT_HEADS):
        qo_ref[:, h * hd:(h + 1) * hd] = (prep(q_ref, gq_ref, h) * ATT_SCORE_SCALE).astype(qo_ref.dtype)
    ones = jnp.ones((ATT_V_ROWS - hd, vo_ref.shape[1]), vo_ref.dtype)
    for h in range(ATT_KV_HEADS):
        ko_ref[:, h * hd:(h + 1) * hd] = prep(k_ref, gk_ref, h).astype(ko_ref.dtype)
        vo_ref[h * ATT_V_ROWS:h * ATT_V_ROWS + hd, :] = v_ref[:, h * hd:(h + 1) * hd].T.astype(vo_ref.dtype)
        vo_ref[h * ATT_V_ROWS + hd:(h + 1) * ATT_V_ROWS, :] = ones


def att_prep(dm, p, cos_t, sin_t, gq, gk):
    tm = dm.tile(TM_PREP)
    n = dm.n_all
    lat_tiles, per_batch = dm.n_lat // tm, dm.s // tm
    qw = ATT_HEADS * ATT_HEAD_DIM
    kw = ATT_KV_HEADS * ATT_HEAD_DIM
    q_blk, k_blk, v_blk = ATT_Q_COL // qw, ATT_K_COL // kw, ATT_V_COL // kw
    tab = lambda i: (jnp.where(i < lat_tiles, i % per_batch, 0), 0)
    return pl.pallas_call(
        functools.partial(_att_prep_kernel, lat_tiles=lat_tiles),
        grid=(n // tm,),
        in_specs=[pl.BlockSpec((tm, qw), lambda i: (i, q_blk)),
                  pl.BlockSpec((tm, kw), lambda i: (i, k_blk)),
                  pl.BlockSpec((tm, kw), lambda i: (i, v_blk)),
                  pl.BlockSpec((tm, ATT_HEAD_DIM), tab),
                  pl.BlockSpec((tm, ATT_HEAD_DIM), tab),
                  pl.BlockSpec((1, ATT_HEAD_DIM), lambda i: (0, 0)),
                  pl.BlockSpec((1, ATT_HEAD_DIM), lambda i: (0, 0))],
        out_specs=[pl.BlockSpec((tm, qw), lambda i: (i, 0)),
                   pl.BlockSpec((tm, kw), lambda i: (i, 0)),
                   pl.BlockSpec((ATT_KV_HEADS * ATT_V_ROWS, tm), lambda i: (0, i))],
        out_shape=[jax.ShapeDtypeStruct((n, qw), BF16),
                   jax.ShapeDtypeStruct((n, kw), BF16),
                   jax.ShapeDtypeStruct((ATT_KV_HEADS * ATT_V_ROWS, n), BF16)],
        compiler_params=_cparams(("parallel",)),
        name="att_prep",
    )(p, p, p, cos_t, sin_t, gq.reshape(1, -1), gk.reshape(1, -1))


def _attn_kernel(*refs, with_latent_keys):
    ns = ATT_STAGES
    if with_latent_keys:
        qt_ref, kl_ref, vtl_ref, kc_ref, vtc_ref, o_ref, acc_ref = refs[:7]
    else:
        qt_ref, kc_ref, vtc_ref, o_ref, acc_ref = refs[:5]
    s_refs, p_refs = refs[-2 * ns:-ns], refs[-ns:]
    hd = ATT_HEAD_DIM
    grp = ATT_HEADS // ATT_KV_HEADS
    tq = qt_ref.shape[0]
    nq = grp * tq
    tk = kc_ref.shape[0]
    n = 1 + (kl_ref.shape[0] // tk if with_latent_keys else 0)
    q2 = jnp.concatenate([qt_ref[:, j * hd:(j + 1) * hd] for j in range(grp)], axis=0)

    def key_chunk(j):
        if isinstance(j, int) and j == 0:
            return kc_ref[...]
        return kl_ref[pl.ds(pl.multiple_of((j - 1) * tk, tk), tk), :]

    def value_chunk(j):
        if isinstance(j, int) and j == 0:
            return vtc_ref[...]
        return vtl_ref[:, pl.ds(pl.multiple_of((j - 1) * tk, tk), tk)]

    def max_keys(x):
        groups = math.gcd(tk // SUBLANES, ATT_REDUCE_GROUPS)
        part = jnp.max(x.reshape(groups, tk // groups, nq), axis=0)
        return jnp.max(part, axis=0, keepdims=True)

    def time_step(t, slot, carry, do_scores=True, do_softmax=True, do_values=True):
        m, alpha = carry
        if do_scores:
            s_refs[slot][...] = _dot_nt(key_chunk(t), q2)
        if do_values:
            pv = jnp.dot(value_chunk(t - 2), p_refs[(slot - 2) % ns][...], preferred_element_type=F32)
            acc_ref[...] = acc_ref[...] * alpha + pv
        if do_softmax:
            st = s_refs[(slot - 1) % ns][...]
            m_new = jnp.maximum(m, max_keys(st))
            p_refs[(slot - 1) % ns][...] = jnp.exp2(st - m_new).astype(BF16)
            alpha = jnp.exp2(m - m_new)
            m = m_new
        return m, alpha

    acc_ref[...] = jnp.zeros_like(acc_ref)
    carry = (jnp.full((1, nq), -1e30, F32), jnp.zeros((1, nq), F32))
    fill = min(3, n + 2)
    for t in range(fill):
        carry = time_step(t, t % ns, carry, t < n, 0 <= t - 1 < n, 0 <= t - 2 < n)
    steady = max(n - fill, 0)
    body_steps = ns * ATT_UNROLL

    def unrolled(i, cr):
        for k in range(body_steps):
            cr = time_step(fill + body_steps * i + k, (fill + k) % ns, cr)
        return cr

    if steady // body_steps:
        carry = lax.fori_loop(0, steady // body_steps, unrolled, carry)
    for t in range(fill + body_steps * (steady // body_steps), max(n, fill)):
        carry = time_step(t, t % ns, carry)
    for t in range(max(n, fill), n + 2):
        carry = time_step(t, t % ns, carry, False, t - 1 < n, True)
    out = (acc_ref[0:hd, :] / acc_ref[hd:hd + 1, :]).T
    for j in range(grp):
        o_ref[:, j * hd:(j + 1) * hd] = out[j * tq:(j + 1) * tq].astype(o_ref.dtype)


def attention(dm, qt, kn, vt, latent):
    hd = ATT_HEAD_DIM
    grp = ATT_HEADS // ATT_KV_HEADS
    tk = dm.ctx
    assert dm.s % tk == 0
    ctx_blk0 = dm.n_lat // dm.ctx
    kc_spec = pl.BlockSpec((tk, hd), lambda b, h, i: (ctx_blk0 + b, h))
    vtc_spec = pl.BlockSpec((ATT_V_ROWS, tk), lambda b, h, i: (h, ctx_blk0 + b))
    if latent:
        tq = TQ_ATT
        nq = dm.s // tq
        q_spec = pl.BlockSpec((tq, grp * hd), lambda b, h, i: (b * nq + i, h))
        kl_spec = pl.BlockSpec((dm.s, hd), lambda b, h, i: (b, h))
        vtl_spec = pl.BlockSpec((ATT_V_ROWS, dm.s), lambda b, h, i: (h, b))
        in_specs = [q_spec, kl_spec, vtl_spec, kc_spec, vtc_spec]
        args = (qt, kn, vt, kn, vt)
        rows = dm.n_lat
    else:
        tq, nq = dm.ctx, 1
        q_spec = pl.BlockSpec((tq, grp * hd), lambda b, h, i: (ctx_blk0 + b, h))
        in_specs = [q_spec, kc_spec, vtc_spec]
        args = (qt, kn, vt)
        rows = dm.b * dm.ctx
    return pl.pallas_call(
        functools.partial(_attn_kernel, with_latent_keys=latent),
        grid=(dm.b, ATT_KV_HEADS, nq),
        in_specs=in_specs,
        out_specs=pl.BlockSpec((tq, grp * hd), lambda b, h, i: (b * nq + i, h)),
        out_shape=jax.ShapeDtypeStruct((rows, ATT_HEADS * hd), BF16),
        scratch_shapes=([pltpu.VMEM((ATT_V_ROWS, grp * tq), F32)]
                        + [pltpu.VMEM((tk, grp * tq), F32)] * ATT_STAGES
                        + [pltpu.VMEM((tk, grp * tq), BF16)] * ATT_STAGES),
        compiler_params=_cparams(("parallel", "parallel", "arbitrary")),
        name="attention_latent" if latent else "attention_ctx",
    )(*args)


def _pool_kernel(prev_ref, u_ref, next_ref, w_ref, sc_ref, o_ref, a_ref, b_ref, *, dm):
    tm = u_ref.shape[0]
    H = POOL_HALO
    i = pl.program_id(0)
    lat_tiles = dm.n_lat // tm
    is_lat = i < lat_tiles
    n_seg = jnp.where(is_lat, dm.s, dm.ctx)
    base = jnp.where(is_lat, (i * tm) % dm.s, 0)
    r = lax.broadcasted_iota(jnp.int32, (tm + 2 * H, 1), 0) - H
    pos = jnp.where(is_lat, base + r, r % dm.ctx)
    win = pl.ds(H, tm + 2 * H)

    def shifted(ref, d):
        return ref[pl.ds(H + d, tm + 2 * H), :]

    zeros = jnp.zeros((H, POOL_GROUP_DIM), F32)
    for ref in (a_ref, b_ref):
        ref[pl.ds(0, H), :] = zeros
        ref[pl.ds(tm + 3 * H, H), :] = zeros

    for gi, w in enumerate(POOL_WINDOWS):
        cols = slice(gi * POOL_GROUP_DIM, (gi + 1) * POOL_GROUP_DIM)
        half = w // 2
        a_ref[pl.ds(H, H), :] = prev_ref[:, cols]
        a_ref[pl.ds(2 * H, tm), :] = u_ref[:, cols]
        a_ref[pl.ds(2 * H + tm, H), :] = next_ref[:, cols]
        b_ref[win, :] = jnp.where(pos >= 1, shifted(a_ref, -1), 0.0)
        step = 1
        while step < half:
            b_ref[win, :] = b_ref[win, :] + jnp.where(pos >= step, shifted(b_ref, -step), 0.0)
            step *= 2
        step = 1
        while step < half:
            a_ref[win, :] = a_ref[win, :] + jnp.where(pos + step < n_seg, shifted(a_ref, step), 0.0)
            step *= 2
        tile = pl.ds(2 * H, tm)
        pt = pos[H:H + tm]
        cnt = (jnp.minimum(pt + half, n_seg) - jnp.maximum(pt - half, 0)).astype(F32)
        u = u_ref[:, cols]
        diff = (a_ref[tile, :] + b_ref[tile, :]) / cnt - u
        o_ref[:, cols] = (_dot(diff, w_ref[gi]) * sc_ref[:, cols]).astype(o_ref.dtype)


def pool_mixer(dm, p, pool_w, pool_scale):
    tm = dm.tile(TM_POOL)
    n = dm.n_all
    H = POOL_HALO
    width = len(POOL_WINDOWS) * POOL_GROUP_DIM
    col = 2560 // width
    per = tm // H
    last = n // H - 1
    return pl.pallas_call(
        functools.partial(_pool_kernel, dm=dm),
        grid=(n // tm,),
        in_specs=[pl.BlockSpec((H, width), lambda i: (jnp.maximum(i * per - 1, 0), col)),
                  pl.BlockSpec((tm, width), lambda i: (i, col)),
                  pl.BlockSpec((H, width), lambda i: (jnp.minimum((i + 1) * per, last), col)),
                  pl.BlockSpec(pool_w.shape, lambda i: (0, 0, 0)),
                  pl.BlockSpec((1, width), lambda i: (0, 0))],
        out_specs=pl.BlockSpec((tm, width), lambda i: (i, 0)),
        out_shape=jax.ShapeDtypeStruct((n, width), BF16),
        scratch_shapes=[pltpu.VMEM((tm + 4 * H, POOL_GROUP_DIM), F32),
                        pltpu.VMEM((tm + 4 * H, POOL_GROUP_DIM), F32)],
        compiler_params=_cparams(("parallel",)),
        name="pool_mixer",
    )(p, p, p, pool_w, pool_scale.reshape(1, width))


def _merge_kernel(yr_ref, ya_ref, yp_ref, gl_ref, *refs, lat_tiles):
    mod_ref, wb_ref, wo_ref, o_ref = refs[-4:]
    x_refs = refs[:-4]
    d = o_ref.shape[1]
    mixed = None
    for j, y_ref in enumerate((yr_ref, ya_ref, yp_ref)):
        gate = gl_ref[:, j * d:(j + 1) * d].astype(F32)
        term = gate * jnp.dot(y_ref[...], wb_ref[j], preferred_element_type=F32)
        mixed = term if mixed is None else mixed + term
    y = mod_ref[0][2:3] * _dot(mixed, wo_ref[...])

    def finish(x_ref):
        o_ref[...] = x_ref[...] + y

    _for_row_source(x_refs, lat_tiles, finish)


def merge(dm, n_rows, yr, ya, yp, gates, x_parts, mod, wb, wo):
    tm = dm.tile(TM_MERGE)
    d = dm.d
    bw = yr.shape[1]
    row = dm.mod_row(tm)
    return pl.pallas_call(
        functools.partial(_merge_kernel, lat_tiles=dm.n_lat // tm),
        grid=(n_rows // tm,),
        in_specs=[pl.BlockSpec((tm, bw), lambda i: (i, 0)),
                  pl.BlockSpec((tm, bw), lambda i: (i, 0)),
                  pl.BlockSpec((tm, bw), lambda i: (i, 0)),
                  pl.BlockSpec((tm, 3 * d), lambda i: (i, 0))]
        + _row_sources(dm, tm, x_parts) + [
            pl.BlockSpec((1, 6, d), lambda i: (row(i), 0, 0)),
            pl.BlockSpec(wb.shape, lambda i: (0, 0, 0)),
            pl.BlockSpec(wo.shape, lambda i: (0, 0))],
        out_specs=pl.BlockSpec((tm, d), lambda i: (i, 0)),
        out_shape=jax.ShapeDtypeStruct((n_rows, d), F32),
        compiler_params=_cparams(("parallel",)),
        name="merge",
    )(yr, ya, yp, gates, *x_parts, mod, wb, wo)


def _ffn_kernel(x_ref, mod_ref, gain_ref, w1_ref, w3_ref, w2_ref, o_ref):
    m = mod_ref[0]
    h = _norm_mod(x_ref[...], gain_ref[...], m[3:4], m[4:5]).astype(BF16)
    fdim = w1_ref.shape[1]
    y = None
    for lo in range(0, fdim, MOE_COL_CHUNK):
        cols = slice(lo, min(lo + MOE_COL_CHUNK, fdim))
        a = jnp.dot(h, w1_ref[:, cols], preferred_element_type=F32)
        b = jnp.dot(h, w3_ref[:, cols], preferred_element_type=F32)
        part = _dot(_silu(a) * b, w2_ref[cols, :])
        y = part if y is None else y + part
    o_ref[...] = x_ref[...] + m[5:6] * y


def ffn_dense(dm, x, mod, gain, w1, w3, w2):
    n, d = x.shape
    fdim = w1.shape[1]
    tm = dm.tile(TM_FFN)
    row = dm.mod_row(tm)
    once = dict(pipeline_mode=pl.Buffered(1))
    return pl.pallas_call(
        _ffn_kernel,
        grid=(n // tm,),
        in_specs=[pl.BlockSpec((tm, d), lambda i: (i, 0)),
                  pl.BlockSpec((1, 6, d), lambda i: (row(i), 0, 0)),
                  pl.BlockSpec((1, d), lambda i: (0, 0)),
                  pl.BlockSpec((d, fdim), lambda i: (0, 0), **once),
                  pl.BlockSpec((d, fdim), lambda i: (0, 0), **once),
                  pl.BlockSpec((fdim, d), lambda i: (0, 0), **once)],
        out_specs=pl.BlockSpec((tm, d), lambda i: (i, 0)),
        out_shape=jax.ShapeDtypeStruct((n, d), F32),
        compiler_params=_cparams(("parallel",)),
        name="ffn_dense",
    )(x, mod, gain.reshape(1, d), w1, w3, w2)


def _router_kernel(x_ref, mod_ref, gain_ref, wr_ref, br_ref, h_ref, slot_ref, slot_t_ref, wt_ref, cum_ref):
    tm = x_ref.shape[0]
    m = mod_ref[0]
    h = _norm_mod(x_ref[...], gain_ref[...], m[3:4], m[4:5])
    h_ref[...] = h.astype(BF16)
    logits = _dot3(h, wr_ref[...]) + br_ref[...]
    lane = lax.broadcasted_iota(jnp.int32, logits.shape, 1)
    m1 = jnp.max(logits, axis=-1, keepdims=True)
    i1 = jnp.min(jnp.where(logits == m1, lane, LANES), axis=-1, keepdims=True)
    rest = jnp.where(lane == i1, -jnp.inf, logits)
    m2 = jnp.max(rest, axis=-1, keepdims=True)
    i2 = jnp.min(jnp.where(rest == m2, lane, LANES), axis=-1, keepdims=True)
    e = jnp.exp(m2 - m1)
    w1 = 1.0 / (1.0 + e)
    w2 = e / (1.0 + e)
    wt_ref[...] = jnp.where(lane == i1, w1, jnp.where(lane == i2, w2, 0.0))
    routed = jnp.where((lane == i1) | (lane == i2), 1.0, 0.0)
    sub = SUB_MOE
    lower = (lax.broadcasted_iota(jnp.int32, (sub, sub), 0) > lax.broadcasted_iota(jnp.int32, (sub, sub), 1))
    lower = lower.astype(BF16)
    offset = jnp.zeros((1, LANES), F32)
    cum_ref[...] = jnp.zeros_like(cum_ref)
    for blk in range(tm // sub):
        rows = slice(blk * sub, (blk + 1) * sub)
        mask = routed[rows]
        cum_ref[0, blk:blk + 1, :] = offset.astype(jnp.int32)
        rank = jnp.dot(lower, mask.astype(BF16), preferred_element_type=F32) + offset
        slot_ref[rows, :] = jnp.where(mask > 0.0, rank, -1.0)
        count = jnp.sum(mask, axis=0, keepdims=True)
        offset = offset + jnp.ceil(count * (1.0 / SLOT_ALIGN)) * SLOT_ALIGN
    cum_ref[0, tm // sub:tm // sub + 1, :] = offset.astype(jnp.int32)
    slot_t_ref[...] = slot_ref[...].T[:SUBLANES, :]


def moe_route(dm, n_rows, x, mod, gain, wr_pad, br_pad):
    tm = dm.tile(TM_MOE)
    d = dm.d
    row = dm.mod_row(tm)
    nt = n_rows // tm
    cum_rows = -(-(tm // SUB_MOE + 1) // SUBLANES) * SUBLANES
    return pl.pallas_call(
        _router_kernel,
        grid=(nt,),
        in_specs=[pl.BlockSpec((tm, d), lambda i: (i, 0)),
                  pl.BlockSpec((1, 6, d), lambda i: (row(i), 0, 0)),
                  pl.BlockSpec((1, d), lambda i: (0, 0)),
                  pl.BlockSpec((d, LANES), lambda i: (0, 0)),
                  pl.BlockSpec((1, LANES), lambda i: (0, 0))],
        out_specs=[pl.BlockSpec((tm, d), lambda i: (i, 0)),
                   pl.BlockSpec((tm, LANES), lambda i: (i, 0)),
                   pl.BlockSpec((SUBLANES, tm), lambda i: (0, i)),
                   pl.BlockSpec((tm, LANES), lambda i: (i, 0)),
                   pl.BlockSpec((1, cum_rows, LANES), lambda i: (i, 0, 0))],
        out_shape=[jax.ShapeDtypeStruct((n_rows, d), BF16),
                   jax.ShapeDtypeStruct((n_rows, LANES), F32),
                   jax.ShapeDtypeStruct((SUBLANES, n_rows), F32),
                   jax.ShapeDtypeStruct((n_rows, LANES), F32),
                   jax.ShapeDtypeStruct((nt, cum_rows, LANES), jnp.int32)],
        compiler_params=_cparams(("parallel",)),
        name="moe_router",
    )(x, mod, gain.reshape(1, d), wr_pad, br_pad)


def _moe_windows(base_ref, row0_ref, e, step=None):
    i, j = pl.program_id(0), pl.program_id(1)
    if step is not None:
        i, j = step // pl.num_programs(1), step % pl.num_programs(1)
    start = base_ref[i, j * N_EXPERTS + e]
    nwin = (base_ref[i, (j + 1) * N_EXPERTS + e] - start + WIN_MOE - 1) // WIN_MOE
    return start, nwin, row0_ref[i, e]


def _moe_gather_kernel(base_ref, row0_ref, h_ref, slot_t_ref, xs_in_ref, xs_ref, buf_ref, sem):
    del xs_in_ref
    win, sub = WIN_MOE, SUB_MOE
    spare = N_EXPERTS

    def gathered(e, start, w):
        want = (lax.broadcasted_iota(jnp.int32, (win, sub), 0) + (start + w * win)).astype(F32)
        onehot = (slot_t_ref[e:e + 1, :] == want).astype(BF16)
        return jnp.dot(onehot, h_ref[...], preferred_element_type=F32).astype(BF16)

    def copy(e, buf, w):
        start, _, row0 = _moe_windows(base_ref, row0_ref, e)
        rows = pl.ds(pl.multiple_of(row0 + start + w * win, SLOT_ALIGN), win)
        return pltpu.make_async_copy(buf_ref.at[buf], xs_ref.at[rows, :], sem.at[buf])

    for e in range(N_EXPERTS):
        start, _, _ = _moe_windows(base_ref, row0_ref, e)
        buf_ref[e] = gathered(e, start, 0)
        copy(e, e, 0).start()

    for e in range(N_EXPERTS):
        start, nwin, _ = _moe_windows(base_ref, row0_ref, e)

        def extra(w, carry, e=e, start=start):
            buf_ref[spare] = gathered(e, start, w)
            cp = copy(e, spare, w)
            cp.start()
            cp.wait()
            return carry

        lax.fori_loop(1, jnp.maximum(nwin, 1), extra, 0)

    for e in range(N_EXPERTS):
        copy(e, e, 0).wait()


def moe_gather(dm, n_rows, h, slot_t, base, row0, cap):
    tm, sub, d = dm.tile(TM_MOE), SUB_MOE, dm.d
    nsub = tm // sub
    grid_spec = pltpu.PrefetchScalarGridSpec(
        num_scalar_prefetch=2,
        grid=(n_rows // tm, nsub),
        in_specs=[pl.BlockSpec((sub, d), lambda i, j, b, r: (i * nsub + j, 0)),
                  pl.BlockSpec((SUBLANES, sub), lambda i, j, b, r: (0, i * nsub + j)),
                  pl.BlockSpec(memory_space=pl.ANY)],
        out_specs=pl.BlockSpec(memory_space=pl.ANY),
        scratch_shapes=[pltpu.VMEM((N_EXPERTS + 1, WIN_MOE, d), BF16),
                        pltpu.SemaphoreType.DMA((N_EXPERTS + 1,))],
    )
    return pl.pallas_call(
        _moe_gather_kernel,
        grid_spec=grid_spec,
        out_shape=jax.ShapeDtypeStruct((cap, d), BF16),
        input_output_aliases={4: 0},
        compiler_params=_cparams(("arbitrary", "arbitrary")),
        name="moe_gather",
    )(base, row0, h, slot_t, jnp.zeros((cap, d), BF16))


def _moe_ffn_kernel(bexp_ref, bvalid_ref, xs_ref, w1_ref, w3_ref, w2_ref, o_ref):
    del bexp_ref
    valid = bvalid_ref[pl.program_id(0)] == 1

    @pl.when(valid)
    def _():
        xs = xs_ref[...]
        fdim = w1_ref.shape[2]
        y = None
        for lo in range(0, fdim, MOE_COL_CHUNK):
            cols = slice(lo, min(lo + MOE_COL_CHUNK, fdim))
            a = jnp.dot(xs, w1_ref[0, :, cols], preferred_element_type=F32)
            b = jnp.dot(xs, w3_ref[0, :, cols], preferred_element_type=F32)
            part = _dot(_silu(a) * b, w2_ref[0, cols, :])
            y = part if y is None else y + part
        o_ref[...] = y.astype(o_ref.dtype)

    @pl.when(jnp.logical_not(valid))
    def _():
        o_ref[...] = jnp.zeros_like(o_ref)


def moe_ffn(xs, blk_expert, blk_valid, w1, w3, w2):
    cap, d = xs.shape
    _, _, fdim = w1.shape
    blk = BLK_MOE
    once = dict(pipeline_mode=pl.Buffered(1))
    grid_spec = pltpu.PrefetchScalarGridSpec(
        num_scalar_prefetch=2,
        grid=(cap // blk,),
        in_specs=[pl.BlockSpec((blk, d), lambda k, be, bv: (k, 0)),
                  pl.BlockSpec((1, d, fdim), lambda k, be, bv: (be[k], 0, 0), **once),
                  pl.BlockSpec((1, d, fdim), lambda k, be, bv: (be[k], 0, 0), **once),
                  pl.BlockSpec((1, fdim, d), lambda k, be, bv: (be[k], 0, 0), **once)],
        out_specs=pl.BlockSpec((blk, d), lambda k, be, bv: (k, 0)),
    )
    return pl.pallas_call(
        _moe_ffn_kernel,
        grid_spec=grid_spec,
        out_shape=jax.ShapeDtypeStruct((cap, d), BF16),
        compiler_params=_cparams(("arbitrary",)),
        name="moe_ffn",
    )(blk_expert, blk_valid, xs, w1, w3, w2)


def _moe_scatter_kernel(base_ref, row0_ref, slot_ref, wt_ref, x_ref, mod_ref, gain_ref, ys_ref, o_ref,
                        buf_ref, y_ref, sem, *, final):
    win, sub = WIN_MOE, SUB_MOE
    nbuf = N_EXPERTS + 1
    step = pl.program_id(0) * pl.num_programs(1) + pl.program_id(1)
    last_step = pl.num_programs(0) * pl.num_programs(1) - 1
    mine = (step % 2) * nbuf
    spare = mine + N_EXPERTS

    def copy(e, buf, w, at=None):
        start, _, row0 = _moe_windows(base_ref, row0_ref, e, at)
        rows = pl.ds(pl.multiple_of(row0 + start + w * win, SLOT_ALIGN), win)
        return pltpu.make_async_copy(ys_ref.at[rows, :], buf_ref.at[buf], sem.at[buf])

    @pl.when(step == 0)
    def _():
        for e in range(N_EXPERTS):
            copy(e, e, 0).start()

    @pl.when(step < last_step)
    def _():
        for e in range(N_EXPERTS):
            copy(e, nbuf - mine + e, 0, at=step + 1).start()

    def scattered(e, buf, w):
        start, _, _ = _moe_windows(base_ref, row0_ref, e)
        want = (lax.broadcasted_iota(jnp.int32, (sub, win), 1) + (start + w * win)).astype(F32)
        onehot = (slot_ref[:, e:e + 1] == want).astype(BF16)
        return wt_ref[:, e:e + 1] * jnp.dot(onehot, buf_ref[buf], preferred_element_type=F32)

    y_ref[...] = jnp.zeros_like(y_ref)
    for e in range(N_EXPERTS):
        _, nwin, _ = _moe_windows(base_ref, row0_ref, e)

        def extra(w, carry, e=e):
            cp = copy(e, spare, w)
            cp.start()
            cp.wait()
            y_ref[...] += scattered(e, spare, w)
            return carry

        lax.fori_loop(1, jnp.maximum(nwin, 1), extra, 0)

    y = y_ref[...]
    for e in range(N_EXPERTS):
        copy(e, mine + e, 0).wait()
        y = y + scattered(e, mine + e, 0)
    x = x_ref[...] + mod_ref[0][5:6] * y
    if final:
        x = x * lax.rsqrt(jnp.mean(x * x, axis=-1, keepdims=True) + NORM_EPS) * gain_ref[...]
    o_ref[...] = x


def moe_scatter(dm, n_rows, ys, slot, wt, base, row0, x, mod, final_gain):
    tm, sub, d = dm.tile(TM_MOE), SUB_MOE, dm.d
    nsub = tm // sub
    row = dm.mod_row(sub)
    final = final_gain is not None
    gain = final_gain if final else jnp.ones((d,), F32)
    tok = lambda i, j, b, r: (i * nsub + j, 0)
    grid_spec = pltpu.PrefetchScalarGridSpec(
        num_scalar_prefetch=2,
        grid=(n_rows // tm, nsub),
        in_specs=[pl.BlockSpec((sub, LANES), tok),
                  pl.BlockSpec((sub, LANES), tok),
                  pl.BlockSpec((sub, d), tok),
                  pl.BlockSpec((1, 6, d), lambda i, j, b, r: (row(i * nsub + j), 0, 0)),
                  pl.BlockSpec((1, d), lambda i, j, b, r: (0, 0)),
                  pl.BlockSpec(memory_space=pl.ANY)],
        out_specs=pl.BlockSpec((sub, d), tok),
        scratch_shapes=[pltpu.VMEM((2 * (N_EXPERTS + 1), WIN_MOE, d), BF16),
                        pltpu.VMEM((sub, d), F32),
                        pltpu.SemaphoreType.DMA((2 * (N_EXPERTS + 1),))],
    )
    return pl.pallas_call(
        functools.partial(_moe_scatter_kernel, final=final),
        grid_spec=grid_spec,
        out_shape=jax.ShapeDtypeStruct((n_rows, d), F32),
        compiler_params=_cparams(("arbitrary", "arbitrary")),
        name="moe_scatter",
    )(base, row0, slot, wt, x, mod, gain.reshape(1, d), ys)


def moe_layout(dm, n_rows, cum):
    tm = dm.tile(TM_MOE)
    nt, nsub = n_rows // tm, tm // SUB_MOE
    base = cum[:, :nsub + 1, :N_EXPERTS]
    total = base[:, nsub, :]
    per_expert = jnp.sum(total, axis=0)
    padded = (per_expert + WIN_MOE + BLK_MOE - 1) // BLK_MOE * BLK_MOE
    expert_row0 = jnp.cumsum(padded) - padded
    row0 = expert_row0[None, :] + jnp.cumsum(total, axis=0) - total
    slack = nt * nsub * N_EXPERTS * (SLOT_ALIGN - 1) + N_EXPERTS * (WIN_MOE + BLK_MOE)
    nblk = -(-(2 * n_rows + slack) // BLK_MOE)
    ends = jnp.cumsum(padded // BLK_MOE)
    k = jnp.arange(nblk, dtype=jnp.int32)
    blk_expert = jnp.minimum(jnp.searchsorted(ends, k, side="right"), N_EXPERTS - 1).astype(jnp.int32)
    blk_valid = (k < ends[-1]).astype(jnp.int32)
    return base.reshape(nt, (nsub + 1) * N_EXPERTS), row0.astype(jnp.int32), blk_expert, blk_valid, nblk * BLK_MOE


def _final_norm_kernel(x_ref, gain_ref, o_ref):
    x = x_ref[...]
    o_ref[...] = x * lax.rsqrt(jnp.mean(x * x, axis=-1, keepdims=True) + NORM_EPS) * gain_ref[...]


def final_rmsnorm(dm, n_rows, x, gain):
    tm = dm.tile(TM_FFN)
    d = dm.d
    return pl.pallas_call(
        _final_norm_kernel,
        grid=(n_rows // tm,),
        in_specs=[pl.BlockSpec((tm, d), lambda i: (i, 0)), pl.BlockSpec((1, d), lambda i: (0, 0))],
        out_specs=pl.BlockSpec((tm, d), lambda i: (i, 0)),
        out_shape=jax.ShapeDtypeStruct((n_rows, d), F32),
        compiler_params=_cparams(("parallel",)),
        name="final_norm",
    )(x, gain.reshape(1, d))


def _rope_tables(s, ctx):
    def angles(pos, dim):
        freqs = 1.0 / (ROPE_THETA ** (jnp.arange(0, dim, 2, dtype=F32) / dim))
        return pos[:, None] * freqs[None, :]

    rows = s // GRID_W
    row = jnp.repeat(jnp.arange(rows, dtype=F32), GRID_W)
    col = jnp.tile(jnp.arange(GRID_W, dtype=F32), rows)
    half = ATT_HEAD_DIM // 2
    ang_att = jnp.concatenate([angles(row, half), angles(col, half)], axis=-1)
    sign = jnp.tile(jnp.array([-1.0, 1.0], F32), LANES // 2)

    def tables(ang):
        reps = LANES // (2 * ang.shape[1])
        cos = jnp.tile(jnp.repeat(jnp.cos(ang), 2, axis=-1), (1, reps))
        sin = jnp.tile(jnp.repeat(jnp.sin(ang), 2, axis=-1), (1, reps)) * sign
        return cos, sin

    ang_ret = angles(jnp.arange(s, dtype=F32), RET_QK_DIM)
    ang_ret = jnp.concatenate([ang_ret, jnp.zeros((ctx, RET_QK_DIM // 2), F32)], axis=0)
    return tables(ang_att) + tables(ang_ret)


def kernel(x, c, ctx, c_ctx, w_ada, b_ada, norm_mix, norm_ffn, w_in, ret_decay, ret_gn, attn_qn, attn_kn, pool_w, pool_scale, w_branch, w_out, ffn_w1, ffn_w3, ffn_w2, moe_router, moe_router_b, moe_w1, moe_w3, moe_w2, final_norm):
    b, s, d = x.shape
    n_ctx = ctx.shape[1]
    depth = w_in.shape[0]
    dm = Dims(b, s, n_ctx, d)

    x_parts = (x.reshape(b * s, d), ctx.reshape(b * n_ctx, d))
    mod_rows = 2 * SUBLANES
    cc = jnp.concatenate([c, c_ctx[None, :], jnp.zeros((mod_rows - b - 1, d), F32)], axis=0)
    mods = ada_table(cc, w_ada, b_ada).reshape(depth, mod_rows, 6, d)

    att_cos, att_sin, ret_cos, ret_sin = _rope_tables(s, n_ctx)

    for i in range(depth):
        need_ctx = i < depth - 1
        n_rows = dm.n_all if need_ctx else dm.n_lat
        mod = mods[i]
        p, gates = proj_in(dm, x_parts, mod, norm_mix[i], w_in[i].astype(BF16))
        y_ret = retention(dm, p, ret_decay[i], ret_cos, ret_sin, ret_gn[i])
        y_pool = pool_mixer(dm, p, pool_w[i].astype(BF16), pool_scale[i])
        qt, kn, vt = att_prep(dm, p, att_cos, att_sin, attn_qn[i], attn_kn[i])
        y_att = attention(dm, qt, kn, vt, latent=True)
        if need_ctx:
            y_att = jnp.concatenate([y_att, attention(dm, qt, kn, vt, latent=False)], axis=0)
        xa = merge(dm, n_rows, y_ret, y_att, y_pool, gates, x_parts, mod,
                   w_branch[i].astype(BF16), w_out[i].astype(BF16))
        j = i // 2
        last = i == depth - 1
        if i % 2 == 0:
            xa = ffn_dense(dm, xa, mod, norm_ffn[i], ffn_w1[j].astype(BF16), ffn_w3[j].astype(BF16),
                           ffn_w2[j].astype(BF16))
            if last:
                xa = final_rmsnorm(dm, n_rows, xa, final_norm)
        else:
            wr = jnp.zeros((d, LANES), F32).at[:, :N_EXPERTS].set(moe_router[j])
            br = jnp.full((1, LANES), -1e30, F32).at[0, :N_EXPERTS].set(moe_router_b[j])
            h, slot, slot_t, wt, cnt = moe_route(dm, n_rows, xa, mod, norm_ffn[i], wr, br)
            base, row0, blk_expert, blk_valid, cap = moe_layout(dm, n_rows, cnt)
            xs = moe_gather(dm, n_rows, h, slot_t, base, row0, cap)
            ys = moe_ffn(xs, blk_expert, blk_valid,
                         moe_w1[j].astype(BF16), moe_w3[j].astype(BF16), moe_w2[j].astype(BF16))
            xa = moe_scatter(dm, n_rows, ys, slot, wt, base, row0, xa, mod, final_norm if last else None)
        x_parts = (xa,)
    return xa[:dm.n_lat].reshape(b, s, d)
```

```python
import functools
import math

import jax
import jax.numpy as jnp
from jax import lax
from jax.experimental import pallas as pl
from jax.experimental.pallas import tpu as pltpu

F32 = jnp.float32
BF16 = jnp.bfloat16

GRID_W = 64
RET_HEADS = 4
RET_QK_DIM = 64
RET_V_DIM = 128
ATT_HEADS = 4
ATT_KV_HEADS = 2
ATT_HEAD_DIM = 128
ROPE_THETA = 10000.0
POOL_WINDOWS = (2, 4, 8, 16)
POOL_GROUP_DIM = 128
N_EXPERTS = 8
NORM_EPS = 1e-6
GN_EPS = 1e-5

LANES = 128
SUBLANES = 8
VMEM_LIMIT_BYTES = 56 * 1024 * 1024

TM_PROJ = 1024
TN_PROJ = 1536
TM_PREP = 1024
TM_POOL = 1024
TM_MERGE = 1024
TM_FFN = 1024
TM_MOE = 2048
MOE_COL_CHUNK = 512
BLK_MOE = 512
WIN_MOE = 256
SUB_MOE = 512
SLOT_ALIGN = 16
RET_L = 256
TQ_ATT = 1024
ATT_Q_COL = 2 * RET_HEADS * RET_QK_DIM + 2 * RET_HEADS * RET_V_DIM
ATT_K_COL = ATT_Q_COL + ATT_HEADS * ATT_HEAD_DIM
ATT_V_COL = ATT_K_COL + ATT_KV_HEADS * ATT_HEAD_DIM
ATT_SCORE_SCALE = (ATT_HEAD_DIM ** -0.5) * math.log2(math.e)
ATT_V_ROWS = ATT_HEAD_DIM + 16
ATT_STAGES = 3
ATT_UNROLL = 4
ATT_REDUCE_GROUPS = 8
POOL_HALO = 8


def _cparams(sem):
    return pltpu.CompilerParams(dimension_semantics=sem, vmem_limit_bytes=VMEM_LIMIT_BYTES)


def _dot(a, b):
    return jnp.dot(a.astype(BF16), b.astype(BF16), preferred_element_type=F32)


def _dot_nt(a, b):
    return lax.dot_general(a.astype(BF16), b.astype(BF16), (((1,), (1,)), ((), ())),
                           preferred_element_type=F32)


def _dot_tn(a, b):
    return lax.dot_general(a.astype(BF16), b.astype(BF16), (((0,), (0,)), ((), ())),
                           preferred_element_type=F32)


def _split_bf16(a):
    hi = a.astype(BF16)
    lo = (a - hi.astype(F32)).astype(BF16)
    return hi, lo


def _dot3(a, b):
    ah, al = _split_bf16(a)
    bh, bl = _split_bf16(b)
    d = functools.partial(jnp.dot, preferred_element_type=F32)
    return d(ah, bh) + (d(al, bh) + d(ah, bl))


def _norm_mod(x, gain, shift, scale):
    y = x * lax.rsqrt(jnp.mean(x * x, axis=-1, keepdims=True) + NORM_EPS)
    return (y * gain) * (1.0 + scale) + shift


def _sigmoid(x):
    return 1.0 / (1.0 + jnp.exp(-x))


def _silu(x):
    return x * _sigmoid(x)


class Dims:
    def __init__(self, b, s, ctx, d):
        self.b, self.s, self.ctx, self.d = b, s, ctx, d
        self.n_lat = b * s
        self.n_all = b * s + b * ctx

    def tile(self, preferred):
        t = preferred
        while self.s % t or (self.b * self.ctx) % t:
            t //= 2
        return t

    def mod_row(self, tm):
        lat_tiles, per_batch, b = self.n_lat // tm, self.s // tm, self.b
        return lambda i: jnp.where(i < lat_tiles, i // per_batch, b)


def _ada_kernel(c_ref, w_ref, b_ref, o_ref):
    o_ref[0] = _dot3(_silu(c_ref[...]), w_ref[0]) + b_ref[0]


def ada_table(cc, w_ada, b_ada):
    depth, d, n6 = w_ada.shape
    tn = n6 // 4
    rows = cc.shape[0]
    return pl.pallas_call(
        _ada_kernel,
        grid=(depth, n6 // tn),
        in_specs=[pl.BlockSpec((rows, d), lambda l, j: (0, 0)),
                  pl.BlockSpec((1, d, tn), lambda l, j: (l, 0, j)),
                  pl.BlockSpec((1, 1, tn), lambda l, j: (l, 0, j))],
        out_specs=pl.BlockSpec((1, rows, tn), lambda l, j: (l, 0, j)),
        out_shape=jax.ShapeDtypeStruct((depth, rows, n6), F32),
        compiler_params=_cparams(("arbitrary", "arbitrary")),
        name="ada_table",
    )(cc, w_ada, b_ada.reshape(depth, 1, n6))


def _row_sources(dm, tm, parts):
    d = dm.d
    if len(parts) == 1:
        return [pl.BlockSpec((tm, d), lambda i, *_: (i, 0))]
    lat_tiles = dm.n_lat // tm
    return [pl.BlockSpec((tm, d), lambda i, *_: (jnp.minimum(i, lat_tiles - 1), 0)),
            pl.BlockSpec((tm, d), lambda i, *_: (jnp.maximum(i - lat_tiles, 0), 0))]


def _for_row_source(x_refs, lat_tiles, fn):
    if len(x_refs) == 1:
        fn(x_refs[0])
        return
    is_lat = pl.program_id(0) < lat_tiles
    pl.when(is_lat)(lambda: fn(x_refs[0]))
    pl.when(jnp.logical_not(is_lat))(lambda: fn(x_refs[1]))


def _proj_in_kernel(*refs, lat_tiles):
    mod_ref, gain_ref, wm_ref, wg_ref, o_ref, g_ref = refs[-6:]
    x_refs = refs[:-6]
    x = x_refs[0][...]
    if len(x_refs) == 2:
        x = jnp.where(pl.program_id(0) < lat_tiles, x, x_refs[1][...])
    m = mod_ref[0]
    h = _norm_mod(x, gain_ref[...], m[0:1], m[1:2]).astype(BF16)
    o_ref[...] = jnp.dot(h, wm_ref[...], preferred_element_type=F32)
    logits = jnp.dot(h, wg_ref[...], preferred_element_type=F32)
    g_ref[...] = (0.5 * jnp.tanh(0.5 * logits) + 0.5).astype(g_ref.dtype)


def proj_in(dm, x_parts, mod, gain, w):
    n, d = dm.n_all, dm.d
    nout = w.shape[1]
    gate_w = 3 * d
    tm, tn = dm.tile(TM_PROJ), TN_PROJ
    assert nout - gate_w == gate_w and gate_w % tn == 0
    gate_blk0 = gate_w // tn
    row = dm.mod_row(tm)
    return pl.pallas_call(
        functools.partial(_proj_in_kernel, lat_tiles=dm.n_lat // tm),
        grid=(n // tm, gate_w // tn),
        in_specs=_row_sources(dm, tm, x_parts) + [
            pl.BlockSpec((1, 6, d), lambda i, j: (row(i), 0, 0)),
            pl.BlockSpec((1, d), lambda i, j: (0, 0)),
            pl.BlockSpec((d, tn), lambda i, j: (0, j)),
            pl.BlockSpec((d, tn), lambda i, j: (0, gate_blk0 + j))],
        out_specs=[pl.BlockSpec((tm, tn), lambda i, j: (i, j)),
                   pl.BlockSpec((tm, tn), lambda i, j: (i, j))],
        out_shape=[jax.ShapeDtypeStruct((n, nout - gate_w), F32),
                   jax.ShapeDtypeStruct((n, gate_w), BF16)],
        compiler_params=_cparams(("parallel", "arbitrary")),
        name="proj_in",
    )(*x_parts, mod, gain.reshape(1, d), w, w)


def _rope_pairs(x, cos, sin):
    lane = lax.broadcasted_iota(jnp.int32, x.shape, 1)
    partner = jnp.where(lane % 2 == 0, pltpu.roll(x, LANES - 1, 1), pltpu.roll(x, 1, 1))
    return x * cos + partner * sin


def _ret_rope(x, cos, sin):
    return jnp.concatenate([_rope_pairs(x[:, j * LANES:(j + 1) * LANES], cos, sin)
                            for j in range(x.shape[1] // LANES)], axis=1)


def _ret_kernel(dec_ref, q_ref, g_ref, kl_ref, kc_ref, vl_ref, vc_ref, cos_ref, sin_ref, gn_ref, o_ref,
                kr_ref, vs_ref, sf_ref, sb_ref, sball_ref, decay_ref, *, ncc, nlc):
    s = pl.program_id(1)
    L = RET_L
    nc = ncc + nlc
    pos = lax.broadcasted_iota(jnp.int32, (L, 1), 0).astype(F32)

    def log_g(direction, h):
        return -jnp.exp(jnp.full((1, 1), dec_ref[direction, h], F32))

    def chunk(c):
        if isinstance(c, int):
            return pl.ds(c * L, L)
        return pl.ds(pl.multiple_of(c * L, L), L)

    def table(c):
        if isinstance(c, int):
            return chunk(nlc + c if c < ncc else c - ncc)
        return chunk(jnp.where(c < ncc, nlc + c, c - ncc))

    def head(x, h, width):
        return x[:, h * width:(h + 1) * width]

    @pl.when(s == 0)
    def _():
        def stage(c, k, v):
            kr_ref[chunk(c), :] = _ret_rope(k, cos_ref[table(c), :], sin_ref[table(c), :]) * (RET_QK_DIM ** -0.5)
            vs_ref[chunk(c), :] = v.astype(vs_ref.dtype)

        for c in range(ncc):
            stage(c, kc_ref[c * L:(c + 1) * L, :], vc_ref[c * L:(c + 1) * L, :])

        def stage_latent(j, carry):
            stage(ncc + j, kl_ref[chunk(j), :], vl_ref[chunk(j), :])
            return carry

        lax.fori_loop(0, nlc, stage_latent, 0)

        diff = (lax.broadcasted_iota(jnp.int32, (L, L), 0)
                - lax.broadcasted_iota(jnp.int32, (L, L), 1)).astype(F32)
        for h in range(RET_HEADS):
            decay_ref[h] = (jnp.where(diff >= 0, jnp.exp(jnp.maximum(diff, 0.0) * log_g(0, h)), 0.0)
                            + jnp.where(diff <= 0, jnp.exp(jnp.maximum(-diff, 0.0) * log_g(1, h)), 0.0))

        sf_ref[...] = jnp.zeros_like(sf_ref)
        sb_ref[...] = jnp.zeros_like(sb_ref)

        def visit(c, carry):
            k = kr_ref[chunk(c), :]
            v = vs_ref[chunk(c), :]
            for h in range(RET_HEADS):
                lgb = log_g(1, h)
                st = sb_ref[h]
                sball_ref[c, h] = st
                kz = head(k, h, RET_QK_DIM) * jnp.exp(pos * lgb)
                sb_ref[h] = jnp.exp(L * lgb) * st + _dot_tn(kz, head(v, h, RET_V_DIM))
            return carry

        for c in reversed(range(ncc)):
            visit(c, 0)
        lax.fori_loop(0, nlc, lambda j, carry: visit(nc - 1 - j, carry), 0)

    q = _ret_rope(q_ref[...], cos_ref[table(s), :], sin_ref[table(s), :])
    k = kr_ref[chunk(s), :]
    v = vs_ref[chunk(s), :]
    g = g_ref[...]
    for h in range(RET_HEADS):
        lgf = log_g(0, h)
        lgb = log_g(1, h)
        qh = head(q, h, RET_QK_DIM)
        kh = head(k, h, RET_QK_DIM)
        vh = head(v, h, RET_V_DIM)
        st_f = sf_ref[h]
        st_b = sball_ref[s, h]
        o = _dot(_dot_nt(qh, kh) * decay_ref[h], vh)
        o = o + _dot(qh, st_f) * jnp.exp((pos + 1.0) * lgf)
        o = o + _dot(qh, st_b) * jnp.exp((L - pos) * lgb)
        sf_ref[h] = jnp.exp(L * lgf) * st_f + _dot_tn(kh * jnp.exp((L - 1.0 - pos) * lgf), vh)
        mu = jnp.mean(o, axis=-1, keepdims=True)
        var = jnp.mean(jnp.square(o - mu), axis=-1, keepdims=True)
        o = (o - mu) * lax.rsqrt(var + GN_EPS) * head(gn_ref[...], h, RET_V_DIM)
        o_ref[:, h * RET_V_DIM:(h + 1) * RET_V_DIM] = (_silu(head(g, h, RET_V_DIM)) * o).astype(o_ref.dtype)


def retention(dm, p, dec, cos_t, sin_t, gn):
    L = RET_L
    ncc, nlc = dm.ctx // L, dm.s // L
    nc = ncc + nlc
    lat_blocks = dm.n_lat // L
    ctx_blk0 = dm.n_lat // dm.ctx
    qk_w = RET_HEADS * RET_QK_DIM
    v_w = RET_HEADS * RET_V_DIM

    def rows(b, s):
        return jnp.where(s < ncc, lat_blocks + b * ncc + s, b * nlc + s - ncc)

    whole = lambda shape: pl.BlockSpec(shape, lambda b, s, d: (0, 0))
    grid_spec = pltpu.PrefetchScalarGridSpec(
        num_scalar_prefetch=1,
        grid=(dm.b, nc),
        in_specs=[pl.BlockSpec((L, qk_w), lambda b, s, d: (rows(b, s), 0)),
                  pl.BlockSpec((L, v_w), lambda b, s, d: (rows(b, s), 2)),
                  pl.BlockSpec((dm.s, qk_w), lambda b, s, d: (b, 1)),
                  pl.BlockSpec((dm.ctx, qk_w), lambda b, s, d: (ctx_blk0 + b, 1)),
                  pl.BlockSpec((dm.s, v_w), lambda b, s, d: (b, 1)),
                  pl.BlockSpec((dm.ctx, v_w), lambda b, s, d: (ctx_blk0 + b, 1)),
                  whole(cos_t.shape), whole(sin_t.shape), whole((1, v_w))],
        out_specs=pl.BlockSpec((L, v_w), lambda b, s, d: (rows(b, s), 0)),
        scratch_shapes=[pltpu.VMEM((nc * L, qk_w), F32),
                        pltpu.VMEM((nc * L, v_w), BF16),
                        pltpu.VMEM((RET_HEADS, RET_QK_DIM, RET_V_DIM), F32),
                        pltpu.VMEM((RET_HEADS, RET_QK_DIM, RET_V_DIM), F32),
                        pltpu.VMEM((nc, RET_HEADS, RET_QK_DIM, RET_V_DIM), F32),
                        pltpu.VMEM((RET_HEADS, L, L), F32)],
    )
    return pl.pallas_call(
        functools.partial(_ret_kernel, ncc=ncc, nlc=nlc),
        grid_spec=grid_spec,
        out_shape=jax.ShapeDtypeStruct((dm.n_all, v_w), BF16),
        compiler_params=_cparams(("parallel", "arbitrary")),
        name="retention",
    )(dec, p, p, p, p, p, p, cos_t, sin_t, gn.reshape(1, v_w))


def _att_prep_kernel(q_ref, k_ref, v_ref, cos_ref, sin_ref, gq_ref, gk_ref, qo_ref, ko_ref, vo_ref,
                     *, lat_tiles):
    is_lat = pl.program_id(0) < lat_tiles
    cos = jnp.where(is_lat, cos_ref[...], 1.0)
    sin = jnp.where(is_lat, sin_ref[...], 0.0)

    hd = ATT_HEAD_DIM

    def prep(x_ref, gain_ref, h):
        xh = x_ref[:, h * hd:(h + 1) * hd]
        xn = xh * lax.rsqrt(jnp.mean(xh * xh, axis=-1, keepdims=True) + NORM_EPS) * gain_ref[...]
        return _rope_pairs(xn, cos, sin)

    for h in range(ATT_HEADS):
        qo_ref[:, h * hd:(h + 1) * hd] = (prep(q_ref, gq_ref, h) * ATT_SCORE_SCALE).astype(qo_ref.dtype)
    ones = jnp.ones((ATT_V_ROWS - hd, vo_ref.shape[1]), vo_ref.dtype)
    for h in range(ATT_KV_HEADS):
        ko_ref[:, h * hd:(h + 1) * hd] = prep(k_ref, gk_ref, h).astype(ko_ref.dtype)
        vo_ref[h * ATT_V_ROWS:h * ATT_V_ROWS + hd, :] = v_ref[:, h * hd:(h + 1) * hd].T.astype(vo_ref.dtype)
        vo_ref[h * ATT_V_ROWS + hd:(h + 1) * ATT_V_ROWS, :] = ones


def att_prep(dm, p, cos_t, sin_t, gq, gk):
    tm = dm.tile(TM_PREP)
    n = dm.n_all
    lat_tiles, per_batch = dm.n_lat // tm, dm.s // tm
    qw = ATT_HEADS * ATT_HEAD_DIM
    kw = ATT_KV_HEADS * ATT_HEAD_DIM
    q_blk, k_blk, v_blk = ATT_Q_COL // qw, ATT_K_COL // kw, ATT_V_COL // kw
    tab = lambda i: (jnp.where(i < lat_tiles, i % per_batch, 0), 0)
    return pl.pallas_call(
        functools.partial(_att_prep_kernel, lat_tiles=lat_tiles),
        grid=(n // tm,),
        in_specs=[pl.BlockSpec((tm, qw), lambda i: (i, q_blk)),
                  pl.BlockSpec((tm, kw), lambda i: (i, k_blk)),
                  pl.BlockSpec((tm, kw), lambda i: (i, v_blk)),
                  pl.BlockSpec((tm, ATT_HEAD_DIM), tab),
                  pl.BlockSpec((tm, ATT_HEAD_DIM), tab),
                  pl.BlockSpec((1, ATT_HEAD_DIM), lambda i: (0, 0)),
                  pl.BlockSpec((1, ATT_HEAD_DIM), lambda i: (0, 0))],
        out_specs=[pl.BlockSpec((tm, qw), lambda i: (i, 0)),
                   pl.BlockSpec((tm, kw), lambda i: (i, 0)),
                   pl.BlockSpec((ATT_KV_HEADS * ATT_V_ROWS, tm), lambda i: (0, i))],
        out_shape=[jax.ShapeDtypeStruct((n, qw), BF16),
                   jax.ShapeDtypeStruct((n, kw), BF16),
                   jax.ShapeDtypeStruct((ATT_KV_HEADS * ATT_V_ROWS, n), BF16)],
        compiler_params=_cparams(("parallel",)),
        name="att_prep",
    )(p, p, p, cos_t, sin_t, gq.reshape(1, -1), gk.reshape(1, -1))


def _attn_kernel(*refs, with_latent_keys):
    ns = ATT_STAGES
    if with_latent_keys:
        qt_ref, kl_ref, vtl_ref, kc_ref, vtc_ref, o_ref, acc_ref = refs[:7]
    else:
        qt_ref, kc_ref, vtc_ref, o_ref, acc_ref = refs[:5]
    s_refs, p_refs = refs[-2 * ns:-ns], refs[-ns:]
    hd = ATT_HEAD_DIM
    grp = ATT_HEADS // ATT_KV_HEADS
    tq = qt_ref.shape[0]
    nq = grp * tq
    tk = kc_ref.shape[0]
    n = 1 + (kl_ref.shape[0] // tk if with_latent_keys else 0)
    q2 = jnp.concatenate([qt_ref[:, j * hd:(j + 1) * hd] for j in range(grp)], axis=0)

    def key_chunk(j):
        if isinstance(j, int) and j == 0:
            return kc_ref[...]
        return kl_ref[pl.ds(pl.multiple_of((j - 1) * tk, tk), tk), :]

    def value_chunk(j):
        if isinstance(j, int) and j == 0:
            return vtc_ref[...]
        return vtl_ref[:, pl.ds(pl.multiple_of((j - 1) * tk, tk), tk)]

    def max_keys(x):
        groups = math.gcd(tk // SUBLANES, ATT_REDUCE_GROUPS)
        part = jnp.max(x.reshape(groups, tk // groups, nq), axis=0)
        return jnp.max(part, axis=0, keepdims=True)

    def time_step(t, slot, carry, do_scores=True, do_softmax=True, do_values=True):
        m, alpha = carry
        if do_scores:
            s_refs[slot][...] = _dot_nt(key_chunk(t), q2)
        if do_values:
            pv = jnp.dot(value_chunk(t - 2), p_refs[(slot - 2) % ns][...], preferred_element_type=F32)
            acc_ref[...] = acc_ref[...] * alpha + pv
        if do_softmax:
            st = s_refs[(slot - 1) % ns][...]
            m_new = jnp.maximum(m, max_keys(st))
            p_refs[(slot - 1) % ns][...] = jnp.exp2(st - m_new).astype(BF16)
            alpha = jnp.exp2(m - m_new)
            m = m_new
        return m, alpha

    acc_ref[...] = jnp.zeros_like(acc_ref)
    carry = (jnp.full((1, nq), -1e30, F32), jnp.zeros((1, nq), F32))
    fill = min(3, n + 2)
    for t in range(fill):
        carry = time_step(t, t % ns, carry, t < n, 0 <= t - 1 < n, 0 <= t - 2 < n)
    steady = max(n - fill, 0)
    body_steps = ns * ATT_UNROLL

    def unrolled(i, cr):
        for k in range(body_steps):
            cr = time_step(fill + body_steps * i + k, (fill + k) % ns, cr)
        return cr

    if steady // body_steps:
        carry = lax.fori_loop(0, steady // body_steps, unrolled, carry)
    for t in range(fill + body_steps * (steady // body_steps), max(n, fill)):
        carry = time_step(t, t % ns, carry)
    for t in range(max(n, fill), n + 2):
        carry = time_step(t, t % ns, carry, False, t - 1 < n, True)
    out = (acc_ref[0:hd, :] / acc_ref[hd:hd + 1, :]).T
    for j in range(grp):
        o_ref[:, j * hd:(j + 1) * hd] = out[j * tq:(j + 1) * tq].astype(o_ref.dtype)


def attention(dm, qt, kn, vt, latent):
    hd = ATT_HEAD_DIM
    grp = ATT_HEADS // ATT_KV_HEADS
    tk = dm.ctx
    assert dm.s % tk == 0
    ctx_blk0 = dm.n_lat // dm.ctx
    kc_spec = pl.BlockSpec((tk, hd), lambda b, h, i: (ctx_blk0 + b, h))
    vtc_spec = pl.BlockSpec((ATT_V_ROWS, tk), lambda b, h, i: (h, ctx_blk0 + b))
    if latent:
        tq = TQ_ATT
        nq = dm.s // tq
        q_spec = pl.BlockSpec((tq, grp * hd), lambda b, h, i: (b * nq + i, h))
        kl_spec = pl.BlockSpec((dm.s, hd), lambda b, h, i: (b, h))
        vtl_spec = pl.BlockSpec((ATT_V_ROWS, dm.s), lambda b, h, i: (h, b))
        in_specs = [q_spec, kl_spec, vtl_spec, kc_spec, vtc_spec]
        args = (qt, kn, vt, kn, vt)
        rows = dm.n_lat
    else:
        tq, nq = dm.ctx, 1
        q_spec = pl.BlockSpec((tq, grp * hd), lambda b, h, i: (ctx_blk0 + b, h))
        in_specs = [q_spec, kc_spec, vtc_spec]
        args = (qt, kn, vt)
        rows = dm.b * dm.ctx
    return pl.pallas_call(
        functools.partial(_attn_kernel, with_latent_keys=latent),
        grid=(dm.b, ATT_KV_HEADS, nq),
        in_specs=in_specs,
        out_specs=pl.BlockSpec((tq, grp * hd), lambda b, h, i: (b * nq + i, h)),
        out_shape=jax.ShapeDtypeStruct((rows, ATT_HEADS * hd), BF16),
        scratch_shapes=([pltpu.VMEM((ATT_V_ROWS, grp * tq), F32)]
                        + [pltpu.VMEM((tk, grp * tq), F32)] * ATT_STAGES
                        + [pltpu.VMEM((tk, grp * tq), BF16)] * ATT_STAGES),
        compiler_params=_cparams(("parallel", "parallel", "arbitrary")),
        name="attention_latent" if latent else "attention_ctx",
    )(*args)


def _pool_kernel(prev_ref, u_ref, next_ref, w_ref, sc_ref, o_ref, a_ref, b_ref, *, dm):
    tm = u_ref.shape[0]
    H = POOL_HALO
    i = pl.program_id(0)
    lat_tiles = dm.n_lat // tm
    is_lat = i < lat_tiles
    n_seg = jnp.where(is_lat, dm.s, dm.ctx)
    base = jnp.where(is_lat, (i * tm) % dm.s, 0)
    r = lax.broadcasted_iota(jnp.int32, (tm + 2 * H, 1), 0) - H
    pos = jnp.where(is_lat, base + r, r % dm.ctx)
    win = pl.ds(H, tm + 2 * H)

    def shifted(ref, d):
        return ref[pl.ds(H + d, tm + 2 * H), :]

    zeros = jnp.zeros((H, POOL_GROUP_DIM), F32)
    for ref in (a_ref, b_ref):
        ref[pl.ds(0, H), :] = zeros
        ref[pl.ds(tm + 3 * H, H), :] = zeros

    for gi, w in enumerate(POOL_WINDOWS):
        cols = slice(gi * POOL_GROUP_DIM, (gi + 1) * POOL_GROUP_DIM)
        half = w // 2
        a_ref[pl.ds(H, H), :] = prev_ref[:, cols]
        a_ref[pl.ds(2 * H, tm), :] = u_ref[:, cols]
        a_ref[pl.ds(2 * H + tm, H), :] = next_ref[:, cols]
        b_ref[win, :] = jnp.where(pos >= 1, shifted(a_ref, -1), 0.0)
        step = 1
        while step < half:
            b_ref[win, :] = b_ref[win, :] + jnp.where(pos >= step, shifted(b_ref, -step), 0.0)
            step *= 2
        step = 1
        while step < half:
            a_ref[win, :] = a_ref[win, :] + jnp.where(pos + step < n_seg, shifted(a_ref, step), 0.0)
            step *= 2
        tile = pl.ds(2 * H, tm)
        pt = pos[H:H + tm]
        cnt = (jnp.minimum(pt + half, n_seg) - jnp.maximum(pt - half, 0)).astype(F32)
        u = u_ref[:, cols]
        diff = (a_ref[tile, :] + b_ref[tile, :]) / cnt - u
        o_ref[:, cols] = (_dot(diff, w_ref[gi]) * sc_ref[:, cols]).astype(o_ref.dtype)


def pool_mixer(dm, p, pool_w, pool_scale):
    tm = dm.tile(TM_POOL)
    n = dm.n_all
    H = POOL_HALO
    width = len(POOL_WINDOWS) * POOL_GROUP_DIM
    col = 2560 // width
    per = tm // H
    last = n // H - 1
    return pl.pallas_call(
        functools.partial(_pool_kernel, dm=dm),
        grid=(n // tm,),
        in_specs=[pl.BlockSpec((H, width), lambda i: (jnp.maximum(i * per - 1, 0), col)),
                  pl.BlockSpec((tm, width), lambda i: (i, col)),
                  pl.BlockSpec((H, width), lambda i: (jnp.minimum((i + 1) * per, last), col)),
                  pl.BlockSpec(pool_w.shape, lambda i: (0, 0, 0)),
                  pl.BlockSpec((1, width), lambda i: (0, 0))],
        out_specs=pl.BlockSpec((tm, width), lambda i: (i, 0)),
        out_shape=jax.ShapeDtypeStruct((n, width), BF16),
        scratch_shapes=[pltpu.VMEM((tm + 4 * H, POOL_GROUP_DIM), F32),
                        pltpu.VMEM((tm + 4 * H, POOL_GROUP_DIM), F32)],
        compiler_params=_cparams(("parallel",)),
        name="pool_mixer",
    )(p, p, p, pool_w, pool_scale.reshape(1, width))


def _merge_kernel(yr_ref, ya_ref, yp_ref, gl_ref, *refs, lat_tiles):
    mod_ref, wb_ref, wo_ref, o_ref = refs[-4:]
    x_refs = refs[:-4]
    d = o_ref.shape[1]
    mixed = None
    for j, y_ref in enumerate((yr_ref, ya_ref, yp_ref)):
        gate = gl_ref[:, j * d:(j + 1) * d].astype(F32)
        term = gate * jnp.dot(y_ref[...], wb_ref[j], preferred_element_type=F32)
        mixed = term if mixed is None else mixed + term
    y = mod_ref[0][2:3] * _dot(mixed, wo_ref[...])

    def finish(x_ref):
        o_ref[...] = x_ref[...] + y

    _for_row_source(x_refs, lat_tiles, finish)


def merge(dm, n_rows, yr, ya, yp, gates, x_parts, mod, wb, wo):
    tm = dm.tile(TM_MERGE)
    d = dm.d
    bw = yr.shape[1]
    row = dm.mod_row(tm)
    return pl.pallas_call(
        functools.partial(_merge_kernel, lat_tiles=dm.n_lat // tm),
        grid=(n_rows // tm,),
        in_specs=[pl.BlockSpec((tm, bw), lambda i: (i, 0)),
                  pl.BlockSpec((tm, bw), lambda i: (i, 0)),
                  pl.BlockSpec((tm, bw), lambda i: (i, 0)),
                  pl.BlockSpec((tm, 3 * d), lambda i: (i, 0))]
        + _row_sources(dm, tm, x_parts) + [
            pl.BlockSpec((1, 6, d), lambda i: (row(i), 0, 0)),
            pl.BlockSpec(wb.shape, lambda i: (0, 0, 0)),
            pl.BlockSpec(wo.shape, lambda i: (0, 0))],
        out_specs=pl.BlockSpec((tm, d), lambda i: (i, 0)),
        out_shape=jax.ShapeDtypeStruct((n_rows, d), F32),
        compiler_params=_cparams(("parallel",)),
        name="merge",
    )(yr, ya, yp, gates, *x_parts, mod, wb, wo)


def _ffn_kernel(x_ref, mod_ref, gain_ref, w1_ref, w3_ref, w2_ref, o_ref):
    m = mod_ref[0]
    h = _norm_mod(x_ref[...], gain_ref[...], m[3:4], m[4:5]).astype(BF16)
    fdim = w1_ref.shape[1]
    y = None
    for lo in range(0, fdim, MOE_COL_CHUNK):
        cols = slice(lo, min(lo + MOE_COL_CHUNK, fdim))
        a = jnp.dot(h, w1_ref[:, cols], preferred_element_type=F32)
        b = jnp.dot(h, w3_ref[:, cols], preferred_element_type=F32)
        part = _dot(_silu(a) * b, w2_ref[cols, :])
        y = part if y is None else y + part
    o_ref[...] = x_ref[...] + m[5:6] * y


def ffn_dense(dm, x, mod, gain, w1, w3, w2):
    n, d = x.shape
    fdim = w1.shape[1]
    tm = dm.tile(TM_FFN)
    row = dm.mod_row(tm)
    once = dict(pipeline_mode=pl.Buffered(1))
    return pl.pallas_call(
        _ffn_kernel,
        grid=(n // tm,),
        in_specs=[pl.BlockSpec((tm, d), lambda i: (i, 0)),
                  pl.BlockSpec((1, 6, d), lambda i: (row(i), 0, 0)),
                  pl.BlockSpec((1, d), lambda i: (0, 0)),
                  pl.BlockSpec((d, fdim), lambda i: (0, 0), **once),
                  pl.BlockSpec((d, fdim), lambda i: (0, 0), **once),
                  pl.BlockSpec((fdim, d), lambda i: (0, 0), **once)],
        out_specs=pl.BlockSpec((tm, d), lambda i: (i, 0)),
        out_shape=jax.ShapeDtypeStruct((n, d), F32),
        compiler_params=_cparams(("parallel",)),
        name="ffn_dense",
    )(x, mod, gain.reshape(1, d), w1, w3, w2)


def _router_kernel(x_ref, mod_ref, gain_ref, wr_ref, br_ref, h_ref, slot_ref, slot_t_ref, wt_ref, cum_ref):
    tm = x_ref.shape[0]
    m = mod_ref[0]
    h = _norm_mod(x_ref[...], gain_ref[...], m[3:4], m[4:5])
    h_ref[...] = h.astype(BF16)
    logits = _dot3(h, wr_ref[...]) + br_ref[...]
    lane = lax.broadcasted_iota(jnp.int32, logits.shape, 1)
    m1 = jnp.max(logits, axis=-1, keepdims=True)
    i1 = jnp.min(jnp.where(logits == m1, lane, LANES), axis=-1, keepdims=True)
    rest = jnp.where(lane == i1, -jnp.inf, logits)
    m2 = jnp.max(rest, axis=-1, keepdims=True)
    i2 = jnp.min(jnp.where(rest == m2, lane, LANES), axis=-1, keepdims=True)
    e = jnp.exp(m2 - m1)
    w1 = 1.0 / (1.0 + e)
    w2 = e / (1.0 + e)
    wt_ref[...] = jnp.where(lane == i1, w1, jnp.where(lane == i2, w2, 0.0))
    routed = jnp.where((lane == i1) | (lane == i2), 1.0, 0.0)
    sub = SUB_MOE
    lower = (lax.broadcasted_iota(jnp.int32, (sub, sub), 0) > lax.broadcasted_iota(jnp.int32, (sub, sub), 1))
    lower = lower.astype(BF16)
    offset = jnp.zeros((1, LANES), F32)
    cum_ref[...] = jnp.zeros_like(cum_ref)
    for blk in range(tm // sub):
        rows = slice(blk * sub, (blk + 1) * sub)
        mask = routed[rows]
        cum_ref[0, blk:blk + 1, :] = offset.astype(jnp.int32)
        rank = jnp.dot(lower, mask.astype(BF16), preferred_element_type=F32) + offset
        slot_ref[rows, :] = jnp.where(mask > 0.0, rank, -1.0)
        count = jnp.sum(mask, axis=0, keepdims=True)
        offset = offset + jnp.ceil(count * (1.0 / SLOT_ALIGN)) * SLOT_ALIGN
    cum_ref[0, tm // sub:tm // sub + 1, :] = offset.astype(jnp.int32)
    slot_t_ref[...] = slot_ref[...].T[:SUBLANES, :]


def moe_route(dm, n_rows, x, mod, gain, wr_pad, br_pad):
    tm = dm.tile(TM_MOE)
    d = dm.d
    row = dm.mod_row(tm)
    nt = n_rows // tm
    cum_rows = -(-(tm // SUB_MOE + 1) // SUBLANES) * SUBLANES
    return pl.pallas_call(
        _router_kernel,
        grid=(nt,),
        in_specs=[pl.BlockSpec((tm, d), lambda i: (i, 0)),
                  pl.BlockSpec((1, 6, d), lambda i: (row(i), 0, 0)),
                  pl.BlockSpec((1, d), lambda i: (0, 0)),
                  pl.BlockSpec((d, LANES), lambda i: (0, 0)),
                  pl.BlockSpec((1, LANES), lambda i: (0, 0))],
        out_specs=[pl.BlockSpec((tm, d), lambda i: (i, 0)),
                   pl.BlockSpec((tm, LANES), lambda i: (i, 0)),
                   pl.BlockSpec((SUBLANES, tm), lambda i: (0, i)),
                   pl.BlockSpec((tm, LANES), lambda i: (i, 0)),
                   pl.BlockSpec((1, cum_rows, LANES), lambda i: (i, 0, 0))],
        out_shape=[jax.ShapeDtypeStruct((n_rows, d), BF16),
                   jax.ShapeDtypeStruct((n_rows, LANES), F32),
                   jax.ShapeDtypeStruct((SUBLANES, n_rows), F32),
                   jax.ShapeDtypeStruct((n_rows, LANES), F32),
                   jax.ShapeDtypeStruct((nt, cum_rows, LANES), jnp.int32)],
        compiler_params=_cparams(("parallel",)),
        name="moe_router",
    )(x, mod, gain.reshape(1, d), wr_pad, br_pad)


def _moe_windows(base_ref, row0_ref, e, step=None):
    i, j = pl.program_id(0), pl.program_id(1)
    if step is not None:
        i, j = step // pl.num_programs(1), step % pl.num_programs(1)
    start = base_ref[i, j * N_EXPERTS + e]
    nwin = (base_ref[i, (j + 1) * N_EXPERTS + e] - start + WIN_MOE - 1) // WIN_MOE
    return start, nwin, row0_ref[i, e]


def _moe_gather_kernel(base_ref, row0_ref, h_ref, slot_t_ref, xs_in_ref, xs_ref, buf_ref, sem):
    del xs_in_ref
    win, sub = WIN_MOE, SUB_MOE
    spare = N_EXPERTS

    def gathered(e, start, w):
        want = (lax.broadcasted_iota(jnp.int32, (win, sub), 0) + (start + w * win)).astype(F32)
        onehot = (slot_t_ref[e:e + 1, :] == want).astype(BF16)
        return jnp.dot(onehot, h_ref[...], preferred_element_type=F32).astype(BF16)

    def copy(e, buf, w):
        start, _, row0 = _moe_windows(base_ref, row0_ref, e)
        rows = pl.ds(pl.multiple_of(row0 + start + w * win, SLOT_ALIGN), win)
        return pltpu.make_async_copy(buf_ref.at[buf], xs_ref.at[rows, :], sem.at[buf])

    for e in range(N_EXPERTS):
        start, _, _ = _moe_windows(base_ref, row0_ref, e)
        buf_ref[e] = gathered(e, start, 0)
        copy(e, e, 0).start()

    for e in range(N_EXPERTS):
        start, nwin, _ = _moe_windows(base_ref, row0_ref, e)

        def extra(w, carry, e=e, start=start):
            buf_ref[spare] = gathered(e, start, w)
            cp = copy(e, spare, w)
            cp.start()
            cp.wait()
            return carry

        lax.fori_loop(1, jnp.maximum(nwin, 1), extra, 0)

    for e in range(N_EXPERTS):
        copy(e, e, 0).wait()


def moe_gather(dm, n_rows, h, slot_t, base, row0, cap):
    tm, sub, d = dm.tile(TM_MOE), SUB_MOE, dm.d
    nsub = tm // sub
    grid_spec = pltpu.PrefetchScalarGridSpec(
        num_scalar_prefetch=2,
        grid=(n_rows // tm, nsub),
        in_specs=[pl.BlockSpec((sub, d), lambda i, j, b, r: (i * nsub + j, 0)),
                  pl.BlockSpec((SUBLANES, sub), lambda i, j, b, r: (0, i * nsub + j)),
                  pl.BlockSpec(memory_space=pl.ANY)],
        out_specs=pl.BlockSpec(memory_space=pl.ANY),
        scratch_shapes=[pltpu.VMEM((N_EXPERTS + 1, WIN_MOE, d), BF16),
                        pltpu.SemaphoreType.DMA((N_EXPERTS + 1,))],
    )
    return pl.pallas_call(
        _moe_gather_kernel,
        grid_spec=grid_spec,
        out_shape=jax.ShapeDtypeStruct((cap, d), BF16),
        input_output_aliases={4: 0},
        compiler_params=_cparams(("arbitrary", "arbitrary")),
        name="moe_gather",
    )(base, row0, h, slot_t, jnp.zeros((cap, d), BF16))


def _moe_ffn_kernel(bexp_ref, bvalid_ref, xs_ref, w1_ref, w3_ref, w2_ref, o_ref):
    del bexp_ref
    valid = bvalid_ref[pl.program_id(0)] == 1

    @pl.when(valid)
    def _():
        xs = xs_ref[...]
        fdim = w1_ref.shape[2]
        y = None
        for lo in range(0, fdim, MOE_COL_CHUNK):
            cols = slice(lo, min(lo + MOE_COL_CHUNK, fdim))
            a = jnp.dot(xs, w1_ref[0, :, cols], preferred_element_type=F32)
            b = jnp.dot(xs, w3_ref[0, :, cols], preferred_element_type=F32)
            part = _dot(_silu(a) * b, w2_ref[0, cols, :])
            y = part if y is None else y + part
        o_ref[...] = y.astype(o_ref.dtype)

    @pl.when(jnp.logical_not(valid))
    def _():
        o_ref[...] = jnp.zeros_like(o_ref)


def moe_ffn(xs, blk_expert, blk_valid, w1, w3, w2):
    cap, d = xs.shape
    _, _, fdim = w1.shape
    blk = BLK_MOE
    once = dict(pipeline_mode=pl.Buffered(1))
    grid_spec = pltpu.PrefetchScalarGridSpec(
        num_scalar_prefetch=2,
        grid=(cap // blk,),
        in_specs=[pl.BlockSpec((blk, d), lambda k, be, bv: (k, 0)),
                  pl.BlockSpec((1, d, fdim), lambda k, be, bv: (be[k], 0, 0), **once),
                  pl.BlockSpec((1, d, fdim), lambda k, be, bv: (be[k], 0, 0), **once),
                  pl.BlockSpec((1, fdim, d), lambda k, be, bv: (be[k], 0, 0), **once)],
        out_specs=pl.BlockSpec((blk, d), lambda k, be, bv: (k, 0)),
    )
    return pl.pallas_call(
        _moe_ffn_kernel,
        grid_spec=grid_spec,
        out_shape=jax.ShapeDtypeStruct((cap, d), BF16),
        compiler_params=_cparams(("arbitrary",)),
        name="moe_ffn",
    )(blk_expert, blk_valid, xs, w1, w3, w2)


def _moe_scatter_kernel(base_ref, row0_ref, slot_ref, wt_ref, x_ref, mod_ref, gain_ref, ys_ref, o_ref,
                        buf_ref, y_ref, sem, *, final):
    win, sub = WIN_MOE, SUB_MOE
    nbuf = N_EXPERTS + 1
    step = pl.program_id(0) * pl.num_programs(1) + pl.program_id(1)
    last_step = pl.num_programs(0) * pl.num_programs(1) - 1
    mine = (step % 2) * nbuf
    spare = mine + N_EXPERTS

    def copy(e, buf, w, at=None):
        start, _, row0 = _moe_windows(base_ref, row0_ref, e, at)
        rows = pl.ds(pl.multiple_of(row0 + start + w * win, SLOT_ALIGN), win)
        return pltpu.make_async_copy(ys_ref.at[rows, :], buf_ref.at[buf], sem.at[buf])

    @pl.when(step == 0)
    def _():
        for e in range(N_EXPERTS):
            copy(e, e, 0).start()

    @pl.when(step < last_step)
    def _():
        for e in range(N_EXPERTS):
            copy(e, nbuf - mine + e, 0, at=step + 1).start()

    def scattered(e, buf, w):
        start, _, _ = _moe_windows(base_ref, row0_ref, e)
        want = (lax.broadcasted_iota(jnp.int32, (sub, win), 1) + (start + w * win)).astype(F32)
        onehot = (slot_ref[:, e:e + 1] == want).astype(BF16)
        return wt_ref[:, e:e + 1] * jnp.dot(onehot, buf_ref[buf], preferred_element_type=F32)

    y_ref[...] = jnp.zeros_like(y_ref)
    for e in range(N_EXPERTS):
        _, nwin, _ = _moe_windows(base_ref, row0_ref, e)

        def extra(w, carry, e=e):
            cp = copy(e, spare, w)
            cp.start()
            cp.wait()
            y_ref[...] += scattered(e, spare, w)
            return carry

        lax.fori_loop(1, jnp.maximum(nwin, 1), extra, 0)

    y = y_ref[...]
    for e in range(N_EXPERTS):
        copy(e, mine + e, 0).wait()
        y = y + scattered(e, mine + e, 0)
    x = x_ref[...] + mod_ref[0][5:6] * y
    if final:
        x = x * lax.rsqrt(jnp.mean(x * x, axis=-1, keepdims=True) + NORM_EPS) * gain_ref[...]
    o_ref[...] = x


def moe_scatter(dm, n_rows, ys, slot, wt, base, row0, x, mod, final_gain):
    tm, sub, d = dm.tile(TM_MOE), SUB_MOE, dm.d
    nsub = tm // sub
    row = dm.mod_row(sub)
    final = final_gain is not None
    gain = final_gain if final else jnp.ones((d,), F32)
    tok = lambda i, j, b, r: (i * nsub + j, 0)
    grid_spec = pltpu.PrefetchScalarGridSpec(
        num_scalar_prefetch=2,
        grid=(n_rows // tm, nsub),
        in_specs=[pl.BlockSpec((sub, LANES), tok),
                  pl.BlockSpec((sub, LANES), tok),
                  pl.BlockSpec((sub, d), tok),
                  pl.BlockSpec((1, 6, d), lambda i, j, b, r: (row(i * nsub + j), 0, 0)),
                  pl.BlockSpec((1, d), lambda i, j, b, r: (0, 0)),
                  pl.BlockSpec(memory_space=pl.ANY)],
        out_specs=pl.BlockSpec((sub, d), tok),
        scratch_shapes=[pltpu.VMEM((2 * (N_EXPERTS + 1), WIN_MOE, d), BF16),
                        pltpu.VMEM((sub, d), F32),
                        pltpu.SemaphoreType.DMA((2 * (N_EXPERTS + 1),))],
    )
    return pl.pallas_call(
        functools.partial(_moe_scatter_kernel, final=final),
        grid_spec=grid_spec,
        out_shape=jax.ShapeDtypeStruct((n_rows, d), F32),
        compiler_params=_cparams(("arbitrary", "arbitrary")),
        name="moe_scatter",
    )(base, row0, slot, wt, x, mod, gain.reshape(1, d), ys)


def moe_layout(dm, n_rows, cum):
    tm = dm.tile(TM_MOE)
    nt, nsub = n_rows // tm, tm // SUB_MOE
    base = cum[:, :nsub + 1, :N_EXPERTS]
    total = base[:, nsub, :]
    per_expert = jnp.sum(total, axis=0)
    padded = (per_expert + WIN_MOE + BLK_MOE - 1) // BLK_MOE * BLK_MOE
    expert_row0 = jnp.cumsum(padded) - padded
    row0 = expert_row0[None, :] + jnp.cumsum(total, axis=0) - total
    slack = nt * nsub * N_EXPERTS * (SLOT_ALIGN - 1) + N_EXPERTS * (WIN_MOE + BLK_MOE)
    nblk = -(-(2 * n_rows + slack) // BLK_MOE)
    ends = jnp.cumsum(padded // BLK_MOE)
    k = jnp.arange(nblk, dtype=jnp.int32)
    blk_expert = jnp.minimum(jnp.searchsorted(ends, k, side="right"), N_EXPERTS - 1).astype(jnp.int32)
    blk_valid = (k < ends[-1]).astype(jnp.int32)
    return base.reshape(nt, (nsub + 1) * N_EXPERTS), row0.astype(jnp.int32), blk_expert, blk_valid, nblk * BLK_MOE


def _final_norm_kernel(x_ref, gain_ref, o_ref):
    x = x_ref[...]
    o_ref[...] = x * lax.rsqrt(jnp.mean(x * x, axis=-1, keepdims=True) + NORM_EPS) * gain_ref[...]


def final_rmsnorm(dm, n_rows, x, gain):
    tm = dm.tile(TM_FFN)
    d = dm.d
    return pl.pallas_call(
        _final_norm_kernel,
        grid=(n_rows // tm,),
        in_specs=[pl.BlockSpec((tm, d), lambda i: (i, 0)), pl.BlockSpec((1, d), lambda i: (0, 0))],
        out_specs=pl.BlockSpec((tm, d), lambda i: (i, 0)),
        out_shape=jax.ShapeDtypeStruct((n_rows, d), F32),
        compiler_params=_cparams(("parallel",)),
        name="final_norm",
    )(x, gain.reshape(1, d))


def _rope_tables(s, ctx):
    def angles(pos, dim):
        freqs = 1.0 / (ROPE_THETA ** (jnp.arange(0, dim, 2, dtype=F32) / dim))
        return pos[:, None] * freqs[None, :]

    rows = s // GRID_W
    row = jnp.repeat(jnp.arange(rows, dtype=F32), GRID_W)
    col = jnp.tile(jnp.arange(GRID_W, dtype=F32), rows)
    half = ATT_HEAD_DIM // 2
    ang_att = jnp.concatenate([angles(row, half), angles(col, half)], axis=-1)
    sign = jnp.tile(jnp.array([-1.0, 1.0], F32), LANES // 2)

    def tables(ang):
        reps = LANES // (2 * ang.shape[1])
        cos = jnp.tile(jnp.repeat(jnp.cos(ang), 2, axis=-1), (1, reps))
        sin = jnp.tile(jnp.repeat(jnp.sin(ang), 2, axis=-1), (1, reps)) * sign
        return cos, sin

    ang_ret = angles(jnp.arange(s, dtype=F32), RET_QK_DIM)
    ang_ret = jnp.concatenate([ang_ret, jnp.zeros((ctx, RET_QK_DIM // 2), F32)], axis=0)
    return tables(ang_att) + tables(ang_ret)


def kernel(x, c, ctx, c_ctx, w_ada, b_ada, norm_mix, norm_ffn, w_in, ret_decay, ret_gn, attn_qn, attn_kn, pool_w, pool_scale, w_branch, w_out, ffn_w1, ffn_w3, ffn_w2, moe_router, moe_router_b, moe_w1, moe_w3, moe_w2, final_norm):
    b, s, d = x.shape
    n_ctx = ctx.shape[1]
    depth = w_in.shape[0]
    dm = Dims(b, s, n_ctx, d)

    x_parts = (x.reshape(b * s, d), ctx.reshape(b * n_ctx, d))
    mod_rows = 2 * SUBLANES
    cc = jnp.concatenate([c, c_ctx[None, :], jnp.zeros((mod_rows - b - 1, d), F32)], axis=0)
    mods = ada_table(cc, w_ada, b_ada).reshape(depth, mod_rows, 6, d)

    att_cos, att_sin, ret_cos, ret_sin = _rope_tables(s, n_ctx)

    for i in range(depth):
        need_ctx = i < depth - 1
        n_rows = dm.n_all if need_ctx else dm.n_lat
        mod = mods[i]
        p, gates = proj_in(dm, x_parts, mod, norm_mix[i], w_in[i].astype(BF16))
        y_ret = retention(dm, p, ret_decay[i], ret_cos, ret_sin, ret_gn[i])
        y_pool = pool_mixer(dm, p, pool_w[i].astype(BF16), pool_scale[i])
        qt, kn, vt = att_prep(dm, p, att_cos, att_sin, attn_qn[i], attn_kn[i])
        y_att = attention(dm, qt, kn, vt, latent=True)
        if need_ctx:
            y_att = jnp.concatenate([y_att, attention(dm, qt, kn, vt, latent=False)], axis=0)
        xa = merge(dm, n_rows, y_ret, y_att, y_pool, gates, x_parts, mod,
                   w_branch[i].astype(BF16), w_out[i].astype(BF16))
        j = i // 2
        last = i == depth - 1
        if i % 2 == 0:
            xa = ffn_dense(dm, xa, mod, norm_ffn[i], ffn_w1[j].astype(BF16), ffn_w3[j].astype(BF16),
                           ffn_w2[j].astype(BF16))
            if last:
                xa = final_rmsnorm(dm, n_rows, xa, final_norm)
        else:
            wr = jnp.zeros((d, LANES), F32).at[:, :N_EXPERTS].set(moe_router[j])
            br = jnp.full((1, LANES), -1e30, F32).at[0, :N_EXPERTS].set(moe_router_b[j])
            h, slot, slot_t, wt, cnt = moe_route(dm, n_rows, xa, mod, norm_ffn[i], wr, br)
            base, row0, blk_expert, blk_valid, cap = moe_layout(dm, n_rows, cnt)
            xs = moe_gather(dm, n_rows, h, slot_t, base, row0, cap)
            ys = moe_ffn(xs, blk_expert, blk_valid,
                         moe_w1[j].astype(BF16), moe_w3[j].astype(BF16), moe_w2[j].astype(BF16))
            xa = moe_scatter(dm, n_rows, ys, slot, wt, base, row0, xa, mod, final_norm if last else None)
        x_parts = (xa,)
    return xa[:dm.n_lat].reshape(b, s, d)
```

```python
import functools
import math

import jax
import jax.numpy as jnp
from jax import lax
from jax.experimental import pallas as pl
from jax.experimental.pallas import tpu as pltpu

F32 = jnp.float32
BF16 = jnp.bfloat16

GRID_W = 64
RET_HEADS = 4
RET_QK_DIM = 64
RET_V_DIM = 128
ATT_HEADS = 4
ATT_KV_HEADS = 2
ATT_HEAD_DIM = 128
ROPE_THETA = 10000.0
POOL_WINDOWS = (2, 4, 8, 16)
POOL_GROUP_DIM = 128
N_EXPERTS = 8
NORM_EPS = 1e-6
GN_EPS = 1e-5

LANES = 128
SUBLANES = 8
VMEM_LIMIT_BYTES = 56 * 1024 * 1024

TM_PROJ = 1024
TN_PROJ = 1536
TM_PREP = 1024
TM_POOL = 1024
TM_MERGE = 1024
TM_FFN = 1024
TM_MOE = 2048
MOE_COL_CHUNK = 512
BLK_MOE = 512
WIN_MOE = 256
SUB_MOE = 512
SLOT_ALIGN = 16
RET_L = 256
TQ_ATT = 1024
ATT_Q_COL = 2 * RET_HEADS * RET_QK_DIM + 2 * RET_HEADS * RET_V_DIM
ATT_K_COL = ATT_Q_COL + ATT_HEADS * ATT_HEAD_DIM
ATT_V_COL = ATT_K_COL + ATT_KV_HEADS * ATT_HEAD_DIM
ATT_SCORE_SCALE = (ATT_HEAD_DIM ** -0.5) * math.log2(math.e)
ATT_V_ROWS = ATT_HEAD_DIM + 16
ATT_STAGES = 3
ATT_UNROLL = 4
ATT_REDUCE_GROUPS = 8
POOL_HALO = 8


def _cparams(sem):
    return pltpu.CompilerParams(dimension_semantics=sem, vmem_limit_bytes=VMEM_LIMIT_BYTES)


def _dot(a, b):
    return jnp.dot(a.astype(BF16), b.astype(BF16), preferred_element_type=F32)


def _dot_nt(a, b):
    return lax.dot_general(a.astype(BF16), b.astype(BF16), (((1,), (1,)), ((), ())),
                           preferred_element_type=F32)


def _dot_tn(a, b):
    return lax.dot_general(a.astype(BF16), b.astype(BF16), (((0,), (0,)), ((), ())),
                           preferred_element_type=F32)


def _split_bf16(a):
    hi = a.astype(BF16)
    lo = (a - hi.astype(F32)).astype(BF16)
    return hi, lo


def _dot3(a, b):
    ah, al = _split_bf16(a)
    bh, bl = _split_bf16(b)
    d = functools.partial(jnp.dot, preferred_element_type=F32)
    return d(ah, bh) + (d(al, bh) + d(ah, bl))


def _norm_mod(x, gain, shift, scale):
    y = x * lax.rsqrt(jnp.mean(x * x, axis=-1, keepdims=True) + NORM_EPS)
    return (y * gain) * (1.0 + scale) + shift


def _sigmoid(x):
    return 1.0 / (1.0 + jnp.exp(-x))


def _silu(x):
    return x * _sigmoid(x)


class Dims:
    def __init__(self, b, s, ctx, d):
        self.b, self.s, self.ctx, self.d = b, s, ctx, d
        self.n_lat = b * s
        self.n_all = b * s + b * ctx

    def tile(self, preferred):
        t = preferred
        while self.s % t or (self.b * self.ctx) % t:
            t //= 2
        return t

    def mod_row(self, tm):
        lat_tiles, per_batch, b = self.n_lat // tm, self.s // tm, self.b
        return lambda i: jnp.where(i < lat_tiles, i // per_batch, b)


def _ada_kernel(c_ref, w_ref, b_ref, o_ref):
    o_ref[0] = _dot3(_silu(c_ref[...]), w_ref[0]) + b_ref[0]


def ada_table(cc, w_ada, b_ada):
    depth, d, n6 = w_ada.shape
    tn = n6 // 4
    rows = cc.shape[0]
    return pl.pallas_call(
        _ada_kernel,
        grid=(depth, n6 // tn),
        in_specs=[pl.BlockSpec((rows, d), lambda l, j: (0, 0)),
                  pl.BlockSpec((1, d, tn), lambda l, j: (l, 0, j)),
                  pl.BlockSpec((1, 1, tn), lambda l, j: (l, 0, j))],
        out_specs=pl.BlockSpec((1, rows, tn), lambda l, j: (l, 0, j)),
        out_shape=jax.ShapeDtypeStruct((depth, rows, n6), F32),
        compiler_params=_cparams(("arbitrary", "arbitrary")),
        name="ada_table",
    )(cc, w_ada, b_ada.reshape(depth, 1, n6))


def _row_sources(dm, tm, parts):
    d = dm.d
    if len(parts) == 1:
        return [pl.BlockSpec((tm, d), lambda i, *_: (i, 0))]
    lat_tiles = dm.n_lat // tm
    return [pl.BlockSpec((tm, d), lambda i, *_: (jnp.minimum(i, lat_tiles - 1), 0)),
            pl.BlockSpec((tm, d), lambda i, *_: (jnp.maximum(i - lat_tiles, 0), 0))]


def _for_row_source(x_refs, lat_tiles, fn):
    if len(x_refs) == 1:
        fn(x_refs[0])
        return
    is_lat = pl.program_id(0) < lat_tiles
    pl.when(is_lat)(lambda: fn(x_refs[0]))
    pl.when(jnp.logical_not(is_lat))(lambda: fn(x_refs[1]))


def _proj_in_kernel(*refs, lat_tiles):
    mod_ref, gain_ref, wm_ref, wg_ref, o_ref, g_ref = refs[-6:]
    x_refs = refs[:-6]
    x = x_refs[0][...]
    if len(x_refs) == 2:
        x = jnp.where(pl.program_id(0) < lat_tiles, x, x_refs[1][...])
    m = mod_ref[0]
    h = _norm_mod(x, gain_ref[...], m[0:1], m[1:2]).astype(BF16)
    o_ref[...] = jnp.dot(h, wm_ref[...], preferred_element_type=F32)
    logits = jnp.dot(h, wg_ref[...], preferred_element_type=F32)
    g_ref[...] = (0.5 * jnp.tanh(0.5 * logits) + 0.5).astype(g_ref.dtype)


def proj_in(dm, x_parts, mod, gain, w):
    n, d = dm.n_all, dm.d
    nout = w.shape[1]
    gate_w = 3 * d
    tm, tn = dm.tile(TM_PROJ), TN_PROJ
    assert nout - gate_w == gate_w and gate_w % tn == 0
    gate_blk0 = gate_w // tn
    row = dm.mod_row(tm)
    return pl.pallas_call(
        functools.partial(_proj_in_kernel, lat_tiles=dm.n_lat // tm),
        grid=(n // tm, gate_w // tn),
        in_specs=_row_sources(dm, tm, x_parts) + [
            pl.BlockSpec((1, 6, d), lambda i, j: (row(i), 0, 0)),
            pl.BlockSpec((1, d), lambda i, j: (0, 0)),
            pl.BlockSpec((d, tn), lambda i, j: (0, j)),
            pl.BlockSpec((d, tn), lambda i, j: (0, gate_blk0 + j))],
        out_specs=[pl.BlockSpec((tm, tn), lambda i, j: (i, j)),
                   pl.BlockSpec((tm, tn), lambda i, j: (i, j))],
        out_shape=[jax.ShapeDtypeStruct((n, nout - gate_w), F32),
                   jax.ShapeDtypeStruct((n, gate_w), BF16)],
        compiler_params=_cparams(("parallel", "arbitrary")),
        name="proj_in",
    )(*x_parts, mod, gain.reshape(1, d), w, w)


def _rope_pairs(x, cos, sin):
    lane = lax.broadcasted_iota(jnp.int32, x.shape, 1)
    partner = jnp.where(lane % 2 == 0, pltpu.roll(x, LANES - 1, 1), pltpu.roll(x, 1, 1))
    return x * cos + partner * sin


def _ret_rope(x, cos, sin):
    return jnp.concatenate([_rope_pairs(x[:, j * LANES:(j + 1) * LANES], cos, sin)
                            for j in range(x.shape[1] // LANES)], axis=1)


def _ret_kernel(dec_ref, q_ref, g_ref, kl_ref, kc_ref, vl_ref, vc_ref, cos_ref, sin_ref, gn_ref, o_ref,
                kr_ref, vs_ref, sf_ref, sb_ref, sball_ref, decay_ref, *, ncc, nlc):
    s = pl.program_id(1)
    L = RET_L
    nc = ncc + nlc
    pos = lax.broadcasted_iota(jnp.int32, (L, 1), 0).astype(F32)

    def log_g(direction, h):
        return -jnp.exp(jnp.full((1, 1), dec_ref[direction, h], F32))

    def chunk(c):
        if isinstance(c, int):
            return pl.ds(c * L, L)
        return pl.ds(pl.multiple_of(c * L, L), L)

    def table(c):
        if isinstance(c, int):
            return chunk(nlc + c if c < ncc else c - ncc)
        return chunk(jnp.where(c < ncc, nlc + c, c - ncc))

    def head(x, h, width):
        return x[:, h * width:(h + 1) * width]

    @pl.when(s == 0)
    def _():
        def stage(c, k, v):
            kr_ref[chunk(c), :] = _ret_rope(k, cos_ref[table(c), :], sin_ref[table(c), :]) * (RET_QK_DIM ** -0.5)
            vs_ref[chunk(c), :] = v.astype(vs_ref.dtype)

        for c in range(ncc):
            stage(c, kc_ref[c * L:(c + 1) * L, :], vc_ref[c * L:(c + 1) * L, :])

        def stage_latent(j, carry):
            stage(ncc + j, kl_ref[chunk(j), :], vl_ref[chunk(j), :])
            return carry

        lax.fori_loop(0, nlc, stage_latent, 0)

        diff = (lax.broadcasted_iota(jnp.int32, (L, L), 0)
                - lax.broadcasted_iota(jnp.int32, (L, L), 1)).astype(F32)
        for h in range(RET_HEADS):
            decay_ref[h] = (jnp.where(diff >= 0, jnp.exp(jnp.maximum(diff, 0.0) * log_g(0, h)), 0.0)
                            + jnp.where(diff <= 0, jnp.exp(jnp.maximum(-diff, 0.0) * log_g(1, h)), 0.0))

        sf_ref[...] = jnp.zeros_like(sf_ref)
        sb_ref[...] = jnp.zeros_like(sb_ref)

        def visit(c, carry):
            k = kr_ref[chunk(c), :]
            v = vs_ref[chunk(c), :]
            for h in range(RET_HEADS):
                lgb = log_g(1, h)
                st = sb_ref[h]
                sball_ref[c, h] = st
                kz = head(k, h, RET_QK_DIM) * jnp.exp(pos * lgb)
                sb_ref[h] = jnp.exp(L * lgb) * st + _dot_tn(kz, head(v, h, RET_V_DIM))
            return carry

        for c in reversed(range(ncc)):
            visit(c, 0)
        lax.fori_loop(0, nlc, lambda j, carry: visit(nc - 1 - j, carry), 0)

    q = _ret_rope(q_ref[...], cos_ref[table(s), :], sin_ref[table(s), :])
    k = kr_ref[chunk(s), :]
    v = vs_ref[chunk(s), :]
    g = g_ref[...]
    for h in range(RET_HEADS):
        lgf = log_g(0, h)
        lgb = log_g(1, h)
        qh = head(q, h, RET_QK_DIM)
        kh = head(k, h, RET_QK_DIM)
        vh = head(v, h, RET_V_DIM)
        st_f = sf_ref[h]
        st_b = sball_ref[s, h]
        o = _dot(_dot_nt(qh, kh) * decay_ref[h], vh)
        o = o + _dot(qh, st_f) * jnp.exp((pos + 1.0) * lgf)
        o = o + _dot(qh, st_b) * jnp.exp((L - pos) * lgb)
        sf_ref[h] = jnp.exp(L * lgf) * st_f + _dot_tn(kh * jnp.exp((L - 1.0 - pos) * lgf), vh)
        mu = jnp.mean(o, axis=-1, keepdims=True)
        var = jnp.mean(jnp.square(o - mu), axis=-1, keepdims=True)
        o = (o - mu) * lax.rsqrt(var + GN_EPS) * head(gn_ref[...], h, RET_V_DIM)
        o_ref[:, h * RET_V_DIM:(h + 1) * RET_V_DIM] = (_silu(head(g, h, RET_V_DIM)) * o).astype(o_ref.dtype)


def retention(dm, p, dec, cos_t, sin_t, gn):
    L = RET_L
    ncc, nlc = dm.ctx // L, dm.s // L
    nc = ncc + nlc
    lat_blocks = dm.n_lat // L
    ctx_blk0 = dm.n_lat // dm.ctx
    qk_w = RET_HEADS * RET_QK_DIM
    v_w = RET_HEADS * RET_V_DIM

    def rows(b, s):
        return jnp.where(s < ncc, lat_blocks + b * ncc + s, b * nlc + s - ncc)

    whole = lambda shape: pl.BlockSpec(shape, lambda b, s, d: (0, 0))
    grid_spec = pltpu.PrefetchScalarGridSpec(
        num_scalar_prefetch=1,
        grid=(dm.b, nc),
        in_specs=[pl.BlockSpec((L, qk_w), lambda b, s, d: (rows(b, s), 0)),
                  pl.BlockSpec((L, v_w), lambda b, s, d: (rows(b, s), 2)),
                  pl.BlockSpec((dm.s, qk_w), lambda b, s, d: (b, 1)),
                  pl.BlockSpec((dm.ctx, qk_w), lambda b, s, d: (ctx_blk0 + b, 1)),
                  pl.BlockSpec((dm.s, v_w), lambda b, s, d: (b, 1)),
                  pl.BlockSpec((dm.ctx, v_w), lambda b, s, d: (ctx_blk0 + b, 1)),
                  whole(cos_t.shape), whole(sin_t.shape), whole((1, v_w))],
        out_specs=pl.BlockSpec((L, v_w), lambda b, s, d: (rows(b, s), 0)),
        scratch_shapes=[pltpu.VMEM((nc * L, qk_w), F32),
                        pltpu.VMEM((nc * L, v_w), BF16),
                        pltpu.VMEM((RET_HEADS, RET_QK_DIM, RET_V_DIM), F32),
                        pltpu.VMEM((RET_HEADS, RET_QK_DIM, RET_V_DIM), F32),
                        pltpu.VMEM((nc, RET_HEADS, RET_QK_DIM, RET_V_DIM), F32),
                        pltpu.VMEM((RET_HEADS, L, L), F32)],
    )
    return pl.pallas_call(
        functools.partial(_ret_kernel, ncc=ncc, nlc=nlc),
        grid_spec=grid_spec,
        out_shape=jax.ShapeDtypeStruct((dm.n_all, v_w), BF16),
        compiler_params=_cparams(("parallel", "arbitrary")),
        name="retention",
    )(dec, p, p, p, p, p, p, cos_t, sin_t, gn.reshape(1, v_w))


def _att_prep_kernel(q_ref, k_ref, v_ref, cos_ref, sin_ref, gq_ref, gk_ref, qo_ref, ko_ref, vo_ref,
                     *, lat_tiles):
    is_lat = pl.program_id(0) < lat_tiles
    cos = jnp.where(is_lat, cos_ref[...], 1.0)
    sin = jnp.where(is_lat, sin_ref[...], 0.0)

    hd = ATT_HEAD_DIM

    def prep(x_ref, gain_ref, h):
        xh = x_ref[:, h * hd:(h + 1) * hd]
        xn = xh * lax.rsqrt(jnp.mean(xh * xh, axis=-1, keepdims=True) + NORM_EPS) * gain_ref[...]
        return _rope_pairs(xn, cos, sin)

    for h in range(ATT_HEADS):
        qo_ref[:, h * hd:(h + 1) * hd] = (prep(q_ref, gq_ref, h) * ATT_SCORE_SCALE).astype(qo_ref.dtype)
    ones = jnp.ones((ATT_V_ROWS - hd, vo_ref.shape[1]), vo_ref.dtype)
    for h in range(ATT_KV_HEADS):
        ko_ref[:, h * hd:(h + 1) * hd] = prep(k_ref, gk_ref, h).astype(ko_ref.dtype)
        vo_ref[h * ATT_V_ROWS:h * ATT_V_ROWS + hd, :] = v_ref[:, h * hd:(h + 1) * hd].T.astype(vo_ref.dtype)
        vo_ref[h * ATT_V_ROWS + hd:(h + 1) * ATT_V_ROWS, :] = ones


def att_prep(dm, p, cos_t, sin_t, gq, gk):
    tm = dm.tile(TM_PREP)
    n = dm.n_all
    lat_tiles, per_batch = dm.n_lat // tm, dm.s // tm
    qw = ATT_HEADS * ATT_HEAD_DIM
    kw = ATT_KV_HEADS * ATT_HEAD_DIM
    q_blk, k_blk, v_blk = ATT_Q_COL // qw, ATT_K_COL // kw, ATT_V_COL // kw
    tab = lambda i: (jnp.where(i < lat_tiles, i % per_batch, 0), 0)
    return pl.pallas_call(
        functools.partial(_att_prep_kernel, lat_tiles=lat_tiles),
        grid=(n // tm,),
        in_specs=[pl.BlockSpec((tm, qw), lambda i: (i, q_blk)),
                  pl.BlockSpec((tm, kw), lambda i: (i, k_blk)),
                  pl.BlockSpec((tm, kw), lambda i: (i, v_blk)),
                  pl.BlockSpec((tm, ATT_HEAD_DIM), tab),
                  pl.BlockSpec((tm, ATT_HEAD_DIM), tab),
                  pl.BlockSpec((1, ATT_HEAD_DIM), lambda i: (0, 0)),
                  pl.BlockSpec((1, ATT_HEAD_DIM), lambda i: (0, 0))],
        out_specs=[pl.BlockSpec((tm, qw), lambda i: (i, 0)),
                   pl.BlockSpec((tm, kw), lambda i: (i, 0)),
                   pl.BlockSpec((ATT_KV_HEADS * ATT_V_ROWS, tm), lambda i: (0, i))],
        out_shape=[jax.ShapeDtypeStruct((n, qw), BF16),
                   jax.ShapeDtypeStruct((n, kw), BF16),
                   jax.ShapeDtypeStruct((ATT_KV_HEADS * ATT_V_ROWS, n), BF16)],
        compiler_params=_cparams(("parallel",)),
        name="att_prep",
    )(p, p, p, cos_t, sin_t, gq.reshape(1, -1), gk.reshape(1, -1))


def _attn_kernel(*refs, with_latent_keys):
    ns = ATT_STAGES
    if with_latent_keys:
        qt_ref, kl_ref, vtl_ref, kc_ref, vtc_ref, o_ref, acc_ref = refs[:7]
    else:
        qt_ref, kc_ref, vtc_ref, o_ref, acc_ref = refs[:5]
    s_refs, p_refs = refs[-2 * ns:-ns], refs[-ns:]
    hd = ATT_HEAD_DIM
    grp = ATT_HEADS // ATT_KV_HEADS
    tq = qt_ref.shape[0]
    nq = grp * tq
    tk = kc_ref.shape[0]
    n = 1 + (kl_ref.shape[0] // tk if with_latent_keys else 0)
    q2 = jnp.concatenate([qt_ref[:, j * hd:(j + 1) * hd] for j in range(grp)], axis=0)

    def key_chunk(j):
        if isinstance(j, int) and j == 0:
            return kc_ref[...]
        return kl_ref[pl.ds(pl.multiple_of((j - 1) * tk, tk), tk), :]

    def value_chunk(j):
        if isinstance(j, int) and j == 0:
            return vtc_ref[...]
        return vtl_ref[:, pl.ds(pl.multiple_of((j - 1) * tk, tk), tk)]

    def max_keys(x):
        groups = math.gcd(tk // SUBLANES, ATT_REDUCE_GROUPS)
        part = jnp.max(x.reshape(groups, tk // groups, nq), axis=0)
        return jnp.max(part, axis=0, keepdims=True)

    def time_step(t, slot, carry, do_scores=True, do_softmax=True, do_values=True):
        m, alpha = carry
        if do_scores:
            s_refs[slot][...] = _dot_nt(key_chunk(t), q2)
        if do_values:
            pv = jnp.dot(value_chunk(t - 2), p_refs[(slot - 2) % ns][...], preferred_element_type=F32)
            acc_ref[...] = acc_ref[...] * alpha + pv
        if do_softmax:
            st = s_refs[(slot - 1) % ns][...]
            m_new = jnp.maximum(m, max_keys(st))
            p_refs[(slot - 1) % ns][...] = jnp.exp2(st - m_new).astype(BF16)
            alpha = jnp.exp2(m - m_new)
            m = m_new
        return m, alpha

    acc_ref[...] = jnp.zeros_like(acc_ref)
    carry = (jnp.full((1, nq), -1e30, F32), jnp.zeros((1, nq), F32))
    fill = min(3, n + 2)
    for t in range(fill):
        carry = time_step(t, t % ns, carry, t < n, 0 <= t - 1 < n, 0 <= t - 2 < n)
    steady = max(n - fill, 0)
    body_steps = ns * ATT_UNROLL

    def unrolled(i, cr):
        for k in range(body_steps):
            cr = time_step(fill + body_steps * i + k, (fill + k) % ns, cr)
        return cr

    if steady // body_steps:
        carry = lax.fori_loop(0, steady // body_steps, unrolled, carry)
    for t in range(fill + body_steps * (steady // body_steps), max(n, fill)):
        carry = time_step(t, t % ns, carry)
    for t in range(max(n, fill), n + 2):
        carry = time_step(t, t % ns, carry, False, t - 1 < n, True)
    out = (acc_ref[0:hd, :] / acc_ref[hd:hd + 1, :]).T
    for j in range(grp):
        o_ref[:, j * hd:(j + 1) * hd] = out[j * tq:(j + 1) * tq].astype(o_ref.dtype)


def attention(dm, qt, kn, vt, latent):
    hd = ATT_HEAD_DIM
    grp = ATT_HEADS // ATT_KV_HEADS
    tk = dm.ctx
    assert dm.s % tk == 0
    ctx_blk0 = dm.n_lat // dm.ctx
    kc_spec = pl.BlockSpec((tk, hd), lambda b, h, i: (ctx_blk0 + b, h))
    vtc_spec = pl.BlockSpec((ATT_V_ROWS, tk), lambda b, h, i: (h, ctx_blk0 + b))
    if latent:
        tq = math.gcd(TQ_ATT, dm.s)
        nq = dm.s // tq
        q_spec = pl.BlockSpec((tq, grp * hd), lambda b, h, i: (b * nq + i, h))
        kl_spec = pl.BlockSpec((dm.s, hd), lambda b, h, i: (b, h))
        vtl_spec = pl.BlockSpec((ATT_V_ROWS, dm.s), lambda b, h, i: (h, b))
        in_specs = [q_spec, kl_spec, vtl_spec, kc_spec, vtc_spec]
        args = (qt, kn, vt, kn, vt)
        rows = dm.n_lat
    else:
        tq, nq = dm.ctx, 1
        q_spec = pl.BlockSpec((tq, grp * hd), lambda b, h, i: (ctx_blk0 + b, h))
        in_specs = [q_spec, kc_spec, vtc_spec]
        args = (qt, kn, vt)
        rows = dm.b * dm.ctx
    return pl.pallas_call(
        functools.partial(_attn_kernel, with_latent_keys=latent),
        grid=(dm.b, ATT_KV_HEADS, nq),
        in_specs=in_specs,
        out_specs=pl.BlockSpec((tq, grp * hd), lambda b, h, i: (b * nq + i, h)),
        out_shape=jax.ShapeDtypeStruct((rows, ATT_HEADS * hd), BF16),
        scratch_shapes=([pltpu.VMEM((ATT_V_ROWS, grp * tq), F32)]
                        + [pltpu.VMEM((tk, grp * tq), F32)] * ATT_STAGES
                        + [pltpu.VMEM((tk, grp * tq), BF16)] * ATT_STAGES),
        compiler_params=_cparams(("parallel", "parallel", "arbitrary")),
        name="attention_latent" if latent else "attention_ctx",
    )(*args)


def _pool_kernel(prev_ref, u_ref, next_ref, w_ref, sc_ref, o_ref, a_ref, b_ref, *, dm):
    tm = u_ref.shape[0]
    H = POOL_HALO
    i = pl.program_id(0)
    lat_tiles = dm.n_lat // tm
    is_lat = i < lat_tiles
    n_seg = jnp.where(is_lat, dm.s, dm.ctx)
    base = jnp.where(is_lat, (i * tm) % dm.s, 0)
    r = lax.broadcasted_iota(jnp.int32, (tm + 2 * H, 1), 0) - H
    pos = jnp.where(is_lat, base + r, r % dm.ctx)
    win = pl.ds(H, tm + 2 * H)

    def shifted(ref, d):
        return ref[pl.ds(H + d, tm + 2 * H), :]

    zeros = jnp.zeros((H, POOL_GROUP_DIM), F32)
    for ref in (a_ref, b_ref):
        ref[pl.ds(0, H), :] = zeros
        ref[pl.ds(tm + 3 * H, H), :] = zeros

    for gi, w in enumerate(POOL_WINDOWS):
        cols = slice(gi * POOL_GROUP_DIM, (gi + 1) * POOL_GROUP_DIM)
        half = w // 2
        a_ref[pl.ds(H, H), :] = prev_ref[:, cols]
        a_ref[pl.ds(2 * H, tm), :] = u_ref[:, cols]
        a_ref[pl.ds(2 * H + tm, H), :] = next_ref[:, cols]
        b_ref[win, :] = jnp.where(pos >= 1, shifted(a_ref, -1), 0.0)
        step = 1
        while step < half:
            b_ref[win, :] = b_ref[win, :] + jnp.where(pos >= step, shifted(b_ref, -step), 0.0)
            step *= 2
        step = 1
        while step < half:
            a_ref[win, :] = a_ref[win, :] + jnp.where(pos + step < n_seg, shifted(a_ref, step), 0.0)
            step *= 2
        tile = pl.ds(2 * H, tm)
        pt = pos[H:H + tm]
        cnt = (jnp.minimum(pt + half, n_seg) - jnp.maximum(pt - half, 0)).astype(F32)
        u = u_ref[:, cols]
        diff = (a_ref[tile, :] + b_ref[tile, :]) / cnt - u
        o_ref[:, cols] = (_dot(diff, w_ref[gi]) * sc_ref[:, cols]).astype(o_ref.dtype)


def pool_mixer(dm, p, pool_w, pool_scale):
    tm = dm.tile(TM_POOL)
    n = dm.n_all
    H = POOL_HALO
    width = len(POOL_WINDOWS) * POOL_GROUP_DIM
    col = 2560 // width
    per = tm // H
    last = n // H - 1
    return pl.pallas_call(
        functools.partial(_pool_kernel, dm=dm),
        grid=(n // tm,),
        in_specs=[pl.BlockSpec((H, width), lambda i: (jnp.maximum(i * per - 1, 0), col)),
                  pl.BlockSpec((tm, width), lambda i: (i, col)),
                  pl.BlockSpec((H, width), lambda i: (jnp.minimum((i + 1) * per, last), col)),
                  pl.BlockSpec(pool_w.shape, lambda i: (0, 0, 0)),
                  pl.BlockSpec((1, width), lambda i: (0, 0))],
        out_specs=pl.BlockSpec((tm, width), lambda i: (i, 0)),
        out_shape=jax.ShapeDtypeStruct((n, width), BF16),
        scratch_shapes=[pltpu.VMEM((tm + 4 * H, POOL_GROUP_DIM), F32),
                        pltpu.VMEM((tm + 4 * H, POOL_GROUP_DIM), F32)],
        compiler_params=_cparams(("parallel",)),
        name="pool_mixer",
    )(p, p, p, pool_w, pool_scale.reshape(1, width))


def _merge_kernel(yr_ref, ya_ref, yp_ref, gl_ref, *refs, lat_tiles):
    mod_ref, wb_ref, wo_ref, o_ref = refs[-4:]
    x_refs = refs[:-4]
    d = o_ref.shape[1]
    mixed = None
    for j, y_ref in enumerate((yr_ref, ya_ref, yp_ref)):
        gate = gl_ref[:, j * d:(j + 1) * d].astype(F32)
        term = gate * jnp.dot(y_ref[...], wb_ref[j], preferred_element_type=F32)
        mixed = term if mixed is None else mixed + term
    y = mod_ref[0][2:3] * _dot(mixed, wo_ref[...])

    def finish(x_ref):
        o_ref[...] = x_ref[...] + y

    _for_row_source(x_refs, lat_tiles, finish)


def merge(dm, n_rows, yr, ya, yp, gates, x_parts, mod, wb, wo):
    tm = dm.tile(TM_MERGE)
    d = dm.d
    bw = yr.shape[1]
    row = dm.mod_row(tm)
    return pl.pallas_call(
        functools.partial(_merge_kernel, lat_tiles=dm.n_lat // tm),
        grid=(n_rows // tm,),
        in_specs=[pl.BlockSpec((tm, bw), lambda i: (i, 0)),
                  pl.BlockSpec((tm, bw), lambda i: (i, 0)),
                  pl.BlockSpec((tm, bw), lambda i: (i, 0)),
                  pl.BlockSpec((tm, 3 * d), lambda i: (i, 0))]
        + _row_sources(dm, tm, x_parts) + [
            pl.BlockSpec((1, 6, d), lambda i: (row(i), 0, 0)),
            pl.BlockSpec(wb.shape, lambda i: (0, 0, 0)),
            pl.BlockSpec(wo.shape, lambda i: (0, 0))],
        out_specs=pl.BlockSpec((tm, d), lambda i: (i, 0)),
        out_shape=jax.ShapeDtypeStruct((n_rows, d), F32),
        compiler_params=_cparams(("parallel",)),
        name="merge",
    )(yr, ya, yp, gates, *x_parts, mod, wb, wo)


def _ffn_kernel(x_ref, mod_ref, gain_ref, w1_ref, w3_ref, w2_ref, o_ref):
    m = mod_ref[0]
    h = _norm_mod(x_ref[...], gain_ref[...], m[3:4], m[4:5]).astype(BF16)
    fdim = w1_ref.shape[1]
    y = None
    for lo in range(0, fdim, MOE_COL_CHUNK):
        cols = slice(lo, min(lo + MOE_COL_CHUNK, fdim))
        a = jnp.dot(h, w1_ref[:, cols], preferred_element_type=F32)
        b = jnp.dot(h, w3_ref[:, cols], preferred_element_type=F32)
        part = _dot(_silu(a) * b, w2_ref[cols, :])
        y = part if y is None else y + part
    o_ref[...] = x_ref[...] + m[5:6] * y


def ffn_dense(dm, x, mod, gain, w1, w3, w2):
    n, d = x.shape
    fdim = w1.shape[1]
    tm = dm.tile(TM_FFN)
    row = dm.mod_row(tm)
    once = dict(pipeline_mode=pl.Buffered(1))
    return pl.pallas_call(
        _ffn_kernel,
        grid=(n // tm,),
        in_specs=[pl.BlockSpec((tm, d), lambda i: (i, 0)),
                  pl.BlockSpec((1, 6, d), lambda i: (row(i), 0, 0)),
                  pl.BlockSpec((1, d), lambda i: (0, 0)),
                  pl.BlockSpec((d, fdim), lambda i: (0, 0), **once),
                  pl.BlockSpec((d, fdim), lambda i: (0, 0), **once),
                  pl.BlockSpec((fdim, d), lambda i: (0, 0), **once)],
        out_specs=pl.BlockSpec((tm, d), lambda i: (i, 0)),
        out_shape=jax.ShapeDtypeStruct((n, d), F32),
        compiler_params=_cparams(("parallel",)),
        name="ffn_dense",
    )(x, mod, gain.reshape(1, d), w1, w3, w2)


def _router_kernel(x_ref, mod_ref, gain_ref, wr_ref, br_ref, h_ref, slot_ref, slot_t_ref, wt_ref, cum_ref):
    tm = x_ref.shape[0]
    m = mod_ref[0]
    h = _norm_mod(x_ref[...], gain_ref[...], m[3:4], m[4:5])
    h_ref[...] = h.astype(BF16)
    logits = _dot3(h, wr_ref[...]) + br_ref[...]
    lane = lax.broadcasted_iota(jnp.int32, logits.shape, 1)
    m1 = jnp.max(logits, axis=-1, keepdims=True)
    i1 = jnp.min(jnp.where(logits == m1, lane, LANES), axis=-1, keepdims=True)
    rest = jnp.where(lane == i1, -jnp.inf, logits)
    m2 = jnp.max(rest, axis=-1, keepdims=True)
    i2 = jnp.min(jnp.where(rest == m2, lane, LANES), axis=-1, keepdims=True)
    e = jnp.exp(m2 - m1)
    w1 = 1.0 / (1.0 + e)
    w2 = e / (1.0 + e)
    wt_ref[...] = jnp.where(lane == i1, w1, jnp.where(lane == i2, w2, 0.0))
    routed = jnp.where((lane == i1) | (lane == i2), 1.0, 0.0)
    sub = SUB_MOE
    lower = (lax.broadcasted_iota(jnp.int32, (sub, sub), 0) > lax.broadcasted_iota(jnp.int32, (sub, sub), 1))
    lower = lower.astype(BF16)
    offset = jnp.zeros((1, LANES), F32)
    cum_ref[...] = jnp.zeros_like(cum_ref)
    for blk in range(tm // sub):
        rows = slice(blk * sub, (blk + 1) * sub)
        mask = routed[rows]
        cum_ref[0, blk:blk + 1, :] = offset.astype(jnp.int32)
        rank = jnp.dot(lower, mask.astype(BF16), preferred_element_type=F32) + offset
        slot_ref[rows, :] = jnp.where(mask > 0.0, rank, -1.0)
        count = jnp.sum(mask, axis=0, keepdims=True)
        offset = offset + jnp.ceil(count * (1.0 / SLOT_ALIGN)) * SLOT_ALIGN
    cum_ref[0, tm // sub:tm // sub + 1, :] = offset.astype(jnp.int32)
    slot_t_ref[...] = slot_ref[...].T[:SUBLANES, :]


def moe_route(dm, n_rows, x, mod, gain, wr_pad, br_pad):
    tm = dm.tile(TM_MOE)
    d = dm.d
    row = dm.mod_row(tm)
    nt = n_rows // tm
    cum_rows = -(-(tm // SUB_MOE + 1) // SUBLANES) * SUBLANES
    return pl.pallas_call(
        _router_kernel,
        grid=(nt,),
        in_specs=[pl.BlockSpec((tm, d), lambda i: (i, 0)),
                  pl.BlockSpec((1, 6, d), lambda i: (row(i), 0, 0)),
                  pl.BlockSpec((1, d), lambda i: (0, 0)),
                  pl.BlockSpec((d, LANES), lambda i: (0, 0)),
                  pl.BlockSpec((1, LANES), lambda i: (0, 0))],
        out_specs=[pl.BlockSpec((tm, d), lambda i: (i, 0)),
                   pl.BlockSpec((tm, LANES), lambda i: (i, 0)),
                   pl.BlockSpec((SUBLANES, tm), lambda i: (0, i)),
                   pl.BlockSpec((tm, LANES), lambda i: (i, 0)),
                   pl.BlockSpec((1, cum_rows, LANES), lambda i: (i, 0, 0))],
        out_shape=[jax.ShapeDtypeStruct((n_rows, d), BF16),
                   jax.ShapeDtypeStruct((n_rows, LANES), F32),
                   jax.ShapeDtypeStruct((SUBLANES, n_rows), F32),
                   jax.ShapeDtypeStruct((n_rows, LANES), F32),
                   jax.ShapeDtypeStruct((nt, cum_rows, LANES), jnp.int32)],
        compiler_params=_cparams(("parallel",)),
        name="moe_router",
    )(x, mod, gain.reshape(1, d), wr_pad, br_pad)


def _moe_windows(base_ref, row0_ref, e, step=None):
    i, j = pl.program_id(0), pl.program_id(1)
    if step is not None:
        i, j = step // pl.num_programs(1), step % pl.num_programs(1)
    start = base_ref[i, j * N_EXPERTS + e]
    nwin = (base_ref[i, (j + 1) * N_EXPERTS + e] - start + WIN_MOE - 1) // WIN_MOE
    return start, nwin, row0_ref[i, e]


def _moe_gather_kernel(base_ref, row0_ref, h_ref, slot_t_ref, xs_in_ref, xs_ref, buf_ref, sem):
    del xs_in_ref
    win, sub = WIN_MOE, SUB_MOE
    spare = N_EXPERTS

    def gathered(e, start, w):
        want = (lax.broadcasted_iota(jnp.int32, (win, sub), 0) + (start + w * win)).astype(F32)
        onehot = (slot_t_ref[e:e + 1, :] == want).astype(BF16)
        return jnp.dot(onehot, h_ref[...], preferred_element_type=F32).astype(BF16)

    def copy(e, buf, w):
        start, _, row0 = _moe_windows(base_ref, row0_ref, e)
        rows = pl.ds(pl.multiple_of(row0 + start + w * win, SLOT_ALIGN), win)
        return pltpu.make_async_copy(buf_ref.at[buf], xs_ref.at[rows, :], sem.at[buf])

    for e in range(N_EXPERTS):
        start, _, _ = _moe_windows(base_ref, row0_ref, e)
        buf_ref[e] = gathered(e, start, 0)
        copy(e, e, 0).start()

    for e in range(N_EXPERTS):
        start, nwin, _ = _moe_windows(base_ref, row0_ref, e)

        def extra(w, carry, e=e, start=start):
            buf_ref[spare] = gathered(e, start, w)
            cp = copy(e, spare, w)
            cp.start()
            cp.wait()
            return carry

        lax.fori_loop(1, jnp.maximum(nwin, 1), extra, 0)

    for e in range(N_EXPERTS):
        copy(e, e, 0).wait()


def moe_gather(dm, n_rows, h, slot_t, base, row0, cap):
    tm, sub, d = dm.tile(TM_MOE), SUB_MOE, dm.d
    nsub = tm // sub
    grid_spec = pltpu.PrefetchScalarGridSpec(
        num_scalar_prefetch=2,
        grid=(n_rows // tm, nsub),
        in_specs=[pl.BlockSpec((sub, d), lambda i, j, b, r: (i * nsub + j, 0)),
                  pl.BlockSpec((SUBLANES, sub), lambda i, j, b, r: (0, i * nsub + j)),
                  pl.BlockSpec(memory_space=pl.ANY)],
        out_specs=pl.BlockSpec(memory_space=pl.ANY),
        scratch_shapes=[pltpu.VMEM((N_EXPERTS + 1, WIN_MOE, d), BF16),
                        pltpu.SemaphoreType.DMA((N_EXPERTS + 1,))],
    )
    return pl.pallas_call(
        _moe_gather_kernel,
        grid_spec=grid_spec,
        out_shape=jax.ShapeDtypeStruct((cap, d), BF16),
        input_output_aliases={4: 0},
        compiler_params=_cparams(("arbitrary", "arbitrary")),
        name="moe_gather",
    )(base, row0, h, slot_t, jnp.zeros((cap, d), BF16))


def _moe_ffn_kernel(bexp_ref, bvalid_ref, xs_ref, w1_ref, w3_ref, w2_ref, o_ref):
    del bexp_ref
    valid = bvalid_ref[pl.program_id(0)] == 1

    @pl.when(valid)
    def _():
        xs = xs_ref[...]
        fdim = w1_ref.shape[2]
        y = None
        for lo in range(0, fdim, MOE_COL_CHUNK):
            cols = slice(lo, min(lo + MOE_COL_CHUNK, fdim))
            a = jnp.dot(xs, w1_ref[0, :, cols], preferred_element_type=F32)
            b = jnp.dot(xs, w3_ref[0, :, cols], preferred_element_type=F32)
            part = _dot(_silu(a) * b, w2_ref[0, cols, :])
            y = part if y is None else y + part
        o_ref[...] = y.astype(o_ref.dtype)

    @pl.when(jnp.logical_not(valid))
    def _():
        o_ref[...] = jnp.zeros_like(o_ref)


def moe_ffn(xs, blk_expert, blk_valid, w1, w3, w2):
    cap, d = xs.shape
    _, _, fdim = w1.shape
    blk = BLK_MOE
    once = dict(pipeline_mode=pl.Buffered(1))
    grid_spec = pltpu.PrefetchScalarGridSpec(
        num_scalar_prefetch=2,
        grid=(cap // blk,),
        in_specs=[pl.BlockSpec((blk, d), lambda k, be, bv: (k, 0)),
                  pl.BlockSpec((1, d, fdim), lambda k, be, bv: (be[k], 0, 0), **once),
                  pl.BlockSpec((1, d, fdim), lambda k, be, bv: (be[k], 0, 0), **once),
                  pl.BlockSpec((1, fdim, d), lambda k, be, bv: (be[k], 0, 0), **once)],
        out_specs=pl.BlockSpec((blk, d), lambda k, be, bv: (k, 0)),
    )
    return pl.pallas_call(
        _moe_ffn_kernel,
        grid_spec=grid_spec,
        out_shape=jax.ShapeDtypeStruct((cap, d), BF16),
        compiler_params=_cparams(("arbitrary",)),
        name="moe_ffn",
    )(blk_expert, blk_valid, xs, w1, w3, w2)


def _moe_scatter_kernel(base_ref, row0_ref, slot_ref, wt_ref, x_ref, mod_ref, gain_ref, ys_ref, o_ref,
                        buf_ref, y_ref, sem, *, final):
    win, sub = WIN_MOE, SUB_MOE
    nbuf = N_EXPERTS + 1
    step = pl.program_id(0) * pl.num_programs(1) + pl.program_id(1)
    last_step = pl.num_programs(0) * pl.num_programs(1) - 1
    mine = (step % 2) * nbuf
    spare = mine + N_EXPERTS

    def copy(e, buf, w, at=None):
        start, _, row0 = _moe_windows(base_ref, row0_ref, e, at)
        rows = pl.ds(pl.multiple_of(row0 + start + w * win, SLOT_ALIGN), win)
        return pltpu.make_async_copy(ys_ref.at[rows, :], buf_ref.at[buf], sem.at[buf])

    @pl.when(step == 0)
    def _():
        for e in range(N_EXPERTS):
            copy(e, e, 0).start()

    @pl.when(step < last_step)
    def _():
        for e in range(N_EXPERTS):
            copy(e, nbuf - mine + e, 0, at=step + 1).start()

    def scattered(e, buf, w):
        start, _, _ = _moe_windows(base_ref, row0_ref, e)
        want = (lax.broadcasted_iota(jnp.int32, (sub, win), 1) + (start + w * win)).astype(F32)
        onehot = (slot_ref[:, e:e + 1] == want).astype(BF16)
        return wt_ref[:, e:e + 1] * jnp.dot(onehot, buf_ref[buf], preferred_element_type=F32)

    y_ref[...] = jnp.zeros_like(y_ref)
    for e in range(N_EXPERTS):
        _, nwin, _ = _moe_windows(base_ref, row0_ref, e)

        def extra(w, carry, e=e):
            cp = copy(e, spare, w)
            cp.start()
            cp.wait()
            y_ref[...] += scattered(e, spare, w)
            return carry

        lax.fori_loop(1, jnp.maximum(nwin, 1), extra, 0)

    y = y_ref[...]
    for e in range(N_EXPERTS):
        copy(e, mine + e, 0).wait()
        y = y + scattered(e, mine + e, 0)
    x = x_ref[...] + mod_ref[0][5:6] * y
    if final:
        x = x * lax.rsqrt(jnp.mean(x * x, axis=-1, keepdims=True) + NORM_EPS) * gain_ref[...]
    o_ref[...] = x


def moe_scatter(dm, n_rows, ys, slot, wt, base, row0, x, mod, final_gain):
    tm, sub, d = dm.tile(TM_MOE), SUB_MOE, dm.d
    nsub = tm // sub
    row = dm.mod_row(sub)
    final = final_gain is not None
    gain = final_gain if final else jnp.ones((d,), F32)
    tok = lambda i, j, b, r: (i * nsub + j, 0)
    grid_spec = pltpu.PrefetchScalarGridSpec(
        num_scalar_prefetch=2,
        grid=(n_rows // tm, nsub),
        in_specs=[pl.BlockSpec((sub, LANES), tok),
                  pl.BlockSpec((sub, LANES), tok),
                  pl.BlockSpec((sub, d), tok),
                  pl.BlockSpec((1, 6, d), lambda i, j, b, r: (row(i * nsub + j), 0, 0)),
                  pl.BlockSpec((1, d), lambda i, j, b, r: (0, 0)),
                  pl.BlockSpec(memory_space=pl.ANY)],
        out_specs=pl.BlockSpec((sub, d), tok),
        scratch_shapes=[pltpu.VMEM((2 * (N_EXPERTS + 1), WIN_MOE, d), BF16),
                        pltpu.VMEM((sub, d), F32),
                        pltpu.SemaphoreType.DMA((2 * (N_EXPERTS + 1),))],
    )
    return pl.pallas_call(
        functools.partial(_moe_scatter_kernel, final=final),
        grid_spec=grid_spec,
        out_shape=jax.ShapeDtypeStruct((n_rows, d), F32),
        compiler_params=_cparams(("arbitrary", "arbitrary")),
        name="moe_scatter",
    )(base, row0, slot, wt, x, mod, gain.reshape(1, d), ys)


def moe_layout(dm, n_rows, cum):
    tm = dm.tile(TM_MOE)
    nt, nsub = n_rows // tm, tm // SUB_MOE
    base = cum[:, :nsub + 1, :N_EXPERTS]
    total = base[:, nsub, :]
    per_expert = jnp.sum(total, axis=0)
    padded = (per_expert + WIN_MOE + BLK_MOE - 1) // BLK_MOE * BLK_MOE
    expert_row0 = jnp.cumsum(padded) - padded
    row0 = expert_row0[None, :] + jnp.cumsum(total, axis=0) - total
    slack = nt * nsub * N_EXPERTS * (SLOT_ALIGN - 1) + N_EXPERTS * (WIN_MOE + BLK_MOE)
    nblk = -(-(2 * n_rows + slack) // BLK_MOE)
    ends = jnp.cumsum(padded // BLK_MOE)
    k = jnp.arange(nblk, dtype=jnp.int32)
    blk_expert = jnp.minimum(jnp.sum(k[:, None] >= ends[None, :], axis=1), N_EXPERTS - 1).astype(jnp.int32)
    blk_valid = (k < ends[-1]).astype(jnp.int32)
    return base.reshape(nt, (nsub + 1) * N_EXPERTS), row0.astype(jnp.int32), blk_expert, blk_valid, nblk * BLK_MOE


def _final_norm_kernel(x_ref, gain_ref, o_ref):
    x = x_ref[...]
    o_ref[...] = x * lax.rsqrt(jnp.mean(x * x, axis=-1, keepdims=True) + NORM_EPS) * gain_ref[...]


def final_rmsnorm(dm, n_rows, x, gain):
    tm = dm.tile(TM_FFN)
    d = dm.d
    return pl.pallas_call(
        _final_norm_kernel,
        grid=(n_rows // tm,),
        in_specs=[pl.BlockSpec((tm, d), lambda i: (i, 0)), pl.BlockSpec((1, d), lambda i: (0, 0))],
        out_specs=pl.BlockSpec((tm, d), lambda i: (i, 0)),
        out_shape=jax.ShapeDtypeStruct((n_rows, d), F32),
        compiler_params=_cparams(("parallel",)),
        name="final_norm",
    )(x, gain.reshape(1, d))


def _rope_tables(s, ctx):
    def angles(pos, dim):
        freqs = 1.0 / (ROPE_THETA ** (jnp.arange(0, dim, 2, dtype=F32) / dim))
        return pos[:, None] * freqs[None, :]

    rows = s // GRID_W
    row = jnp.repeat(jnp.arange(rows, dtype=F32), GRID_W)
    col = jnp.tile(jnp.arange(GRID_W, dtype=F32), rows)
    half = ATT_HEAD_DIM // 2
    ang_att = jnp.concatenate([angles(row, half), angles(col, half)], axis=-1)
    sign = jnp.tile(jnp.array([-1.0, 1.0], F32), LANES // 2)

    def tables(ang):
        reps = LANES // (2 * ang.shape[1])
        cos = jnp.tile(jnp.repeat(jnp.cos(ang), 2, axis=-1), (1, reps))
        sin = jnp.tile(jnp.repeat(jnp.sin(ang), 2, axis=-1), (1, reps)) * sign
        return cos, sin

    ang_ret = angles(jnp.arange(s, dtype=F32), RET_QK_DIM)
    ang_ret = jnp.concatenate([ang_ret, jnp.zeros((ctx, RET_QK_DIM // 2), F32)], axis=0)
    return tables(ang_att) + tables(ang_ret)


def kernel(x, c, ctx, c_ctx, w_ada, b_ada, norm_mix, norm_ffn, w_in, ret_decay, ret_gn, attn_qn, attn_kn, pool_w, pool_scale, w_branch, w_out, ffn_w1, ffn_w3, ffn_w2, moe_router, moe_router_b, moe_w1, moe_w3, moe_w2, final_norm):
    b, s, d = x.shape
    n_ctx = ctx.shape[1]
    depth = w_in.shape[0]
    dm = Dims(b, s, n_ctx, d)

    x_parts = (x.reshape(b * s, d), ctx.reshape(b * n_ctx, d))
    mod_rows = 2 * SUBLANES
    cc = jnp.concatenate([c, c_ctx[None, :], jnp.zeros((mod_rows - b - 1, d), F32)], axis=0)
    mods = ada_table(cc, w_ada, b_ada).reshape(depth, mod_rows, 6, d)

    att_cos, att_sin, ret_cos, ret_sin = _rope_tables(s, n_ctx)

    for i in range(depth):
        need_ctx = i < depth - 1
        n_rows = dm.n_all if need_ctx else dm.n_lat
        mod = mods[i]
        p, gates = proj_in(dm, x_parts, mod, norm_mix[i], w_in[i].astype(BF16))
        y_ret = retention(dm, p, ret_decay[i], ret_cos, ret_sin, ret_gn[i])
        y_pool = pool_mixer(dm, p, pool_w[i].astype(BF16), pool_scale[i])
        qt, kn, vt = att_prep(dm, p, att_cos, att_sin, attn_qn[i], attn_kn[i])
        y_att = attention(dm, qt, kn, vt, latent=True)
        if need_ctx:
            y_att = jnp.concatenate([y_att, attention(dm, qt, kn, vt, latent=False)], axis=0)
        xa = merge(dm, n_rows, y_ret, y_att, y_pool, gates, x_parts, mod,
                   w_branch[i].astype(BF16), w_out[i].astype(BF16))
        j = i // 2
        last = i == depth - 1
        if i % 2 == 0:
            xa = ffn_dense(dm, xa, mod, norm_ffn[i], ffn_w1[j].astype(BF16), ffn_w3[j].astype(BF16),
                           ffn_w2[j].astype(BF16))
            if last:
                xa = final_rmsnorm(dm, n_rows, xa, final_norm)
        else:
            wr = jnp.zeros((d, LANES), F32).at[:, :N_EXPERTS].set(moe_router[j])
            br = jnp.full((1, LANES), -1e30, F32).at[0, :N_EXPERTS].set(moe_router_b[j])
            h, slot, slot_t, wt, cnt = moe_route(dm, n_rows, xa, mod, norm_ffn[i], wr, br)
            base, row0, blk_expert, blk_valid, cap = moe_layout(dm, n_rows, cnt)
            xs = moe_gather(dm, n_rows, h, slot_t, base, row0, cap)
            ys = moe_ffn(xs, blk_expert, blk_valid,
                         moe_w1[j].astype(BF16), moe_w3[j].astype(BF16), moe_w2[j].astype(BF16))
            xa = moe_scatter(dm, n_rows, ys, slot, wt, base, row0, xa, mod, final_norm if last else None)
        x_parts = (xa,)
    return xa[:dm.n_lat].reshape(b, s, d)
```

```python
import functools
import math

import jax
import jax.numpy as jnp
from jax import lax
from jax.experimental import pallas as pl
from jax.experimental.pallas import tpu as pltpu

F32 = jnp.float32
BF16 = jnp.bfloat16

GRID_W = 64
RET_HEADS = 4
RET_QK_DIM = 64
RET_V_DIM = 128
ATT_HEADS = 4
ATT_KV_HEADS = 2
ATT_HEAD_DIM = 128
ROPE_THETA = 10000.0
POOL_WINDOWS = (2, 4, 8, 16)
POOL_GROUP_DIM = 128
N_EXPERTS = 8
NORM_EPS = 1e-6
GN_EPS = 1e-5

LANES = 128
SUBLANES = 8
VMEM_LIMIT_BYTES = 56 * 1024 * 1024

TM_PROJ = 1024
TN_PROJ = 1536
TM_PREP = 2048
TM_POOL = 2048
TM_MERGE = 1024
TM_FFN = 1024
TM_MOE = 2048
MOE_COL_CHUNK = 512
BLK_MOE = 512
WIN_MOE = 256
SUB_MOE = 512
SLOT_ALIGN = 16
RET_L = 256
TQ_ATT = 1024
ATT_Q_COL = 2 * RET_HEADS * RET_QK_DIM + 2 * RET_HEADS * RET_V_DIM
ATT_K_COL = ATT_Q_COL + ATT_HEADS * ATT_HEAD_DIM
ATT_V_COL = ATT_K_COL + ATT_KV_HEADS * ATT_HEAD_DIM
ATT_SCORE_SCALE = (ATT_HEAD_DIM ** -0.5) * math.log2(math.e)
ATT_V_ROWS = ATT_HEAD_DIM + 16
ATT_STAGES = 3
ATT_UNROLL = 4
ATT_REDUCE_GROUPS = 8
POOL_HALO = 8


def _cparams(sem):
    return pltpu.CompilerParams(dimension_semantics=sem, vmem_limit_bytes=VMEM_LIMIT_BYTES)


def _dot(a, b):
    return jnp.dot(a.astype(BF16), b.astype(BF16), preferred_element_type=F32)


def _dot_nt(a, b):
    return lax.dot_general(a.astype(BF16), b.astype(BF16), (((1,), (1,)), ((), ())),
                           preferred_element_type=F32)


def _dot_tn(a, b):
    return lax.dot_general(a.astype(BF16), b.astype(BF16), (((0,), (0,)), ((), ())),
                           preferred_element_type=F32)


def _split_bf16(a):
    hi = a.astype(BF16)
    lo = (a - hi.astype(F32)).astype(BF16)
    return hi, lo


def _dot3(a, b):
    ah, al = _split_bf16(a)
    bh, bl = _split_bf16(b)
    d = functools.partial(jnp.dot, preferred_element_type=F32)
    return d(ah, bh) + (d(al, bh) + d(ah, bl))


def _norm_mod(x, gain, shift, scale):
    y = x * lax.rsqrt(jnp.mean(x * x, axis=-1, keepdims=True) + NORM_EPS)
    return (y * gain) * (1.0 + scale) + shift


def _sigmoid(x):
    return 1.0 / (1.0 + jnp.exp(-x))


def _silu(x):
    return x * _sigmoid(x)


class Dims:
    def __init__(self, b, s, ctx, d):
        self.b, self.s, self.ctx, self.d = b, s, ctx, d
        self.n_lat = b * s
        self.n_all = b * s + b * ctx

    def tile(self, preferred):
        t = preferred
        while self.s % t or (self.b * self.ctx) % t:
            t //= 2
        return t

    def mod_row(self, tm):
        lat_tiles, per_batch, b = self.n_lat // tm, self.s // tm, self.b
        return lambda i: jnp.where(i < lat_tiles, i // per_batch, b)


def _ada_kernel(c_ref, w_ref, b_ref, o_ref):
    o_ref[0] = _dot3(_silu(c_ref[...]), w_ref[0]) + b_ref[0]


def ada_table(cc, w_ada, b_ada):
    depth, d, n6 = w_ada.shape
    tn = n6 // 4
    rows = cc.shape[0]
    return pl.pallas_call(
        _ada_kernel,
        grid=(depth, n6 // tn),
        in_specs=[pl.BlockSpec((rows, d), lambda l, j: (0, 0)),
                  pl.BlockSpec((1, d, tn), lambda l, j: (l, 0, j)),
                  pl.BlockSpec((1, 1, tn), lambda l, j: (l, 0, j))],
        out_specs=pl.BlockSpec((1, rows, tn), lambda l, j: (l, 0, j)),
        out_shape=jax.ShapeDtypeStruct((depth, rows, n6), F32),
        compiler_params=_cparams(("arbitrary", "arbitrary")),
        name="ada_table",
    )(cc, w_ada, b_ada.reshape(depth, 1, n6))


def _row_sources(dm, tm, parts):
    width = parts[0].shape[1]
    if len(parts) == 1:
        return [pl.BlockSpec((tm, width), lambda i, *_: (i, 0))]
    lat_tiles = dm.n_lat // tm
    return [pl.BlockSpec((tm, width), lambda i, *_: (jnp.minimum(i, lat_tiles - 1), 0)),
            pl.BlockSpec((tm, width), lambda i, *_: (jnp.maximum(i - lat_tiles, 0), 0))]


def _row_tile(refs, lat_tiles):
    if len(refs) == 1:
        return refs[0][...]
    return jnp.where(pl.program_id(0) < lat_tiles, refs[0][...], refs[1][...])


def _proj_in_kernel(*refs, lat_tiles):
    mod_ref, gain_ref, wm_ref, wg_ref, o_ref, g_ref = refs[-6:]
    x_refs = refs[:-6]
    m = mod_ref[0]
    h = _norm_mod(_row_tile(x_refs, lat_tiles), gain_ref[...], m[0:1], m[1:2]).astype(BF16)
    o_ref[...] = jnp.dot(h, wm_ref[...], preferred_element_type=F32)
    logits = jnp.dot(h, wg_ref[...], preferred_element_type=F32)
    g_ref[...] = (0.5 * jnp.tanh(0.5 * logits) + 0.5).astype(g_ref.dtype)


def proj_in(dm, x_parts, mod, gain, w):
    n, d = dm.n_all, dm.d
    nout = w.shape[1]
    gate_w = 3 * d
    tm, tn = dm.tile(TM_PROJ), TN_PROJ
    assert nout - gate_w == gate_w and gate_w % tn == 0
    gate_blk0 = gate_w // tn
    row = dm.mod_row(tm)
    return pl.pallas_call(
        functools.partial(_proj_in_kernel, lat_tiles=dm.n_lat // tm),
        grid=(n // tm, gate_w // tn),
        in_specs=_row_sources(dm, tm, x_parts) + [
            pl.BlockSpec((1, 6, d), lambda i, j: (row(i), 0, 0)),
            pl.BlockSpec((1, d), lambda i, j: (0, 0)),
            pl.BlockSpec((d, tn), lambda i, j: (0, j)),
            pl.BlockSpec((d, tn), lambda i, j: (0, gate_blk0 + j))],
        out_specs=[pl.BlockSpec((tm, tn), lambda i, j: (i, j)),
                   pl.BlockSpec((tm, tn), lambda i, j: (i, j))],
        out_shape=[jax.ShapeDtypeStruct((n, nout - gate_w), F32),
                   jax.ShapeDtypeStruct((n, gate_w), BF16)],
        compiler_params=_cparams(("parallel", "arbitrary")),
        name="proj_in",
    )(*x_parts, mod, gain.reshape(1, d), w, w)


def _rope_pairs(x, cos, sin):
    lane = lax.broadcasted_iota(jnp.int32, x.shape, 1)
    partner = jnp.where(lane % 2 == 0, pltpu.roll(x, LANES - 1, 1), pltpu.roll(x, 1, 1))
    return x * cos + partner * sin


def _ret_rope(x, cos, sin):
    return jnp.concatenate([_rope_pairs(x[:, j * LANES:(j + 1) * LANES], cos, sin)
                            for j in range(x.shape[1] // LANES)], axis=1)


def _ret_kernel(dec_ref, q_ref, g_ref, kl_ref, kc_ref, vl_ref, vc_ref, cos_ref, sin_ref, gn_ref, o_ref,
                kr_ref, vs_ref, sf_ref, sb_ref, sball_ref, decay_ref, *, ncc, nlc):
    s = pl.program_id(1)
    L = RET_L
    nc = ncc + nlc
    pos = lax.broadcasted_iota(jnp.int32, (L, 1), 0).astype(F32)

    def log_g(direction, h):
        return -jnp.exp(jnp.full((1, 1), dec_ref[direction, h], F32))

    def chunk(c):
        if isinstance(c, int):
            return pl.ds(c * L, L)
        return pl.ds(pl.multiple_of(c * L, L), L)

    def table(c):
        if isinstance(c, int):
            return chunk(nlc + c if c < ncc else c - ncc)
        return chunk(jnp.where(c < ncc, nlc + c, c - ncc))

    def head(x, h, width):
        return x[:, h * width:(h + 1) * width]

    @pl.when(s == 0)
    def _():
        def stage(c, k, v):
            kr_ref[chunk(c), :] = _ret_rope(k, cos_ref[table(c), :], sin_ref[table(c), :]) * (RET_QK_DIM ** -0.5)
            vs_ref[chunk(c), :] = v.astype(vs_ref.dtype)

        for c in range(ncc):
            stage(c, kc_ref[c * L:(c + 1) * L, :], vc_ref[c * L:(c + 1) * L, :])

        def stage_latent(j, carry):
            stage(ncc + j, kl_ref[chunk(j), :], vl_ref[chunk(j), :])
            return carry

        lax.fori_loop(0, nlc, stage_latent, 0)

        diff = (lax.broadcasted_iota(jnp.int32, (L, L), 0)
                - lax.broadcasted_iota(jnp.int32, (L, L), 1)).astype(F32)
        for h in range(RET_HEADS):
            decay_ref[h] = (jnp.where(diff >= 0, jnp.exp(jnp.maximum(diff, 0.0) * log_g(0, h)), 0.0)
                            + jnp.where(diff <= 0, jnp.exp(jnp.maximum(-diff, 0.0) * log_g(1, h)), 0.0))

        sf_ref[...] = jnp.zeros_like(sf_ref)
        sb_ref[...] = jnp.zeros_like(sb_ref)

        def visit(c, carry):
            k = kr_ref[chunk(c), :]
            v = vs_ref[chunk(c), :]
            for h in range(RET_HEADS):
                lgb = log_g(1, h)
                st = sb_ref[h]
                sball_ref[c, h] = st
                kz = head(k, h, RET_QK_DIM) * jnp.exp(pos * lgb)
                sb_ref[h] = jnp.exp(L * lgb) * st + _dot_tn(kz, head(v, h, RET_V_DIM))
            return carry

        for c in reversed(range(ncc)):
            visit(c, 0)
        lax.fori_loop(0, nlc, lambda j, carry: visit(nc - 1 - j, carry), 0)

    q = _ret_rope(q_ref[...], cos_ref[table(s), :], sin_ref[table(s), :])
    k = kr_ref[chunk(s), :]
    v = vs_ref[chunk(s), :]
    g = g_ref[...]
    for h in range(RET_HEADS):
        lgf = log_g(0, h)
        lgb = log_g(1, h)
        qh = head(q, h, RET_QK_DIM)
        kh = head(k, h, RET_QK_DIM)
        vh = head(v, h, RET_V_DIM)
        st_f = sf_ref[h]
        st_b = sball_ref[s, h]
        o = _dot(_dot_nt(qh, kh) * decay_ref[h], vh)
        o = o + _dot(qh, st_f) * jnp.exp((pos + 1.0) * lgf)
        o = o + _dot(qh, st_b) * jnp.exp((L - pos) * lgb)
        sf_ref[h] = jnp.exp(L * lgf) * st_f + _dot_tn(kh * jnp.exp((L - 1.0 - pos) * lgf), vh)
        mu = jnp.mean(o, axis=-1, keepdims=True)
        var = jnp.mean(jnp.square(o - mu), axis=-1, keepdims=True)
        o = (o - mu) * lax.rsqrt(var + GN_EPS) * head(gn_ref[...], h, RET_V_DIM)
        o_ref[:, h * RET_V_DIM:(h + 1) * RET_V_DIM] = (_silu(head(g, h, RET_V_DIM)) * o).astype(o_ref.dtype)


def retention(dm, p, dec, cos_t, sin_t, gn):
    L = RET_L
    ncc, nlc = dm.ctx // L, dm.s // L
    nc = ncc + nlc
    lat_blocks = dm.n_lat // L
    ctx_blk0 = dm.n_lat // dm.ctx
    qk_w = RET_HEADS * RET_QK_DIM
    v_w = RET_HEADS * RET_V_DIM

    def rows(b, s):
        return jnp.where(s < ncc, lat_blocks + b * ncc + s, b * nlc + s - ncc)

    whole = lambda shape: pl.BlockSpec(shape, lambda b, s, d: (0, 0))
    grid_spec = pltpu.PrefetchScalarGridSpec(
        num_scalar_prefetch=1,
        grid=(dm.b, nc),
        in_specs=[pl.BlockSpec((L, qk_w), lambda b, s, d: (rows(b, s), 0)),
                  pl.BlockSpec((L, v_w), lambda b, s, d: (rows(b, s), 2)),
                  pl.BlockSpec((dm.s, qk_w), lambda b, s, d: (b, 1)),
                  pl.BlockSpec((dm.ctx, qk_w), lambda b, s, d: (ctx_blk0 + b, 1)),
                  pl.BlockSpec((dm.s, v_w), lambda b, s, d: (b, 1)),
                  pl.BlockSpec((dm.ctx, v_w), lambda b, s, d: (ctx_blk0 + b, 1)),
                  whole(cos_t.shape), whole(sin_t.shape), whole((1, v_w))],
        out_specs=pl.BlockSpec((L, v_w), lambda b, s, d: (rows(b, s), 0)),
        scratch_shapes=[pltpu.VMEM((nc * L, qk_w), F32),
                        pltpu.VMEM((nc * L, v_w), BF16),
                        pltpu.VMEM((RET_HEADS, RET_QK_DIM, RET_V_DIM), F32),
                        pltpu.VMEM((RET_HEADS, RET_QK_DIM, RET_V_DIM), F32),
                        pltpu.VMEM((nc, RET_HEADS, RET_QK_DIM, RET_V_DIM), F32),
                        pltpu.VMEM((RET_HEADS, L, L), F32)],
    )
    return pl.pallas_call(
        functools.partial(_ret_kernel, ncc=ncc, nlc=nlc),
        grid_spec=grid_spec,
        out_shape=jax.ShapeDtypeStruct((dm.n_all, v_w), BF16),
        compiler_params=_cparams(("parallel", "arbitrary")),
        name="retention",
    )(dec, p, p, p, p, p, p, cos_t, sin_t, gn.reshape(1, v_w))


def _att_prep_kernel(q_ref, k_ref, v_ref, cos_ref, sin_ref, gq_ref, gk_ref, qo_ref, ko_ref, vo_ref,
                     *, lat_tiles):
    is_lat = pl.program_id(0) < lat_tiles
    cos = jnp.where(is_lat, cos_ref[...], 1.0)
    sin = jnp.where(is_lat, sin_ref[...], 0.0)

    hd = ATT_HEAD_DIM

    def prep(x_ref, gain_ref, h):
        xh = x_ref[:, h * hd:(h + 1) * hd]
        xn = xh * lax.rsqrt(jnp.mean(xh * xh, axis=-1, keepdims=True) + NORM_EPS) * gain_ref[...]
        return _rope_pairs(xn, cos, sin)

    for h in range(ATT_HEADS):
        qo_ref[:, h * hd:(h + 1) * hd] = (prep(q_ref, gq_ref, h) * ATT_SCORE_SCALE).astype(qo_ref.dtype)
    ones = jnp.ones((ATT_V_ROWS - hd, vo_ref.shape[1]), vo_ref.dtype)
    for h in range(ATT_KV_HEADS):
        ko_ref[:, h * hd:(h + 1) * hd] = prep(k_ref, gk_ref, h).astype(ko_ref.dtype)
        vo_ref[h * ATT_V_ROWS:h * ATT_V_ROWS + hd, :] = v_ref[:, h * hd:(h + 1) * hd].T.astype(vo_ref.dtype)
        vo_ref[h * ATT_V_ROWS + hd:(h + 1) * ATT_V_ROWS, :] = ones


def att_prep(dm, p, cos_t, sin_t, gq, gk):
    tm = dm.tile(TM_PREP)
    n = dm.n_all
    lat_tiles, per_batch = dm.n_lat // tm, dm.s // tm
    qw = ATT_HEADS * ATT_HEAD_DIM
    kw = ATT_KV_HEADS * ATT_HEAD_DIM
    q_blk, k_blk, v_blk = ATT_Q_COL // qw, ATT_K_COL // kw, ATT_V_COL // kw
    tab = lambda i: (jnp.where(i < lat_tiles, i % per_batch, 0), 0)
    return pl.pallas_call(
        functools.partial(_att_prep_kernel, lat_tiles=lat_tiles),
        grid=(n // tm,),
        in_specs=[pl.BlockSpec((tm, qw), lambda i: (i, q_blk)),
                  pl.BlockSpec((tm, kw), lambda i: (i, k_blk)),
                  pl.BlockSpec((tm, kw), lambda i: (i, v_blk)),
                  pl.BlockSpec((tm, ATT_HEAD_DIM), tab),
                  pl.BlockSpec((tm, ATT_HEAD_DIM), tab),
                  pl.BlockSpec((1, ATT_HEAD_DIM), lambda i: (0, 0)),
                  pl.BlockSpec((1, ATT_HEAD_DIM), lambda i: (0, 0))],
        out_specs=[pl.BlockSpec((tm, qw), lambda i: (i, 0)),
                   pl.BlockSpec((tm, kw), lambda i: (i, 0)),
                   pl.BlockSpec((ATT_KV_HEADS * ATT_V_ROWS, tm), lambda i: (0, i))],
        out_shape=[jax.ShapeDtypeStruct((n, qw), BF16),
                   jax.ShapeDtypeStruct((n, kw), BF16),
                   jax.ShapeDtypeStruct((ATT_KV_HEADS * ATT_V_ROWS, n), BF16)],
        compiler_params=_cparams(("parallel",)),
        name="att_prep",
    )(p, p, p, cos_t, sin_t, gq.reshape(1, -1), gk.reshape(1, -1))


def _attn_kernel(*refs, with_latent_keys):
    ns = ATT_STAGES
    if with_latent_keys:
        qn_ref, kl_ref, vtl_ref, kc_ref, vtc_ref, o_ref, acc_ref = refs[:7]
    else:
        qn_ref, kc_ref, vtc_ref, o_ref, acc_ref = refs[:5]
    s_refs, p_refs = refs[-2 * ns:-ns], refs[-ns:]
    hd = ATT_HEAD_DIM
    grp = ATT_HEADS // ATT_KV_HEADS
    tq = qn_ref.shape[0]
    nq = grp * tq
    tk = kc_ref.shape[0]
    n = 1 + (kl_ref.shape[0] // tk if with_latent_keys else 0)
    q2 = jnp.concatenate([qn_ref[:, j * hd:(j + 1) * hd] for j in range(grp)], axis=0)

    def key_chunk(j):
        if isinstance(j, int) and j == 0:
            return kc_ref[...]
        return kl_ref[pl.ds(pl.multiple_of((j - 1) * tk, tk), tk), :]

    def value_chunk(j):
        if isinstance(j, int) and j == 0:
            return vtc_ref[...]
        return vtl_ref[:, pl.ds(pl.multiple_of((j - 1) * tk, tk), tk)]

    def max_keys(x):
        groups = math.gcd(tk // SUBLANES, ATT_REDUCE_GROUPS)
        part = jnp.max(x.reshape(groups, tk // groups, nq), axis=0)
        return jnp.max(part, axis=0, keepdims=True)

    def time_step(t, slot, carry, do_scores=True, do_softmax=True, do_values=True):
        m, alpha = carry
        if do_scores:
            s_refs[slot][...] = _dot_nt(key_chunk(t), q2)
        if do_values:
            pv = jnp.dot(value_chunk(t - 2), p_refs[(slot - 2) % ns][...], preferred_element_type=F32)
            acc_ref[...] = acc_ref[...] * alpha + pv
        if do_softmax:
            st = s_refs[(slot - 1) % ns][...]
            m_new = jnp.maximum(m, max_keys(st))
            p_refs[(slot - 1) % ns][...] = jnp.exp2(st - m_new).astype(BF16)
            alpha = jnp.exp2(m - m_new)
            m = m_new
        return m, alpha

    acc_ref[...] = jnp.zeros_like(acc_ref)
    carry = (jnp.full((1, nq), -1e30, F32), jnp.zeros((1, nq), F32))
    fill = min(3, n + 2)
    for t in range(fill):
        carry = time_step(t, t % ns, carry, t < n, 0 <= t - 1 < n, 0 <= t - 2 < n)
    steady = max(n - fill, 0)
    body_steps = ns * ATT_UNROLL

    def unrolled(i, cr):
        for k in range(body_steps):
            cr = time_step(fill + body_steps * i + k, (fill + k) % ns, cr)
        return cr

    if steady // body_steps:
        carry = lax.fori_loop(0, steady // body_steps, unrolled, carry)
    for t in range(fill + body_steps * (steady // body_steps), max(n, fill)):
        carry = time_step(t, t % ns, carry)
    for t in range(max(n, fill), n + 2):
        carry = time_step(t, t % ns, carry, False, t - 1 < n, True)
    out = (acc_ref[0:hd, :] / acc_ref[hd:hd + 1, :]).T
    for j in range(grp):
        o_ref[:, j * hd:(j + 1) * hd] = out[j * tq:(j + 1) * tq].astype(o_ref.dtype)


def attention(dm, qn, kn, vt, latent):
    hd = ATT_HEAD_DIM
    grp = ATT_HEADS // ATT_KV_HEADS
    tk = dm.ctx
    assert dm.s % tk == 0
    ctx_blk0 = dm.n_lat // dm.ctx
    kc_spec = pl.BlockSpec((tk, hd), lambda b, h, i: (ctx_blk0 + b, h))
    vtc_spec = pl.BlockSpec((ATT_V_ROWS, tk), lambda b, h, i: (h, ctx_blk0 + b))
    if latent:
        tq = math.gcd(TQ_ATT, dm.s)
        nq = dm.s // tq
        q_spec = pl.BlockSpec((tq, grp * hd), lambda b, h, i: (b * nq + i, h))
        kl_spec = pl.BlockSpec((dm.s, hd), lambda b, h, i: (b, h))
        vtl_spec = pl.BlockSpec((ATT_V_ROWS, dm.s), lambda b, h, i: (h, b))
        in_specs = [q_spec, kl_spec, vtl_spec, kc_spec, vtc_spec]
        args = (qn, kn, vt, kn, vt)
        rows = dm.n_lat
    else:
        tq, nq = dm.ctx, 1
        q_spec = pl.BlockSpec((tq, grp * hd), lambda b, h, i: (ctx_blk0 + b, h))
        in_specs = [q_spec, kc_spec, vtc_spec]
        args = (qn, kn, vt)
        rows = dm.b * dm.ctx
    return pl.pallas_call(
        functools.partial(_attn_kernel, with_latent_keys=latent),
        grid=(dm.b, ATT_KV_HEADS, nq),
        in_specs=in_specs,
        out_specs=pl.BlockSpec((tq, grp * hd), lambda b, h, i: (b * nq + i, h)),
        out_shape=jax.ShapeDtypeStruct((rows, ATT_HEADS * hd), BF16),
        scratch_shapes=([pltpu.VMEM((ATT_V_ROWS, grp * tq), F32)]
                        + [pltpu.VMEM((tk, grp * tq), F32)] * ATT_STAGES
                        + [pltpu.VMEM((tk, grp * tq), BF16)] * ATT_STAGES),
        compiler_params=_cparams(("parallel", "parallel", "arbitrary")),
        name="attention_latent" if latent else "attention_ctx",
    )(*args)


def _pool_kernel(prev_ref, u_ref, next_ref, w_ref, sc_ref, o_ref, a_ref, b_ref, *, dm):
    tm = u_ref.shape[0]
    H = POOL_HALO
    i = pl.program_id(0)
    lat_tiles = dm.n_lat // tm
    is_lat = i < lat_tiles
    n_seg = jnp.where(is_lat, dm.s, dm.ctx)
    base = jnp.where(is_lat, (i * tm) % dm.s, 0)
    r = lax.broadcasted_iota(jnp.int32, (tm + 2 * H, 1), 0) - H
    pos = jnp.where(is_lat, base + r, r % dm.ctx)
    win = pl.ds(H, tm + 2 * H)

    def shifted(ref, d):
        return ref[pl.ds(H + d, tm + 2 * H), :]

    zeros = jnp.zeros((H, POOL_GROUP_DIM), F32)
    for ref in (a_ref, b_ref):
        ref[pl.ds(0, H), :] = zeros
        ref[pl.ds(tm + 3 * H, H), :] = zeros

    for gi, w in enumerate(POOL_WINDOWS):
        cols = slice(gi * POOL_GROUP_DIM, (gi + 1) * POOL_GROUP_DIM)
        half = w // 2
        a_ref[pl.ds(H, H), :] = prev_ref[:, cols]
        a_ref[pl.ds(2 * H, tm), :] = u_ref[:, cols]
        a_ref[pl.ds(2 * H + tm, H), :] = next_ref[:, cols]
        b_ref[win, :] = jnp.where(pos >= 1, shifted(a_ref, -1), 0.0)
        step = 1
        while step < half:
            b_ref[win, :] = b_ref[win, :] + jnp.where(pos >= step, shifted(b_ref, -step), 0.0)
            step *= 2
        step = 1
        while step < half:
            a_ref[win, :] = a_ref[win, :] + jnp.where(pos + step < n_seg, shifted(a_ref, step), 0.0)
            step *= 2
        tile = pl.ds(2 * H, tm)
        pt = pos[H:H + tm]
        cnt = (jnp.minimum(pt + half, n_seg) - jnp.maximum(pt - half, 0)).astype(F32)
        u = u_ref[:, cols]
        diff = (a_ref[tile, :] + b_ref[tile, :]) / cnt - u
        o_ref[:, cols] = (_dot(diff, w_ref[gi]) * sc_ref[:, cols]).astype(o_ref.dtype)


def pool_mixer(dm, p, pool_w, pool_scale):
    tm = dm.tile(TM_POOL)
    n = dm.n_all
    H = POOL_HALO
    width = len(POOL_WINDOWS) * POOL_GROUP_DIM
    col = 2560 // width
    per = tm // H
    last = n // H - 1
    return pl.pallas_call(
        functools.partial(_pool_kernel, dm=dm),
        grid=(n // tm,),
        in_specs=[pl.BlockSpec((H, width), lambda i: (jnp.maximum(i * per - 1, 0), col)),
                  pl.BlockSpec((tm, width), lambda i: (i, col)),
                  pl.BlockSpec((H, width), lambda i: (jnp.minimum((i + 1) * per, last), col)),
                  pl.BlockSpec(pool_w.shape, lambda i: (0, 0, 0)),
                  pl.BlockSpec((1, width), lambda i: (0, 0))],
        out_specs=pl.BlockSpec((tm, width), lambda i: (i, 0)),
        out_shape=jax.ShapeDtypeStruct((n, width), BF16),
        scratch_shapes=[pltpu.VMEM((tm + 4 * H, POOL_GROUP_DIM), F32),
                        pltpu.VMEM((tm + 4 * H, POOL_GROUP_DIM), F32)],
        compiler_params=_cparams(("parallel",)),
        name="pool_mixer",
    )(p, p, p, pool_w, pool_scale.reshape(1, width))


def _merge_kernel(*refs, lat_tiles, n_att, n_x):
    yr_ref, yp_ref, gl_ref = refs[:3]
    ya_refs, x_refs = refs[3:3 + n_att], refs[3 + n_att:3 + n_att + n_x]
    mod_ref, wb_ref, wo_ref, o_ref = refs[-4:]
    d = o_ref.shape[1]
    branches = (yr_ref[...], _row_tile(ya_refs, lat_tiles), yp_ref[...])
    mixed = None
    for j, y in enumerate(branches):
        gate = gl_ref[:, j * d:(j + 1) * d].astype(F32)
        term = gate * jnp.dot(y, wb_ref[j], preferred_element_type=F32)
        mixed = term if mixed is None else mixed + term
    o_ref[...] = _row_tile(x_refs, lat_tiles) + mod_ref[0][2:3] * _dot(mixed, wo_ref[...])


def merge(dm, n_rows, yr, ya_parts, yp, gates, x_parts, mod, wb, wo):
    tm = dm.tile(TM_MERGE)
    d = dm.d
    bw = yr.shape[1]
    row = dm.mod_row(tm)
    return pl.pallas_call(
        functools.partial(_merge_kernel, lat_tiles=dm.n_lat // tm, n_att=len(ya_parts), n_x=len(x_parts)),
        grid=(n_rows // tm,),
        in_specs=[pl.BlockSpec((tm, bw), lambda i: (i, 0)),
                  pl.BlockSpec((tm, bw), lambda i: (i, 0)),
                  pl.BlockSpec((tm, 3 * d), lambda i: (i, 0))]
        + _row_sources(dm, tm, ya_parts) + _row_sources(dm, tm, x_parts) + [
            pl.BlockSpec((1, 6, d), lambda i: (row(i), 0, 0)),
            pl.BlockSpec(wb.shape, lambda i: (0, 0, 0)),
            pl.BlockSpec(wo.shape, lambda i: (0, 0))],
        out_specs=pl.BlockSpec((tm, d), lambda i: (i, 0)),
        out_shape=jax.ShapeDtypeStruct((n_rows, d), F32),
        compiler_params=_cparams(("parallel",)),
        name="merge",
    )(yr, yp, gates, *ya_parts, *x_parts, mod, wb, wo)


def _ffn_kernel(x_ref, mod_ref, gain_ref, w1_ref, w3_ref, w2_ref, o_ref):
    m = mod_ref[0]
    h = _norm_mod(x_ref[...], gain_ref[...], m[3:4], m[4:5]).astype(BF16)
    fdim = w1_ref.shape[1]
    y = None
    for lo in range(0, fdim, MOE_COL_CHUNK):
        cols = slice(lo, min(lo + MOE_COL_CHUNK, fdim))
        a = jnp.dot(h, w1_ref[:, cols], preferred_element_type=F32)
        b = jnp.dot(h, w3_ref[:, cols], preferred_element_type=F32)
        part = _dot(_silu(a) * b, w2_ref[cols, :])
        y = part if y is None else y + part
    o_ref[...] = x_ref[...] + m[5:6] * y


def ffn_dense(dm, x, mod, gain, w1, w3, w2):
    n, d = x.shape
    fdim = w1.shape[1]
    tm = dm.tile(TM_FFN)
    row = dm.mod_row(tm)
    once = dict(pipeline_mode=pl.Buffered(1))
    return pl.pallas_call(
        _ffn_kernel,
        grid=(n // tm,),
        in_specs=[pl.BlockSpec((tm, d), lambda i: (i, 0)),
                  pl.BlockSpec((1, 6, d), lambda i: (row(i), 0, 0)),
                  pl.BlockSpec((1, d), lambda i: (0, 0)),
                  pl.BlockSpec((d, fdim), lambda i: (0, 0), **once),
                  pl.BlockSpec((d, fdim), lambda i: (0, 0), **once),
                  pl.BlockSpec((fdim, d), lambda i: (0, 0), **once)],
        out_specs=pl.BlockSpec((tm, d), lambda i: (i, 0)),
        out_shape=jax.ShapeDtypeStruct((n, d), F32),
        compiler_params=_cparams(("parallel",)),
        name="ffn_dense",
    )(x, mod, gain.reshape(1, d), w1, w3, w2)


def _router_kernel(x_ref, mod_ref, gain_ref, wr_ref, br_ref, h_ref, slot_ref, slot_t_ref, wt_ref, cum_ref):
    tm = x_ref.shape[0]
    m = mod_ref[0]
    h = _norm_mod(x_ref[...], gain_ref[...], m[3:4], m[4:5])
    h_ref[...] = h.astype(BF16)
    logits = _dot3(h, wr_ref[...]) + br_ref[...]
    lane = lax.broadcasted_iota(jnp.int32, logits.shape, 1)
    m1 = jnp.max(logits, axis=-1, keepdims=True)
    i1 = jnp.min(jnp.where(logits == m1, lane, LANES), axis=-1, keepdims=True)
    rest = jnp.where(lane == i1, -jnp.inf, logits)
    m2 = jnp.max(rest, axis=-1, keepdims=True)
    i2 = jnp.min(jnp.where(rest == m2, lane, LANES), axis=-1, keepdims=True)
    e = jnp.exp(m2 - m1)
    w1 = 1.0 / (1.0 + e)
    w2 = e / (1.0 + e)
    wt_ref[...] = jnp.where(lane == i1, w1, jnp.where(lane == i2, w2, 0.0))
    routed = jnp.where((lane == i1) | (lane == i2), 1.0, 0.0)
    sub = SUB_MOE
    lower = (lax.broadcasted_iota(jnp.int32, (sub, sub), 0) > lax.broadcasted_iota(jnp.int32, (sub, sub), 1))
    lower = lower.astype(BF16)
    offset = jnp.zeros((1, LANES), F32)
    cum_ref[...] = jnp.zeros_like(cum_ref)
    for blk in range(tm // sub):
        rows = slice(blk * sub, (blk + 1) * sub)
        mask = routed[rows]
        cum_ref[0, blk:blk + 1, :] = offset.astype(jnp.int32)
        rank = jnp.dot(lower, mask.astype(BF16), preferred_element_type=F32) + offset
        slot_ref[rows, :] = jnp.where(mask > 0.0, rank, -1.0)
        count = jnp.sum(mask, axis=0, keepdims=True)
        offset = offset + jnp.ceil(count * (1.0 / SLOT_ALIGN)) * SLOT_ALIGN
    cum_ref[0, tm // sub:tm // sub + 1, :] = offset.astype(jnp.int32)
    slot_t_ref[...] = slot_ref[...].T[:SUBLANES, :]


def moe_route(dm, n_rows, x, mod, gain, wr_pad, br_pad):
    tm = dm.tile(TM_MOE)
    d = dm.d
    row = dm.mod_row(tm)
    nt = n_rows // tm
    cum_rows = -(-(tm // SUB_MOE + 1) // SUBLANES) * SUBLANES
    return pl.pallas_call(
        _router_kernel,
        grid=(nt,),
        in_specs=[pl.BlockSpec((tm, d), lambda i: (i, 0)),
                  pl.BlockSpec((1, 6, d), lambda i: (row(i), 0, 0)),
                  pl.BlockSpec((1, d), lambda i: (0, 0)),
                  pl.BlockSpec((d, LANES), lambda i: (0, 0)),
                  pl.BlockSpec((1, LANES), lambda i: (0, 0))],
        out_specs=[pl.BlockSpec((tm, d), lambda i: (i, 0)),
                   pl.BlockSpec((tm, LANES), lambda i: (i, 0)),
                   pl.BlockSpec((SUBLANES, tm), lambda i: (0, i)),
                   pl.BlockSpec((tm, LANES), lambda i: (i, 0)),
                   pl.BlockSpec((1, cum_rows, LANES), lambda i: (i, 0, 0))],
        out_shape=[jax.ShapeDtypeStruct((n_rows, d), BF16),
                   jax.ShapeDtypeStruct((n_rows, LANES), F32),
                   jax.ShapeDtypeStruct((SUBLANES, n_rows), F32),
                   jax.ShapeDtypeStruct((n_rows, LANES), F32),
                   jax.ShapeDtypeStruct((nt, cum_rows, LANES), jnp.int32)],
        compiler_params=_cparams(("parallel",)),
        name="moe_router",
    )(x, mod, gain.reshape(1, d), wr_pad, br_pad)


def _moe_windows(base_ref, row0_ref, e, step=None):
    i, j = pl.program_id(0), pl.program_id(1)
    if step is not None:
        i, j = step // pl.num_programs(1), step % pl.num_programs(1)
    start = base_ref[i, j * N_EXPERTS + e]
    nwin = (base_ref[i, (j + 1) * N_EXPERTS + e] - start + WIN_MOE - 1) // WIN_MOE
    return start, nwin, row0_ref[i, e]


def _moe_gather_kernel(base_ref, row0_ref, h_ref, slot_t_ref, xs_in_ref, xs_ref, buf_ref, sem):
    del xs_in_ref
    win, sub = WIN_MOE, SUB_MOE
    spare = N_EXPERTS

    def gathered(e, start, w):
        want = (lax.broadcasted_iota(jnp.int32, (win, sub), 0) + (start + w * win)).astype(F32)
        onehot = (slot_t_ref[e:e + 1, :] == want).astype(BF16)
        return jnp.dot(onehot, h_ref[...], preferred_element_type=F32).astype(BF16)

    def copy(e, buf, w):
        start, _, row0 = _moe_windows(base_ref, row0_ref, e)
        rows = pl.ds(pl.multiple_of(row0 + start + w * win, SLOT_ALIGN), win)
        return pltpu.make_async_copy(buf_ref.at[buf], xs_ref.at[rows, :], sem.at[buf])

    for e in range(N_EXPERTS):
        start, _, _ = _moe_windows(base_ref, row0_ref, e)
        buf_ref[e] = gathered(e, start, 0)
        copy(e, e, 0).start()

    for e in range(N_EXPERTS):
        start, nwin, _ = _moe_windows(base_ref, row0_ref, e)

        def extra(w, carry, e=e, start=start):
            buf_ref[spare] = gathered(e, start, w)
            cp = copy(e, spare, w)
            cp.start()
            cp.wait()
            return carry

        lax.fori_loop(1, jnp.maximum(nwin, 1), extra, 0)

    for e in range(N_EXPERTS):
        copy(e, e, 0).wait()


def moe_gather(dm, n_rows, h, slot_t, base, row0, cap):
    tm, sub, d = dm.tile(TM_MOE), SUB_MOE, dm.d
    nsub = tm // sub
    grid_spec = pltpu.PrefetchScalarGridSpec(
        num_scalar_prefetch=2,
        grid=(n_rows // tm, nsub),
        in_specs=[pl.BlockSpec((sub, d), lambda i, j, b, r: (i * nsub + j, 0)),
                  pl.BlockSpec((SUBLANES, sub), lambda i, j, b, r: (0, i * nsub + j)),
                  pl.BlockSpec(memory_space=pl.ANY)],
        out_specs=pl.BlockSpec(memory_space=pl.ANY),
        scratch_shapes=[pltpu.VMEM((N_EXPERTS + 1, WIN_MOE, d), BF16),
                        pltpu.SemaphoreType.DMA((N_EXPERTS + 1,))],
    )
    return pl.pallas_call(
        _moe_gather_kernel,
        grid_spec=grid_spec,
        out_shape=jax.ShapeDtypeStruct((cap, d), BF16),
        input_output_aliases={4: 0},
        compiler_params=_cparams(("arbitrary", "arbitrary")),
        name="moe_gather",
    )(base, row0, h, slot_t, jnp.zeros((cap, d), BF16))


def _moe_ffn_kernel(bexp_ref, bvalid_ref, xs_ref, w1_ref, w3_ref, w2_ref, o_ref):
    del bexp_ref
    valid = bvalid_ref[pl.program_id(0)] == 1

    @pl.when(valid)
    def _():
        xs = xs_ref[...]
        fdim = w1_ref.shape[2]
        y = None
        for lo in range(0, fdim, MOE_COL_CHUNK):
            cols = slice(lo, min(lo + MOE_COL_CHUNK, fdim))
            a = jnp.dot(xs, w1_ref[0, :, cols], preferred_element_type=F32)
            b = jnp.dot(xs, w3_ref[0, :, cols], preferred_element_type=F32)
            part = _dot(_silu(a) * b, w2_ref[0, cols, :])
            y = part if y is None else y + part
        o_ref[...] = y.astype(o_ref.dtype)

    @pl.when(jnp.logical_not(valid))
    def _():
        o_ref[...] = jnp.zeros_like(o_ref)


def moe_ffn(xs, blk_expert, blk_valid, w1, w3, w2):
    cap, d = xs.shape
    _, _, fdim = w1.shape
    blk = BLK_MOE
    once = dict(pipeline_mode=pl.Buffered(1))
    grid_spec = pltpu.PrefetchScalarGridSpec(
        num_scalar_prefetch=2,
        grid=(cap // blk,),
        in_specs=[pl.BlockSpec((blk, d), lambda k, be, bv: (k, 0)),
                  pl.BlockSpec((1, d, fdim), lambda k, be, bv: (be[k], 0, 0), **once),
                  pl.BlockSpec((1, d, fdim), lambda k, be, bv: (be[k], 0, 0), **once),
                  pl.BlockSpec((1, fdim, d), lambda k, be, bv: (be[k], 0, 0), **once)],
        out_specs=pl.BlockSpec((blk, d), lambda k, be, bv: (k, 0)),
    )
    return pl.pallas_call(
        _moe_ffn_kernel,
        grid_spec=grid_spec,
        out_shape=jax.ShapeDtypeStruct((cap, d), BF16),
        compiler_params=_cparams(("arbitrary",)),
        name="moe_ffn",
    )(blk_expert, blk_valid, xs, w1, w3, w2)


def _moe_scatter_kernel(base_ref, row0_ref, slot_ref, wt_ref, x_ref, mod_ref, gain_ref, ys_ref, o_ref,
                        buf_ref, y_ref, sem, *, final):
    win, sub = WIN_MOE, SUB_MOE
    nbuf = N_EXPERTS + 1
    step = pl.program_id(0) * pl.num_programs(1) + pl.program_id(1)
    last_step = pl.num_programs(0) * pl.num_programs(1) - 1
    mine = (step % 2) * nbuf
    spare = mine + N_EXPERTS

    def copy(e, buf, w, at=None):
        start, _, row0 = _moe_windows(base_ref, row0_ref, e, at)
        rows = pl.ds(pl.multiple_of(row0 + start + w * win, SLOT_ALIGN), win)
        return pltpu.make_async_copy(ys_ref.at[rows, :], buf_ref.at[buf], sem.at[buf])

    @pl.when(step == 0)
    def _():
        for e in range(N_EXPERTS):
            copy(e, e, 0).start()

    @pl.when(step < last_step)
    def _():
        for e in range(N_EXPERTS):
            copy(e, nbuf - mine + e, 0, at=step + 1).start()

    def scattered(e, buf, w):
        start, _, _ = _moe_windows(base_ref, row0_ref, e)
        want = (lax.broadcasted_iota(jnp.int32, (sub, win), 1) + (start + w * win)).astype(F32)
        onehot = (slot_ref[:, e:e + 1] == want).astype(BF16)
        return wt_ref[:, e:e + 1] * jnp.dot(onehot, buf_ref[buf], preferred_element_type=F32)

    y_ref[...] = jnp.zeros_like(y_ref)
    for e in range(N_EXPERTS):
        _, nwin, _ = _moe_windows(base_ref, row0_ref, e)

        def extra(w, carry, e=e):
            cp = copy(e, spare, w)
            cp.start()
            cp.wait()
            y_ref[...] += scattered(e, spare, w)
            return carry

        lax.fori_loop(1, jnp.maximum(nwin, 1), extra, 0)

    y = y_ref[...]
    for e in range(N_EXPERTS):
        copy(e, mine + e, 0).wait()
        y = y + scattered(e, mine + e, 0)
    x = x_ref[...] + mod_ref[0][5:6] * y
    if final:
        x = x * lax.rsqrt(jnp.mean(x * x, axis=-1, keepdims=True) + NORM_EPS) * gain_ref[...]
    o_ref[...] = x


def moe_scatter(dm, n_rows, ys, slot, wt, base, row0, x, mod, final_gain):
    tm, sub, d = dm.tile(TM_MOE), SUB_MOE, dm.d
    nsub = tm // sub
    row = dm.mod_row(sub)
    final = final_gain is not None
    gain = final_gain if final else jnp.ones((d,), F32)
    tok = lambda i, j, b, r: (i * nsub + j, 0)
    grid_spec = pltpu.PrefetchScalarGridSpec(
        num_scalar_prefetch=2,
        grid=(n_rows // tm, nsub),
        in_specs=[pl.BlockSpec((sub, LANES), tok),
                  pl.BlockSpec((sub, LANES), tok),
                  pl.BlockSpec((sub, d), tok),
                  pl.BlockSpec((1, 6, d), lambda i, j, b, r: (row(i * nsub + j), 0, 0)),
                  pl.BlockSpec((1, d), lambda i, j, b, r: (0, 0)),
                  pl.BlockSpec(memory_space=pl.ANY)],
        out_specs=pl.BlockSpec((sub, d), tok),
        scratch_shapes=[pltpu.VMEM((2 * (N_EXPERTS + 1), WIN_MOE, d), BF16),
                        pltpu.VMEM((sub, d), F32),
                        pltpu.SemaphoreType.DMA((2 * (N_EXPERTS + 1),))],
    )
    return pl.pallas_call(
        functools.partial(_moe_scatter_kernel, final=final),
        grid_spec=grid_spec,
        out_shape=jax.ShapeDtypeStruct((n_rows, d), F32),
        compiler_params=_cparams(("arbitrary", "arbitrary")),
        name="moe_scatter",
    )(base, row0, slot, wt, x, mod, gain.reshape(1, d), ys)


def moe_layout(dm, n_rows, cum):
    tm = dm.tile(TM_MOE)
    nt, nsub = n_rows // tm, tm // SUB_MOE
    base = cum[:, :nsub + 1, :N_EXPERTS]
    total = base[:, nsub, :]
    per_expert = jnp.sum(total, axis=0)
    padded = (per_expert + WIN_MOE + BLK_MOE - 1) // BLK_MOE * BLK_MOE
    expert_row0 = jnp.cumsum(padded) - padded
    row0 = expert_row0[None, :] + jnp.cumsum(total, axis=0) - total
    slack = nt * nsub * N_EXPERTS * (SLOT_ALIGN - 1) + N_EXPERTS * (WIN_MOE + BLK_MOE)
    nblk = -(-(2 * n_rows + slack) // BLK_MOE)
    ends = jnp.cumsum(padded // BLK_MOE)
    k = jnp.arange(nblk, dtype=jnp.int32)
    blk_expert = jnp.minimum(jnp.sum(k[:, None] >= ends[None, :], axis=1), N_EXPERTS - 1).astype(jnp.int32)
    blk_valid = (k < ends[-1]).astype(jnp.int32)
    return base.reshape(nt, (nsub + 1) * N_EXPERTS), row0.astype(jnp.int32), blk_expert, blk_valid, nblk * BLK_MOE


def _final_norm_kernel(x_ref, gain_ref, o_ref):
    x = x_ref[...]
    o_ref[...] = x * lax.rsqrt(jnp.mean(x * x, axis=-1, keepdims=True) + NORM_EPS) * gain_ref[...]


def final_rmsnorm(dm, n_rows, x, gain):
    tm = dm.tile(TM_FFN)
    d = dm.d
    return pl.pallas_call(
        _final_norm_kernel,
        grid=(n_rows // tm,),
        in_specs=[pl.BlockSpec((tm, d), lambda i: (i, 0)), pl.BlockSpec((1, d), lambda i: (0, 0))],
        out_specs=pl.BlockSpec((tm, d), lambda i: (i, 0)),
        out_shape=jax.ShapeDtypeStruct((n_rows, d), F32),
        compiler_params=_cparams(("parallel",)),
        name="final_norm",
    )(x, gain.reshape(1, d))


def _rope_tables(s, ctx):
    def angles(pos, dim):
        freqs = 1.0 / (ROPE_THETA ** (jnp.arange(0, dim, 2, dtype=F32) / dim))
        return pos[:, None] * freqs[None, :]

    rows = s // GRID_W
    row = jnp.repeat(jnp.arange(rows, dtype=F32), GRID_W)
    col = jnp.tile(jnp.arange(GRID_W, dtype=F32), rows)
    half = ATT_HEAD_DIM // 2
    ang_att = jnp.concatenate([angles(row, half), angles(col, half)], axis=-1)
    sign = jnp.tile(jnp.array([-1.0, 1.0], F32), LANES // 2)

    def tables(ang):
        reps = LANES // (2 * ang.shape[1])
        cos = jnp.tile(jnp.repeat(jnp.cos(ang), 2, axis=-1), (1, reps))
        sin = jnp.tile(jnp.repeat(jnp.sin(ang), 2, axis=-1), (1, reps)) * sign
        return cos, sin

    ang_ret = angles(jnp.arange(s, dtype=F32), RET_QK_DIM)
    ang_ret = jnp.concatenate([ang_ret, jnp.zeros((ctx, RET_QK_DIM // 2), F32)], axis=0)
    return tables(ang_att) + tables(ang_ret)


def kernel(x, c, ctx, c_ctx, w_ada, b_ada, norm_mix, norm_ffn, w_in, ret_decay, ret_gn, attn_qn, attn_kn, pool_w, pool_scale, w_branch, w_out, ffn_w1, ffn_w3, ffn_w2, moe_router, moe_router_b, moe_w1, moe_w3, moe_w2, final_norm):
    b, s, d = x.shape
    n_ctx = ctx.shape[1]
    depth = w_in.shape[0]
    dm = Dims(b, s, n_ctx, d)

    x_parts = (x.reshape(b * s, d), ctx.reshape(b * n_ctx, d))
    mod_rows = 2 * SUBLANES
    cc = jnp.concatenate([c, c_ctx[None, :], jnp.zeros((mod_rows - b - 1, d), F32)], axis=0)
    mods = ada_table(cc, w_ada, b_ada).reshape(depth, mod_rows, 6, d)

    att_cos, att_sin, ret_cos, ret_sin = _rope_tables(s, n_ctx)

    for i in range(depth):
        need_ctx = i < depth - 1
        n_rows = dm.n_all if need_ctx else dm.n_lat
        mod = mods[i]
        p, gates = proj_in(dm, x_parts, mod, norm_mix[i], w_in[i].astype(BF16))
        y_ret = retention(dm, p, ret_decay[i], ret_cos, ret_sin, ret_gn[i])
        y_pool = pool_mixer(dm, p, pool_w[i].astype(BF16), pool_scale[i])
        qn, kn, vt = att_prep(dm, p, att_cos, att_sin, attn_qn[i], attn_kn[i])
        y_att = (attention(dm, qn, kn, vt, latent=True),)
        if need_ctx:
            y_att += (attention(dm, qn, kn, vt, latent=False),)
        xa = merge(dm, n_rows, y_ret, y_att, y_pool, gates, x_parts, mod,
                   w_branch[i].astype(BF16), w_out[i].astype(BF16))
        j = i // 2
        last = i == depth - 1
        if i % 2 == 0:
            xa = ffn_dense(dm, xa, mod, norm_ffn[i], ffn_w1[j].astype(BF16), ffn_w3[j].astype(BF16),
                           ffn_w2[j].astype(BF16))
            if last:
                xa = final_rmsnorm(dm, n_rows, xa, final_norm)
        else:
            wr = jnp.zeros((d, LANES), F32).at[:, :N_EXPERTS].set(moe_router[j])
            br = jnp.full((1, LANES), -1e30, F32).at[0, :N_EXPERTS].set(moe_router_b[j])
            h, slot, slot_t, wt, cnt = moe_route(dm, n_rows, xa, mod, norm_ffn[i], wr, br)
            base, row0, blk_expert, blk_valid, cap = moe_layout(dm, n_rows, cnt)
            xs = moe_gather(dm, n_rows, h, slot_t, base, row0, cap)
            ys = moe_ffn(xs, blk_expert, blk_valid,
                         moe_w1[j].astype(BF16), moe_w3[j].astype(BF16), moe_w2[j].astype(BF16))
            xa = moe_scatter(dm, n_rows, ys, slot, wt, base, row0, xa, mod, final_norm if last else None)
        x_parts = (xa,)
    return xa[:dm.n_lat].reshape(b, s, d)
```

```python
import functools
import math

import jax
import jax.numpy as jnp
from jax import lax
from jax.experimental import pallas as pl
from jax.experimental.pallas import tpu as pltpu

F32 = jnp.float32
BF16 = jnp.bfloat16

GRID_W = 64
RET_HEADS = 4
RET_QK_DIM = 64
RET_V_DIM = 128
ATT_HEADS = 4
ATT_KV_HEADS = 2
ATT_HEAD_DIM = 128
ROPE_THETA = 10000.0
POOL_WINDOWS = (2, 4, 8, 16)
POOL_GROUP_DIM = 128
N_EXPERTS = 8
NORM_EPS = 1e-6
GN_EPS = 1e-5

LANES = 128
SUBLANES = 8
VMEM_LIMIT_BYTES = 56 * 1024 * 1024

TM_PROJ = 1024
TN_PROJ = 1536
TM_PREP = 2048
TM_POOL = 1024
TM_MERGE = 1024
TM_FFN = 1024
TM_MOE = 2048
MOE_COL_CHUNK = 512
BLK_MOE = 512
WIN_MOE = 256
SUB_MOE = 512
SLOT_ALIGN = 16
assert N_EXPERTS == SUBLANES
assert WIN_MOE % SLOT_ALIGN == 0 and BLK_MOE % WIN_MOE == 0 and SUB_MOE % SLOT_ALIGN == 0
RET_L = 256
TQ_ATT = 1024
ATT_Q_COL = 2 * RET_HEADS * RET_QK_DIM + 2 * RET_HEADS * RET_V_DIM
ATT_K_COL = ATT_Q_COL + ATT_HEADS * ATT_HEAD_DIM
ATT_V_COL = ATT_K_COL + ATT_KV_HEADS * ATT_HEAD_DIM
ATT_SCORE_SCALE = (ATT_HEAD_DIM ** -0.5) * math.log2(math.e)
ATT_V_ROWS = ATT_HEAD_DIM + 16
ATT_STAGES = 3
ATT_UNROLL = 4
ATT_REDUCE_GROUPS = 8
POOL_HALO = 8


def _cparams(sem):
    return pltpu.CompilerParams(dimension_semantics=sem, vmem_limit_bytes=VMEM_LIMIT_BYTES)


def _dot(a, b):
    return jnp.dot(a.astype(BF16), b.astype(BF16), preferred_element_type=F32)


def _dot_nt(a, b):
    return lax.dot_general(a.astype(BF16), b.astype(BF16), (((1,), (1,)), ((), ())),
                           preferred_element_type=F32)


def _dot_tn(a, b):
    return lax.dot_general(a.astype(BF16), b.astype(BF16), (((0,), (0,)), ((), ())),
                           preferred_element_type=F32)


def _split_bf16(a):
    hi = a.astype(BF16)
    lo = (a - hi.astype(F32)).astype(BF16)
    return hi, lo


def _dot3(a, b):
    ah, al = _split_bf16(a)
    bh, bl = _split_bf16(b)
    d = functools.partial(jnp.dot, preferred_element_type=F32)
    return d(ah, bh) + (d(al, bh) + d(ah, bl))


def _norm_mod(x, gain, shift, scale):
    y = x * lax.rsqrt(jnp.mean(x * x, axis=-1, keepdims=True) + NORM_EPS)
    return (y * gain) * (1.0 + scale) + shift


def _sigmoid(x):
    return 1.0 / (1.0 + jnp.exp(-x))


def _silu(x):
    return x * _sigmoid(x)


class Dims:
    def __init__(self, b, s, ctx, d):
        self.b, self.s, self.ctx, self.d = b, s, ctx, d
        self.n_lat = b * s
        self.n_all = b * s + b * ctx

    def tile(self, preferred):
        t = preferred
        while self.s % t or (self.b * self.ctx) % t:
            t //= 2
        return t

    def mod_row(self, tm):
        lat_tiles, per_batch, b = self.n_lat // tm, self.s // tm, self.b
        return lambda i: jnp.where(i < lat_tiles, i // per_batch, b)


def _ada_kernel(c_ref, w_ref, b_ref, o_ref):
    o_ref[0] = _dot3(_silu(c_ref[...]), w_ref[0]) + b_ref[0]


def ada_table(cc, w_ada, b_ada):
    depth, d, n6 = w_ada.shape
    tn = n6 // 4
    rows = cc.shape[0]
    return pl.pallas_call(
        _ada_kernel,
        grid=(depth, n6 // tn),
        in_specs=[pl.BlockSpec((rows, d), lambda l, j: (0, 0)),
                  pl.BlockSpec((1, d, tn), lambda l, j: (l, 0, j)),
                  pl.BlockSpec((1, 1, tn), lambda l, j: (l, 0, j))],
        out_specs=pl.BlockSpec((1, rows, tn), lambda l, j: (l, 0, j)),
        out_shape=jax.ShapeDtypeStruct((depth, rows, n6), F32),
        compiler_params=_cparams(("arbitrary", "arbitrary")),
        name="ada_table",
    )(cc, w_ada, b_ada.reshape(depth, 1, n6))


def _row_sources(dm, tm, parts):
    width = parts[0].shape[1]
    if len(parts) == 1:
        return [pl.BlockSpec((tm, width), lambda i, *_: (i, 0))]
    lat_tiles = dm.n_lat // tm
    return [pl.BlockSpec((tm, width), lambda i, *_: (jnp.minimum(i, lat_tiles - 1), 0)),
            pl.BlockSpec((tm, width), lambda i, *_: (jnp.maximum(i - lat_tiles, 0), 0))]


def _row_tile(refs, lat_tiles):
    if len(refs) == 1:
        return refs[0][...]
    return jnp.where(pl.program_id(0) < lat_tiles, refs[0][...], refs[1][...])


def _proj_in_kernel(*refs, lat_tiles):
    mod_ref, gain_ref, wm_ref, wg_ref, o_ref, g_ref = refs[-6:]
    x_refs = refs[:-6]
    m = mod_ref[0]
    h = _norm_mod(_row_tile(x_refs, lat_tiles), gain_ref[...], m[0:1], m[1:2]).astype(BF16)
    o_ref[...] = jnp.dot(h, wm_ref[...], preferred_element_type=F32)
    logits = jnp.dot(h, wg_ref[...], preferred_element_type=F32)
    g_ref[...] = (0.5 * jnp.tanh(0.5 * logits) + 0.5).astype(g_ref.dtype)


def proj_in(dm, x_parts, mod, gain, w):
    n, d = dm.n_all, dm.d
    nout = w.shape[1]
    gate_w = 3 * d
    tm, tn = dm.tile(TM_PROJ), TN_PROJ
    assert nout - gate_w == gate_w and gate_w % tn == 0
    gate_blk0 = gate_w // tn
    row = dm.mod_row(tm)
    return pl.pallas_call(
        functools.partial(_proj_in_kernel, lat_tiles=dm.n_lat // tm),
        grid=(n // tm, gate_w // tn),
        in_specs=_row_sources(dm, tm, x_parts) + [
            pl.BlockSpec((1, 6, d), lambda i, j: (row(i), 0, 0)),
            pl.BlockSpec((1, d), lambda i, j: (0, 0)),
            pl.BlockSpec((d, tn), lambda i, j: (0, j)),
            pl.BlockSpec((d, tn), lambda i, j: (0, gate_blk0 + j))],
        out_specs=[pl.BlockSpec((tm, tn), lambda i, j: (i, j)),
                   pl.BlockSpec((tm, tn), lambda i, j: (i, j))],
        out_shape=[jax.ShapeDtypeStruct((n, nout - gate_w), F32),
                   jax.ShapeDtypeStruct((n, gate_w), BF16)],
        compiler_params=_cparams(("parallel", "arbitrary")),
        name="proj_in",
    )(*x_parts, mod, gain.reshape(1, d), w, w)


def _rope_pairs(x, cos, sin):
    lane = lax.broadcasted_iota(jnp.int32, x.shape, 1)
    partner = jnp.where(lane % 2 == 0, pltpu.roll(x, LANES - 1, 1), pltpu.roll(x, 1, 1))
    return x * cos + partner * sin


def _ret_rope(x, cos, sin):
    return jnp.concatenate([_rope_pairs(x[:, j * LANES:(j + 1) * LANES], cos, sin)
                            for j in range(x.shape[1] // LANES)], axis=1)


def _ret_kernel(dec_ref, q_ref, g_ref, kl_ref, kc_ref, vl_ref, vc_ref, cos_ref, sin_ref, gn_ref, o_ref,
                kr_ref, vs_ref, sf_ref, sb_ref, sball_ref, decay_ref, *, ncc, nlc):
    s = pl.program_id(1)
    L = RET_L
    nc = ncc + nlc
    pos = lax.broadcasted_iota(jnp.int32, (L, 1), 0).astype(F32)

    def log_g(direction, h):
        return -jnp.exp(jnp.full((1, 1), dec_ref[direction, h], F32))

    def chunk(c):
        if isinstance(c, int):
            return pl.ds(c * L, L)
        return pl.ds(pl.multiple_of(c * L, L), L)

    def table(c):
        if isinstance(c, int):
            return chunk(nlc + c if c < ncc else c - ncc)
        return chunk(jnp.where(c < ncc, nlc + c, c - ncc))

    def head(x, h, width):
        return x[:, h * width:(h + 1) * width]

    @pl.when(s == 0)
    def _():
        def stage(c, k, v):
            kr_ref[chunk(c), :] = _ret_rope(k, cos_ref[table(c), :], sin_ref[table(c), :]) * (RET_QK_DIM ** -0.5)
            vs_ref[chunk(c), :] = v.astype(vs_ref.dtype)

        for c in range(ncc):
            stage(c, kc_ref[c * L:(c + 1) * L, :], vc_ref[c * L:(c + 1) * L, :])

        def stage_latent(j, carry):
            stage(ncc + j, kl_ref[chunk(j), :], vl_ref[chunk(j), :])
            return carry

        lax.fori_loop(0, nlc, stage_latent, 0)

        diff = (lax.broadcasted_iota(jnp.int32, (L, L), 0)
                - lax.broadcasted_iota(jnp.int32, (L, L), 1)).astype(F32)
        for h in range(RET_HEADS):
            decay_ref[h] = (jnp.where(diff >= 0, jnp.exp(jnp.maximum(diff, 0.0) * log_g(0, h)), 0.0)
                            + jnp.where(diff <= 0, jnp.exp(jnp.maximum(-diff, 0.0) * log_g(1, h)), 0.0))

        sf_ref[...] = jnp.zeros_like(sf_ref)
        sb_ref[...] = jnp.zeros_like(sb_ref)

        def visit(c, carry):
            k = kr_ref[chunk(c), :]
            v = vs_ref[chunk(c), :]
            for h in range(RET_HEADS):
                lgb = log_g(1, h)
                st = sb_ref[h]
                sball_ref[c, h] = st
                kz = head(k, h, RET_QK_DIM) * jnp.exp(pos * lgb)
                sb_ref[h] = jnp.exp(L * lgb) * st + _dot_tn(kz, head(v, h, RET_V_DIM))
            return carry

        for c in reversed(range(ncc)):
            visit(c, 0)
        lax.fori_loop(0, nlc, lambda j, carry: visit(nc - 1 - j, carry), 0)

    q = _ret_rope(q_ref[...], cos_ref[table(s), :], sin_ref[table(s), :])
    k = kr_ref[chunk(s), :]
    v = vs_ref[chunk(s), :]
    g = g_ref[...]
    for h in range(RET_HEADS):
        lgf = log_g(0, h)
        lgb = log_g(1, h)
        qh = head(q, h, RET_QK_DIM)
        kh = head(k, h, RET_QK_DIM)
        vh = head(v, h, RET_V_DIM)
        st_f = sf_ref[h]
        st_b = sball_ref[s, h]
        o = _dot(_dot_nt(qh, kh) * decay_ref[h], vh)
        o = o + _dot(qh, st_f) * jnp.exp((pos + 1.0) * lgf)
        o = o + _dot(qh, st_b) * jnp.exp((L - pos) * lgb)
        sf_ref[h] = jnp.exp(L * lgf) * st_f + _dot_tn(kh * jnp.exp((L - 1.0 - pos) * lgf), vh)
        mu = jnp.mean(o, axis=-1, keepdims=True)
        var = jnp.mean(jnp.square(o - mu), axis=-1, keepdims=True)
        o = (o - mu) * lax.rsqrt(var + GN_EPS) * head(gn_ref[...], h, RET_V_DIM)
        o_ref[:, h * RET_V_DIM:(h + 1) * RET_V_DIM] = (_silu(head(g, h, RET_V_DIM)) * o).astype(o_ref.dtype)


def retention(dm, p, dec, cos_t, sin_t, gn):
    L = RET_L
    ncc, nlc = dm.ctx // L, dm.s // L
    nc = ncc + nlc
    lat_blocks = dm.n_lat // L
    ctx_blk0 = dm.n_lat // dm.ctx
    qk_w = RET_HEADS * RET_QK_DIM
    v_w = RET_HEADS * RET_V_DIM

    def rows(b, s):
        return jnp.where(s < ncc, lat_blocks + b * ncc + s, b * nlc + s - ncc)

    whole = lambda shape: pl.BlockSpec(shape, lambda b, s, d: (0, 0))
    grid_spec = pltpu.PrefetchScalarGridSpec(
        num_scalar_prefetch=1,
        grid=(dm.b, nc),
        in_specs=[pl.BlockSpec((L, qk_w), lambda b, s, d: (rows(b, s), 0)),
                  pl.BlockSpec((L, v_w), lambda b, s, d: (rows(b, s), 2)),
                  pl.BlockSpec((dm.s, qk_w), lambda b, s, d: (b, 1)),
                  pl.BlockSpec((dm.ctx, qk_w), lambda b, s, d: (ctx_blk0 + b, 1)),
                  pl.BlockSpec((dm.s, v_w), lambda b, s, d: (b, 1)),
                  pl.BlockSpec((dm.ctx, v_w), lambda b, s, d: (ctx_blk0 + b, 1)),
                  whole(cos_t.shape), whole(sin_t.shape), whole((1, v_w))],
        out_specs=pl.BlockSpec((L, v_w), lambda b, s, d: (rows(b, s), 0)),
        scratch_shapes=[pltpu.VMEM((nc * L, qk_w), F32),
                        pltpu.VMEM((nc * L, v_w), BF16),
                        pltpu.VMEM((RET_HEADS, RET_QK_DIM, RET_V_DIM), F32),
                        pltpu.VMEM((RET_HEADS, RET_QK_DIM, RET_V_DIM), F32),
                        pltpu.VMEM((nc, RET_HEADS, RET_QK_DIM, RET_V_DIM), F32),
                        pltpu.VMEM((RET_HEADS, L, L), F32)],
    )
    return pl.pallas_call(
        functools.partial(_ret_kernel, ncc=ncc, nlc=nlc),
        grid_spec=grid_spec,
        out_shape=jax.ShapeDtypeStruct((dm.n_all, v_w), BF16),
        compiler_params=_cparams(("parallel", "arbitrary")),
        name="retention",
    )(dec, p, p, p, p, p, p, cos_t, sin_t, gn.reshape(1, v_w))


def _att_prep_kernel(q_ref, k_ref, v_ref, cos_ref, sin_ref, gq_ref, gk_ref, qo_ref, ko_ref, vo_ref,
                     *, lat_tiles):
    is_lat = pl.program_id(0) < lat_tiles
    cos = jnp.where(is_lat, cos_ref[...], 1.0)
    sin = jnp.where(is_lat, sin_ref[...], 0.0)

    hd = ATT_HEAD_DIM

    def prep(x_ref, gain_ref, h):
        xh = x_ref[:, h * hd:(h + 1) * hd]
        xn = xh * lax.rsqrt(jnp.mean(xh * xh, axis=-1, keepdims=True) + NORM_EPS) * gain_ref[...]
        return _rope_pairs(xn, cos, sin)

    for h in range(ATT_HEADS):
        qo_ref[:, h * hd:(h + 1) * hd] = (prep(q_ref, gq_ref, h) * ATT_SCORE_SCALE).astype(qo_ref.dtype)
    ones = jnp.ones((ATT_V_ROWS - hd, vo_ref.shape[1]), vo_ref.dtype)
    for h in range(ATT_KV_HEADS):
        ko_ref[:, h * hd:(h + 1) * hd] = prep(k_ref, gk_ref, h).astype(ko_ref.dtype)
        vo_ref[h * ATT_V_ROWS:h * ATT_V_ROWS + hd, :] = v_ref[:, h * hd:(h + 1) * hd].T.astype(vo_ref.dtype)
        vo_ref[h * ATT_V_ROWS + hd:(h + 1) * ATT_V_ROWS, :] = ones


def att_prep(dm, p, cos_t, sin_t, gq, gk):
    tm = dm.tile(TM_PREP)
    n = dm.n_all
    lat_tiles, per_batch = dm.n_lat // tm, dm.s // tm
    qw = ATT_HEADS * ATT_HEAD_DIM
    kw = ATT_KV_HEADS * ATT_HEAD_DIM
    q_blk, k_blk, v_blk = ATT_Q_COL // qw, ATT_K_COL // kw, ATT_V_COL // kw
    tab = lambda i: (jnp.where(i < lat_tiles, i % per_batch, 0), 0)
    return pl.pallas_call(
        functools.partial(_att_prep_kernel, lat_tiles=lat_tiles),
        grid=(n // tm,),
        in_specs=[pl.BlockSpec((tm, qw), lambda i: (i, q_blk)),
                  pl.BlockSpec((tm, kw), lambda i: (i, k_blk)),
                  pl.BlockSpec((tm, kw), lambda i: (i, v_blk)),
                  pl.BlockSpec((tm, ATT_HEAD_DIM), tab),
                  pl.BlockSpec((tm, ATT_HEAD_DIM), tab),
                  pl.BlockSpec((1, ATT_HEAD_DIM), lambda i: (0, 0)),
                  pl.BlockSpec((1, ATT_HEAD_DIM), lambda i: (0, 0))],
        out_specs=[pl.BlockSpec((tm, qw), lambda i: (i, 0)),
                   pl.BlockSpec((tm, kw), lambda i: (i, 0)),
                   pl.BlockSpec((ATT_KV_HEADS * ATT_V_ROWS, tm), lambda i: (0, i))],
        out_shape=[jax.ShapeDtypeStruct((n, qw), BF16),
                   jax.ShapeDtypeStruct((n, kw), BF16),
                   jax.ShapeDtypeStruct((ATT_KV_HEADS * ATT_V_ROWS, n), BF16)],
        compiler_params=_cparams(("parallel",)),
        name="att_prep",
    )(p, p, p, cos_t, sin_t, gq.reshape(1, -1), gk.reshape(1, -1))


def _attn_kernel(*refs, with_latent_keys):
    ns = ATT_STAGES
    if with_latent_keys:
        qn_ref, kl_ref, vtl_ref, kc_ref, vtc_ref, o_ref, acc_ref = refs[:7]
    else:
        qn_ref, kc_ref, vtc_ref, o_ref, acc_ref = refs[:5]
    s_refs, p_refs = refs[-2 * ns:-ns], refs[-ns:]
    hd = ATT_HEAD_DIM
    grp = ATT_HEADS // ATT_KV_HEADS
    tq = qn_ref.shape[0]
    nq = grp * tq
    tk = kc_ref.shape[0]
    n = 1 + (kl_ref.shape[0] // tk if with_latent_keys else 0)
    q2 = jnp.concatenate([qn_ref[:, j * hd:(j + 1) * hd] for j in range(grp)], axis=0)

    def key_chunk(j):
        if isinstance(j, int) and j == 0:
            return kc_ref[...]
        return kl_ref[pl.ds(pl.multiple_of((j - 1) * tk, tk), tk), :]

    def value_chunk(j):
        if isinstance(j, int) and j == 0:
            return vtc_ref[...]
        return vtl_ref[:, pl.ds(pl.multiple_of((j - 1) * tk, tk), tk)]

    def max_keys(x):
        groups = math.gcd(tk // SUBLANES, ATT_REDUCE_GROUPS)
        part = jnp.max(x.reshape(groups, tk // groups, nq), axis=0)
        return jnp.max(part, axis=0, keepdims=True)

    def time_step(t, slot, carry, do_scores=True, do_softmax=True, do_values=True):
        m, alpha = carry
        if do_scores:
            s_refs[slot][...] = _dot_nt(key_chunk(t), q2)
        if do_values:
            pv = jnp.dot(value_chunk(t - 2), p_refs[(slot - 2) % ns][...], preferred_element_type=F32)
            acc_ref[...] = acc_ref[...] * alpha + pv
        if do_softmax:
            st = s_refs[(slot - 1) % ns][...]
            m_new = jnp.maximum(m, max_keys(st))
            p_refs[(slot - 1) % ns][...] = jnp.exp2(st - m_new).astype(BF16)
            alpha = jnp.exp2(m - m_new)
            m = m_new
        return m, alpha

    acc_ref[...] = jnp.zeros_like(acc_ref)
    carry = (jnp.full((1, nq), -1e30, F32), jnp.zeros((1, nq), F32))
    fill = min(3, n + 2)
    for t in range(fill):
        carry = time_step(t, t % ns, carry, t < n, 0 <= t - 1 < n, 0 <= t - 2 < n)
    steady = max(n - fill, 0)
    body_steps = ns * ATT_UNROLL

    def unrolled(i, cr):
        for k in range(body_steps):
            cr = time_step(fill + body_steps * i + k, (fill + k) % ns, cr)
        return cr

    if steady // body_steps:
        carry = lax.fori_loop(0, steady // body_steps, unrolled, carry)
    for t in range(fill + body_steps * (steady // body_steps), max(n, fill)):
        carry = time_step(t, t % ns, carry)
    for t in range(max(n, fill), n + 2):
        carry = time_step(t, t % ns, carry, False, t - 1 < n, True)
    out = (acc_ref[0:hd, :] / acc_ref[hd:hd + 1, :]).T
    for j in range(grp):
        o_ref[:, j * hd:(j + 1) * hd] = out[j * tq:(j + 1) * tq].astype(o_ref.dtype)


def attention(dm, qn, kn, vt, latent):
    hd = ATT_HEAD_DIM
    grp = ATT_HEADS // ATT_KV_HEADS
    tk = dm.ctx
    assert dm.s % tk == 0
    ctx_blk0 = dm.n_lat // dm.ctx
    kc_spec = pl.BlockSpec((tk, hd), lambda b, h, i: (ctx_blk0 + b, h))
    vtc_spec = pl.BlockSpec((ATT_V_ROWS, tk), lambda b, h, i: (h, ctx_blk0 + b))
    if latent:
        tq = math.gcd(TQ_ATT, dm.s)
        nq = dm.s // tq
        q_spec = pl.BlockSpec((tq, grp * hd), lambda b, h, i: (b * nq + i, h))
        kl_spec = pl.BlockSpec((dm.s, hd), lambda b, h, i: (b, h))
        vtl_spec = pl.BlockSpec((ATT_V_ROWS, dm.s), lambda b, h, i: (h, b))
        in_specs = [q_spec, kl_spec, vtl_spec, kc_spec, vtc_spec]
        args = (qn, kn, vt, kn, vt)
        rows = dm.n_lat
    else:
        tq, nq = dm.ctx, 1
        q_spec = pl.BlockSpec((tq, grp * hd), lambda b, h, i: (ctx_blk0 + b, h))
        in_specs = [q_spec, kc_spec, vtc_spec]
        args = (qn, kn, vt)
        rows = dm.b * dm.ctx
    return pl.pallas_call(
        functools.partial(_attn_kernel, with_latent_keys=latent),
        grid=(dm.b, ATT_KV_HEADS, nq),
        in_specs=in_specs,
        out_specs=pl.BlockSpec((tq, grp * hd), lambda b, h, i: (b * nq + i, h)),
        out_shape=jax.ShapeDtypeStruct((rows, ATT_HEADS * hd), BF16),
        scratch_shapes=([pltpu.VMEM((ATT_V_ROWS, grp * tq), F32)]
                        + [pltpu.VMEM((tk, grp * tq), F32)] * ATT_STAGES
                        + [pltpu.VMEM((tk, grp * tq), BF16)] * ATT_STAGES),
        compiler_params=_cparams(("parallel", "parallel", "arbitrary")),
        name="attention_latent" if latent else "attention_ctx",
    )(*args)


def _pool_kernel(prev_ref, u_ref, next_ref, w_ref, sc_ref, o_ref, a_ref, b_ref, *, dm):
    tm = u_ref.shape[0]
    H = POOL_HALO
    i = pl.program_id(0)
    lat_tiles = dm.n_lat // tm
    is_lat = i < lat_tiles
    n_seg = jnp.where(is_lat, dm.s, dm.ctx)
    base = jnp.where(is_lat, (i * tm) % dm.s, 0)
    r = lax.broadcasted_iota(jnp.int32, (tm + 2 * H, 1), 0) - H
    pos = jnp.where(is_lat, base + r, r % dm.ctx)
    win = pl.ds(H, tm + 2 * H)

    def shifted(ref, d):
        return ref[pl.ds(H + d, tm + 2 * H), :]

    zeros = jnp.zeros((H, POOL_GROUP_DIM), F32)
    for ref in (a_ref, b_ref):
        ref[pl.ds(0, H), :] = zeros
        ref[pl.ds(tm + 3 * H, H), :] = zeros

    for gi, w in enumerate(POOL_WINDOWS):
        cols = slice(gi * POOL_GROUP_DIM, (gi + 1) * POOL_GROUP_DIM)
        half = w // 2
        a_ref[pl.ds(H, H), :] = prev_ref[:, cols]
        a_ref[pl.ds(2 * H, tm), :] = u_ref[:, cols]
        a_ref[pl.ds(2 * H + tm, H), :] = next_ref[:, cols]
        b_ref[win, :] = jnp.where(pos >= 1, shifted(a_ref, -1), 0.0)
        step = 1
        while step < half:
            b_ref[win, :] = b_ref[win, :] + jnp.where(pos >= step, shifted(b_ref, -step), 0.0)
            step *= 2
        step = 1
        while step < half:
            a_ref[win, :] = a_ref[win, :] + jnp.where(pos + step < n_seg, shifted(a_ref, step), 0.0)
            step *= 2
        tile = pl.ds(2 * H, tm)
        pt = pos[H:H + tm]
        cnt = (jnp.minimum(pt + half, n_seg) - jnp.maximum(pt - half, 0)).astype(F32)
        u = u_ref[:, cols]
        diff = (a_ref[tile, :] + b_ref[tile, :]) / cnt - u
        o_ref[:, cols] = (_dot(diff, w_ref[gi]) * sc_ref[:, cols]).astype(o_ref.dtype)


def pool_mixer(dm, p, pool_w, pool_scale):
    tm = dm.tile(TM_POOL)
    n = dm.n_all
    H = POOL_HALO
    width = len(POOL_WINDOWS) * POOL_GROUP_DIM
    col = 2560 // width
    per = tm // H
    last = n // H - 1
    return pl.pallas_call(
        functools.partial(_pool_kernel, dm=dm),
        grid=(n // tm,),
        in_specs=[pl.BlockSpec((H, width), lambda i: (jnp.maximum(i * per - 1, 0), col)),
                  pl.BlockSpec((tm, width), lambda i: (i, col)),
                  pl.BlockSpec((H, width), lambda i: (jnp.minimum((i + 1) * per, last), col)),
                  pl.BlockSpec(pool_w.shape, lambda i: (0, 0, 0)),
                  pl.BlockSpec((1, width), lambda i: (0, 0))],
        out_specs=pl.BlockSpec((tm, width), lambda i: (i, 0)),
        out_shape=jax.ShapeDtypeStruct((n, width), BF16),
        scratch_shapes=[pltpu.VMEM((tm + 4 * H, POOL_GROUP_DIM), F32),
                        pltpu.VMEM((tm + 4 * H, POOL_GROUP_DIM), F32)],
        compiler_params=_cparams(("parallel",)),
        name="pool_mixer",
    )(p, p, p, pool_w, pool_scale.reshape(1, width))


def _merge_kernel(*refs, lat_tiles, n_att, n_x):
    yr_ref, yp_ref, gl_ref = refs[:3]
    ya_refs, x_refs = refs[3:3 + n_att], refs[3 + n_att:3 + n_att + n_x]
    mod_ref, wb_ref, wo_ref, o_ref = refs[-4:]
    d = o_ref.shape[1]
    branches = (yr_ref[...], _row_tile(ya_refs, lat_tiles), yp_ref[...])
    mixed = None
    for j, y in enumerate(branches):
        gate = gl_ref[:, j * d:(j + 1) * d].astype(F32)
        term = gate * jnp.dot(y, wb_ref[j], preferred_element_type=F32)
        mixed = term if mixed is None else mixed + term
    o_ref[...] = _row_tile(x_refs, lat_tiles) + mod_ref[0][2:3] * _dot(mixed, wo_ref[...])


def merge(dm, n_rows, yr, ya_parts, yp, gates, x_parts, mod, wb, wo):
    tm = dm.tile(TM_MERGE)
    d = dm.d
    bw = yr.shape[1]
    row = dm.mod_row(tm)
    return pl.pallas_call(
        functools.partial(_merge_kernel, lat_tiles=dm.n_lat // tm, n_att=len(ya_parts), n_x=len(x_parts)),
        grid=(n_rows // tm,),
        in_specs=[pl.BlockSpec((tm, bw), lambda i: (i, 0)),
                  pl.BlockSpec((tm, bw), lambda i: (i, 0)),
                  pl.BlockSpec((tm, 3 * d), lambda i: (i, 0))]
        + _row_sources(dm, tm, ya_parts) + _row_sources(dm, tm, x_parts) + [
            pl.BlockSpec((1, 6, d), lambda i: (row(i), 0, 0)),
            pl.BlockSpec(wb.shape, lambda i: (0, 0, 0)),
            pl.BlockSpec(wo.shape, lambda i: (0, 0))],
        out_specs=pl.BlockSpec((tm, d), lambda i: (i, 0)),
        out_shape=jax.ShapeDtypeStruct((n_rows, d), F32),
        compiler_params=_cparams(("parallel",)),
        name="merge",
    )(yr, yp, gates, *ya_parts, *x_parts, mod, wb, wo)


def _ffn_kernel(x_ref, mod_ref, gain_ref, w1_ref, w3_ref, w2_ref, o_ref):
    m = mod_ref[0]
    h = _norm_mod(x_ref[...], gain_ref[...], m[3:4], m[4:5]).astype(BF16)
    fdim = w1_ref.shape[1]
    y = None
    for lo in range(0, fdim, MOE_COL_CHUNK):
        cols = slice(lo, min(lo + MOE_COL_CHUNK, fdim))
        a = jnp.dot(h, w1_ref[:, cols], preferred_element_type=F32)
        b = jnp.dot(h, w3_ref[:, cols], preferred_element_type=F32)
        part = _dot(_silu(a) * b, w2_ref[cols, :])
        y = part if y is None else y + part
    o_ref[...] = x_ref[...] + m[5:6] * y


def ffn_dense(dm, x, mod, gain, w1, w3, w2):
    n, d = x.shape
    fdim = w1.shape[1]
    tm = dm.tile(TM_FFN)
    row = dm.mod_row(tm)
    once = dict(pipeline_mode=pl.Buffered(1))
    return pl.pallas_call(
        _ffn_kernel,
        grid=(n // tm,),
        in_specs=[pl.BlockSpec((tm, d), lambda i: (i, 0)),
                  pl.BlockSpec((1, 6, d), lambda i: (row(i), 0, 0)),
                  pl.BlockSpec((1, d), lambda i: (0, 0)),
                  pl.BlockSpec((d, fdim), lambda i: (0, 0), **once),
                  pl.BlockSpec((d, fdim), lambda i: (0, 0), **once),
                  pl.BlockSpec((fdim, d), lambda i: (0, 0), **once)],
        out_specs=pl.BlockSpec((tm, d), lambda i: (i, 0)),
        out_shape=jax.ShapeDtypeStruct((n, d), F32),
        compiler_params=_cparams(("parallel",)),
        name="ffn_dense",
    )(x, mod, gain.reshape(1, d), w1, w3, w2)


def _router_kernel(x_ref, mod_ref, gain_ref, wr_ref, br_ref, h_ref, slot_ref, slot_t_ref, wt_ref, cum_ref):
    tm = x_ref.shape[0]
    m = mod_ref[0]
    h = _norm_mod(x_ref[...], gain_ref[...], m[3:4], m[4:5])
    h_ref[...] = h.astype(BF16)
    logits = _dot3(h, wr_ref[...]) + br_ref[...]
    lane = lax.broadcasted_iota(jnp.int32, logits.shape, 1)
    m1 = jnp.max(logits, axis=-1, keepdims=True)
    i1 = jnp.min(jnp.where(logits == m1, lane, LANES), axis=-1, keepdims=True)
    rest = jnp.where(lane == i1, -jnp.inf, logits)
    m2 = jnp.max(rest, axis=-1, keepdims=True)
    i2 = jnp.min(jnp.where(rest == m2, lane, LANES), axis=-1, keepdims=True)
    e = jnp.exp(m2 - m1)
    w1 = 1.0 / (1.0 + e)
    w2 = e / (1.0 + e)
    wt_ref[...] = jnp.where(lane == i1, w1, jnp.where(lane == i2, w2, 0.0))
    routed = jnp.where((lane == i1) | (lane == i2), 1.0, 0.0)
    sub = SUB_MOE
    lower = (lax.broadcasted_iota(jnp.int32, (sub, sub), 0) > lax.broadcasted_iota(jnp.int32, (sub, sub), 1))
    lower = lower.astype(BF16)
    offset = jnp.zeros((1, LANES), F32)
    cum_ref[...] = jnp.zeros_like(cum_ref)
    for blk in range(tm // sub):
        rows = slice(blk * sub, (blk + 1) * sub)
        mask = routed[rows]
        cum_ref[0, blk:blk + 1, :] = offset.astype(jnp.int32)
        rank = jnp.dot(lower, mask.astype(BF16), preferred_element_type=F32) + offset
        slot_ref[rows, :] = jnp.where(mask > 0.0, rank, -1.0)
        count = jnp.sum(mask, axis=0, keepdims=True)
        offset = offset + jnp.ceil(count * (1.0 / SLOT_ALIGN)) * SLOT_ALIGN
    cum_ref[0, tm // sub:tm // sub + 1, :] = offset.astype(jnp.int32)
    slot_t_ref[...] = slot_ref[...].T[:SUBLANES, :]


def moe_route(dm, n_rows, x, mod, gain, wr_pad, br_pad):
    tm = dm.tile(TM_MOE)
    d = dm.d
    row = dm.mod_row(tm)
    nt = n_rows // tm
    cum_rows = -(-(tm // SUB_MOE + 1) // SUBLANES) * SUBLANES
    return pl.pallas_call(
        _router_kernel,
        grid=(nt,),
        in_specs=[pl.BlockSpec((tm, d), lambda i: (i, 0)),
                  pl.BlockSpec((1, 6, d), lambda i: (row(i), 0, 0)),
                  pl.BlockSpec((1, d), lambda i: (0, 0)),
                  pl.BlockSpec((d, LANES), lambda i: (0, 0)),
                  pl.BlockSpec((1, LANES), lambda i: (0, 0))],
        out_specs=[pl.BlockSpec((tm, d), lambda i: (i, 0)),
                   pl.BlockSpec((tm, LANES), lambda i: (i, 0)),
                   pl.BlockSpec((SUBLANES, tm), lambda i: (0, i)),
                   pl.BlockSpec((tm, LANES), lambda i: (i, 0)),
                   pl.BlockSpec((1, cum_rows, LANES), lambda i: (i, 0, 0))],
        out_shape=[jax.ShapeDtypeStruct((n_rows, d), BF16),
                   jax.ShapeDtypeStruct((n_rows, LANES), F32),
                   jax.ShapeDtypeStruct((SUBLANES, n_rows), F32),
                   jax.ShapeDtypeStruct((n_rows, LANES), F32),
                   jax.ShapeDtypeStruct((nt, cum_rows, LANES), jnp.int32)],
        compiler_params=_cparams(("parallel",)),
        name="moe_router",
    )(x, mod, gain.reshape(1, d), wr_pad, br_pad)


def _moe_windows(base_ref, row0_ref, e, step=None):
    i, j = pl.program_id(0), pl.program_id(1)
    if step is not None:
        i, j = step // pl.num_programs(1), step % pl.num_programs(1)
    start = base_ref[i, j * N_EXPERTS + e]
    nwin = (base_ref[i, (j + 1) * N_EXPERTS + e] - start + WIN_MOE - 1) // WIN_MOE
    return start, nwin, row0_ref[i, e]


def _moe_gather_kernel(base_ref, row0_ref, h_ref, slot_t_ref, xs_in_ref, xs_ref, buf_ref, sem):
    del xs_in_ref
    win, sub = WIN_MOE, SUB_MOE
    spare = N_EXPERTS

    def gathered(e, start, w):
        want = (lax.broadcasted_iota(jnp.int32, (win, sub), 0) + (start + w * win)).astype(F32)
        onehot = (slot_t_ref[e:e + 1, :] == want).astype(BF16)
        return jnp.dot(onehot, h_ref[...], preferred_element_type=F32).astype(BF16)

    def copy(e, buf, w):
        start, _, row0 = _moe_windows(base_ref, row0_ref, e)
        rows = pl.ds(pl.multiple_of(row0 + start + w * win, SLOT_ALIGN), win)
        return pltpu.make_async_copy(buf_ref.at[buf], xs_ref.at[rows, :], sem.at[buf])

    for e in range(N_EXPERTS):
        start, _, _ = _moe_windows(base_ref, row0_ref, e)
        buf_ref[e] = gathered(e, start, 0)
        copy(e, e, 0).start()

    for e in range(N_EXPERTS):
        start, nwin, _ = _moe_windows(base_ref, row0_ref, e)

        def extra(w, carry, e=e, start=start):
            buf_ref[spare] = gathered(e, start, w)
            cp = copy(e, spare, w)
            cp.start()
            cp.wait()
            return carry

        lax.fori_loop(1, jnp.maximum(nwin, 1), extra, 0)

    for e in range(N_EXPERTS):
        copy(e, e, 0).wait()


def moe_gather(dm, n_rows, h, slot_t, base, row0, cap):
    tm, sub, d = dm.tile(TM_MOE), SUB_MOE, dm.d
    nsub = tm // sub
    grid_spec = pltpu.PrefetchScalarGridSpec(
        num_scalar_prefetch=2,
        grid=(n_rows // tm, nsub),
        in_specs=[pl.BlockSpec((sub, d), lambda i, j, b, r: (i * nsub + j, 0)),
                  pl.BlockSpec((SUBLANES, sub), lambda i, j, b, r: (0, i * nsub + j)),
                  pl.BlockSpec(memory_space=pl.ANY)],
        out_specs=pl.BlockSpec(memory_space=pl.ANY),
        scratch_shapes=[pltpu.VMEM((N_EXPERTS + 1, WIN_MOE, d), BF16),
                        pltpu.SemaphoreType.DMA((N_EXPERTS + 1,))],
    )
    return pl.pallas_call(
        _moe_gather_kernel,
        grid_spec=grid_spec,
        out_shape=jax.ShapeDtypeStruct((cap, d), BF16),
        input_output_aliases={4: 0},
        compiler_params=_cparams(("arbitrary", "arbitrary")),
        name="moe_gather",
    )(base, row0, h, slot_t, jnp.zeros((cap, d), BF16))


def _moe_ffn_kernel(bexp_ref, bvalid_ref, xs_ref, w1_ref, w3_ref, w2_ref, o_ref):
    del bexp_ref
    valid = bvalid_ref[pl.program_id(0)] == 1

    @pl.when(valid)
    def _():
        xs = xs_ref[...]
        fdim = w1_ref.shape[2]
        y = None
        for lo in range(0, fdim, MOE_COL_CHUNK):
            cols = slice(lo, min(lo + MOE_COL_CHUNK, fdim))
            a = jnp.dot(xs, w1_ref[0, :, cols], preferred_element_type=F32)
            b = jnp.dot(xs, w3_ref[0, :, cols], preferred_element_type=F32)
            part = _dot(_silu(a) * b, w2_ref[0, cols, :])
            y = part if y is None else y + part
        o_ref[...] = y.astype(o_ref.dtype)

    @pl.when(jnp.logical_not(valid))
    def _():
        o_ref[...] = jnp.zeros_like(o_ref)


def moe_ffn(xs, blk_expert, blk_valid, w1, w3, w2):
    cap, d = xs.shape
    _, _, fdim = w1.shape
    blk = BLK_MOE
    once = dict(pipeline_mode=pl.Buffered(1))
    grid_spec = pltpu.PrefetchScalarGridSpec(
        num_scalar_prefetch=2,
        grid=(cap // blk,),
        in_specs=[pl.BlockSpec((blk, d), lambda k, be, bv: (k, 0)),
                  pl.BlockSpec((1, d, fdim), lambda k, be, bv: (be[k], 0, 0), **once),
                  pl.BlockSpec((1, d, fdim), lambda k, be, bv: (be[k], 0, 0), **once),
                  pl.BlockSpec((1, fdim, d), lambda k, be, bv: (be[k], 0, 0), **once)],
        out_specs=pl.BlockSpec((blk, d), lambda k, be, bv: (k, 0)),
    )
    return pl.pallas_call(
        _moe_ffn_kernel,
        grid_spec=grid_spec,
        out_shape=jax.ShapeDtypeStruct((cap, d), BF16),
        compiler_params=_cparams(("arbitrary",)),
        name="moe_ffn",
    )(blk_expert, blk_valid, xs, w1, w3, w2)


def _moe_scatter_kernel(base_ref, row0_ref, slot_ref, wt_ref, x_ref, mod_ref, gain_ref, ys_ref, o_ref,
                        buf_ref, y_ref, sem, *, final):
    win, sub = WIN_MOE, SUB_MOE
    nbuf = N_EXPERTS + 1
    step = pl.program_id(0) * pl.num_programs(1) + pl.program_id(1)
    last_step = pl.num_programs(0) * pl.num_programs(1) - 1
    mine = (step % 2) * nbuf
    spare = mine + N_EXPERTS

    def copy(e, buf, w, at=None):
        start, _, row0 = _moe_windows(base_ref, row0_ref, e, at)
        rows = pl.ds(pl.multiple_of(row0 + start + w * win, SLOT_ALIGN), win)
        return pltpu.make_async_copy(ys_ref.at[rows, :], buf_ref.at[buf], sem.at[buf])

    @pl.when(step == 0)
    def _():
        for e in range(N_EXPERTS):
            copy(e, e, 0).start()

    @pl.when(step < last_step)
    def _():
        for e in range(N_EXPERTS):
            copy(e, nbuf - mine + e, 0, at=step + 1).start()

    def scattered(e, buf, w):
        start, _, _ = _moe_windows(base_ref, row0_ref, e)
        want = (lax.broadcasted_iota(jnp.int32, (sub, win), 1) + (start + w * win)).astype(F32)
        onehot = (slot_ref[:, e:e + 1] == want).astype(BF16)
        return wt_ref[:, e:e + 1] * jnp.dot(onehot, buf_ref[buf], preferred_element_type=F32)

    y_ref[...] = jnp.zeros_like(y_ref)
    for e in range(N_EXPERTS):
        _, nwin, _ = _moe_windows(base_ref, row0_ref, e)

        def extra(w, carry, e=e):
            cp = copy(e, spare, w)
            cp.start()
            cp.wait()
            y_ref[...] += scattered(e, spare, w)
            return carry

        lax.fori_loop(1, jnp.maximum(nwin, 1), extra, 0)

    y = y_ref[...]
    for e in range(N_EXPERTS):
        copy(e, mine + e, 0).wait()
        y = y + scattered(e, mine + e, 0)
    x = x_ref[...] + mod_ref[0][5:6] * y
    if final:
        x = x * lax.rsqrt(jnp.mean(x * x, axis=-1, keepdims=True) + NORM_EPS) * gain_ref[...]
    o_ref[...] = x


def moe_scatter(dm, n_rows, ys, slot, wt, base, row0, x, mod, final_gain):
    tm, sub, d = dm.tile(TM_MOE), SUB_MOE, dm.d
    nsub = tm // sub
    row = dm.mod_row(sub)
    final = final_gain is not None
    gain = final_gain if final else jnp.ones((d,), F32)
    tok = lambda i, j, b, r: (i * nsub + j, 0)
    grid_spec = pltpu.PrefetchScalarGridSpec(
        num_scalar_prefetch=2,
        grid=(n_rows // tm, nsub),
        in_specs=[pl.BlockSpec((sub, LANES), tok),
                  pl.BlockSpec((sub, LANES), tok),
                  pl.BlockSpec((sub, d), tok),
                  pl.BlockSpec((1, 6, d), lambda i, j, b, r: (row(i * nsub + j), 0, 0)),
                  pl.BlockSpec((1, d), lambda i, j, b, r: (0, 0)),
                  pl.BlockSpec(memory_space=pl.ANY)],
        out_specs=pl.BlockSpec((sub, d), tok),
        scratch_shapes=[pltpu.VMEM((2 * (N_EXPERTS + 1), WIN_MOE, d), BF16),
                        pltpu.VMEM((sub, d), F32),
                        pltpu.SemaphoreType.DMA((2 * (N_EXPERTS + 1),))],
    )
    return pl.pallas_call(
        functools.partial(_moe_scatter_kernel, final=final),
        grid_spec=grid_spec,
        out_shape=jax.ShapeDtypeStruct((n_rows, d), F32),
        compiler_params=_cparams(("arbitrary", "arbitrary")),
        name="moe_scatter",
    )(base, row0, slot, wt, x, mod, gain.reshape(1, d), ys)


def moe_layout(dm, n_rows, cum):
    tm = dm.tile(TM_MOE)
    nt, nsub = n_rows // tm, tm // SUB_MOE
    base = cum[:, :nsub + 1, :N_EXPERTS]
    total = base[:, nsub, :]
    per_expert = jnp.sum(total, axis=0)
    padded = (per_expert + WIN_MOE + BLK_MOE - 1) // BLK_MOE * BLK_MOE
    expert_row0 = jnp.cumsum(padded) - padded
    row0 = expert_row0[None, :] + jnp.cumsum(total, axis=0) - total
    slack = nt * nsub * N_EXPERTS * (SLOT_ALIGN - 1) + N_EXPERTS * (WIN_MOE + BLK_MOE)
    nblk = -(-(2 * n_rows + slack) // BLK_MOE)
    ends = jnp.cumsum(padded // BLK_MOE)
    k = jnp.arange(nblk, dtype=jnp.int32)
    blk_expert = jnp.minimum(jnp.sum(k[:, None] >= ends[None, :], axis=1), N_EXPERTS - 1).astype(jnp.int32)
    blk_valid = (k < ends[-1]).astype(jnp.int32)
    return base.reshape(nt, (nsub + 1) * N_EXPERTS), row0.astype(jnp.int32), blk_expert, blk_valid, nblk * BLK_MOE


def _final_norm_kernel(x_ref, gain_ref, o_ref):
    x = x_ref[...]
    o_ref[...] = x * lax.rsqrt(jnp.mean(x * x, axis=-1, keepdims=True) + NORM_EPS) * gain_ref[...]


def final_rmsnorm(dm, n_rows, x, gain):
    tm = dm.tile(TM_FFN)
    d = dm.d
    return pl.pallas_call(
        _final_norm_kernel,
        grid=(n_rows // tm,),
        in_specs=[pl.BlockSpec((tm, d), lambda i: (i, 0)), pl.BlockSpec((1, d), lambda i: (0, 0))],
        out_specs=pl.BlockSpec((tm, d), lambda i: (i, 0)),
        out_shape=jax.ShapeDtypeStruct((n_rows, d), F32),
        compiler_params=_cparams(("parallel",)),
        name="final_norm",
    )(x, gain.reshape(1, d))


def _rope_tables(s, ctx):
    def angles(pos, dim):
        freqs = 1.0 / (ROPE_THETA ** (jnp.arange(0, dim, 2, dtype=F32) / dim))
        return pos[:, None] * freqs[None, :]

    rows = s // GRID_W
    row = jnp.repeat(jnp.arange(rows, dtype=F32), GRID_W)
    col = jnp.tile(jnp.arange(GRID_W, dtype=F32), rows)
    half = ATT_HEAD_DIM // 2
    ang_att = jnp.concatenate([angles(row, half), angles(col, half)], axis=-1)
    sign = jnp.tile(jnp.array([-1.0, 1.0], F32), LANES // 2)

    def tables(ang):
        reps = LANES // (2 * ang.shape[1])
        cos = jnp.tile(jnp.repeat(jnp.cos(ang), 2, axis=-1), (1, reps))
        sin = jnp.tile(jnp.repeat(jnp.sin(ang), 2, axis=-1), (1, reps)) * sign
        return cos, sin

    ang_ret = angles(jnp.arange(s, dtype=F32), RET_QK_DIM)
    ang_ret = jnp.concatenate([ang_ret, jnp.zeros((ctx, RET_QK_DIM // 2), F32)], axis=0)
    return tables(ang_att) + tables(ang_ret)


def kernel(x, c, ctx, c_ctx, w_ada, b_ada, norm_mix, norm_ffn, w_in, ret_decay, ret_gn, attn_qn, attn_kn, pool_w, pool_scale, w_branch, w_out, ffn_w1, ffn_w3, ffn_w2, moe_router, moe_router_b, moe_w1, moe_w3, moe_w2, final_norm):
    b, s, d = x.shape
    n_ctx = ctx.shape[1]
    depth = w_in.shape[0]
    dm = Dims(b, s, n_ctx, d)
    assert moe_router.shape[-1] == N_EXPERTS and s % GRID_W == 0 and b + 1 <= 2 * SUBLANES

    x_parts = (x.reshape(b * s, d), ctx.reshape(b * n_ctx, d))
    mod_rows = 2 * SUBLANES
    cc = jnp.concatenate([c, c_ctx[None, :], jnp.zeros((mod_rows - b - 1, d), F32)], axis=0)
    mods = ada_table(cc, w_ada, b_ada).reshape(depth, mod_rows, 6, d)

    att_cos, att_sin, ret_cos, ret_sin = _rope_tables(s, n_ctx)

    for i in range(depth):
        need_ctx = i < depth - 1
        n_rows = dm.n_all if need_ctx else dm.n_lat
        mod = mods[i]
        p, gates = proj_in(dm, x_parts, mod, norm_mix[i], w_in[i].astype(BF16))
        y_ret = retention(dm, p, ret_decay[i], ret_cos, ret_sin, ret_gn[i])
        y_pool = pool_mixer(dm, p, pool_w[i].astype(BF16), pool_scale[i])
        qn, kn, vt = att_prep(dm, p, att_cos, att_sin, attn_qn[i], attn_kn[i])
        y_att = (attention(dm, qn, kn, vt, latent=True),)
        if need_ctx:
            y_att += (attention(dm, qn, kn, vt, latent=False),)
        xa = merge(dm, n_rows, y_ret, y_att, y_pool, gates, x_parts, mod,
                   w_branch[i].astype(BF16), w_out[i].astype(BF16))
        j = i // 2
        last = i == depth - 1
        if i % 2 == 0:
            xa = ffn_dense(dm, xa, mod, norm_ffn[i], ffn_w1[j].astype(BF16), ffn_w3[j].astype(BF16),
                           ffn_w2[j].astype(BF16))
            if last:
                xa = final_rmsnorm(dm, n_rows, xa, final_norm)
        else:
            wr = jnp.zeros((d, LANES), F32).at[:, :N_EXPERTS].set(moe_router[j])
            br = jnp.full((1, LANES), -1e30, F32).at[0, :N_EXPERTS].set(moe_router_b[j])
            h, slot, slot_t, wt, cnt = moe_route(dm, n_rows, xa, mod, norm_ffn[i], wr, br)
            base, row0, blk_expert, blk_valid, cap = moe_layout(dm, n_rows, cnt)
            xs = moe_gather(dm, n_rows, h, slot_t, base, row0, cap)
            ys = moe_ffn(xs, blk_expert, blk_valid,
                         moe_w1[j].astype(BF16), moe_w3[j].astype(BF16), moe_w2[j].astype(BF16))
            xa = moe_scatter(dm, n_rows, ys, slot, wt, base, row0, xa, mod, final_norm if last else None)
        x_parts = (xa,)
    return xa[:dm.n_lat].reshape(b, s, d)
```

```python
import functools
import math

import jax
import jax.numpy as jnp
from jax import lax
from jax.experimental import pallas as pl
from jax.experimental.pallas import tpu as pltpu

F32 = jnp.float32
BF16 = jnp.bfloat16

GRID_W = 64
RET_HEADS = 4
RET_QK_DIM = 64
RET_V_DIM = 128
ATT_HEADS = 4
ATT_KV_HEADS = 2
ATT_HEAD_DIM = 128
ROPE_THETA = 10000.0
POOL_WINDOWS = (2, 4, 8, 16)
POOL_GROUP_DIM = 128
N_EXPERTS = 8
NORM_EPS = 1e-6
GN_EPS = 1e-5

LANES = 128
SUBLANES = 8
VMEM_LIMIT_BYTES = 56 * 1024 * 1024

TM_PROJ = 1024
TN_PROJ = 1536
TM_PREP = 2048
TM_POOL = 2048
TM_MERGE = 1024
TM_FFN = 1024
TM_MOE = 2048
MOE_COL_CHUNK = 512
BLK_MOE = 512
WIN_MOE = 256
SUB_MOE = 512
SLOT_ALIGN = 16
RET_L = 256
TQ_ATT = 1024
ATT_Q_COL = 2 * RET_HEADS * RET_QK_DIM + 2 * RET_HEADS * RET_V_DIM
ATT_K_COL = ATT_Q_COL + ATT_HEADS * ATT_HEAD_DIM
ATT_V_COL = ATT_K_COL + ATT_KV_HEADS * ATT_HEAD_DIM
ATT_SCORE_SCALE = (ATT_HEAD_DIM ** -0.5) * math.log2(math.e)
ATT_V_ROWS = ATT_HEAD_DIM + 16
ATT_STAGES = 3
ATT_UNROLL = 4
ATT_REDUCE_GROUPS = 8
POOL_HALO = 8


def _cparams(sem):
    return pltpu.CompilerParams(dimension_semantics=sem, vmem_limit_bytes=VMEM_LIMIT_BYTES)


def _dot(a, b):
    return jnp.dot(a.astype(BF16), b.astype(BF16), preferred_element_type=F32)


def _dot_nt(a, b):
    return lax.dot_general(a.astype(BF16), b.astype(BF16), (((1,), (1,)), ((), ())),
                           preferred_element_type=F32)


def _dot_tn(a, b):
    return lax.dot_general(a.astype(BF16), b.astype(BF16), (((0,), (0,)), ((), ())),
                           preferred_element_type=F32)


def _split_bf16(a):
    hi = a.astype(BF16)
    lo = (a - hi.astype(F32)).astype(BF16)
    return hi, lo


def _dot3(a, b):
    ah, al = _split_bf16(a)
    bh, bl = _split_bf16(b)
    d = functools.partial(jnp.dot, preferred_element_type=F32)
    return d(ah, bh) + (d(al, bh) + d(ah, bl))


def _norm_mod(x, gain, shift, scale):
    y = x * lax.rsqrt(jnp.mean(x * x, axis=-1, keepdims=True) + NORM_EPS)
    return (y * gain) * (1.0 + scale) + shift


def _sigmoid(x):
    return 1.0 / (1.0 + jnp.exp(-x))


def _silu(x):
    return x * _sigmoid(x)


class Dims:
    def __init__(self, b, s, ctx, d):
        self.b, self.s, self.ctx, self.d = b, s, ctx, d
        self.n_lat = b * s
        self.n_all = b * s + b * ctx

    def tile(self, preferred):
        t = preferred
        while self.s % t or (self.b * self.ctx) % t:
            t //= 2
        return t

    def mod_row(self, tm):
        lat_tiles, per_batch, b = self.n_lat // tm, self.s // tm, self.b
        return lambda i: jnp.where(i < lat_tiles, i // per_batch, b)


def _ada_kernel(c_ref, w_ref, b_ref, o_ref):
    o_ref[0] = _dot3(_silu(c_ref[...]), w_ref[0]) + b_ref[0]


def ada_table(cc, w_ada, b_ada):
    depth, d, n6 = w_ada.shape
    tn = n6 // 4
    rows = cc.shape[0]
    return pl.pallas_call(
        _ada_kernel,
        grid=(depth, n6 // tn),
        in_specs=[pl.BlockSpec((rows, d), lambda l, j: (0, 0)),
                  pl.BlockSpec((1, d, tn), lambda l, j: (l, 0, j)),
                  pl.BlockSpec((1, 1, tn), lambda l, j: (l, 0, j))],
        out_specs=pl.BlockSpec((1, rows, tn), lambda l, j: (l, 0, j)),
        out_shape=jax.ShapeDtypeStruct((depth, rows, n6), F32),
        compiler_params=_cparams(("arbitrary", "arbitrary")),
        name="ada_table",
    )(cc, w_ada, b_ada.reshape(depth, 1, n6))


def _row_sources(dm, tm, parts):
    width = parts[0].shape[1]
    if len(parts) == 1:
        return [pl.BlockSpec((tm, width), lambda i, *_: (i, 0))]
    lat_tiles = dm.n_lat // tm
    return [pl.BlockSpec((tm, width), lambda i, *_: (jnp.minimum(i, lat_tiles - 1), 0)),
            pl.BlockSpec((tm, width), lambda i, *_: (jnp.maximum(i - lat_tiles, 0), 0))]


def _row_tile(refs, lat_tiles):
    if len(refs) == 1:
        return refs[0][...]
    return jnp.where(pl.program_id(0) < lat_tiles, refs[0][...], refs[1][...])


def _proj_in_kernel(*refs, lat_tiles):
    mod_ref, gain_ref, wm_ref, wg_ref, o_ref, g_ref = refs[-6:]
    x_refs = refs[:-6]
    m = mod_ref[0]
    h = _norm_mod(_row_tile(x_refs, lat_tiles), gain_ref[...], m[0:1], m[1:2]).astype(BF16)
    o_ref[...] = jnp.dot(h, wm_ref[...], preferred_element_type=F32)
    logits = jnp.dot(h, wg_ref[...], preferred_element_type=F32)
    g_ref[...] = (0.5 * jnp.tanh(0.5 * logits) + 0.5).astype(g_ref.dtype)


def proj_in(dm, x_parts, mod, gain, w):
    n, d = dm.n_all, dm.d
    nout = w.shape[1]
    gate_w = 3 * d
    tm, tn = dm.tile(TM_PROJ), TN_PROJ
    assert nout - gate_w == gate_w and gate_w % tn == 0
    gate_blk0 = gate_w // tn
    row = dm.mod_row(tm)
    return pl.pallas_call(
        functools.partial(_proj_in_kernel, lat_tiles=dm.n_lat // tm),
        grid=(n // tm, gate_w // tn),
        in_specs=_row_sources(dm, tm, x_parts) + [
            pl.BlockSpec((1, 6, d), lambda i, j: (row(i), 0, 0)),
            pl.BlockSpec((1, d), lambda i, j: (0, 0)),
            pl.BlockSpec((d, tn), lambda i, j: (0, j)),
            pl.BlockSpec((d, tn), lambda i, j: (0, gate_blk0 + j))],
        out_specs=[pl.BlockSpec((tm, tn), lambda i, j: (i, j)),
                   pl.BlockSpec((tm, tn), lambda i, j: (i, j))],
        out_shape=[jax.ShapeDtypeStruct((n, nout - gate_w), F32),
                   jax.ShapeDtypeStruct((n, gate_w), BF16)],
        compiler_params=_cparams(("parallel", "arbitrary")),
        name="proj_in",
    )(*x_parts, mod, gain.reshape(1, d), w, w)


def _rope_pairs(x, cos, sin):
    lane = lax.broadcasted_iota(jnp.int32, x.shape, 1)
    partner = jnp.where(lane % 2 == 0, pltpu.roll(x, LANES - 1, 1), pltpu.roll(x, 1, 1))
    return x * cos + partner * sin


def _ret_rope(x, cos, sin):
    return jnp.concatenate([_rope_pairs(x[:, j * LANES:(j + 1) * LANES], cos, sin)
                            for j in range(x.shape[1] // LANES)], axis=1)


def _ret_kernel(dec_ref, q_ref, g_ref, kl_ref, kc_ref, vl_ref, vc_ref, cos_ref, sin_ref, gn_ref, o_ref,
                kr_ref, krt_ref, vs_ref, sf_ref, sb_ref, sball_ref, decay_ref, *, ncc, nlc):
    s = pl.program_id(1)
    L = RET_L
    nc = ncc + nlc
    pos = lax.broadcasted_iota(jnp.int32, (L, 1), 0).astype(F32)
    pos_row = lax.broadcasted_iota(jnp.int32, (1, L), 1).astype(F32)

    def log_g(direction, h):
        return -jnp.exp(jnp.full((1, 1), dec_ref[direction, h], F32))

    def chunk(c):
        if isinstance(c, int):
            return pl.ds(c * L, L)
        return pl.ds(pl.multiple_of(c * L, L), L)

    def table(c):
        if isinstance(c, int):
            return chunk(nlc + c if c < ncc else c - ncc)
        return chunk(jnp.where(c < ncc, nlc + c, c - ncc))

    def head(x, h, width):
        return x[:, h * width:(h + 1) * width]

    @pl.when(s == 0)
    def _():
        def stage(c, k, v):
            kr = _ret_rope(k, cos_ref[table(c), :], sin_ref[table(c), :]) * (RET_QK_DIM ** -0.5)
            kr_ref[chunk(c), :] = kr
            krt_ref[:, chunk(c)] = kr.T
            vs_ref[chunk(c), :] = v.astype(vs_ref.dtype)

        for c in range(ncc):
            stage(c, kc_ref[c * L:(c + 1) * L, :], vc_ref[c * L:(c + 1) * L, :])

        def stage_latent(j, carry):
            stage(ncc + j, kl_ref[chunk(j), :], vl_ref[chunk(j), :])
            return carry

        lax.fori_loop(0, nlc, stage_latent, 0)

        diff = (lax.broadcasted_iota(jnp.int32, (L, L), 0)
                - lax.broadcasted_iota(jnp.int32, (L, L), 1)).astype(F32)
        for h in range(RET_HEADS):
            decay_ref[h] = (jnp.where(diff >= 0, jnp.exp(jnp.maximum(diff, 0.0) * log_g(0, h)), 0.0)
                            + jnp.where(diff <= 0, jnp.exp(jnp.maximum(-diff, 0.0) * log_g(1, h)), 0.0))

        sf_ref[...] = jnp.zeros_like(sf_ref)
        sb_ref[...] = jnp.zeros_like(sb_ref)

        def visit(c, carry):
            v = vs_ref[chunk(c), :]
            for h in range(RET_HEADS):
                lgb = log_g(1, h)
                st = sb_ref[h]
                sball_ref[c, h] = st
                kzt = krt_ref[h * RET_QK_DIM:(h + 1) * RET_QK_DIM, chunk(c)] * jnp.exp(pos_row * lgb)
                sb_ref[h] = jnp.exp(L * lgb) * st + _dot(kzt, head(v, h, RET_V_DIM))
            return carry

        for c in reversed(range(ncc)):
            visit(c, 0)
        lax.fori_loop(0, nlc, lambda j, carry: visit(nc - 1 - j, carry), 0)

    q = _ret_rope(q_ref[...], cos_ref[table(s), :], sin_ref[table(s), :])
    k = kr_ref[chunk(s), :]
    v = vs_ref[chunk(s), :]
    g = g_ref[...]
    for h in range(RET_HEADS):
        lgf = log_g(0, h)
        lgb = log_g(1, h)
        qh = head(q, h, RET_QK_DIM)
        kh = head(k, h, RET_QK_DIM)
        vh = head(v, h, RET_V_DIM)
        st_f = sf_ref[h]
        st_b = sball_ref[s, h]
        o = _dot(_dot_nt(qh, kh) * decay_ref[h], vh)
        o = o + _dot(qh, st_f) * jnp.exp((pos + 1.0) * lgf)
        o = o + _dot(qh, st_b) * jnp.exp((L - pos) * lgb)
        kzt = krt_ref[h * RET_QK_DIM:(h + 1) * RET_QK_DIM, chunk(s)] * jnp.exp((L - 1.0 - pos_row) * lgf)
        sf_ref[h] = jnp.exp(L * lgf) * st_f + _dot(kzt, vh)
        mu = jnp.mean(o, axis=-1, keepdims=True)
        var = jnp.mean(jnp.square(o - mu), axis=-1, keepdims=True)
        o = (o - mu) * lax.rsqrt(var + GN_EPS) * head(gn_ref[...], h, RET_V_DIM)
        o_ref[:, h * RET_V_DIM:(h + 1) * RET_V_DIM] = (_silu(head(g, h, RET_V_DIM)) * o).astype(o_ref.dtype)


def retention(dm, p, dec, cos_t, sin_t, gn):
    L = RET_L
    ncc, nlc = dm.ctx // L, dm.s // L
    nc = ncc + nlc
    lat_blocks = dm.n_lat // L
    ctx_blk0 = dm.n_lat // dm.ctx
    qk_w = RET_HEADS * RET_QK_DIM
    v_w = RET_HEADS * RET_V_DIM

    def rows(b, s):
        return jnp.where(s < ncc, lat_blocks + b * ncc + s, b * nlc + s - ncc)

    whole = lambda shape: pl.BlockSpec(shape, lambda b, s, d: (0, 0))
    grid_spec = pltpu.PrefetchScalarGridSpec(
        num_scalar_prefetch=1,
        grid=(dm.b, nc),
        in_specs=[pl.BlockSpec((L, qk_w), lambda b, s, d: (rows(b, s), 0)),
                  pl.BlockSpec((L, v_w), lambda b, s, d: (rows(b, s), 2)),
                  pl.BlockSpec((dm.s, qk_w), lambda b, s, d: (b, 1)),
                  pl.BlockSpec((dm.ctx, qk_w), lambda b, s, d: (ctx_blk0 + b, 1)),
                  pl.BlockSpec((dm.s, v_w), lambda b, s, d: (b, 1)),
                  pl.BlockSpec((dm.ctx, v_w), lambda b, s, d: (ctx_blk0 + b, 1)),
                  whole(cos_t.shape), whole(sin_t.shape), whole((1, v_w))],
        out_specs=pl.BlockSpec((L, v_w), lambda b, s, d: (rows(b, s), 0)),
        scratch_shapes=[pltpu.VMEM((nc * L, qk_w), F32),
                        pltpu.VMEM((qk_w, nc * L), F32),
                        pltpu.VMEM((nc * L, v_w), BF16),
                        pltpu.VMEM((RET_HEADS, RET_QK_DIM, RET_V_DIM), F32),
                        pltpu.VMEM((RET_HEADS, RET_QK_DIM, RET_V_DIM), F32),
                        pltpu.VMEM((nc, RET_HEADS, RET_QK_DIM, RET_V_DIM), F32),
                        pltpu.VMEM((RET_HEADS, L, L), F32)],
    )
    return pl.pallas_call(
        functools.partial(_ret_kernel, ncc=ncc, nlc=nlc),
        grid_spec=grid_spec,
        out_shape=jax.ShapeDtypeStruct((dm.n_all, v_w), BF16),
        compiler_params=_cparams(("parallel", "arbitrary")),
        name="retention",
    )(dec, p, p, p, p, p, p, cos_t, sin_t, gn.reshape(1, v_w))


def _att_prep_kernel(q_ref, k_ref, v_ref, cos_ref, sin_ref, gq_ref, gk_ref, qo_ref, ko_ref, vo_ref,
                     *, lat_tiles):
    is_lat = pl.program_id(0) < lat_tiles
    cos = jnp.where(is_lat, cos_ref[...], 1.0)
    sin = jnp.where(is_lat, sin_ref[...], 0.0)

    hd = ATT_HEAD_DIM

    def prep(x_ref, gain_ref, h):
        xh = x_ref[:, h * hd:(h + 1) * hd]
        xn = xh * lax.rsqrt(jnp.mean(xh * xh, axis=-1, keepdims=True) + NORM_EPS) * gain_ref[...]
        return _rope_pairs(xn, cos, sin)

    for h in range(ATT_HEADS):
        qo_ref[:, h * hd:(h + 1) * hd] = (prep(q_ref, gq_ref, h) * ATT_SCORE_SCALE).astype(qo_ref.dtype)
    ones = jnp.ones((ATT_V_ROWS - hd, vo_ref.shape[1]), vo_ref.dtype)
    for h in range(ATT_KV_HEADS):
        ko_ref[:, h * hd:(h + 1) * hd] = prep(k_ref, gk_ref, h).astype(ko_ref.dtype)
        vo_ref[h * ATT_V_ROWS:h * ATT_V_ROWS + hd, :] = v_ref[:, h * hd:(h + 1) * hd].T.astype(vo_ref.dtype)
        vo_ref[h * ATT_V_ROWS + hd:(h + 1) * ATT_V_ROWS, :] = ones


def att_prep(dm, p, cos_t, sin_t, gq, gk):
    tm = dm.tile(TM_PREP)
    n = dm.n_all
    lat_tiles, per_batch = dm.n_lat // tm, dm.s // tm
    qw = ATT_HEADS * ATT_HEAD_DIM
    kw = ATT_KV_HEADS * ATT_HEAD_DIM
    q_blk, k_blk, v_blk = ATT_Q_COL // qw, ATT_K_COL // kw, ATT_V_COL // kw
    tab = lambda i: (jnp.where(i < lat_tiles, i % per_batch, 0), 0)
    return pl.pallas_call(
        functools.partial(_att_prep_kernel, lat_tiles=lat_tiles),
        grid=(n // tm,),
        in_specs=[pl.BlockSpec((tm, qw), lambda i: (i, q_blk)),
                  pl.BlockSpec((tm, kw), lambda i: (i, k_blk)),
                  pl.BlockSpec((tm, kw), lambda i: (i, v_blk)),
                  pl.BlockSpec((tm, ATT_HEAD_DIM), tab),
                  pl.BlockSpec((tm, ATT_HEAD_DIM), tab),
                  pl.BlockSpec((1, ATT_HEAD_DIM), lambda i: (0, 0)),
                  pl.BlockSpec((1, ATT_HEAD_DIM), lambda i: (0, 0))],
        out_specs=[pl.BlockSpec((tm, qw), lambda i: (i, 0)),
                   pl.BlockSpec((tm, kw), lambda i: (i, 0)),
                   pl.BlockSpec((ATT_KV_HEADS * ATT_V_ROWS, tm), lambda i: (0, i))],
        out_shape=[jax.ShapeDtypeStruct((n, qw), BF16),
                   jax.ShapeDtypeStruct((n, kw), BF16),
                   jax.ShapeDtypeStruct((ATT_KV_HEADS * ATT_V_ROWS, n), BF16)],
        compiler_params=_cparams(("parallel",)),
        name="att_prep",
    )(p, p, p, cos_t, sin_t, gq.reshape(1, -1), gk.reshape(1, -1))


def _attn_kernel(*refs, with_latent_keys):
    ns = ATT_STAGES
    if with_latent_keys:
        qn_ref, kl_ref, vtl_ref, kc_ref, vtc_ref, o_ref, acc_ref = refs[:7]
    else:
        qn_ref, kc_ref, vtc_ref, o_ref, acc_ref = refs[:5]
    s_refs, p_refs = refs[-2 * ns:-ns], refs[-ns:]
    hd = ATT_HEAD_DIM
    grp = ATT_HEADS // ATT_KV_HEADS
    tq = qn_ref.shape[0]
    nq = grp * tq
    tk = kc_ref.shape[0]
    n = 1 + (kl_ref.shape[0] // tk if with_latent_keys else 0)
    q2 = jnp.concatenate([qn_ref[:, j * hd:(j + 1) * hd] for j in range(grp)], axis=0)

    def key_chunk(j):
        if isinstance(j, int) and j == 0:
            return kc_ref[...]
        return kl_ref[pl.ds(pl.multiple_of((j - 1) * tk, tk), tk), :]

    def value_chunk(j):
        if isinstance(j, int) and j == 0:
            return vtc_ref[...]
        return vtl_ref[:, pl.ds(pl.multiple_of((j - 1) * tk, tk), tk)]

    def max_keys(x):
        groups = math.gcd(tk // SUBLANES, ATT_REDUCE_GROUPS)
        part = jnp.max(x.reshape(groups, tk // groups, nq), axis=0)
        return jnp.max(part, axis=0, keepdims=True)

    def time_step(t, slot, carry, do_scores=True, do_softmax=True, do_values=True):
        m, alpha = carry
        if do_scores:
            s_refs[slot][...] = _dot_nt(key_chunk(t), q2)
        if do_values:
            pv = jnp.dot(value_chunk(t - 2), p_refs[(slot - 2) % ns][...], preferred_element_type=F32)
            acc_ref[...] = acc_ref[...] * alpha + pv
        if do_softmax:
            st = s_refs[(slot - 1) % ns][...]
            m_new = jnp.maximum(m, max_keys(st))
            p_refs[(slot - 1) % ns][...] = jnp.exp2(st - m_new).astype(BF16)
            alpha = jnp.exp2(m - m_new)
            m = m_new
        return m, alpha

    acc_ref[...] = jnp.zeros_like(acc_ref)
    carry = (jnp.full((1, nq), -1e30, F32), jnp.zeros((1, nq), F32))
    fill = min(3, n + 2)
    for t in range(fill):
        carry = time_step(t, t % ns, carry, t < n, 0 <= t - 1 < n, 0 <= t - 2 < n)
    steady = max(n - fill, 0)
    body_steps = ns * ATT_UNROLL

    def unrolled(i, cr):
        for k in range(body_steps):
            cr = time_step(fill + body_steps * i + k, (fill + k) % ns, cr)
        return cr

    if steady // body_steps:
        carry = lax.fori_loop(0, steady // body_steps, unrolled, carry)
    for t in range(fill + body_steps * (steady // body_steps), max(n, fill)):
        carry = time_step(t, t % ns, carry)
    for t in range(max(n, fill), n + 2):
        carry = time_step(t, t % ns, carry, False, t - 1 < n, True)
    out = (acc_ref[0:hd, :] / acc_ref[hd:hd + 1, :]).T
    for j in range(grp):
        o_ref[:, j * hd:(j + 1) * hd] = out[j * tq:(j + 1) * tq].astype(o_ref.dtype)


def attention(dm, qn, kn, vt, latent):
    hd = ATT_HEAD_DIM
    grp = ATT_HEADS // ATT_KV_HEADS
    tk = dm.ctx
    assert dm.s % tk == 0
    ctx_blk0 = dm.n_lat // dm.ctx
    kc_spec = pl.BlockSpec((tk, hd), lambda b, h, i: (ctx_blk0 + b, h))
    vtc_spec = pl.BlockSpec((ATT_V_ROWS, tk), lambda b, h, i: (h, ctx_blk0 + b))
    if latent:
        tq = math.gcd(TQ_ATT, dm.s)
        nq = dm.s // tq
        q_spec = pl.BlockSpec((tq, grp * hd), lambda b, h, i: (b * nq + i, h))
        kl_spec = pl.BlockSpec((dm.s, hd), lambda b, h, i: (b, h))
        vtl_spec = pl.BlockSpec((ATT_V_ROWS, dm.s), lambda b, h, i: (h, b))
        in_specs = [q_spec, kl_spec, vtl_spec, kc_spec, vtc_spec]
        args = (qn, kn, vt, kn, vt)
        rows = dm.n_lat
    else:
        tq, nq = dm.ctx, 1
        q_spec = pl.BlockSpec((tq, grp * hd), lambda b, h, i: (ctx_blk0 + b, h))
        in_specs = [q_spec, kc_spec, vtc_spec]
        args = (qn, kn, vt)
        rows = dm.b * dm.ctx
    return pl.pallas_call(
        functools.partial(_attn_kernel, with_latent_keys=latent),
        grid=(dm.b, ATT_KV_HEADS, nq),
        in_specs=in_specs,
        out_specs=pl.BlockSpec((tq, grp * hd), lambda b, h, i: (b * nq + i, h)),
        out_shape=jax.ShapeDtypeStruct((rows, ATT_HEADS * hd), BF16),
        scratch_shapes=([pltpu.VMEM((ATT_V_ROWS, grp * tq), F32)]
                        + [pltpu.VMEM((tk, grp * tq), F32)] * ATT_STAGES
                        + [pltpu.VMEM((tk, grp * tq), BF16)] * ATT_STAGES),
        compiler_params=_cparams(("parallel", "parallel", "arbitrary")),
        name="attention_latent" if latent else "attention_ctx",
    )(*args)


def _pool_kernel(prev_ref, u_ref, next_ref, w_ref, sc_ref, o_ref, a_ref, b_ref, *, dm):
    tm = u_ref.shape[0]
    H = POOL_HALO
    i = pl.program_id(0)
    lat_tiles = dm.n_lat // tm
    is_lat = i < lat_tiles
    n_seg = jnp.where(is_lat, dm.s, dm.ctx)
    base = jnp.where(is_lat, (i * tm) % dm.s, 0)
    r = lax.broadcasted_iota(jnp.int32, (tm + 2 * H, 1), 0) - H
    pos = jnp.where(is_lat, base + r, r % dm.ctx)
    win = pl.ds(H, tm + 2 * H)

    def shifted(ref, d):
        return ref[pl.ds(H + d, tm + 2 * H), :]

    zeros = jnp.zeros((H, POOL_GROUP_DIM), F32)
    for ref in (a_ref, b_ref):
        ref[pl.ds(0, H), :] = zeros
        ref[pl.ds(tm + 3 * H, H), :] = zeros

    for gi, w in enumerate(POOL_WINDOWS):
        cols = slice(gi * POOL_GROUP_DIM, (gi + 1) * POOL_GROUP_DIM)
        half = w // 2
        a_ref[pl.ds(H, H), :] = prev_ref[:, cols]
        a_ref[pl.ds(2 * H, tm), :] = u_ref[:, cols]
        a_ref[pl.ds(2 * H + tm, H), :] = next_ref[:, cols]
        b_ref[win, :] = jnp.where(pos >= 1, shifted(a_ref, -1), 0.0)
        step = 1
        while step < half:
            b_ref[win, :] = b_ref[win, :] + jnp.where(pos >= step, shifted(b_ref, -step), 0.0)
            step *= 2
        step = 1
        while step < half:
            a_ref[win, :] = a_ref[win, :] + jnp.where(pos + step < n_seg, shifted(a_ref, step), 0.0)
            step *= 2
        tile = pl.ds(2 * H, tm)
        pt = pos[H:H + tm]
        cnt = (jnp.minimum(pt + half, n_seg) - jnp.maximum(pt - half, 0)).astype(F32)
        u = u_ref[:, cols]
        diff = (a_ref[tile, :] + b_ref[tile, :]) / cnt - u
        o_ref[:, cols] = (_dot(diff, w_ref[gi]) * sc_ref[:, cols]).astype(o_ref.dtype)


def pool_mixer(dm, p, pool_w, pool_scale):
    tm = dm.tile(TM_POOL)
    n = dm.n_all
    H = POOL_HALO
    width = len(POOL_WINDOWS) * POOL_GROUP_DIM
    col = 2560 // width
    per = tm // H
    last = n // H - 1
    return pl.pallas_call(
        functools.partial(_pool_kernel, dm=dm),
        grid=(n // tm,),
        in_specs=[pl.BlockSpec((H, width), lambda i: (jnp.maximum(i * per - 1, 0), col)),
                  pl.BlockSpec((tm, width), lambda i: (i, col)),
                  pl.BlockSpec((H, width), lambda i: (jnp.minimum((i + 1) * per, last), col)),
                  pl.BlockSpec(pool_w.shape, lambda i: (0, 0, 0)),
                  pl.BlockSpec((1, width), lambda i: (0, 0))],
        out_specs=pl.BlockSpec((tm, width), lambda i: (i, 0)),
        out_shape=jax.ShapeDtypeStruct((n, width), BF16),
        scratch_shapes=[pltpu.VMEM((tm + 4 * H, POOL_GROUP_DIM), F32),
                        pltpu.VMEM((tm + 4 * H, POOL_GROUP_DIM), F32)],
        compiler_params=_cparams(("parallel",)),
        name="pool_mixer",
    )(p, p, p, pool_w, pool_scale.reshape(1, width))


def _merge_kernel(*refs, lat_tiles, n_att, n_x):
    yr_ref, yp_ref, gl_ref = refs[:3]
    ya_refs, x_refs = refs[3:3 + n_att], refs[3 + n_att:3 + n_att + n_x]
    mod_ref, wb_ref, wo_ref, o_ref = refs[-4:]
    d = o_ref.shape[1]
    branches = (yr_ref[...], _row_tile(ya_refs, lat_tiles), yp_ref[...])
    mixed = None
    for j, y in enumerate(branches):
        gate = gl_ref[:, j * d:(j + 1) * d].astype(F32)
        term = gate * jnp.dot(y, wb_ref[j], preferred_element_type=F32)
        mixed = term if mixed is None else mixed + term
    o_ref[...] = _row_tile(x_refs, lat_tiles) + mod_ref[0][2:3] * _dot(mixed, wo_ref[...])


def merge(dm, n_rows, yr, ya_parts, yp, gates, x_parts, mod, wb, wo):
    tm = dm.tile(TM_MERGE)
    d = dm.d
    bw = yr.shape[1]
    row = dm.mod_row(tm)
    return pl.pallas_call(
        functools.partial(_merge_kernel, lat_tiles=dm.n_lat // tm, n_att=len(ya_parts), n_x=len(x_parts)),
        grid=(n_rows // tm,),
        in_specs=[pl.BlockSpec((tm, bw), lambda i: (i, 0)),
                  pl.BlockSpec((tm, bw), lambda i: (i, 0)),
                  pl.BlockSpec((tm, 3 * d), lambda i: (i, 0))]
        + _row_sources(dm, tm, ya_parts) + _row_sources(dm, tm, x_parts) + [
            pl.BlockSpec((1, 6, d), lambda i: (row(i), 0, 0)),
            pl.BlockSpec(wb.shape, lambda i: (0, 0, 0)),
            pl.BlockSpec(wo.shape, lambda i: (0, 0))],
        out_specs=pl.BlockSpec((tm, d), lambda i: (i, 0)),
        out_shape=jax.ShapeDtypeStruct((n_rows, d), F32),
        compiler_params=_cparams(("parallel",)),
        name="merge",
    )(yr, yp, gates, *ya_parts, *x_parts, mod, wb, wo)


def _ffn_kernel(x_ref, mod_ref, gain_ref, w1_ref, w3_ref, w2_ref, o_ref):
    m = mod_ref[0]
    h = _norm_mod(x_ref[...], gain_ref[...], m[3:4], m[4:5]).astype(BF16)
    fdim = w1_ref.shape[1]
    y = None
    for lo in range(0, fdim, MOE_COL_CHUNK):
        cols = slice(lo, min(lo + MOE_COL_CHUNK, fdim))
        a = jnp.dot(h, w1_ref[:, cols], preferred_element_type=F32)
        b = jnp.dot(h, w3_ref[:, cols], preferred_element_type=F32)
        part = _dot(_silu(a) * b, w2_ref[cols, :])
        y = part if y is None else y + part
    o_ref[...] = x_ref[...] + m[5:6] * y


def ffn_dense(dm, x, mod, gain, w1, w3, w2):
    n, d = x.shape
    fdim = w1.shape[1]
    tm = dm.tile(TM_FFN)
    row = dm.mod_row(tm)
    once = dict(pipeline_mode=pl.Buffered(1))
    return pl.pallas_call(
        _ffn_kernel,
        grid=(n // tm,),
        in_specs=[pl.BlockSpec((tm, d), lambda i: (i, 0)),
                  pl.BlockSpec((1, 6, d), lambda i: (row(i), 0, 0)),
                  pl.BlockSpec((1, d), lambda i: (0, 0)),
                  pl.BlockSpec((d, fdim), lambda i: (0, 0), **once),
                  pl.BlockSpec((d, fdim), lambda i: (0, 0), **once),
                  pl.BlockSpec((fdim, d), lambda i: (0, 0), **once)],
        out_specs=pl.BlockSpec((tm, d), lambda i: (i, 0)),
        out_shape=jax.ShapeDtypeStruct((n, d), F32),
        compiler_params=_cparams(("parallel",)),
        name="ffn_dense",
    )(x, mod, gain.reshape(1, d), w1, w3, w2)


def _router_kernel(x_ref, mod_ref, gain_ref, wr_ref, br_ref, h_ref, slot_ref, slot_t_ref, wt_ref, cum_ref):
    tm = x_ref.shape[0]
    m = mod_ref[0]
    h = _norm_mod(x_ref[...], gain_ref[...], m[3:4], m[4:5])
    h_ref[...] = h.astype(BF16)
    logits = _dot3(h, wr_ref[...]) + br_ref[...]
    lane = lax.broadcasted_iota(jnp.int32, logits.shape, 1)
    m1 = jnp.max(logits, axis=-1, keepdims=True)
    i1 = jnp.min(jnp.where(logits == m1, lane, LANES), axis=-1, keepdims=True)
    rest = jnp.where(lane == i1, -jnp.inf, logits)
    m2 = jnp.max(rest, axis=-1, keepdims=True)
    i2 = jnp.min(jnp.where(rest == m2, lane, LANES), axis=-1, keepdims=True)
    e = jnp.exp(m2 - m1)
    w1 = 1.0 / (1.0 + e)
    w2 = e / (1.0 + e)
    wt_ref[...] = jnp.where(lane == i1, w1, jnp.where(lane == i2, w2, 0.0))
    routed = jnp.where((lane == i1) | (lane == i2), 1.0, 0.0)
    sub = SUB_MOE
    lower = (lax.broadcasted_iota(jnp.int32, (sub, sub), 0) > lax.broadcasted_iota(jnp.int32, (sub, sub), 1))
    lower = lower.astype(BF16)
    offset = jnp.zeros((1, LANES), F32)
    cum_ref[...] = jnp.zeros_like(cum_ref)
    for blk in range(tm // sub):
        rows = slice(blk * sub, (blk + 1) * sub)
        mask = routed[rows]
        cum_ref[0, blk:blk + 1, :] = offset.astype(jnp.int32)
        rank = jnp.dot(lower, mask.astype(BF16), preferred_element_type=F32) + offset
        slot_ref[rows, :] = jnp.where(mask > 0.0, rank, -1.0)
        count = jnp.sum(mask, axis=0, keepdims=True)
        offset = offset + jnp.ceil(count * (1.0 / SLOT_ALIGN)) * SLOT_ALIGN
    cum_ref[0, tm // sub:tm // sub + 1, :] = offset.astype(jnp.int32)
    slot_t_ref[...] = slot_ref[...].T[:SUBLANES, :]


def moe_route(dm, n_rows, x, mod, gain, wr_pad, br_pad):
    tm = dm.tile(TM_MOE)
    d = dm.d
    row = dm.mod_row(tm)
    nt = n_rows // tm
    cum_rows = -(-(tm // SUB_MOE + 1) // SUBLANES) * SUBLANES
    return pl.pallas_call(
        _router_kernel,
        grid=(nt,),
        in_specs=[pl.BlockSpec((tm, d), lambda i: (i, 0)),
                  pl.BlockSpec((1, 6, d), lambda i: (row(i), 0, 0)),
                  pl.BlockSpec((1, d), lambda i: (0, 0)),
                  pl.BlockSpec((d, LANES), lambda i: (0, 0)),
                  pl.BlockSpec((1, LANES), lambda i: (0, 0))],
        out_specs=[pl.BlockSpec((tm, d), lambda i: (i, 0)),
                   pl.BlockSpec((tm, LANES), lambda i: (i, 0)),
                   pl.BlockSpec((SUBLANES, tm), lambda i: (0, i)),
                   pl.BlockSpec((tm, LANES), lambda i: (i, 0)),
                   pl.BlockSpec((1, cum_rows, LANES), lambda i: (i, 0, 0))],
        out_shape=[jax.ShapeDtypeStruct((n_rows, d), BF16),
                   jax.ShapeDtypeStruct((n_rows, LANES), F32),
                   jax.ShapeDtypeStruct((SUBLANES, n_rows), F32),
                   jax.ShapeDtypeStruct((n_rows, LANES), F32),
                   jax.ShapeDtypeStruct((nt, cum_rows, LANES), jnp.int32)],
        compiler_params=_cparams(("parallel",)),
        name="moe_router",
    )(x, mod, gain.reshape(1, d), wr_pad, br_pad)


def _moe_windows(base_ref, row0_ref, e, step=None):
    i, j = pl.program_id(0), pl.program_id(1)
    if step is not None:
        i, j = step // pl.num_programs(1), step % pl.num_programs(1)
    start = base_ref[i, j * N_EXPERTS + e]
    nwin = (base_ref[i, (j + 1) * N_EXPERTS + e] - start + WIN_MOE - 1) // WIN_MOE
    return start, nwin, row0_ref[i, e]


def _moe_gather_kernel(base_ref, row0_ref, h_ref, slot_t_ref, xs_in_ref, xs_ref, buf_ref, sem):
    del xs_in_ref
    win, sub = WIN_MOE, SUB_MOE
    spare = N_EXPERTS

    def gathered(e, start, w):
        want = (lax.broadcasted_iota(jnp.int32, (win, sub), 0) + (start + w * win)).astype(F32)
        onehot = (slot_t_ref[e:e + 1, :] == want).astype(BF16)
        return jnp.dot(onehot, h_ref[...], preferred_element_type=F32).astype(BF16)

    def copy(e, buf, w):
        start, _, row0 = _moe_windows(base_ref, row0_ref, e)
        rows = pl.ds(pl.multiple_of(row0 + start + w * win, SLOT_ALIGN), win)
        return pltpu.make_async_copy(buf_ref.at[buf], xs_ref.at[rows, :], sem.at[buf])

    for e in range(N_EXPERTS):
        start, _, _ = _moe_windows(base_ref, row0_ref, e)
        buf_ref[e] = gathered(e, start, 0)
        copy(e, e, 0).start()

    for e in range(N_EXPERTS):
        start, nwin, _ = _moe_windows(base_ref, row0_ref, e)

        def extra(w, carry, e=e, start=start):
            buf_ref[spare] = gathered(e, start, w)
            cp = copy(e, spare, w)
            cp.start()
            cp.wait()
            return carry

        lax.fori_loop(1, jnp.maximum(nwin, 1), extra, 0)

    for e in range(N_EXPERTS):
        copy(e, e, 0).wait()


def moe_gather(dm, n_rows, h, slot_t, base, row0, cap):
    tm, sub, d = dm.tile(TM_MOE), SUB_MOE, dm.d
    nsub = tm // sub
    grid_spec = pltpu.PrefetchScalarGridSpec(
        num_scalar_prefetch=2,
        grid=(n_rows // tm, nsub),
        in_specs=[pl.BlockSpec((sub, d), lambda i, j, b, r: (i * nsub + j, 0)),
                  pl.BlockSpec((SUBLANES, sub), lambda i, j, b, r: (0, i * nsub + j)),
                  pl.BlockSpec(memory_space=pl.ANY)],
        out_specs=pl.BlockSpec(memory_space=pl.ANY),
        scratch_shapes=[pltpu.VMEM((N_EXPERTS + 1, WIN_MOE, d), BF16),
                        pltpu.SemaphoreType.DMA((N_EXPERTS + 1,))],
    )
    return pl.pallas_call(
        _moe_gather_kernel,
        grid_spec=grid_spec,
        out_shape=jax.ShapeDtypeStruct((cap, d), BF16),
        input_output_aliases={4: 0},
        compiler_params=_cparams(("arbitrary", "arbitrary")),
        name="moe_gather",
    )(base, row0, h, slot_t, jnp.zeros((cap, d), BF16))


def _moe_ffn_kernel(bexp_ref, bvalid_ref, xs_ref, w1_ref, w3_ref, w2_ref, o_ref):
    del bexp_ref
    valid = bvalid_ref[pl.program_id(0)] == 1

    @pl.when(valid)
    def _():
        xs = xs_ref[...]
        fdim = w1_ref.shape[2]
        y = None
        for lo in range(0, fdim, MOE_COL_CHUNK):
            cols = slice(lo, min(lo + MOE_COL_CHUNK, fdim))
            a = jnp.dot(xs, w1_ref[0, :, cols], preferred_element_type=F32)
            b = jnp.dot(xs, w3_ref[0, :, cols], preferred_element_type=F32)
            part = _dot(_silu(a) * b, w2_ref[0, cols, :])
            y = part if y is None else y + part
        o_ref[...] = y.astype(o_ref.dtype)

    @pl.when(jnp.logical_not(valid))
    def _():
        o_ref[...] = jnp.zeros_like(o_ref)


def moe_ffn(xs, blk_expert, blk_valid, w1, w3, w2):
    cap, d = xs.shape
    _, _, fdim = w1.shape
    blk = BLK_MOE
    once = dict(pipeline_mode=pl.Buffered(1))
    grid_spec = pltpu.PrefetchScalarGridSpec(
        num_scalar_prefetch=2,
        grid=(cap // blk,),
        in_specs=[pl.BlockSpec((blk, d), lambda k, be, bv: (k, 0)),
                  pl.BlockSpec((1, d, fdim), lambda k, be, bv: (be[k], 0, 0), **once),
                  pl.BlockSpec((1, d, fdim), lambda k, be, bv: (be[k], 0, 0), **once),
                  pl.BlockSpec((1, fdim, d), lambda k, be, bv: (be[k], 0, 0), **once)],
        out_specs=pl.BlockSpec((blk, d), lambda k, be, bv: (k, 0)),
    )
    return pl.pallas_call(
        _moe_ffn_kernel,
        grid_spec=grid_spec,
        out_shape=jax.ShapeDtypeStruct((cap, d), BF16),
        compiler_params=_cparams(("arbitrary",)),
        name="moe_ffn",
    )(blk_expert, blk_valid, xs, w1, w3, w2)


def _moe_scatter_kernel(base_ref, row0_ref, slot_ref, wt_ref, x_ref, mod_ref, gain_ref, ys_ref, o_ref,
                        buf_ref, y_ref, sem, *, final):
    win, sub = WIN_MOE, SUB_MOE
    nbuf = N_EXPERTS + 1
    step = pl.program_id(0) * pl.num_programs(1) + pl.program_id(1)
    last_step = pl.num_programs(0) * pl.num_programs(1) - 1
    mine = (step % 2) * nbuf
    spare = mine + N_EXPERTS

    def copy(e, buf, w, at=None):
        start, _, row0 = _moe_windows(base_ref, row0_ref, e, at)
        rows = pl.ds(pl.multiple_of(row0 + start + w * win, SLOT_ALIGN), win)
        return pltpu.make_async_copy(ys_ref.at[rows, :], buf_ref.at[buf], sem.at[buf])

    @pl.when(step == 0)
    def _():
        for e in range(N_EXPERTS):
            copy(e, e, 0).start()

    @pl.when(step < last_step)
    def _():
        for e in range(N_EXPERTS):
            copy(e, nbuf - mine + e, 0, at=step + 1).start()

    def scattered(e, buf, w):
        start, _, _ = _moe_windows(base_ref, row0_ref, e)
        want = (lax.broadcasted_iota(jnp.int32, (sub, win), 1) + (start + w * win)).astype(F32)
        onehot = (slot_ref[:, e:e + 1] == want).astype(BF16)
        return wt_ref[:, e:e + 1] * jnp.dot(onehot, buf_ref[buf], preferred_element_type=F32)

    y_ref[...] = jnp.zeros_like(y_ref)
    for e in range(N_EXPERTS):
        _, nwin, _ = _moe_windows(base_ref, row0_ref, e)

        def extra(w, carry, e=e):
            cp = copy(e, spare, w)
            cp.start()
            cp.wait()
            y_ref[...] += scattered(e, spare, w)
            return carry

        lax.fori_loop(1, jnp.maximum(nwin, 1), extra, 0)

    y = y_ref[...]
    for e in range(N_EXPERTS):
        copy(e, mine + e, 0).wait()
        y = y + scattered(e, mine + e, 0)
    x = x_ref[...] + mod_ref[0][5:6] * y
    if final:
        x = x * lax.rsqrt(jnp.mean(x * x, axis=-1, keepdims=True) + NORM_EPS) * gain_ref[...]
    o_ref[...] = x


def moe_scatter(dm, n_rows, ys, slot, wt, base, row0, x, mod, final_gain):
    tm, sub, d = dm.tile(TM_MOE), SUB_MOE, dm.d
    nsub = tm // sub
    row = dm.mod_row(sub)
    final = final_gain is not None
    gain = final_gain if final else jnp.ones((d,), F32)
    tok = lambda i, j, b, r: (i * nsub + j, 0)
    grid_spec = pltpu.PrefetchScalarGridSpec(
        num_scalar_prefetch=2,
        grid=(n_rows // tm, nsub),
        in_specs=[pl.BlockSpec((sub, LANES), tok),
                  pl.BlockSpec((sub, LANES), tok),
                  pl.BlockSpec((sub, d), tok),
                  pl.BlockSpec((1, 6, d), lambda i, j, b, r: (row(i * nsub + j), 0, 0)),
                  pl.BlockSpec((1, d), lambda i, j, b, r: (0, 0)),
                  pl.BlockSpec(memory_space=pl.ANY)],
        out_specs=pl.BlockSpec((sub, d), tok),
        scratch_shapes=[pltpu.VMEM((2 * (N_EXPERTS + 1), WIN_MOE, d), BF16),
                        pltpu.VMEM((sub, d), F32),
                        pltpu.SemaphoreType.DMA((2 * (N_EXPERTS + 1),))],
    )
    return pl.pallas_call(
        functools.partial(_moe_scatter_kernel, final=final),
        grid_spec=grid_spec,
        out_shape=jax.ShapeDtypeStruct((n_rows, d), F32),
        compiler_params=_cparams(("arbitrary", "arbitrary")),
        name="moe_scatter",
    )(base, row0, slot, wt, x, mod, gain.reshape(1, d), ys)


def moe_layout(dm, n_rows, cum):
    tm = dm.tile(TM_MOE)
    nt, nsub = n_rows // tm, tm // SUB_MOE
    base = cum[:, :nsub + 1, :N_EXPERTS]
    total = base[:, nsub, :]
    per_expert = jnp.sum(total, axis=0)
    padded = (per_expert + WIN_MOE + BLK_MOE - 1) // BLK_MOE * BLK_MOE
    expert_row0 = jnp.cumsum(padded) - padded
    row0 = expert_row0[None, :] + jnp.cumsum(total, axis=0) - total
    slack = nt * nsub * N_EXPERTS * (SLOT_ALIGN - 1) + N_EXPERTS * (WIN_MOE + BLK_MOE)
    nblk = -(-(2 * n_rows + slack) // BLK_MOE)
    ends = jnp.cumsum(padded // BLK_MOE)
    k = jnp.arange(nblk, dtype=jnp.int32)
    blk_expert = jnp.minimum(jnp.sum(k[:, None] >= ends[None, :], axis=1), N_EXPERTS - 1).astype(jnp.int32)
    blk_valid = (k < ends[-1]).astype(jnp.int32)
    return base.reshape(nt, (nsub + 1) * N_EXPERTS), row0.astype(jnp.int32), blk_expert, blk_valid, nblk * BLK_MOE


def _final_norm_kernel(x_ref, gain_ref, o_ref):
    x = x_ref[...]
    o_ref[...] = x * lax.rsqrt(jnp.mean(x * x, axis=-1, keepdims=True) + NORM_EPS) * gain_ref[...]


def final_rmsnorm(dm, n_rows, x, gain):
    tm = dm.tile(TM_FFN)
    d = dm.d
    return pl.pallas_call(
        _final_norm_kernel,
        grid=(n_rows // tm,),
        in_specs=[pl.BlockSpec((tm, d), lambda i: (i, 0)), pl.BlockSpec((1, d), lambda i: (0, 0))],
        out_specs=pl.BlockSpec((tm, d), lambda i: (i, 0)),
        out_shape=jax.ShapeDtypeStruct((n_rows, d), F32),
        compiler_params=_cparams(("parallel",)),
        name="final_norm",
    )(x, gain.reshape(1, d))


def _rope_tables(s, ctx):
    def angles(pos, dim):
        freqs = 1.0 / (ROPE_THETA ** (jnp.arange(0, dim, 2, dtype=F32) / dim))
        return pos[:, None] * freqs[None, :]

    rows = s // GRID_W
    row = jnp.repeat(jnp.arange(rows, dtype=F32), GRID_W)
    col = jnp.tile(jnp.arange(GRID_W, dtype=F32), rows)
    half = ATT_HEAD_DIM // 2
    ang_att = jnp.concatenate([angles(row, half), angles(col, half)], axis=-1)
    sign = jnp.tile(jnp.array([-1.0, 1.0], F32), LANES // 2)

    def tables(ang):
        reps = LANES // (2 * ang.shape[1])
        cos = jnp.tile(jnp.repeat(jnp.cos(ang), 2, axis=-1), (1, reps))
        sin = jnp.tile(jnp.repeat(jnp.sin(ang), 2, axis=-1), (1, reps)) * sign
        return cos, sin

    ang_ret = angles(jnp.arange(s, dtype=F32), RET_QK_DIM)
    ang_ret = jnp.concatenate([ang_ret, jnp.zeros((ctx, RET_QK_DIM // 2), F32)], axis=0)
    return tables(ang_att) + tables(ang_ret)


def kernel(x, c, ctx, c_ctx, w_ada, b_ada, norm_mix, norm_ffn, w_in, ret_decay, ret_gn, attn_qn, attn_kn, pool_w, pool_scale, w_branch, w_out, ffn_w1, ffn_w3, ffn_w2, moe_router, moe_router_b, moe_w1, moe_w3, moe_w2, final_norm):
    b, s, d = x.shape
    n_ctx = ctx.shape[1]
    depth = w_in.shape[0]
    dm = Dims(b, s, n_ctx, d)

    x_parts = (x.reshape(b * s, d), ctx.reshape(b * n_ctx, d))
    mod_rows = 2 * SUBLANES
    cc = jnp.concatenate([c, c_ctx[None, :], jnp.zeros((mod_rows - b - 1, d), F32)], axis=0)
    mods = ada_table(cc, w_ada, b_ada).reshape(depth, mod_rows, 6, d)

    att_cos, att_sin, ret_cos, ret_sin = _rope_tables(s, n_ctx)

    for i in range(depth):
        need_ctx = i < depth - 1
        n_rows = dm.n_all if need_ctx else dm.n_lat
        mod = mods[i]
        p, gates = proj_in(dm, x_parts, mod, norm_mix[i], w_in[i].astype(BF16))
        y_ret = retention(dm, p, ret_decay[i], ret_cos, ret_sin, ret_gn[i])
        y_pool = pool_mixer(dm, p, pool_w[i].astype(BF16), pool_scale[i])
        qn, kn, vt = att_prep(dm, p, att_cos, att_sin, attn_qn[i], attn_kn[i])
        y_att = (attention(dm, qn, kn, vt, latent=True),)
        if need_ctx:
            y_att += (attention(dm, qn, kn, vt, latent=False),)
        xa = merge(dm, n_rows, y_ret, y_att, y_pool, gates, x_parts, mod,
                   w_branch[i].astype(BF16), w_out[i].astype(BF16))
        j = i // 2
        last = i == depth - 1
        if i % 2 == 0:
            xa = ffn_dense(dm, xa, mod, norm_ffn[i], ffn_w1[j].astype(BF16), ffn_w3[j].astype(BF16),
                           ffn_w2[j].astype(BF16))
            if last:
                xa = final_rmsnorm(dm, n_rows, xa, final_norm)
        else:
            wr = jnp.zeros((d, LANES), F32).at[:, :N_EXPERTS].set(moe_router[j])
            br = jnp.full((1, LANES), -1e30, F32).at[0, :N_EXPERTS].set(moe_router_b[j])
            h, slot, slot_t, wt, cnt = moe_route(dm, n_rows, xa, mod, norm_ffn[i], wr, br)
            base, row0, blk_expert, blk_valid, cap = moe_layout(dm, n_rows, cnt)
            xs = moe_gather(dm, n_rows, h, slot_t, base, row0, cap)
            ys = moe_ffn(xs, blk_expert, blk_valid,
                         moe_w1[j].astype(BF16), moe_w3[j].astype(BF16), moe_w2[j].astype(BF16))
            xa = moe_scatter(dm, n_rows, ys, slot, wt, base, row0, xa, mod, final_norm if last else None)
        x_parts = (xa,)
    return xa[:dm.n_lat].reshape(b, s, d)
```
